```python
import math
import jax, jax.numpy as jnp
from jax import lax
import numpy as np

D_MODEL = 1024
BATCH = 2
SEQ = 16384
DEPTH = 4

MIX_WIDTH = D_MODEL
HALF_W = MIX_WIDTH // 2
HEAD_DIM = 64
A_HEADS = HALF_W // (2 * HEAD_DIM)
A_QK_W = A_HEADS * 2 * HEAD_DIM
A_V_W = A_HEADS * 2 * HEAD_DIM
B_HEADS = HALF_W // HEAD_DIM
B_W = B_HEADS * HEAD_DIM
IDX_HEADS = 8
IDX_DIM = 32
DSA_TOPK = 256
C_HEADS = HALF_W // HEAD_DIM
C_W = C_HEADS * HEAD_DIM
MOBA_BLOCK = 256
MOBA_TOPK = 3
S5_GROUP_CH = 16
S5_GROUPS = HALF_W // S5_GROUP_CH
S5_STATE = 64
S5_W = S5_GROUPS * S5_GROUP_CH
ROPE_THETA = 500000.0
ROT_HEAD = HEAD_DIM // 4
ROT_IDX = IDX_DIM // 4
Q_BLOCK = 128
MOBA_QBLOCK = 64
N_GROUPS = 4
EXPERTS_PER_GROUP = 8
N_EXPERTS = N_GROUPS * EXPERTS_PER_GROUP
TOPK_FINE = 2
EXPERT_FF = 512
MOE_BLOCK = 128
NORM_EPS = 1e-6
N_EVEN = (DEPTH + 1) // 2
N_ODD = DEPTH // 2
EVEN_SPLIT = (A_QK_W, A_QK_W, A_V_W, B_W, B_W, B_W, IDX_HEADS * IDX_DIM, IDX_DIM, IDX_HEADS)
EVEN_IN = sum(EVEN_SPLIT)
ODD_SPLIT = (C_W, C_W, C_W, S5_W)
ODD_IN = sum(ODD_SPLIT)

kernel_name = "hybrid_diff_dsa_moba_s5_hmoe"

F32 = jnp.float32


def rms_norm(x, gain=None):
    xf = x.astype(F32)
    y = xf * lax.rsqrt(jnp.mean(xf * xf, axis=-1, keepdims=True) + NORM_EPS)
    if gain is not None:
        y = y * gain.astype(F32)
    return y.astype(x.dtype)


def rope_tables(n_pos, rot_dim):
    pos = jnp.arange(n_pos, dtype=F32)
    inv = ROPE_THETA ** (-(jnp.arange(0, rot_dim, 2, dtype=F32) / rot_dim))
    ang = pos[:, None] * inv[None, :]
    return jnp.cos(ang), jnp.sin(ang)


def apply_rope(x, cos, sin):
    half = cos.shape[-1]
    bshape = (1, cos.shape[0]) + (1,) * (x.ndim - 3) + (half,)
    cs = cos.reshape(bshape).astype(x.dtype)
    sn = sin.reshape(bshape).astype(x.dtype)
    x1 = x[..., :half]
    x2 = x[..., half:2 * half]
    return jnp.concatenate([x1 * cs - x2 * sn, x2 * cs + x1 * sn, x[..., 2 * half:]], axis=-1)


def _split(z, sizes):
    out, start = [], 0
    for s in sizes:
        out.append(z[..., start:start + s])
        start += s
    return out


def diff_attention(q, k, v, lam, lam_init, subln_g):
    bsz, seq, n_h, _, d = q.shape
    nqb = seq // Q_BLOCK
    scale = d ** -0.5
    q_blocks = q.reshape(bsz, nqb, Q_BLOCK, n_h, 2, d).swapaxes(0, 1)
    k_pos = jnp.arange(seq)

    def one_block(args):
        q_blk, blk = args
        q_pos = blk * Q_BLOCK + jnp.arange(Q_BLOCK)
        s = jnp.einsum('bqhcd,bkhcd->bhcqk', q_blk, k, preferred_element_type=F32) * scale
        s = jnp.where(k_pos[None, :] <= q_pos[:, None], s, -jnp.inf)
        p = jax.nn.softmax(s, axis=-1)
        attn = p[:, :, 0] - lam * p[:, :, 1]
        return jnp.einsum('bhqk,bkhe->bqhe', attn.astype(v.dtype), v)

    o = lax.map(one_block, (q_blocks, jnp.arange(nqb)))
    o = o.swapaxes(0, 1).reshape(bsz, seq, n_h, v.shape[-1])
    o = rms_norm(o, subln_g) * (1.0 - lam_init)
    return o.reshape(bsz, seq, n_h * v.shape[-1])


def dsa_attention(q, k, v, q_idx, k_idx, w_idx):
    bsz, seq, n_h, d = q.shape
    n_sel = min(DSA_TOPK, seq // 4)
    nqb = seq // Q_BLOCK
    scale = d ** -0.5
    k_pos = jnp.arange(seq)
    b_idx = jnp.arange(bsz)[:, None, None]

    def to_blocks(t):
        return t.reshape((bsz, nqb, Q_BLOCK) + t.shape[2:]).swapaxes(0, 1)

    def one_block(args):
        q_blk, qi_blk, wi_blk, blk = args
        q_pos = blk * Q_BLOCK + jnp.arange(Q_BLOCK)
        logits = jnp.einsum('bqhd,bsd->bqhs', qi_blk, k_idx, preferred_element_type=F32)
        score = jnp.einsum('bqh,bqhs->bqs', wi_blk.astype(F32), jax.nn.relu(logits))
        score = jnp.where((k_pos[None, :] <= q_pos[:, None])[None], score, -jnp.inf)
        _, sel = lax.top_k(score, n_sel)
        valid = sel <= q_pos[None, :, None]
        k_sel = k[b_idx, sel]
        v_sel = v[b_idx, sel]
        s = jnp.einsum('bqhd,bqnhd->bqhn', q_blk, k_sel, preferred_element_type=F32) * scale
        s = jnp.where(valid[:, :, None, :], s, -jnp.inf)
        p = jax.nn.softmax(s, axis=-1)
        return jnp.einsum('bqhn,bqnhd->bqhd', p.astype(v.dtype), v_sel)

    o = lax.map(one_block, (to_blocks(q), to_blocks(q_idx), to_blocks(w_idx), jnp.arange(nqb)))
    return o.swapaxes(0, 1).reshape(bsz, seq, n_h * d)


def moba_attention(q, k, v):
    bsz, seq, n_h, d = q.shape
    n_blk = -(-seq // MOBA_BLOCK)
    n_sel = max(1, min(MOBA_TOPK, n_blk - 1))
    pad = n_blk * MOBA_BLOCK - seq
    scale = d ** -0.5
    k_pad = jnp.pad(k, ((0, 0), (0, pad), (0, 0), (0, 0)))
    v_pad = jnp.pad(v, ((0, 0), (0, pad), (0, 0), (0, 0)))
    k_blocks = k_pad.reshape(bsz, n_blk, MOBA_BLOCK, n_h, d)
    k_mean = jnp.mean(k_blocks.astype(F32), axis=2)
    k_bt = k_blocks.transpose(0, 1, 3, 2, 4)
    v_bt = v_pad.reshape(bsz, n_blk, MOBA_BLOCK, n_h, d).transpose(0, 1, 3, 2, 4)
    nqb = seq // MOBA_QBLOCK
    q_blocks = q.reshape(bsz, nqb, MOBA_QBLOCK, n_h, d).swapaxes(0, 1)
    b_idx = jnp.arange(bsz)[:, None, None, None]
    h_idx = jnp.arange(n_h)[None, None, :, None]

    def one_block(args):
        q_blk, blk = args
        q_start = blk * MOBA_QBLOCK
        q_pos = q_start + jnp.arange(MOBA_QBLOCK)
        own = q_start // MOBA_BLOCK
        gate = jnp.einsum('bqhd,bnhd->bqhn', q_blk.astype(F32), k_mean)
        gate = jnp.where(jnp.arange(n_blk) < own, gate, -jnp.inf)
        _, sel = lax.top_k(gate, n_sel)
        valid = sel < own
        k_sel = k_bt[b_idx, sel, h_idx]
        v_sel = v_bt[b_idx, sel, h_idx]
        s_sel = jnp.einsum('bqhd,bqhnpd->bqhnp', q_blk, k_sel, preferred_element_type=F32) * scale
        s_sel = jnp.where(valid[..., None], s_sel, -jnp.inf)
        s_sel = s_sel.reshape(bsz, MOBA_QBLOCK, n_h, n_sel * MOBA_BLOCK)
        k_own = lax.dynamic_slice_in_dim(k_pad, own * MOBA_BLOCK, MOBA_BLOCK, axis=1)
        v_own = lax.dynamic_slice_in_dim(v_pad, own * MOBA_BLOCK, MOBA_BLOCK, axis=1)
        own_pos = own * MOBA_BLOCK + jnp.arange(MOBA_BLOCK)
        s_own = jnp.einsum('bqhd,bphd->bqhp', q_blk, k_own, preferred_element_type=F32) * scale
        s_own = jnp.where((own_pos[None, :] <= q_pos[:, None])[None, :, None, :], s_own, -jnp.inf)
        p = jax.nn.softmax(jnp.concatenate([s_sel, s_own], axis=-1), axis=-1).astype(v.dtype)
        p_sel = p[..., :n_sel * MOBA_BLOCK].reshape(bsz, MOBA_QBLOCK, n_h, n_sel, MOBA_BLOCK)
        p_own = p[..., n_sel * MOBA_BLOCK:]
        return (jnp.einsum('bqhnp,bqhnpd->bqhd', p_sel, v_sel)
                + jnp.einsum('bqhp,bphd->bqhd', p_own, v_own))

    o = lax.map(one_block, (q_blocks, jnp.arange(nqb)))
    return o.swapaxes(0, 1).reshape(bsz, seq, n_h * d)


def _ssm_combine(left, right):
    a1r, a1i, b1r, b1i = left
    a2r, a2i, b2r, b2i = right
    return (a1r * a2r - a1i * a2i,
            a1r * a2i + a1i * a2r,
            a2r * b1r - a2i * b1i + b2r,
            a2r * b1i + a2i * b1r + b2i)


def s5_scan(u, lam_re, lam_im, b_re, b_im, c_re, c_im, d_skip, log_dt):
    lam_re, lam_im = lam_re.astype(F32), lam_im.astype(F32)
    dt = jnp.exp(log_dt.astype(F32))[:, None]
    mag = jnp.exp(lam_re * dt)
    abar_re = mag * jnp.cos(lam_im * dt)
    abar_im = mag * jnp.sin(lam_im * dt)
    den = lam_re * lam_re + lam_im * lam_im
    coef_re = ((abar_re - 1.0) * lam_re + abar_im * lam_im) / den
    coef_im = (abar_im * lam_re - (abar_re - 1.0) * lam_im) / den
    b_re, b_im = b_re.astype(F32), b_im.astype(F32)
    bbar_re = coef_re[..., None] * b_re - coef_im[..., None] * b_im
    bbar_im = coef_re[..., None] * b_im + coef_im[..., None] * b_re
    uf = u.astype(F32)
    bu_re = jnp.einsum('gpc,bsgc->sbgp', bbar_re, uf)
    bu_im = jnp.einsum('gpc,bsgc->sbgp', bbar_im, uf)
    seq = u.shape[1]
    a_re = jnp.broadcast_to(abar_re[None, None], (seq, 1) + abar_re.shape)
    a_im = jnp.broadcast_to(abar_im[None, None], (seq, 1) + abar_im.shape)
    _, _, h_re, h_im = lax.associative_scan(_ssm_combine, (a_re, a_im, bu_re, bu_im), axis=0)
    y = (jnp.einsum('gcp,sbgp->bsgc', c_re.astype(F32), h_re)
         - jnp.einsum('gcp,sbgp->bsgc', c_im.astype(F32), h_im)
         + d_skip.astype(F32) * uf)
    return y.astype(u.dtype)


def even_mixer(h, w_in, w_out, a_qn, a_kn, a_lq1, a_lk1, a_lq2, a_lk2, a_subln,
               b_qn, b_kn, idx_kn, lam_init, rope_head, rope_idx):
    bsz, seq, _ = h.shape
    cos_h, sin_h = rope_head
    cos_i, sin_i = rope_idx
    z = h @ w_in
    aq, ak, av, bq, bk, bv, iq, ik, iw = _split(z, EVEN_SPLIT)
    aq = apply_rope(rms_norm(aq.reshape(bsz, seq, A_HEADS, 2, HEAD_DIM), a_qn), cos_h, sin_h)
    ak = apply_rope(rms_norm(ak.reshape(bsz, seq, A_HEADS, 2, HEAD_DIM), a_kn), cos_h, sin_h)
    av = av.reshape(bsz, seq, A_HEADS, 2 * HEAD_DIM)
    lam = (jnp.exp(jnp.sum(a_lq1.astype(F32) * a_lk1.astype(F32)))
           - jnp.exp(jnp.sum(a_lq2.astype(F32) * a_lk2.astype(F32))) + lam_init)
    o_a = diff_attention(aq, ak, av, lam, lam_init, a_subln)
    bq = apply_rope(rms_norm(bq.reshape(bsz, seq, B_HEADS, HEAD_DIM), b_qn), cos_h, sin_h)
    bk = apply_rope(rms_norm(bk.reshape(bsz, seq, B_HEADS, HEAD_DIM), b_kn), cos_h, sin_h)
    bv = bv.reshape(bsz, seq, B_HEADS, HEAD_DIM)
    iq = apply_rope(iq.reshape(bsz, seq, IDX_HEADS, IDX_DIM), cos_i, sin_i)
    ik = apply_rope(rms_norm(ik, idx_kn), cos_i, sin_i)
    o_b = dsa_attention(bq, bk, bv, iq, ik, iw)
    return jnp.concatenate([o_a, o_b], axis=-1) @ w_out


def odd_mixer(h, w_in, w_out, c_qn, c_kn, lam_re, lam_im, b_re, b_im, c_re, c_im,
              d_skip, log_dt, glu_w, glu_b, rope_head):
    bsz, seq, _ = h.shape
    cos_h, sin_h = rope_head
    z = h @ w_in
    cq, ck, cv, u = _split(z, ODD_SPLIT)
    cq = apply_rope(rms_norm(cq.reshape(bsz, seq, C_HEADS, HEAD_DIM), c_qn), cos_h, sin_h)
    ck = apply_rope(rms_norm(ck.reshape(bsz, seq, C_HEADS, HEAD_DIM), c_kn), cos_h, sin_h)
    cv = cv.reshape(bsz, seq, C_HEADS, HEAD_DIM)
    o_c = moba_attention(cq, ck, cv)
    y = s5_scan(u.reshape(bsz, seq, S5_GROUPS, S5_GROUP_CH), lam_re, lam_im, b_re, b_im,
                c_re, c_im, d_skip, log_dt).reshape(bsz, seq, S5_W)
    y = jax.nn.gelu(y)
    y = y * jax.nn.sigmoid(y @ glu_w + glu_b)
    return jnp.concatenate([o_c, y], axis=-1) @ w_out


def hier_moe(h, wg, bg, we, be, w1, w3, w2):
    bsz, seq, dm = h.shape
    n_tok = bsz * seq
    xf = h.reshape(n_tok, dm)
    coarse = (xf @ wg).astype(F32) + bg.astype(F32)
    grp = jnp.argmax(coarse, axis=-1)
    tok = jnp.arange(n_tok)
    p_grp = jax.nn.softmax(coarse, axis=-1)[tok, grp][:, None]
    fine = ((xf @ we).astype(F32) + be.astype(F32)).reshape(n_tok, N_GROUPS, EXPERTS_PER_GROUP)
    fine_g = fine[tok, grp]
    top_v, top_i = lax.top_k(fine_g, TOPK_FINE)
    weight = p_grp * jax.nn.softmax(top_v, axis=-1)
    expert = grp[:, None] * EXPERTS_PER_GROUP + top_i
    n_asg = n_tok * TOPK_FINE
    flat_e = expert.reshape(n_asg)
    flat_t = jnp.arange(n_asg) // TOPK_FINE
    flat_w = weight.reshape(n_asg)
    order = jnp.argsort(flat_e, stable=True)
    se = flat_e[order]
    counts = jnp.bincount(flat_e, length=N_EXPERTS)
    padded = ((counts + MOE_BLOCK - 1) // MOE_BLOCK) * MOE_BLOCK
    start = jnp.cumsum(counts) - counts
    pend = jnp.cumsum(padded)
    pstart = pend - padded
    dest = pstart[se] + (jnp.arange(n_asg) - start[se])
    n_pad = n_asg + N_EXPERTS * MOE_BLOCK
    n_blk = n_pad // MOE_BLOCK
    buf_tok = jnp.full((n_pad,), n_tok, dtype=jnp.int32).at[dest].set(flat_t[order].astype(jnp.int32))
    buf_w = jnp.zeros((n_pad,), F32).at[dest].set(flat_w[order])
    blk_e = jnp.clip(jnp.searchsorted(pend, jnp.arange(n_blk) * MOE_BLOCK, side='right'), 0, N_EXPERTS - 1)
    x_pad = jnp.concatenate([xf, jnp.zeros((1, dm), xf.dtype)], axis=0)
    xb = x_pad[buf_tok].reshape(n_blk, MOE_BLOCK, dm)

    def expert_block(args):
        xi, e = args
        hid = jax.nn.silu(xi @ w1[e]) * (xi @ w3[e])
        return hid @ w2[e]

    yb = lax.map(expert_block, (xb, blk_e)).reshape(n_pad, dm)
    yb = yb * buf_w[:, None].astype(yb.dtype)
    out = jnp.zeros((n_tok + 1, dm), yb.dtype).at[buf_tok].add(yb)[:n_tok]
    return out.reshape(bsz, seq, dm)


def setup_inputs(seed: int = 0) -> dict:
    key = jax.random.key(seed)
    ks = iter(jax.random.split(key, 64))

    def nrm(shape, scale):
        return jax.random.normal(next(ks), shape, F32) * scale

    def gain(shape):
        return 1.0 + nrm(shape, 0.1)

    d = D_MODEL
    inputs = {
        "x": nrm((BATCH, SEQ, d), 1.0),
        "c": nrm((BATCH, d), 1.0),
        "ada_w": nrm((DEPTH, d, 6 * d), 0.5 * d ** -0.5),
        "ada_b": nrm((DEPTH, 6 * d), 0.02),
        "moe_wg": nrm((DEPTH, d, N_GROUPS), d ** -0.5),
        "moe_bg": nrm((DEPTH, N_GROUPS), 0.01),
        "moe_we": nrm((DEPTH, d, N_EXPERTS), d ** -0.5),
        "moe_be": nrm((DEPTH, N_EXPERTS), 0.01),
        "moe_w1": nrm((DEPTH, N_EXPERTS, d, EXPERT_FF), d ** -0.5),
        "moe_w3": nrm((DEPTH, N_EXPERTS, d, EXPERT_FF), d ** -0.5),
        "moe_w2": nrm((DEPTH, N_EXPERTS, EXPERT_FF, d), EXPERT_FF ** -0.5),
        "ev_w_in": nrm((N_EVEN, d, EVEN_IN), d ** -0.5),
        "ev_w_out": nrm((N_EVEN, MIX_WIDTH, d), MIX_WIDTH ** -0.5),
        "a_qn": gain((N_EVEN, HEAD_DIM)),
        "a_kn": gain((N_EVEN, HEAD_DIM)),
        "a_lq1": nrm((N_EVEN, HEAD_DIM), 0.1),
        "a_lk1": nrm((N_EVEN, HEAD_DIM), 0.1),
        "a_lq2": nrm((N_EVEN, HEAD_DIM), 0.1),
        "a_lk2": nrm((N_EVEN, HEAD_DIM), 0.1),
        "a_subln": gain((N_EVEN, 2 * HEAD_DIM)),
        "b_qn": gain((N_EVEN, HEAD_DIM)),
        "b_kn": gain((N_EVEN, HEAD_DIM)),
        "b_idx_kn": gain((N_EVEN, IDX_DIM)),
        "od_w_in": nrm((N_ODD, d, ODD_IN), d ** -0.5),
        "od_w_out": nrm((N_ODD, MIX_WIDTH, d), MIX_WIDTH ** -0.5),
        "c_qn": gain((N_ODD, HEAD_DIM)),
        "c_kn": gain((N_ODD, HEAD_DIM)),
        "s5_lam_re": -0.5 + nrm((N_ODD, S5_GROUPS, S5_STATE), 0.01),
        "s5_lam_im": jnp.pi * jnp.arange(S5_STATE, dtype=F32) + nrm((N_ODD, S5_GROUPS, S5_STATE), 0.01),
        "s5_b_re": nrm((N_ODD, S5_GROUPS, S5_STATE, S5_GROUP_CH), (2 * S5_GROUP_CH) ** -0.5),
        "s5_b_im": nrm((N_ODD, S5_GROUPS, S5_STATE, S5_GROUP_CH), (2 * S5_GROUP_CH) ** -0.5),
        "s5_c_re": nrm((N_ODD, S5_GROUPS, S5_GROUP_CH, S5_STATE), 0.5),
        "s5_c_im": nrm((N_ODD, S5_GROUPS, S5_GROUP_CH, S5_STATE), 0.5),
        "s5_d": nrm((N_ODD, S5_GROUPS, S5_GROUP_CH), 0.5),
        "s5_log_dt": jax.random.uniform(next(ks), (N_ODD, S5_GROUPS), F32,
                                        minval=math.log(1e-3), maxval=math.log(1e-1)),
        "s5_glu_w": nrm((N_ODD, S5_W, S5_W), S5_W ** -0.5),
        "s5_glu_b": nrm((N_ODD, S5_W), 0.02),
    }
    return inputs


def reference(x, c, ada_w, ada_b, moe_wg, moe_bg, moe_we, moe_be, moe_w1, moe_w3, moe_w2,
              ev_w_in, ev_w_out, a_qn, a_kn, a_lq1, a_lk1, a_lq2, a_lk2, a_subln,
              b_qn, b_kn, b_idx_kn, od_w_in, od_w_out, c_qn, c_kn,
              s5_lam_re, s5_lam_im, s5_b_re, s5_b_im, s5_c_re, s5_c_im, s5_d, s5_log_dt,
              s5_glu_w, s5_glu_b):
    seq = x.shape[1]
    rope_head = rope_tables(seq, ROT_HEAD)
    rope_idx = rope_tables(seq, ROT_IDX)
    c_act = jax.nn.silu(c)
    for l in range(DEPTH):
        mod = c_act @ ada_w[l] + ada_b[l]
        sh1, sc1, g1, sh2, sc2, g2 = jnp.split(mod, 6, axis=-1)
        h = rms_norm(x) * (1.0 + sc1[:, None, :]) + sh1[:, None, :]
        i = l // 2
        if l % 2 == 0:
            lam_init = 0.8 - 0.6 * math.exp(-0.3 * l)
            y = even_mixer(h, ev_w_in[i], ev_w_out[i], a_qn[i], a_kn[i], a_lq1[i], a_lk1[i],
                           a_lq2[i], a_lk2[i], a_subln[i], b_qn[i], b_kn[i], b_idx_kn[i],
                           lam_init, rope_head, rope_idx)
        else:
            y = odd_mixer(h, od_w_in[i], od_w_out[i], c_qn[i], c_kn[i], s5_lam_re[i], s5_lam_im[i],
                          s5_b_re[i], s5_b_im[i], s5_c_re[i], s5_c_im[i], s5_d[i], s5_log_dt[i],
                          s5_glu_w[i], s5_glu_b[i], rope_head)
        x = x + g1[:, None, :] * y
        h = rms_norm(x) * (1.0 + sc2[:, None, :]) + sh2[:, None, :]
        x = x + g2[:, None, :] * hier_moe(h, moe_wg[l], moe_bg[l], moe_we[l], moe_be[l],
                                          moe_w1[l], moe_w3[l], moe_w2[l])
    return x
```

```python
import functools
import math

import jax
import jax.numpy as jnp
from jax import lax
from jax.experimental import pallas as pl
from jax.experimental.pallas import tpu as pltpu

F32 = jnp.float32
BF16 = jnp.bfloat16
I32 = jnp.int32
HI = lax.Precision.HIGHEST

LANES = 128
NEG = -1e30
I32_MIN = -2147483648

HEAD_DIM = 64
A_HEADS = 4
B_HEADS = 8
IDX_HEADS = 8
IDX_DIM = 32
DSA_TOPK = 256
C_HEADS = 8
MOBA_BLOCK = 256
MOBA_TOPK = 3
S5_GROUP_CH = 16
S5_GROUPS = 32
S5_STATE = 64
ROPE_THETA = 500000.0
ROT_HEAD = HEAD_DIM // 4
ROT_IDX = IDX_DIM // 4
N_GROUPS = 4
EXPERTS_PER_GROUP = 8
N_EXPERTS = N_GROUPS * EXPERTS_PER_GROUP
EXPERT_FF = 512
NORM_EPS = 1e-6
S5_CHUNK = 16


def _cparams(sem, vmem_mb=48):
    return pltpu.CompilerParams(dimension_semantics=sem, vmem_limit_bytes=vmem_mb * 1024 * 1024)


def _nt_dot(a, b, precision=None):
    return lax.dot_general(a, b, (((1,), (1,)), ((), ())), precision=precision,
                           preferred_element_type=F32)


def _mod_kernel(c_ref, w_ref, b_ref, o_ref):
    c = c_ref[...]
    ca = c * jax.nn.sigmoid(c)
    o_ref[0] = jnp.dot(ca, w_ref[0], precision=HI, preferred_element_type=F32) + b_ref[0]


def ada_mod(c, ada_w, ada_b):
    depth, d, n6 = ada_w.shape
    bsz = c.shape[0]
    rows = 8
    c_pad = jnp.zeros((rows, d), F32).at[:bsz].set(c)
    out = pl.pallas_call(
        _mod_kernel,
        grid=(depth, n6 // d),
        in_specs=[pl.BlockSpec((rows, d), lambda l, n: (0, 0)),
                  pl.BlockSpec((1, d, d), lambda l, n: (l, 0, n)),
                  pl.BlockSpec((1, 1, d), lambda l, n: (l, 0, n))],
        out_specs=pl.BlockSpec((1, rows, d), lambda l, n: (l, 0, n)),
        out_shape=jax.ShapeDtypeStruct((depth, rows, n6), F32),
        compiler_params=_cparams(("parallel", "parallel")),
        name="ada_mod",
    )(c_pad, ada_w, ada_b.reshape(depth, 1, n6))
    return out[:, :bsz]


def _norm_mod(x, sc, sh):
    y = x * lax.rsqrt(jnp.mean(x * x, axis=-1, keepdims=True) + NORM_EPS)
    return y * (1.0 + sc) + sh


def _inproj_kernel(x_ref, sc_ref, sh_ref, w_ref, o_ref, h_sc):
    @pl.when(pl.program_id(2) == 0)
    def _():
        h_sc[...] = _norm_mod(x_ref[0], sc_ref[0], sh_ref[0]).astype(BF16)

    o_ref[0] = jnp.dot(h_sc[...], w_ref[...], preferred_element_type=F32)


def _pick_tile(n, prefs):
    for p in prefs:
        if n % p == 0:
            return p
    return n


def in_proj(x, sc, sh, w_bf16, tm=512):
    bsz, seq, d = x.shape
    n = w_bf16.shape[1]
    tm = _pick_tile(seq, (tm, 256, 128))
    tn = _pick_tile(n, (1152, 1024, 512, 384, 256, 128))
    return pl.pallas_call(
        _inproj_kernel,
        grid=(bsz, seq // tm, n // tn),
        in_specs=[pl.BlockSpec((1, tm, d), lambda b, i, j: (b, i, 0)),
                  pl.BlockSpec((1, 1, d), lambda b, i, j: (b, 0, 0)),
                  pl.BlockSpec((1, 1, d), lambda b, i, j: (b, 0, 0)),
                  pl.BlockSpec((d, tn), lambda b, i, j: (0, j))],
        out_specs=pl.BlockSpec((1, tm, tn), lambda b, i, j: (b, i, j)),
        out_shape=jax.ShapeDtypeStruct((bsz, seq, n), F32),
        scratch_shapes=[pltpu.VMEM((tm, d), BF16)],
        compiler_params=_cparams(("parallel", "parallel", "arbitrary")),
        name="in_proj",
    )(x, sc, sh, w_bf16)


def rope_lane_tables(seq, hd, rot):
    half = rot // 2
    pos = jnp.arange(seq, dtype=F32)
    inv = ROPE_THETA ** (-(jnp.arange(0, rot, 2, dtype=F32) / rot))
    ang = pos[:, None] * inv[None, :]
    cos, sin = jnp.cos(ang), jnp.sin(ang)
    dl = jnp.arange(LANES) % hd
    first = dl < half
    second = (dl >= half) & (dl < rot)
    idx = jnp.where(first, dl, jnp.where(second, dl - half, 0))
    cos_l = jnp.where(first | second, cos[:, idx], 1.0)
    sa_l = jnp.where(second, sin[:, idx], 0.0)
    sb_l = jnp.where(first, -sin[:, idx], 0.0)
    return cos_l.astype(F32), sa_l.astype(F32), sb_l.astype(F32)


def _prep_kernel(*refs, hd, half, do_norm, do_rope, scale, n_out):
    z_ref, g_ref, gm_ref, cos_ref, sa_ref, sb_ref = refs[:6]
    o_refs = refs[6:6 + n_out]
    x = z_ref[0]
    width = x.shape[-1]
    if do_norm:
        sq = x * x
        hi = sq.astype(BF16)
        lo = (sq - hi.astype(F32)).astype(BF16)
        gm = gm_ref[...]
        ms = (jnp.dot(hi, gm, preferred_element_type=F32)
              + jnp.dot(lo, gm, preferred_element_type=F32)) * (1.0 / hd)
        x = x * lax.rsqrt(ms + NORM_EPS) * g_ref[...]
    if do_rope:
        cos, sa, sb = cos_ref[...], sa_ref[...], sb_ref[...]
        cols = []
        for r in range(width // LANES):
            xc = x[:, r * LANES:(r + 1) * LANES]
            cols.append(xc * cos + pltpu.roll(xc, half, 1) * sa + pltpu.roll(xc, LANES - half, 1) * sb)
        x = cols[0] if len(cols) == 1 else jnp.concatenate(cols, axis=-1)
    if scale != 1.0:
        x = x * scale
    for o in o_refs:
        o[0] = x.astype(o.dtype)


def prep(z, col_off, width, *, hd, gain=None, rope=None, half=0, scale=1.0, out_dtypes=(BF16,), tm=512):
    bsz, seq, _ = z.shape
    tm = _pick_tile(seq, (tm, 256, 128))
    cb = col_off // width
    assert col_off % width == 0 and width % LANES == 0
    do_norm = gain is not None
    do_rope = rope is not None
    g = jnp.tile(gain.astype(F32), width // hd).reshape(1, width) if do_norm else jnp.ones((1, width), F32)
    gi = jnp.arange(width) // hd
    gm = (gi[:, None] == gi[None, :]).astype(BF16)
    if do_rope:
        cos_l, sa_l, sb_l = rope
    else:
        cos_l = sa_l = sb_l = jnp.zeros((seq, LANES), F32)
    outs = pl.pallas_call(
        functools.partial(_prep_kernel, hd=hd, half=half, do_norm=do_norm, do_rope=do_rope,
                          scale=scale, n_out=len(out_dtypes)),
        grid=(bsz, seq // tm),
        in_specs=[pl.BlockSpec((1, tm, width), lambda b, i: (b, i, cb)),
                  pl.BlockSpec((1, width), lambda b, i: (0, 0)),
                  pl.BlockSpec((width, width), lambda b, i: (0, 0)),
                  pl.BlockSpec((tm, LANES), lambda b, i: (i, 0)),
                  pl.BlockSpec((tm, LANES), lambda b, i: (i, 0)),
                  pl.BlockSpec((tm, LANES), lambda b, i: (i, 0))],
        out_specs=[pl.BlockSpec((1, tm, width), lambda b, i: (b, i, 0)) for _ in out_dtypes],
        out_shape=[jax.ShapeDtypeStruct((bsz, seq, width), dt) for dt in out_dtypes],
        compiler_params=_cparams(("parallel", "parallel")),
        name="prep",
    )(z, g, gm, cos_l, sa_l, sb_l)
    return outs


def _diff_attn_kernel(q_ref, k_ref, v_ref, lq1_ref, lk1_ref, lq2_ref, lk2_ref, g_ref, o_ref,
                      m_sc, l_sc, acc_sc, *, tq, tk, nk, lam_init):
    i = pl.program_id(2)
    j = pl.program_id(3)

    @pl.when(j == 0)
    def _():
        m_sc[...] = jnp.full(m_sc.shape, NEG, F32)
        l_sc[...] = jnp.zeros(l_sc.shape, F32)
        acc_sc[...] = jnp.zeros(acc_sc.shape, F32)

    @pl.when(j * tk <= i * tq + (tq - 1))
    def _():
        q = q_ref[0]
        k = k_ref[0]
        v = v_ref[0]
        lane = lax.broadcasted_iota(I32, (1, LANES), 1)
        qpos = i * tq + lax.broadcasted_iota(I32, (tq, tk), 0)
        kpos = j * tk + lax.broadcasted_iota(I32, (tq, tk), 1)
        causal = kpos <= qpos
        for c in range(2):
            qc = jnp.where((lane >= c * HEAD_DIM) & (lane < (c + 1) * HEAD_DIM), q, jnp.zeros_like(q))
            s = jnp.where(causal, _nt_dot(qc, k), NEG)
            m_prev = m_sc[c]
            m_new = jnp.maximum(m_prev, jnp.max(s, axis=-1, keepdims=True))
            alpha = jnp.exp(m_prev - m_new)
            p = jnp.exp(s - m_new)
            l_sc[c] = alpha * l_sc[c] + jnp.sum(p, axis=-1, keepdims=True)
            acc_sc[c] = alpha * acc_sc[c] + jnp.dot(p.astype(BF16), v, preferred_element_type=F32)
            m_sc[c] = m_new

    @pl.when(j == nk - 1)
    def _():
        lam = (jnp.exp(jnp.sum(lq1_ref[...] * lk1_ref[...], axis=-1, keepdims=True))
               - jnp.exp(jnp.sum(lq2_ref[...] * lk2_ref[...], axis=-1, keepdims=True)) + lam_init)
        o = acc_sc[0] / l_sc[0] - lam * (acc_sc[1] / l_sc[1])
        o = o * lax.rsqrt(jnp.mean(o * o, axis=-1, keepdims=True) + NORM_EPS)
        o_ref[0] = o * g_ref[...] * (1.0 - lam_init)


def diff_attention(q, k, v, lq1, lk1, lq2, lk2, subln, lam_init, tq=512, tk=512):
    bsz, seq, width = q.shape
    n_h = width // LANES
    tq = _pick_tile(seq, (tq, 256, 128))
    tk = _pick_tile(seq, (tk, 256, 128))
    nq, nk = seq // tq, seq // tk
    vec = lambda a: a.astype(F32).reshape(1, -1)

    def kv_map(b, h, i, j):
        return (b, jnp.minimum(j, (i * tq + tq - 1) // tk), h)

    small = pl.BlockSpec((1, HEAD_DIM), lambda b, h, i, j: (0, 0))
    return pl.pallas_call(
        functools.partial(_diff_attn_kernel, tq=tq, tk=tk, nk=nk, lam_init=lam_init),
        grid=(bsz, n_h, nq, nk),
        in_specs=[pl.BlockSpec((1, tq, LANES), lambda b, h, i, j: (b, i, h)),
                  pl.BlockSpec((1, tk, LANES), kv_map),
                  pl.BlockSpec((1, tk, LANES), kv_map),
                  small, small, small, small,
                  pl.BlockSpec((1, LANES), lambda b, h, i, j: (0, 0))],
        out_specs=pl.BlockSpec((1, tq, LANES), lambda b, h, i, j: (b, i, h)),
        out_shape=jax.ShapeDtypeStruct((bsz, seq, width), F32),
        scratch_shapes=[pltpu.VMEM((2, tq, 1), F32), pltpu.VMEM((2, tq, 1), F32),
                        pltpu.VMEM((2, tq, LANES), F32)],
        compiler_params=_cparams(("parallel", "parallel", "parallel", "arbitrary")),
        name="diff_attn",
    )(q, k, v, vec(lq1), vec(lk1), vec(lq2), vec(lk2), vec(subln))


def _sortable_key(score):
    bits = pltpu.bitcast(score + 0.0, I32)
    return bits ^ ((bits >> 31) & 0x7FFFFFFF)


def _dsa_kernel(q_ref, k_ref, v_ref, qi_ref, kit_ref, wz_ref, o_ref, kb_sc, m_sc, l_sc, acc_sc,
                *, tq, tk, nk, n_sel, iw_lane):
    i = pl.program_id(1)
    j = pl.program_id(2)
    n_chunks = (i * tq + tq + tk - 1) // tk
    lane_fold = tk // LANES

    def count_where(pred_fn):
        def body(cidx, acc):
            g = jnp.where(pred_fn(kb_sc[cidx], cidx), 1, 0)
            for r in range(lane_fold):
                acc = acc + g[:, r * LANES:(r + 1) * LANES]
            return acc
        acc = lax.fori_loop(0, n_chunks, body, jnp.zeros((tq, LANES), I32))
        return jnp.sum(acc, axis=1, keepdims=True)

    @pl.when(j == 0)
    def _():
        m_sc[...] = jnp.full(m_sc.shape, NEG, F32)
        l_sc[...] = jnp.zeros(l_sc.shape, F32)
        acc_sc[...] = jnp.zeros(acc_sc.shape, F32)

        qi = qi_ref[0]
        wi = wz_ref[0][:, iw_lane:iw_lane + IDX_HEADS]
        lane_q = lax.broadcasted_iota(I32, (1, IDX_HEADS * IDX_DIM), 1)
        q_heads = [jnp.where((lane_q >= h * IDX_DIM) & (lane_q < (h + 1) * IDX_DIM), qi, 0.0).astype(BF16)
                   for h in range(IDX_HEADS)]
        w_cols = [wi[:, h:h + 1] for h in range(IDX_HEADS)]
        qrow = i * tq + lax.broadcasted_iota(I32, (tq, 1), 0)
        col = lax.broadcasted_iota(I32, (tq, tk), 1)

        def score_chunk(cidx, carry):
            off = pl.multiple_of(cidx * tk, tk)
            kt = kit_ref[0, :, pl.ds(off, tk)].astype(BF16)
            kt_rep = jnp.concatenate([kt] * IDX_HEADS, axis=0)
            score = jnp.zeros((tq, tk), F32)
            for h in range(IDX_HEADS):
                lg = jnp.dot(q_heads[h], kt_rep, preferred_element_type=F32)
                score = score + w_cols[h] * jnp.maximum(lg, 0.0)
            key = _sortable_key(score)
            kb_sc[cidx] = jnp.where(col + off <= qrow, key, I32_MIN)
            return carry

        lax.fori_loop(0, n_chunks, score_chunk, 0)

        def bisect(it, lohi):
            lo, hi = lohi
            mid = (lo >> 1) + (hi >> 1) + (lo & hi & 1)
            ok = count_where(lambda kk, cidx: kk >= mid) >= n_sel
            return jnp.where(ok, mid, lo), jnp.where(ok, hi, mid)

        t, _ = lax.fori_loop(0, 32, bisect, (jnp.full((tq, 1), I32_MIN, I32),
                                             jnp.full((tq, 1), 2147483647, I32)))

        c_gt = count_where(lambda kk, cidx: kk > t)
        c_eq = count_where(lambda kk, cidx: kk == t)
        need = jnp.where(t == I32_MIN, 0, n_sel - c_gt)
        all_ties = c_eq <= need
        jmax0 = jnp.where(need <= 0, -1, jnp.where(all_ties, 2147483647, -1))
        unresolved = jnp.logical_and(need > 0, jnp.logical_not(all_ties))

        def tie_search(_):
            def step(it, lohi):
                lo, hi = lohi
                mid = (lo + hi) >> 1
                cnt = count_where(lambda kk, cidx: jnp.logical_and(kk == t, col + cidx * tk <= mid))
                ok = cnt >= need
                return jnp.where(ok, lo, mid), jnp.where(ok, mid, hi)
            n_steps = 1 + max(1, (nk * tk - 1).bit_length())
            _, hi = lax.fori_loop(0, n_steps, step, (jnp.full((tq, 1), -1, I32),
                                                     jnp.full((tq, 1), nk * tk - 1, I32)))
            return jnp.where(unresolved, hi, jmax0)

        any_unresolved = jnp.max(jnp.where(unresolved, 1, 0)) > 0
        jmax = lax.cond(any_unresolved, tie_search, lambda _: jmax0, 0)

        def bias_chunk(cidx, carry):
            kk = kb_sc[cidx]
            thr = jnp.where(col + cidx * tk <= jmax, t - 1, t)
            bias = jnp.where(kk > thr, 0.0, NEG)
            kb_sc[cidx] = pltpu.bitcast(bias.astype(F32), I32)
            return carry

        lax.fori_loop(0, n_chunks, bias_chunk, 0)

    @pl.when(j * tk <= i * tq + (tq - 1))
    def _():
        q = q_ref[0]
        k = k_ref[0]
        v = v_ref[0]
        bias = pltpu.bitcast(kb_sc[j], F32)
        lane = lax.broadcasted_iota(I32, (1, LANES), 1)
        for h in range(B_HEADS):
            pair, sub = h // 2, h % 2
            cols = slice(pair * LANES, (pair + 1) * LANES)
            qp = q[:, cols]
            qh = jnp.where((lane >= sub * HEAD_DIM) & (lane < (sub + 1) * HEAD_DIM), qp, jnp.zeros_like(qp))
            s = _nt_dot(qh, k[:, cols]) + bias
            m_prev = m_sc[h]
            m_new = jnp.maximum(m_prev, jnp.max(s, axis=-1, keepdims=True))
            alpha = jnp.exp(m_prev - m_new)
            p = jnp.exp(s - m_new)
            l_sc[h] = alpha * l_sc[h] + jnp.sum(p, axis=-1, keepdims=True)
            acc_sc[h] = alpha * acc_sc[h] + jnp.dot(p.astype(BF16), v[:, cols], preferred_element_type=F32)
            m_sc[h] = m_new

    @pl.when(j == nk - 1)
    def _():
        lane = lax.broadcasted_iota(I32, (1, LANES), 1)
        for pair in range(B_HEADS // 2):
            o0 = acc_sc[2 * pair] / l_sc[2 * pair]
            o1 = acc_sc[2 * pair + 1] / l_sc[2 * pair + 1]
            o_ref[0, :, pair * LANES:(pair + 1) * LANES] = jnp.where(lane < HEAD_DIM, o0, o1)


def dsa_attention(q, k, v, qi, kit, z, iw_col, tq=256, tk=512):
    bsz, seq, width = q.shape
    tq = _pick_tile(seq, (tq, 128))
    tk = _pick_tile(seq, (tk, 256, 128))
    nq, nk = seq // tq, seq // tk
    n_sel = min(DSA_TOPK, seq // 4)

    def kv_map(b, i, j):
        return (b, jnp.minimum(j, (i * tq + tq - 1) // tk), 0)

    return pl.pallas_call(
        functools.partial(_dsa_kernel, tq=tq, tk=tk, nk=nk, n_sel=n_sel, iw_lane=iw_col % LANES),
        grid=(bsz, nq, nk),
        in_specs=[pl.BlockSpec((1, tq, width), lambda b, i, j: (b, i, 0)),
                  pl.BlockSpec((1, tk, width), kv_map),
                  pl.BlockSpec((1, tk, width), kv_map),
                  pl.BlockSpec((1, tq, IDX_HEADS * IDX_DIM), lambda b, i, j: (b, i, 0)),
                  pl.BlockSpec((1, IDX_DIM, seq), lambda b, i, j: (b, 0, 0)),
                  pl.BlockSpec((1, tq, LANES), lambda b, i, j: (b, i, iw_col // LANES))],
        out_specs=pl.BlockSpec((1, tq, width), lambda b, i, j: (b, i, 0)),
        out_shape=jax.ShapeDtypeStruct((bsz, seq, width), F32),
        scratch_shapes=[pltpu.VMEM((nk, tq, tk), I32),
                        pltpu.VMEM((B_HEADS, tq, 1), F32), pltpu.VMEM((B_HEADS, tq, 1), F32),
                        pltpu.VMEM((B_HEADS, tq, LANES), F32)],
        compiler_params=_cparams(("parallel", "parallel", "arbitrary"), vmem_mb=56),
        name="dsa_attn",
    )(q, k, v, qi, kit, z)


def _kmean_kernel(k_ref, o_ref, *, nbb):
    x = k_ref[0]
    o_ref[0] = jnp.mean(x.reshape(nbb, MOBA_BLOCK, x.shape[-1]), axis=1)


def block_means(k_f32):
    bsz, seq, width = k_f32.shape
    nb = seq // MOBA_BLOCK
    nbb = 8 if nb % 8 == 0 else nb
    return pl.pallas_call(
        functools.partial(_kmean_kernel, nbb=nbb),
        grid=(bsz, nb // nbb),
        in_specs=[pl.BlockSpec((1, nbb * MOBA_BLOCK, width), lambda b, i: (b, i, 0))],
        out_specs=pl.BlockSpec((1, nbb, width), lambda b, i: (b, i, 0)),
        out_shape=jax.ShapeDtypeStruct((bsz, nb, width), F32),
        compiler_params=_cparams(("parallel", "parallel")),
        name="moba_kmean",
    )(k_f32)


def _moba_kernel(q_ref, km_ref, k_ref, v_ref, o_ref, sel_sc, m_sc, l_sc, acc_sc, *, tq, nkb, n_sel, nbp):
    i = pl.program_id(2)
    j = pl.program_id(3)
    lane = lax.broadcasted_iota(I32, (1, LANES), 1)
    blk = lax.broadcasted_iota(I32, (tq, nbp), 1)

    def head_q(qf, sub):
        return jnp.where((lane >= sub * HEAD_DIM) & (lane < (sub + 1) * HEAD_DIM), qf, 0.0)

    @pl.when(j == 0)
    def _():
        m_sc[...] = jnp.full(m_sc.shape, NEG, F32)
        l_sc[...] = jnp.zeros(l_sc.shape, F32)
        acc_sc[...] = jnp.zeros(acc_sc.shape, F32)
        qf = q_ref[0]
        for sub in range(2):
            gate = _nt_dot(head_q(qf, sub), km_ref[0], precision=HI)
            g = jnp.where(blk < i, gate, -jnp.inf)
            sel = jnp.zeros((tq, nbp), F32)
            for _ in range(n_sel):
                mx = jnp.max(g, axis=1, keepdims=True)
                is_mx = jnp.logical_and(g == mx, mx > -jnp.inf)
                first = jnp.min(jnp.where(is_mx, blk, 2 * nbp), axis=1, keepdims=True)
                pick = blk == first
                sel = jnp.where(pick, 1.0, sel)
                g = jnp.where(pick, -jnp.inf, g)
            sel_sc[sub] = sel

    def attend(mask_fn):
        qf = q_ref[0]
        k = k_ref[0]
        v = v_ref[0]
        for sub in range(2):
            s = mask_fn(_nt_dot(head_q(qf, sub).astype(BF16), k), sub)
            m_prev = m_sc[sub]
            m_new = jnp.maximum(m_prev, jnp.max(s, axis=-1, keepdims=True))
            alpha = jnp.exp(m_prev - m_new)
            p = jnp.exp(s - m_new)
            l_sc[sub] = alpha * l_sc[sub] + jnp.sum(p, axis=-1, keepdims=True)
            acc_sc[sub] = alpha * acc_sc[sub] + jnp.dot(p.astype(BF16), v, preferred_element_type=F32)
            m_sc[sub] = m_new

    @pl.when(j == 0)
    def _():
        row = lax.broadcasted_iota(I32, (tq, MOBA_BLOCK), 0)
        col = lax.broadcasted_iota(I32, (tq, MOBA_BLOCK), 1)
        attend(lambda s, sub: jnp.where(col <= row, s, NEG))

    @pl.when(jnp.logical_and(j > 0, j <= i))
    def _():
        kb = i - j

        def past_mask(s, sub):
            picked = jnp.sum(jnp.where(blk == kb, sel_sc[sub], 0.0), axis=1, keepdims=True)
            return s + jnp.where(picked > 0.0, 0.0, NEG)

        attend(past_mask)

    @pl.when(j == nkb - 1)
    def _():
        o_ref[0] = jnp.where(lane < HEAD_DIM, acc_sc[0] / l_sc[0], acc_sc[1] / l_sc[1])


def moba_attention(q_f32, kmean, k, v):
    bsz, seq, width = q_f32.shape
    tq = MOBA_BLOCK
    nb = seq // MOBA_BLOCK
    n_sel = max(1, min(MOBA_TOPK, nb - 1))
    nbp = -(-nb // LANES) * LANES
    km = jnp.zeros((bsz, nbp, width), F32).at[:, :nb].set(kmean)

    def kv_map(b, h, i, j):
        return (b, jnp.maximum(i - j, 0), h)

    return pl.pallas_call(
        functools.partial(_moba_kernel, tq=tq, nkb=nb, n_sel=n_sel, nbp=nbp),
        grid=(bsz, width // LANES, nb, nb),
        in_specs=[pl.BlockSpec((1, tq, LANES), lambda b, h, i, j: (b, i, h)),
                  pl.BlockSpec((1, nbp, LANES), lambda b, h, i, j: (b, 0, h)),
                  pl.BlockSpec((1, MOBA_BLOCK, LANES), kv_map),
                  pl.BlockSpec((1, MOBA_BLOCK, LANES), kv_map)],
        out_specs=pl.BlockSpec((1, tq, LANES), lambda b, h, i, j: (b, i, h)),
        out_shape=jax.ShapeDtypeStruct((bsz, seq, width), F32),
        scratch_shapes=[pltpu.VMEM((2, tq, nbp), F32),
                        pltpu.VMEM((2, tq, 1), F32), pltpu.VMEM((2, tq, 1), F32),
                        pltpu.VMEM((2, tq, LANES), F32)],
        compiler_params=_cparams(("parallel", "parallel", "parallel", "arbitrary")),
        name="moba_attn",
    )(q_f32, km, k, v)


def s5_operators(lam_re, lam_im, b_re, b_im, c_re, c_im, d_skip, log_dt, chunk):
    n_g, n_p = lam_re.shape
    n_c = d_skip.shape[-1]
    dt = jnp.exp(log_dt.astype(F32))[:, None]
    den = lam_re * lam_re + lam_im * lam_im
    mag = jnp.exp(lam_re * dt)
    abar_re = mag * jnp.cos(lam_im * dt)
    abar_im = mag * jnp.sin(lam_im * dt)
    coef_re = ((abar_re - 1.0) * lam_re + abar_im * lam_im) / den
    coef_im = (abar_im * lam_re - (abar_re - 1.0) * lam_im) / den
    bbar_re = coef_re[..., None] * b_re - coef_im[..., None] * b_im
    bbar_im = coef_re[..., None] * b_im + coef_im[..., None] * b_re
    tau = jnp.arange(chunk + 1, dtype=F32)[:, None, None]
    pmag = jnp.exp(lam_re * dt * tau)
    pw_re = pmag * jnp.cos(lam_im * dt * tau)
    pw_im = pmag * jnp.sin(lam_im * dt * tau)
    ab_re = pw_re[..., None] * bbar_re - pw_im[..., None] * bbar_im
    ab_im = pw_re[..., None] * bbar_im + pw_im[..., None] * bbar_re
    kern = (jnp.einsum('gop,tgpi->tgio', c_re, ab_re, precision=HI)
            - jnp.einsum('gop,tgpi->tgio', c_im, ab_im, precision=HI))
    t_idx = jnp.arange(chunk)
    lag = t_idx[None, :] - t_idx[:, None]
    toep = jnp.where((lag >= 0)[None, :, None, :, None],
                     kern[jnp.clip(lag, 0, chunk)].transpose(2, 0, 3, 1, 4), 0.0)
    eye = (jnp.eye(chunk)[None, :, None, :, None] * jnp.eye(n_c)[None, None, :, None, :])
    toep = toep + eye * d_skip[:, None, :, None, None]
    toep = toep.reshape(n_g, chunk * n_c, chunk * n_c)
    rev = chunk - 1 - t_idx
    w_in = jnp.concatenate([ab_re[rev].transpose(1, 0, 3, 2), ab_im[rev].transpose(1, 0, 3, 2)], axis=-1)
    w_in = w_in.reshape(n_g, chunk * n_c, 2 * n_p)
    a_re, a_im = pw_re[1:chunk + 1], pw_im[1:chunk + 1]
    st_re = (c_re[None] * a_re[:, :, None, :] - c_im[None] * a_im[:, :, None, :])
    st_im = (-c_re[None] * a_im[:, :, None, :] - c_im[None] * a_re[:, :, None, :])
    w_st = jnp.concatenate([st_re.transpose(1, 3, 0, 2), st_im.transpose(1, 3, 0, 2)], axis=1)
    w_st = w_st.reshape(n_g, 2 * n_p, chunk * n_c)
    return toep, w_in, w_st, pw_re[chunk], pw_im[chunk]


def _s5_in_kernel(u_ref, w_ref, o_ref):
    o_ref[0, 0] = jnp.dot(u_ref[0, 0], w_ref[0], precision=HI, preferred_element_type=F32)


def _s5_scan_kernel(x_ref, aa_ref, ab_ref, abs_ref, o_ref, h_sc, hs_sc, *, steps, n_p):
    @pl.when(pl.program_id(0) == 0)
    def _():
        h_sc[...] = jnp.zeros(h_sc.shape, F32)
        hs_sc[...] = jnp.zeros(hs_sc.shape, F32)

    aa, ab, abs_ = aa_ref[...], ab_ref[...], abs_ref[...]

    def body(t, carry):
        h, hs = carry
        o_ref[t] = h
        x = x_ref[t]
        xs = pltpu.roll(x, n_p, 1)
        return aa * h + ab * hs + x, aa * hs + abs_ * h + xs

    h, hs = lax.fori_loop(0, steps, body, (h_sc[...], hs_sc[...]))
    h_sc[...] = h
    hs_sc[...] = hs


def _s5_out_kernel(u_ref, h0_ref, t_ref, w_ref, o_ref):
    y = (jnp.dot(u_ref[0, 0], t_ref[0], precision=HI, preferred_element_type=F32)
         + jnp.dot(h0_ref[0, 0], w_ref[0], precision=HI, preferred_element_type=F32))
    o_ref[0, 0] = jax.nn.gelu(y)


def s5_layer(u, ops):
    toep, w_in, w_st, ac_re, ac_im = ops
    bsz, seq, width = u.shape
    n_g, n_p = ac_re.shape
    n_c = width // n_g
    chunk = S5_CHUNK
    nj = seq // chunk
    row = chunk * n_c
    ur = u.reshape(bsz, nj, chunk, n_g, n_c).transpose(0, 3, 1, 2, 4).reshape(bsz, n_g, nj, row)
    hloc = pl.pallas_call(
        _s5_in_kernel,
        grid=(bsz, n_g),
        in_specs=[pl.BlockSpec((1, 1, nj, row), lambda b, g: (b, g, 0, 0)),
                  pl.BlockSpec((1, row, 2 * n_p), lambda b, g: (g, 0, 0))],
        out_specs=pl.BlockSpec((1, 1, nj, 2 * n_p), lambda b, g: (b, g, 0, 0)),
        out_shape=jax.ShapeDtypeStruct((bsz, n_g, nj, 2 * n_p), F32),
        compiler_params=_cparams(("parallel", "parallel")),
        name="s5_in",
    )(ur, w_in)
    bg = bsz * n_g
    xs = hloc.transpose(2, 0, 1, 3).reshape(nj, bg, 2 * n_p)
    rep = lambda a: jnp.tile(a, (bsz, 1))
    aa = rep(jnp.concatenate([ac_re, ac_re], axis=-1))
    ab = rep(jnp.concatenate([-ac_im, ac_im], axis=-1))
    abs_ = rep(jnp.concatenate([ac_im, -ac_im], axis=-1))
    steps = _pick_tile(nj, (128, 64, 32, 16, 8))
    cst = pl.BlockSpec((bg, 2 * n_p), lambda t: (0, 0))
    h0 = pl.pallas_call(
        functools.partial(_s5_scan_kernel, steps=steps, n_p=n_p),
        grid=(nj // steps,),
        in_specs=[pl.BlockSpec((steps, bg, 2 * n_p), lambda t: (t, 0, 0)), cst, cst, cst],
        out_specs=pl.BlockSpec((steps, bg, 2 * n_p), lambda t: (t, 0, 0)),
        out_shape=jax.ShapeDtypeStruct((nj, bg, 2 * n_p), F32),
        scratch_shapes=[pltpu.VMEM((bg, 2 * n_p), F32), pltpu.VMEM((bg, 2 * n_p), F32)],
        compiler_params=_cparams(("arbitrary",)),
        name="s5_scan",
    )(xs, aa, ab, abs_)
    h0 = h0.reshape(nj, bsz, n_g, 2 * n_p).transpose(1, 2, 0, 3)
    y = pl.pallas_call(
        _s5_out_kernel,
        grid=(bsz, n_g),
        in_specs=[pl.BlockSpec((1, 1, nj, row), lambda b, g: (b, g, 0, 0)),
                  pl.BlockSpec((1, 1, nj, 2 * n_p), lambda b, g: (b, g, 0, 0)),
                  pl.BlockSpec((1, row, row), lambda b, g: (g, 0, 0)),
                  pl.BlockSpec((1, 2 * n_p, row), lambda b, g: (g, 0, 0))],
        out_specs=pl.BlockSpec((1, 1, nj, row), lambda b, g: (b, g, 0, 0)),
        out_shape=jax.ShapeDtypeStruct((bsz, n_g, nj, row), F32),
        compiler_params=_cparams(("parallel", "parallel")),
        name="s5_out",
    )(ur, h0, toep, w_st)
    return y.reshape(bsz, n_g, nj, chunk, n_c).transpose(0, 2, 3, 1, 4).reshape(bsz, seq, width)


def _glu_kernel(y_ref, w_ref, b_ref, o_ref):
    y = y_ref[...]
    gate = jnp.dot(y.astype(BF16), w_ref[...], preferred_element_type=F32) + b_ref[...]
    o_ref[...] = (y * jax.nn.sigmoid(gate)).astype(o_ref.dtype)


def glu(y, w_bf16, b, tm=1024):
    n_tok, width = y.shape
    tm = _pick_tile(n_tok, (tm, 512, 256, 128))
    return pl.pallas_call(
        _glu_kernel,
        grid=(n_tok // tm,),
        in_specs=[pl.BlockSpec((tm, width), lambda i: (i, 0)),
                  pl.BlockSpec((width, width), lambda i: (0, 0)),
                  pl.BlockSpec((1, width), lambda i: (0, 0))],
        out_specs=pl.BlockSpec((tm, width), lambda i: (i, 0)),
        out_shape=jax.ShapeDtypeStruct((n_tok, width), F32),
        compiler_params=_cparams(("parallel",)),
        name="s5_glu",
    )(y, w_bf16, b.astype(F32).reshape(1, width))


def _outproj_kernel(a_ref, b_ref, w_ref, x_ref, g_ref, o_ref, *, half):
    y = (jnp.dot(a_ref[0].astype(BF16), w_ref[:half, :], preferred_element_type=F32)
         + jnp.dot(b_ref[0].astype(BF16), w_ref[half:, :], preferred_element_type=F32))
    o_ref[0] = x_ref[0] + g_ref[0] * y


def out_proj(a, b, w_bf16, x, gate, tm=512):
    bsz, seq, d = x.shape
    half = a.shape[-1]
    tm = _pick_tile(seq, (tm, 256, 128))
    return pl.pallas_call(
        functools.partial(_outproj_kernel, half=half),
        grid=(bsz, seq // tm),
        in_specs=[pl.BlockSpec((1, tm, half), lambda bb, i: (bb, i, 0)),
                  pl.BlockSpec((1, tm, half), lambda bb, i: (bb, i, 0)),
                  pl.BlockSpec((2 * half, d), lambda bb, i: (0, 0)),
                  pl.BlockSpec((1, tm, d), lambda bb, i: (bb, i, 0)),
                  pl.BlockSpec((1, 1, d), lambda bb, i: (bb, 0, 0))],
        out_specs=pl.BlockSpec((1, tm, d), lambda bb, i: (bb, i, 0)),
        out_shape=jax.ShapeDtypeStruct((bsz, seq, d), F32),
        compiler_params=_cparams(("parallel", "parallel")),
        name="out_proj",
    )(a, b, w_bf16, x, gate)


def _router_kernel(x_ref, sc_ref, sh_ref, wr_ref, br_ref, h_ref, r_ref):
    h = _norm_mod(x_ref[0], sc_ref[0], sh_ref[0])
    h_ref[0] = h.astype(BF16)
    logits = jnp.dot(h, wr_ref[...], precision=HI, preferred_element_type=F32) + br_ref[...]
    tm = logits.shape[0]
    lane = lax.broadcasted_iota(I32, (tm, LANES), 1)
    big = 4 * LANES
    coarse = jnp.where(lane < N_GROUPS, logits, -jnp.inf)
    cmax = jnp.max(coarse, axis=1, keepdims=True)
    grp = jnp.min(jnp.where(coarse == cmax, lane, big), axis=1, keepdims=True)
    p_grp = 1.0 / jnp.sum(jnp.exp(coarse - cmax), axis=1, keepdims=True)
    lo = N_GROUPS + grp * EXPERTS_PER_GROUP
    fine = jnp.where(jnp.logical_and(lane >= lo, lane < lo + EXPERTS_PER_GROUP), logits, -jnp.inf)
    v1 = jnp.max(fine, axis=1, keepdims=True)
    i1 = jnp.min(jnp.where(fine == v1, lane, big), axis=1, keepdims=True)
    fine2 = jnp.where(lane == i1, -jnp.inf, fine)
    v2 = jnp.max(fine2, axis=1, keepdims=True)
    i2 = jnp.min(jnp.where(fine2 == v2, lane, big), axis=1, keepdims=True)
    e2 = jnp.exp(v2 - v1)
    wa = p_grp / (1.0 + e2)
    wb = p_grp * e2 / (1.0 + e2)
    ida = (i1 - N_GROUPS).astype(F32)
    idb = (i2 - N_GROUPS).astype(F32)
    r_ref[0] = jnp.where(lane == 0, ida, jnp.where(lane == 1, idb,
                         jnp.where(lane == 2, wa, jnp.where(lane == 3, wb, 0.0))))


def moe_router(x, sc, sh, wg, bg, we, be, tm=512):
    bsz, seq, d = x.shape
    tm = _pick_tile(seq, (tm, 256, 128))
    n_r = N_GROUPS + N_EXPERTS
    wr = jnp.zeros((d, LANES), F32).at[:, :N_GROUPS].set(wg).at[:, N_GROUPS:n_r].set(we)
    br = jnp.zeros((1, LANES), F32).at[0, :N_GROUPS].set(bg).at[0, N_GROUPS:n_r].set(be)
    return pl.pallas_call(
        _router_kernel,
        grid=(bsz, seq // tm),
        in_specs=[pl.BlockSpec((1, tm, d), lambda b, i: (b, i, 0)),
                  pl.BlockSpec((1, 1, d), lambda b, i: (b, 0, 0)),
                  pl.BlockSpec((1, 1, d), lambda b, i: (b, 0, 0)),
                  pl.BlockSpec((d, LANES), lambda b, i: (0, 0)),
                  pl.BlockSpec((1, LANES), lambda b, i: (0, 0))],
        out_specs=[pl.BlockSpec((1, tm, d), lambda b, i: (b, i, 0)),
                   pl.BlockSpec((1, tm, LANES), lambda b, i: (b, i, 0))],
        out_shape=[jax.ShapeDtypeStruct((bsz, seq, d), BF16),
                   jax.ShapeDtypeStruct((bsz, seq, LANES), F32)],
        compiler_params=_cparams(("parallel", "parallel")),
        name="moe_router",
    )(x, sc, sh, wr, br)


MOE_SEG = 256


def _moe_kernel(erow_ref, route_ref, h_ref, x_ref, g_ref, w1_ref, w3_ref, w2_ref, o_ref,
                rrow_sc, rcol_sc, *, chunk, rows):
    e = pl.program_id(1)
    n_seg = chunk // MOE_SEG

    @pl.when(e == 0)
    def _():
        o_ref[...] = jnp.zeros(o_ref.shape, F32)
        si = lax.broadcasted_iota(I32, (MOE_SEG, MOE_SEG), 0)
        sj = lax.broadcasted_iota(I32, (MOE_SEG, MOE_SEG), 1)
        upper = jnp.where(si < sj, 1.0, 0.0).astype(BF16)
        lower = jnp.where(sj < si, 1.0, 0.0).astype(BF16)
        erow = erow_ref[0]
        eid = lax.broadcasted_iota(I32, (N_EXPERTS, chunk), 0)
        member = jnp.where(jnp.logical_or(erow[0:1] == eid, erow[1:2] == eid), 1.0, 0.0)
        base = jnp.zeros((N_EXPERTS, 1), F32)
        parts = []
        for sgi in range(n_seg):
            seg = member[:, sgi * MOE_SEG:(sgi + 1) * MOE_SEG]
            parts.append(jnp.dot(seg.astype(BF16), upper, preferred_element_type=F32) + base)
            base = base + jnp.sum(seg, axis=1, keepdims=True)
        rank = jnp.where(member > 0.0, jnp.concatenate(parts, axis=1), -1.0)
        for ee in range(N_EXPERTS):
            rrow_sc[ee] = rank[ee:ee + 1]
        route = route_ref[...]
        lane = lax.broadcasted_iota(I32, (chunk, LANES), 1).astype(F32)
        member_c = jnp.where(jnp.logical_or(route[:, 0:1] == lane, route[:, 1:2] == lane), 1.0, 0.0)
        base_c = jnp.zeros((1, LANES), F32)
        for sgi in range(n_seg):
            seg = member_c[sgi * MOE_SEG:(sgi + 1) * MOE_SEG]
            rk = jnp.dot(lower, seg.astype(BF16), preferred_element_type=F32) + base_c
            rcol_sc[sgi * MOE_SEG:(sgi + 1) * MOE_SEG, :] = jnp.where(seg > 0.0, rk, -1.0)
            base_c = base_c + jnp.sum(seg, axis=0, keepdims=True)

    rrow = rrow_sc[e]
    ef = e.astype(F32)
    lane = lax.broadcasted_iota(I32, (chunk, LANES), 1)
    rcol = jnp.sum(jnp.where(lane == e, rcol_sc[...], 0.0), axis=1, keepdims=True)
    route = route_ref[...]
    wcol = (jnp.where(route[:, 0:1] == ef, route[:, 2:3], 0.0)
            + jnp.where(route[:, 1:2] == ef, route[:, 3:4], 0.0))
    count = (jnp.max(rrow) + 1.0).astype(I32)
    n_blk = (count + rows - 1) // rows

    def block(bi, carry):
        base = (bi * rows).astype(F32)
        ridx = lax.broadcasted_iota(I32, (rows, 1), 0).astype(F32) + base
        gather = jnp.where(rrow == ridx, 1.0, 0.0).astype(BF16)
        xb = jnp.dot(gather, h_ref[...], preferred_element_type=F32).astype(BF16)
        a = jnp.dot(xb, w1_ref[0], preferred_element_type=F32)
        hid = (a * jax.nn.sigmoid(a)) * jnp.dot(xb, w3_ref[0], preferred_element_type=F32)
        yb = jnp.dot(hid.astype(BF16), w2_ref[0], preferred_element_type=F32)
        cidx = lax.broadcasted_iota(I32, (1, rows), 1).astype(F32) + base
        scatter = jnp.where(rcol == cidx, 1.0, 0.0).astype(BF16)
        o_ref[...] += wcol * jnp.dot(scatter, yb.astype(BF16), preferred_element_type=F32)
        return carry

    lax.fori_loop(0, n_blk, block, 0)

    @pl.when(e == N_EXPERTS - 1)
    def _():
        o_ref[...] = x_ref[...] + g_ref[0] * o_ref[...]


def moe_experts(h, route, x, gate, w1_bf16, w3_bf16, w2_bf16, chunk=1024, rows=128):
    bsz, seq, d = x.shape
    chunk = _pick_tile(seq, (chunk, 512, 256))
    n_tok = bsz * seq
    n_chunks = n_tok // chunk
    per_b = seq // chunk
    ff = w1_bf16.shape[-1]
    route2 = route.reshape(n_tok, LANES)
    erow = route2[:, :2].astype(I32).T.reshape(2, n_chunks, chunk).transpose(1, 0, 2)
    out = pl.pallas_call(
        functools.partial(_moe_kernel, chunk=chunk, rows=rows),
        grid=(n_chunks, N_EXPERTS),
        in_specs=[pl.BlockSpec((1, 2, chunk), lambda c, e: (c, 0, 0)),
                  pl.BlockSpec((chunk, LANES), lambda c, e: (c, 0)),
                  pl.BlockSpec((chunk, d), lambda c, e: (c, 0)),
                  pl.BlockSpec((chunk, d), lambda c, e: (c, 0)),
                  pl.BlockSpec((1, 1, d), lambda c, e: (c // per_b, 0, 0)),
                  pl.BlockSpec((1, d, ff), lambda c, e: (e, 0, 0)),
                  pl.BlockSpec((1, d, ff), lambda c, e: (e, 0, 0)),
                  pl.BlockSpec((1, ff, d), lambda c, e: (e, 0, 0))],
        out_specs=pl.BlockSpec((chunk, d), lambda c, e: (c, 0)),
        out_shape=jax.ShapeDtypeStruct((n_tok, d), F32),
        scratch_shapes=[pltpu.VMEM((N_EXPERTS, 1, chunk), F32), pltpu.VMEM((chunk, LANES), F32)],
        compiler_params=_cparams(("parallel", "arbitrary")),
        name="moe_experts",
    )(erow, route2, h.reshape(n_tok, d), x.reshape(n_tok, d), gate, w1_bf16, w3_bf16, w2_bf16)
    return out.reshape(bsz, seq, d)


def kernel(x, c, ada_w, ada_b, moe_wg, moe_bg, moe_we, moe_be, moe_w1, moe_w3, moe_w2,
           ev_w_in, ev_w_out, a_qn, a_kn, a_lq1, a_lk1, a_lq2, a_lk2, a_subln,
           b_qn, b_kn, b_idx_kn, od_w_in, od_w_out, c_qn, c_kn,
           s5_lam_re, s5_lam_im, s5_b_re, s5_b_im, s5_c_re, s5_c_im, s5_d, s5_log_dt,
           s5_glu_w, s5_glu_b):
    bsz, seq, d = x.shape
    depth = ada_w.shape[0]
    mod = ada_mod(c, ada_w, ada_b)
    rope_h = rope_lane_tables(seq, HEAD_DIM, ROT_HEAD)
    rope_i = rope_lane_tables(seq, IDX_DIM, ROT_IDX)
    qk_scale = HEAD_DIM ** -0.5
    half_h, half_i = ROT_HEAD // 2, ROT_IDX // 2
    hw = A_HEADS * 2 * HEAD_DIM

    def pad_cols(w):
        n = w.shape[1]
        npad = -(-n // LANES) * LANES
        return jnp.zeros((w.shape[0], npad), F32).at[:, :n].set(w).astype(BF16)

    for l in range(depth):
        sh1, sc1, g1, sh2, sc2, g2 = [m[:, None, :] for m in jnp.split(mod[l], 6, axis=-1)]
        i = l // 2
        if l % 2 == 0:
            lam_init = 0.8 - 0.6 * math.exp(-0.3 * l)
            z = in_proj(x, sc1, sh1, pad_cols(ev_w_in[i]))
            (aq,) = prep(z, 0, hw, hd=HEAD_DIM, gain=a_qn[i], rope=rope_h, half=half_h, scale=qk_scale)
            (ak,) = prep(z, hw, hw, hd=HEAD_DIM, gain=a_kn[i], rope=rope_h, half=half_h)
            (av,) = prep(z, 2 * hw, hw, hd=HEAD_DIM)
            o_a = diff_attention(aq, ak, av, a_lq1[i], a_lk1[i], a_lq2[i], a_lk2[i], a_subln[i], lam_init)
            (bq,) = prep(z, 3 * hw, hw, hd=HEAD_DIM, gain=b_qn[i], rope=rope_h, half=half_h, scale=qk_scale)
            (bk,) = prep(z, 4 * hw, hw, hd=HEAD_DIM, gain=b_kn[i], rope=rope_h, half=half_h)
            (bv,) = prep(z, 5 * hw, hw, hd=HEAD_DIM)
            iq_off = 6 * hw
            ik_off = iq_off + IDX_HEADS * IDX_DIM
            (iq,) = prep(z, iq_off, IDX_HEADS * IDX_DIM, hd=IDX_DIM, rope=rope_i, half=half_i, out_dtypes=(F32,))
            (ik,) = prep(z, ik_off, LANES, hd=IDX_DIM, gain=b_idx_kn[i], rope=rope_i, half=half_i,
                         out_dtypes=(F32,))
            kit = ik[:, :, :IDX_DIM].transpose(0, 2, 1)
            o_b = dsa_attention(bq, bk, bv, iq, kit, z, ik_off + IDX_DIM)
            x = out_proj(o_a, o_b, ev_w_out[i].astype(BF16), x, g1)
        else:
            z = in_proj(x, sc1, sh1, pad_cols(od_w_in[i]))
            (cq,) = prep(z, 0, hw, hd=HEAD_DIM, gain=c_qn[i], rope=rope_h, half=half_h, scale=qk_scale,
                         out_dtypes=(F32,))
            ck32, ck = prep(z, hw, hw, hd=HEAD_DIM, gain=c_kn[i], rope=rope_h, half=half_h,
                            out_dtypes=(F32, BF16))
            (cv,) = prep(z, 2 * hw, hw, hd=HEAD_DIM)
            o_c = moba_attention(cq, block_means(ck32), ck, cv)
            ops = s5_operators(s5_lam_re[i], s5_lam_im[i], s5_b_re[i], s5_b_im[i], s5_c_re[i], s5_c_im[i],
                               s5_d[i], s5_log_dt[i], S5_CHUNK)
            y = s5_layer(z[:, :, 3 * hw:4 * hw], ops)
            y = glu(y.reshape(bsz * seq, hw), s5_glu_w[i].astype(BF16), s5_glu_b[i]).reshape(bsz, seq, hw)
            x = out_proj(o_c, y, od_w_out[i].astype(BF16), x, g1)
        h, route = moe_router(x, sc2, sh2, moe_wg[l], moe_bg[l], moe_we[l], moe_be[l])
        x = moe_experts(h, route, x, g2, moe_w1[l].astype(BF16), moe_w3[l].astype(BF16), moe_w2[l].astype(BF16))
    return x
```

```python
import functools
import math

import jax
import jax.numpy as jnp
from jax import lax
from jax.experimental import pallas as pl
from jax.experimental.pallas import tpu as pltpu

F32 = jnp.float32
BF16 = jnp.bfloat16
I32 = jnp.int32
HI = lax.Precision.HIGHEST

LANES = 128
NEG = -1e30
I32_MIN = -2147483648

HEAD_DIM = 64
A_HEADS = 4
B_HEADS = 8
IDX_HEADS = 8
IDX_DIM = 32
DSA_TOPK = 256
C_HEADS = 8
MOBA_BLOCK = 256
MOBA_TOPK = 3
S5_GROUP_CH = 16
S5_GROUPS = 32
S5_STATE = 64
ROPE_THETA = 500000.0
ROT_HEAD = HEAD_DIM // 4
ROT_IDX = IDX_DIM // 4
N_GROUPS = 4
EXPERTS_PER_GROUP = 8
N_EXPERTS = N_GROUPS * EXPERTS_PER_GROUP
EXPERT_FF = 512
NORM_EPS = 1e-6
S5_CHUNK = 16


def _cparams(sem, vmem_mb=48):
    return pltpu.CompilerParams(dimension_semantics=sem, vmem_limit_bytes=vmem_mb * 1024 * 1024)


def _nt_dot(a, b, precision=None):
    return lax.dot_general(a, b, (((1,), (1,)), ((), ())), precision=precision,
                           preferred_element_type=F32)


def _mod_kernel(c_ref, w_ref, b_ref, o_ref):
    c = c_ref[...]
    ca = c * jax.nn.sigmoid(c)
    o_ref[0] = jnp.dot(ca, w_ref[0], precision=HI, preferred_element_type=F32) + b_ref[0]


def ada_mod(c, ada_w, ada_b):
    depth, d, n6 = ada_w.shape
    bsz = c.shape[0]
    rows = 8
    c_pad = jnp.zeros((rows, d), F32).at[:bsz].set(c)
    out = pl.pallas_call(
        _mod_kernel,
        grid=(depth, n6 // d),
        in_specs=[pl.BlockSpec((rows, d), lambda l, n: (0, 0)),
                  pl.BlockSpec((1, d, d), lambda l, n: (l, 0, n)),
                  pl.BlockSpec((1, 1, d), lambda l, n: (l, 0, n))],
        out_specs=pl.BlockSpec((1, rows, d), lambda l, n: (l, 0, n)),
        out_shape=jax.ShapeDtypeStruct((depth, rows, n6), F32),
        compiler_params=_cparams(("parallel", "parallel")),
        name="ada_mod",
    )(c_pad, ada_w, ada_b.reshape(depth, 1, n6))
    return out[:, :bsz]


def _norm_mod(x, sc, sh):
    y = x * lax.rsqrt(jnp.mean(x * x, axis=-1, keepdims=True) + NORM_EPS)
    return y * (1.0 + sc) + sh


def _inproj_kernel(x_ref, sc_ref, sh_ref, w_ref, o_ref, h_sc):
    @pl.when(pl.program_id(2) == 0)
    def _():
        h_sc[...] = _norm_mod(x_ref[0], sc_ref[0], sh_ref[0]).astype(BF16)

    o_ref[0] = jnp.dot(h_sc[...], w_ref[...], preferred_element_type=F32)


def _pick_tile(n, prefs):
    for p in prefs:
        if n % p == 0:
            return p
    return n


def in_proj(x, sc, sh, w_bf16, tm=512):
    bsz, seq, d = x.shape
    n = w_bf16.shape[1]
    tm = _pick_tile(seq, (tm, 256, 128))
    tn = _pick_tile(n, (1152, 1024, 512, 384, 256, 128))
    return pl.pallas_call(
        _inproj_kernel,
        grid=(bsz, seq // tm, n // tn),
        in_specs=[pl.BlockSpec((1, tm, d), lambda b, i, j: (b, i, 0)),
                  pl.BlockSpec((1, 1, d), lambda b, i, j: (b, 0, 0)),
                  pl.BlockSpec((1, 1, d), lambda b, i, j: (b, 0, 0)),
                  pl.BlockSpec((d, tn), lambda b, i, j: (0, j))],
        out_specs=pl.BlockSpec((1, tm, tn), lambda b, i, j: (b, i, j)),
        out_shape=jax.ShapeDtypeStruct((bsz, seq, n), F32),
        scratch_shapes=[pltpu.VMEM((tm, d), BF16)],
        compiler_params=_cparams(("parallel", "parallel", "arbitrary")),
        name="in_proj",
    )(x, sc, sh, w_bf16)


def rope_lane_tables(seq, hd, rot):
    half = rot // 2
    pos = jnp.arange(seq, dtype=F32)
    inv = ROPE_THETA ** (-(jnp.arange(0, rot, 2, dtype=F32) / rot))
    ang = pos[:, None] * inv[None, :]
    cos, sin = jnp.cos(ang), jnp.sin(ang)
    dl = jnp.arange(LANES) % hd
    first = dl < half
    second = (dl >= half) & (dl < rot)
    idx = jnp.where(first, dl, jnp.where(second, dl - half, 0))
    cos_l = jnp.where(first | second, cos[:, idx], 1.0)
    sa_l = jnp.where(second, sin[:, idx], 0.0)
    sb_l = jnp.where(first, -sin[:, idx], 0.0)
    return cos_l.astype(F32), sa_l.astype(F32), sb_l.astype(F32)


def _prep_kernel(*refs, hd, half, do_norm, do_rope, scale, n_out):
    z_ref, g_ref, gm_ref, cos_ref, sa_ref, sb_ref = refs[:6]
    o_refs = refs[6:6 + n_out]
    x = z_ref[0]
    width = x.shape[-1]
    if do_norm:
        sq = x * x
        hi = sq.astype(BF16)
        lo = (sq - hi.astype(F32)).astype(BF16)
        gm = gm_ref[...]
        ms = (jnp.dot(hi, gm, preferred_element_type=F32)
              + jnp.dot(lo, gm, preferred_element_type=F32)) * (1.0 / hd)
        x = x * lax.rsqrt(ms + NORM_EPS) * g_ref[...]
    if do_rope:
        cos, sa, sb = cos_ref[...], sa_ref[...], sb_ref[...]
        cols = []
        for r in range(width // LANES):
            xc = x[:, r * LANES:(r + 1) * LANES]
            cols.append(xc * cos + pltpu.roll(xc, half, 1) * sa + pltpu.roll(xc, LANES - half, 1) * sb)
        x = cols[0] if len(cols) == 1 else jnp.concatenate(cols, axis=-1)
    if scale != 1.0:
        x = x * scale
    for o in o_refs:
        o[0] = x.astype(o.dtype)


def prep(z, col_off, width, *, hd, gain=None, rope=None, half=0, scale=1.0, out_dtypes=(BF16,), tm=512):
    bsz, seq, _ = z.shape
    tm = _pick_tile(seq, (tm, 256, 128))
    cb = col_off // width
    assert col_off % width == 0 and width % LANES == 0
    do_norm = gain is not None
    do_rope = rope is not None
    g = jnp.tile(gain.astype(F32), width // hd).reshape(1, width) if do_norm else jnp.ones((1, width), F32)
    gi = jnp.arange(width) // hd
    gm = (gi[:, None] == gi[None, :]).astype(BF16)
    if do_rope:
        cos_l, sa_l, sb_l = rope
    else:
        cos_l = sa_l = sb_l = jnp.zeros((seq, LANES), F32)
    outs = pl.pallas_call(
        functools.partial(_prep_kernel, hd=hd, half=half, do_norm=do_norm, do_rope=do_rope,
                          scale=scale, n_out=len(out_dtypes)),
        grid=(bsz, seq // tm),
        in_specs=[pl.BlockSpec((1, tm, width), lambda b, i: (b, i, cb)),
                  pl.BlockSpec((1, width), lambda b, i: (0, 0)),
                  pl.BlockSpec((width, width), lambda b, i: (0, 0)),
                  pl.BlockSpec((tm, LANES), lambda b, i: (i, 0)),
                  pl.BlockSpec((tm, LANES), lambda b, i: (i, 0)),
                  pl.BlockSpec((tm, LANES), lambda b, i: (i, 0))],
        out_specs=[pl.BlockSpec((1, tm, width), lambda b, i: (b, i, 0)) for _ in out_dtypes],
        out_shape=[jax.ShapeDtypeStruct((bsz, seq, width), dt) for dt in out_dtypes],
        compiler_params=_cparams(("parallel", "parallel")),
        name="prep",
    )(z, g, gm, cos_l, sa_l, sb_l)
    return outs


ATT_TILE = 256


ONES_ROWS = 16
MAX_FAST_LOGIT = 60.0


def _attn_update(s_t, vt_aug, m_ref, acc_ref, idx, fast):
    if fast:
        acc_ref[idx] += jnp.dot(vt_aug, jnp.exp2(s_t).astype(BF16), preferred_element_type=F32)
    else:
        m_prev = m_ref[idx]
        m_new = jnp.maximum(m_prev, jnp.max(s_t, axis=0, keepdims=True))
        p_t = jnp.exp2(s_t - m_new).astype(BF16)
        acc_ref[idx] = (jnp.exp2(m_prev - m_new) * acc_ref[idx]
                        + jnp.dot(vt_aug, p_t, preferred_element_type=F32))
        m_ref[idx] = m_new


def _attn_init(m_sc, acc_sc):
    m_sc[...] = jnp.full(m_sc.shape, NEG, F32)
    acc_sc[...] = jnp.zeros(acc_sc.shape, F32)


def _attn_out(acc_ref, idx, dv):
    acc = acc_ref[idx]
    return acc[:dv] / acc[dv:dv + 1]


def _vt_aug(v, dv):
    bsz, seq, width = v.shape
    n_h = width // dv
    vt = v.astype(BF16).transpose(0, 2, 1).reshape(bsz, n_h, dv, seq)
    ones = jnp.ones((bsz, n_h, ONES_ROWS, seq), BF16)
    return jnp.concatenate([vt, ones], axis=2).reshape(bsz, n_h * (dv + ONES_ROWS), seq)


def _logit_bound(gain_q, gain_k):
    return (HEAD_DIM ** 0.5 * math.log2(math.e) * 1.02) * jnp.max(jnp.abs(gain_q)) * jnp.max(jnp.abs(gain_k))


def _half_rows(x, sub):
    row = lax.broadcasted_iota(I32, (LANES, 1), 0)
    return jnp.where((row >= sub * HEAD_DIM) & (row < (sub + 1) * HEAD_DIM), x, jnp.zeros_like(x))


def _causal_t(t):
    return lax.broadcasted_iota(I32, (t, t), 0) <= lax.broadcasted_iota(I32, (t, t), 1)


def _to_t(a, dtype=None):
    a = a if dtype is None else a.astype(dtype)
    return a.transpose(0, 2, 1)


def _diff_attn_kernel(qt_ref, k_ref, vt_ref, lq1_ref, lk1_ref, lq2_ref, lk2_ref, g_ref, o_ref,
                      m_sc, acc_sc, *, t, lam_init, fast):
    i = pl.program_id(2)
    _attn_init(m_sc, acc_sc)
    qt = qt_ref[0]
    qc = [_half_rows(qt, c) for c in range(2)]

    def attend(off, mask_fn):
        k = k_ref[0, pl.ds(off, t), :]
        vt = vt_ref[0, :, pl.ds(off, t)]
        s = [mask_fn(jnp.dot(k, qc[c], preferred_element_type=F32)) for c in range(2)]
        for c in range(2):
            _attn_update(s[c], vt, m_sc, acc_sc, c, fast)

    def past_tile(jt, carry):
        attend(pl.multiple_of(jt * t, t), lambda s_t: s_t)
        return carry

    lax.fori_loop(0, i, past_tile, 0)
    causal = _causal_t(t)
    attend(pl.multiple_of(i * t, t), lambda s_t: jnp.where(causal, s_t, NEG))

    lam = (jnp.exp(jnp.sum(lq1_ref[...] * lk1_ref[...], axis=-1, keepdims=True))
           - jnp.exp(jnp.sum(lq2_ref[...] * lk2_ref[...], axis=-1, keepdims=True)) + lam_init)
    o = _attn_out(acc_sc, 0, LANES) - lam * _attn_out(acc_sc, 1, LANES)
    o = o * lax.rsqrt(jnp.mean(o * o, axis=0, keepdims=True) + NORM_EPS)
    o_ref[0] = o * g_ref[...] * (1.0 - lam_init)


DIFF_TILE = 512


def diff_attention(q, k, v, lq1, lk1, lq2, lk2, subln, lam_init, fast_ok):
    bsz, seq, width = q.shape
    n_h = width // LANES
    t = _pick_tile(seq, (DIFF_TILE, 256, 128))
    rows = LANES + ONES_ROWS
    vec = lambda a: a.astype(F32).reshape(1, -1)
    small = pl.BlockSpec((1, HEAD_DIM), lambda b, h, i: (0, 0))

    def call(fast, *args):
        return pl.pallas_call(
            functools.partial(_diff_attn_kernel, t=t, lam_init=lam_init, fast=fast),
            grid=(bsz, n_h, seq // t),
            in_specs=[pl.BlockSpec((1, LANES, t), lambda b, h, i: (b, h, i)),
                      pl.BlockSpec((1, seq, LANES), lambda b, h, i: (b, 0, h)),
                      pl.BlockSpec((1, rows, seq), lambda b, h, i: (b, h, 0)),
                      small, small, small, small,
                      pl.BlockSpec((LANES, 1), lambda b, h, i: (0, 0))],
            out_specs=pl.BlockSpec((1, LANES, t), lambda b, h, i: (b, h, i)),
            out_shape=jax.ShapeDtypeStruct((bsz, width, seq), F32),
            scratch_shapes=[pltpu.VMEM((2, 1, t), F32), pltpu.VMEM((2, rows, t), F32)],
            compiler_params=_cparams(("parallel", "parallel", "arbitrary")),
            name="diff_attn_fast" if fast else "diff_attn_safe",
        )(*args)

    args = (_to_t(q), k, _vt_aug(v, LANES), vec(lq1), vec(lk1), vec(lq2), vec(lk2),
            subln.astype(F32).reshape(-1, 1))
    o_t = lax.cond(fast_ok, functools.partial(call, True), functools.partial(call, False), *args)
    return _to_t(o_t)


def _sortable_key(score):
    bits = pltpu.bitcast(score + 0.0, I32)
    return bits ^ ((bits >> 31) & 0x7FFFFFFF)


def _dsa_kernel(qt_ref, k_ref, vt_ref, qit_ref, ki_ref, wt_ref, o_ref, kb_sc, m_sc, acc_sc,
                *, t, nk, n_sel, fast):
    i = pl.program_id(1)
    pair = pl.program_id(2)
    n_chunks = i + 1
    krow = lax.broadcasted_iota(I32, (t, t), 0)

    def count_where(pred_fn):
        def body(cidx, acc):
            g = jnp.where(pred_fn(kb_sc[cidx], cidx), 1, 0)
            return acc + jnp.sum(g.reshape(t // 8, 8, t), axis=0)
        acc = lax.fori_loop(0, n_chunks, body, jnp.zeros((8, t), I32))
        return jnp.sum(acc, axis=0, keepdims=True)

    @pl.when(pair == 0)
    def _():
        qit = qit_ref[0]
        pad = jnp.zeros((LANES - IDX_DIM, t), BF16)
        q_heads = [jnp.concatenate([qit[h * IDX_DIM:(h + 1) * IDX_DIM], pad], axis=0)
                   for h in range(IDX_HEADS)]
        wt = wt_ref[0]
        w_rows = [wt[h:h + 1] for h in range(IDX_HEADS)]
        qcol = i * t + lax.broadcasted_iota(I32, (1, t), 1)

        def score_chunk(cidx, carry):
            off = pl.multiple_of(cidx * t, t)
            ki = ki_ref[0, pl.ds(off, t), :]
            score = jnp.zeros((t, t), F32)
            for h in range(IDX_HEADS):
                lg = jnp.dot(ki, q_heads[h], preferred_element_type=F32)
                score = score + w_rows[h] * jnp.maximum(lg, 0.0)
            key = _sortable_key(score)
            kb_sc[cidx] = jnp.where(krow + off <= qcol, key, I32_MIN)
            return carry

        lax.fori_loop(0, n_chunks, score_chunk, 0)

        def bisect(it, lohi):
            lo, hi = lohi
            mid = (lo >> 1) + (hi >> 1) + (lo & hi & 1)
            ok = count_where(lambda kk, cidx: kk >= mid) >= n_sel
            return jnp.where(ok, mid, lo), jnp.where(ok, hi, mid)

        thr, _ = lax.fori_loop(0, 32, bisect, (jnp.full((1, t), I32_MIN, I32),
                                               jnp.full((1, t), 2147483647, I32)))

        c_gt = count_where(lambda kk, cidx: kk > thr)
        c_eq = count_where(lambda kk, cidx: kk == thr)
        need = jnp.where(thr == I32_MIN, 0, n_sel - c_gt)
        all_ties = c_eq <= need
        jmax0 = jnp.where(need <= 0, -1, jnp.where(all_ties, 2147483647, -1))
        unresolved = jnp.logical_and(need > 0, jnp.logical_not(all_ties))

        def tie_search(_):
            def step(it, lohi):
                lo, hi = lohi
                mid = (lo + hi) >> 1
                cnt = count_where(lambda kk, cidx: jnp.logical_and(kk == thr, krow + cidx * t <= mid))
                ok = cnt >= need
                return jnp.where(ok, lo, mid), jnp.where(ok, mid, hi)
            n_steps = 1 + max(1, (nk * t - 1).bit_length())
            _, hi = lax.fori_loop(0, n_steps, step, (jnp.full((1, t), -1, I32),
                                                     jnp.full((1, t), nk * t - 1, I32)))
            return jnp.where(unresolved, hi, jmax0)

        any_unresolved = jnp.max(jnp.where(unresolved, 1, 0)) > 0
        jmax = lax.cond(any_unresolved, tie_search, lambda _: jmax0, 0)

        def bias_chunk(cidx, carry):
            kk = kb_sc[cidx]
            cut = jnp.where(krow + cidx * t <= jmax, thr - 1, thr)
            bias = jnp.where(kk > cut, 0.0, NEG)
            kb_sc[cidx] = pltpu.bitcast(bias.astype(F32), I32)
            return carry

        lax.fori_loop(0, n_chunks, bias_chunk, 0)

    _attn_init(m_sc, acc_sc)
    qt = qt_ref[0]
    qh = [_half_rows(qt, sub) for sub in range(2)]
    hrows = HEAD_DIM + ONES_ROWS

    def attend(jt, n):
        off = pl.multiple_of(jt * t, t)
        k = k_ref[0, pl.ds(off, n * t), :]
        vt = vt_ref[0, :, pl.ds(off, n * t)]
        bias = jnp.concatenate([pltpu.bitcast(kb_sc[jt + c], F32) for c in range(n)], axis=0)
        s = [jnp.dot(k, qh[sub], preferred_element_type=F32) + bias for sub in range(2)]
        for sub in range(2):
            _attn_update(s[sub], vt[sub * hrows:(sub + 1) * hrows], m_sc, acc_sc, sub, fast)

    def chunk_pair(jp, carry):
        attend(2 * jp, 2)
        return carry

    lax.fori_loop(0, n_chunks // 2, chunk_pair, 0)

    @pl.when(n_chunks % 2 == 1)
    def _():
        attend(n_chunks - 1, 1)

    o_ref[0] = jnp.concatenate([_attn_out(acc_sc, sub, HEAD_DIM) for sub in range(2)], axis=0)


def dsa_attention(q, k, v, qi, ki, iw, fast_ok):
    bsz, seq, width = q.shape
    t = _pick_tile(seq, (ATT_TILE, 128))
    nq = seq // t
    n_sel = min(DSA_TOPK, seq // 4)
    prow = 2 * (HEAD_DIM + ONES_ROWS)

    def call(fast, *args):
        return pl.pallas_call(
            functools.partial(_dsa_kernel, t=t, nk=nq, n_sel=n_sel, fast=fast),
            grid=(bsz, nq, width // LANES),
            in_specs=[pl.BlockSpec((1, LANES, t), lambda b, i, p: (b, p, i)),
                      pl.BlockSpec((1, seq, LANES), lambda b, i, p: (b, 0, p)),
                      pl.BlockSpec((1, prow, seq), lambda b, i, p: (b, p, 0)),
                      pl.BlockSpec((1, IDX_HEADS * IDX_DIM, t), lambda b, i, p: (b, 0, i)),
                      pl.BlockSpec((1, seq, LANES), lambda b, i, p: (b, 0, 0)),
                      pl.BlockSpec((1, IDX_HEADS, t), lambda b, i, p: (b, 0, i))],
            out_specs=pl.BlockSpec((1, LANES, t), lambda b, i, p: (b, p, i)),
            out_shape=jax.ShapeDtypeStruct((bsz, width, seq), F32),
            scratch_shapes=[pltpu.VMEM((nq, t, t), I32), pltpu.VMEM((2, 1, t), F32),
                            pltpu.VMEM((2, HEAD_DIM + ONES_ROWS, t), F32)],
            compiler_params=_cparams(("parallel", "parallel", "arbitrary"), vmem_mb=56),
            name="dsa_attn_fast" if fast else "dsa_attn_safe",
        )(*args)

    args = (_to_t(q), k, _vt_aug(v, HEAD_DIM), _to_t(qi), ki, _to_t(iw, F32))
    o_t = lax.cond(fast_ok, functools.partial(call, True), functools.partial(call, False), *args)
    return _to_t(o_t)


def _kmean_kernel(k_ref, o_ref, *, nbb):
    x = k_ref[0]
    o_ref[0] = jnp.mean(x.reshape(nbb, MOBA_BLOCK, x.shape[-1]), axis=1)


def block_means(k_f32):
    bsz, seq, width = k_f32.shape
    nb = seq // MOBA_BLOCK
    nbb = 8 if nb % 8 == 0 else nb
    return pl.pallas_call(
        functools.partial(_kmean_kernel, nbb=nbb),
        grid=(bsz, nb // nbb),
        in_specs=[pl.BlockSpec((1, nbb * MOBA_BLOCK, width), lambda b, i: (b, i, 0))],
        out_specs=pl.BlockSpec((1, nbb, width), lambda b, i: (b, i, 0)),
        out_shape=jax.ShapeDtypeStruct((bsz, nb, width), F32),
        compiler_params=_cparams(("parallel", "parallel")),
        name="moba_kmean",
    )(k_f32)


def _moba_kernel(qt_ref, km_ref, k_ref, vt_ref, o_ref, sel_sc, m_sc, acc_sc, *, t, n_sel, nbp, fast):
    i = pl.program_id(2)
    blk = lax.broadcasted_iota(I32, (nbp, t), 0)
    _attn_init(m_sc, acc_sc)
    qt = qt_ref[0]
    qh = [_half_rows(qt, sub) for sub in range(2)]
    qb = [x.astype(BF16) for x in qh]

    for sub in range(2):
        gate = jnp.dot(km_ref[0], qh[sub], precision=HI, preferred_element_type=F32)
        g = jnp.where(blk < i, gate, -jnp.inf)
        sel = jnp.zeros((nbp, t), F32)
        for _ in range(n_sel):
            mx = jnp.max(g, axis=0, keepdims=True)
            is_mx = jnp.logical_and(g == mx, mx > -jnp.inf)
            first = jnp.min(jnp.where(is_mx, blk, 2 * nbp), axis=0, keepdims=True)
            pick = blk == first
            sel = jnp.where(pick, 1.0, sel)
            g = jnp.where(pick, -jnp.inf, g)
        sel_sc[sub] = sel

    hrows = HEAD_DIM + ONES_ROWS

    def attend(kb, n, mask_fn):
        off = pl.multiple_of(kb * t, t)
        k = k_ref[0, pl.ds(off, n * t), :]
        vt = vt_ref[0, :, pl.ds(off, n * t)]
        s = [mask_fn(jnp.dot(k, qb[sub], preferred_element_type=F32), sub) for sub in range(2)]
        for sub in range(2):
            _attn_update(s[sub], vt[sub * hrows:(sub + 1) * hrows], m_sc, acc_sc, sub, fast)

    causal = _causal_t(t)
    attend(i, 1, lambda s_t, sub: jnp.where(causal, s_t, NEG))

    def past_blocks(kb, n):
        def row_mask(s_t, sub):
            rows = [jnp.where(sel_sc[sub, pl.ds(kb + c, 1), :] > 0.0, 0.0, NEG) for c in range(n)]
            bias = jnp.concatenate([jnp.broadcast_to(r, (t, t)) for r in rows], axis=0)
            return s_t + bias
        attend(kb, n, row_mask)

    def block_pair(jp, carry):
        past_blocks(2 * jp, 2)
        return carry

    lax.fori_loop(0, i // 2, block_pair, 0)

    @pl.when(i % 2 == 1)
    def _():
        past_blocks(i - 1, 1)

    o_ref[0] = jnp.concatenate([_attn_out(acc_sc, sub, HEAD_DIM) for sub in range(2)], axis=0)


def moba_attention(q_f32, kmean, k, v, fast_ok):
    bsz, seq, width = q_f32.shape
    t = MOBA_BLOCK
    nb = seq // MOBA_BLOCK
    n_sel = max(1, min(MOBA_TOPK, nb - 1))
    nbp = -(-nb // 8) * 8
    km = jnp.zeros((bsz, nbp, width), F32).at[:, :nb].set(kmean)
    prow = 2 * (HEAD_DIM + ONES_ROWS)

    def call(fast, *args):
        return pl.pallas_call(
            functools.partial(_moba_kernel, t=t, n_sel=n_sel, nbp=nbp, fast=fast),
            grid=(bsz, width // LANES, nb),
            in_specs=[pl.BlockSpec((1, LANES, t), lambda b, h, i: (b, h, i)),
                      pl.BlockSpec((1, nbp, LANES), lambda b, h, i: (b, 0, h)),
                      pl.BlockSpec((1, seq, LANES), lambda b, h, i: (b, 0, h)),
                      pl.BlockSpec((1, prow, seq), lambda b, h, i: (b, h, 0))],
            out_specs=pl.BlockSpec((1, LANES, t), lambda b, h, i: (b, h, i)),
            out_shape=jax.ShapeDtypeStruct((bsz, width, seq), F32),
            scratch_shapes=[pltpu.VMEM((2, nbp, t), F32), pltpu.VMEM((2, 1, t), F32),
                            pltpu.VMEM((2, HEAD_DIM + ONES_ROWS, t), F32)],
            compiler_params=_cparams(("parallel", "parallel", "arbitrary")),
            name="moba_attn_fast" if fast else "moba_attn_safe",
        )(*args)

    args = (_to_t(q_f32), km, k, _vt_aug(v, HEAD_DIM))
    o_t = lax.cond(fast_ok, functools.partial(call, True), functools.partial(call, False), *args)
    return _to_t(o_t)


def s5_operators(lam_re, lam_im, b_re, b_im, c_re, c_im, d_skip, log_dt, chunk):
    n_g, n_p = lam_re.shape
    n_c = d_skip.shape[-1]
    dt = jnp.exp(log_dt.astype(F32))[:, None]
    den = lam_re * lam_re + lam_im * lam_im
    mag = jnp.exp(lam_re * dt)
    abar_re = mag * jnp.cos(lam_im * dt)
    abar_im = mag * jnp.sin(lam_im * dt)
    coef_re = ((abar_re - 1.0) * lam_re + abar_im * lam_im) / den
    coef_im = (abar_im * lam_re - (abar_re - 1.0) * lam_im) / den
    bbar_re = coef_re[..., None] * b_re - coef_im[..., None] * b_im
    bbar_im = coef_re[..., None] * b_im + coef_im[..., None] * b_re
    tau = jnp.arange(chunk + 1, dtype=F32)[:, None, None]
    pmag = jnp.exp(lam_re * dt * tau)
    pw_re = pmag * jnp.cos(lam_im * dt * tau)
    pw_im = pmag * jnp.sin(lam_im * dt * tau)
    ab_re = pw_re[..., None] * bbar_re - pw_im[..., None] * bbar_im
    ab_im = pw_re[..., None] * bbar_im + pw_im[..., None] * bbar_re
    kern = (jnp.einsum('gop,tgpi->tgio', c_re, ab_re, precision=HI)
            - jnp.einsum('gop,tgpi->tgio', c_im, ab_im, precision=HI))
    t_idx = jnp.arange(chunk)
    lag = t_idx[None, :] - t_idx[:, None]
    toep = jnp.where((lag >= 0)[None, :, None, :, None],
                     kern[jnp.clip(lag, 0, chunk)].transpose(2, 0, 3, 1, 4), 0.0)
    eye = (jnp.eye(chunk)[None, :, None, :, None] * jnp.eye(n_c)[None, None, :, None, :])
    toep = toep + eye * d_skip[:, None, :, None, None]
    toep = toep.reshape(n_g, chunk * n_c, chunk * n_c)
    rev = chunk - 1 - t_idx
    w_in = jnp.concatenate([ab_re[rev].transpose(1, 0, 3, 2), ab_im[rev].transpose(1, 0, 3, 2)], axis=-1)
    w_in = w_in.reshape(n_g, chunk * n_c, 2 * n_p)
    a_re, a_im = pw_re[1:chunk + 1], pw_im[1:chunk + 1]
    st_re = (c_re[None] * a_re[:, :, None, :] - c_im[None] * a_im[:, :, None, :])
    st_im = (-c_re[None] * a_im[:, :, None, :] - c_im[None] * a_re[:, :, None, :])
    w_st = jnp.concatenate([st_re.transpose(1, 3, 0, 2), st_im.transpose(1, 3, 0, 2)], axis=1)
    w_st = w_st.reshape(n_g, 2 * n_p, chunk * n_c)
    return toep, w_in, w_st, pw_re[chunk], pw_im[chunk]


def _s5_in_kernel(u_ref, w_ref, o_ref):
    o_ref[0, 0] = jnp.dot(u_ref[0, 0], w_ref[0], precision=HI, preferred_element_type=F32)


def _s5_scan_kernel(x_ref, aa_ref, ab_ref, abs_ref, o_ref, h_sc, hs_sc, *, steps, n_p):
    @pl.when(pl.program_id(0) == 0)
    def _():
        h_sc[...] = jnp.zeros(h_sc.shape, F32)
        hs_sc[...] = jnp.zeros(hs_sc.shape, F32)

    aa, ab, abs_ = aa_ref[...], ab_ref[...], abs_ref[...]

    def body(t, carry):
        h, hs = carry
        o_ref[t] = h
        x = x_ref[t]
        xs = pltpu.roll(x, n_p, 1)
        return aa * h + ab * hs + x, aa * hs + abs_ * h + xs

    h, hs = lax.fori_loop(0, steps, body, (h_sc[...], hs_sc[...]))
    h_sc[...] = h
    hs_sc[...] = hs


def _s5_out_kernel(u_ref, h0_ref, t_ref, w_ref, o_ref):
    y = (jnp.dot(u_ref[0, 0], t_ref[0], precision=HI, preferred_element_type=F32)
         + jnp.dot(h0_ref[0, 0], w_ref[0], precision=HI, preferred_element_type=F32))
    o_ref[0, 0] = jax.nn.gelu(y)


def s5_layer(u, ops):
    toep, w_in, w_st, ac_re, ac_im = ops
    bsz, seq, width = u.shape
    n_g, n_p = ac_re.shape
    n_c = width // n_g
    chunk = S5_CHUNK
    nj = seq // chunk
    row = chunk * n_c
    ur = u.reshape(bsz, nj, chunk, n_g, n_c).transpose(0, 3, 1, 2, 4).reshape(bsz, n_g, nj, row)
    hloc = pl.pallas_call(
        _s5_in_kernel,
        grid=(bsz, n_g),
        in_specs=[pl.BlockSpec((1, 1, nj, row), lambda b, g: (b, g, 0, 0)),
                  pl.BlockSpec((1, row, 2 * n_p), lambda b, g: (g, 0, 0))],
        out_specs=pl.BlockSpec((1, 1, nj, 2 * n_p), lambda b, g: (b, g, 0, 0)),
        out_shape=jax.ShapeDtypeStruct((bsz, n_g, nj, 2 * n_p), F32),
        compiler_params=_cparams(("parallel", "parallel")),
        name="s5_in",
    )(ur, w_in)
    bg = bsz * n_g
    xs = hloc.transpose(2, 0, 1, 3).reshape(nj, bg, 2 * n_p)
    rep = lambda a: jnp.tile(a, (bsz, 1))
    aa = rep(jnp.concatenate([ac_re, ac_re], axis=-1))
    ab = rep(jnp.concatenate([-ac_im, ac_im], axis=-1))
    abs_ = rep(jnp.concatenate([ac_im, -ac_im], axis=-1))
    steps = _pick_tile(nj, (128, 64, 32, 16, 8))
    cst = pl.BlockSpec((bg, 2 * n_p), lambda t: (0, 0))
    h0 = pl.pallas_call(
        functools.partial(_s5_scan_kernel, steps=steps, n_p=n_p),
        grid=(nj // steps,),
        in_specs=[pl.BlockSpec((steps, bg, 2 * n_p), lambda t: (t, 0, 0)), cst, cst, cst],
        out_specs=pl.BlockSpec((steps, bg, 2 * n_p), lambda t: (t, 0, 0)),
        out_shape=jax.ShapeDtypeStruct((nj, bg, 2 * n_p), F32),
        scratch_shapes=[pltpu.VMEM((bg, 2 * n_p), F32), pltpu.VMEM((bg, 2 * n_p), F32)],
        compiler_params=_cparams(("arbitrary",)),
        name="s5_scan",
    )(xs, aa, ab, abs_)
    h0 = h0.reshape(nj, bsz, n_g, 2 * n_p).transpose(1, 2, 0, 3)
    y = pl.pallas_call(
        _s5_out_kernel,
        grid=(bsz, n_g),
        in_specs=[pl.BlockSpec((1, 1, nj, row), lambda b, g: (b, g, 0, 0)),
                  pl.BlockSpec((1, 1, nj, 2 * n_p), lambda b, g: (b, g, 0, 0)),
                  pl.BlockSpec((1, row, row), lambda b, g: (g, 0, 0)),
                  pl.BlockSpec((1, 2 * n_p, row), lambda b, g: (g, 0, 0))],
        out_specs=pl.BlockSpec((1, 1, nj, row), lambda b, g: (b, g, 0, 0)),
        out_shape=jax.ShapeDtypeStruct((bsz, n_g, nj, row), F32),
        compiler_params=_cparams(("parallel", "parallel")),
        name="s5_out",
    )(ur, h0, toep, w_st)
    return y.reshape(bsz, n_g, nj, chunk, n_c).transpose(0, 2, 3, 1, 4).reshape(bsz, seq, width)


def _glu_kernel(y_ref, w_ref, b_ref, o_ref):
    y = y_ref[...]
    gate = jnp.dot(y.astype(BF16), w_ref[...], preferred_element_type=F32) + b_ref[...]
    o_ref[...] = (y * jax.nn.sigmoid(gate)).astype(o_ref.dtype)


def glu(y, w_bf16, b, tm=1024):
    n_tok, width = y.shape
    tm = _pick_tile(n_tok, (tm, 512, 256, 128))
    return pl.pallas_call(
        _glu_kernel,
        grid=(n_tok // tm,),
        in_specs=[pl.BlockSpec((tm, width), lambda i: (i, 0)),
                  pl.BlockSpec((width, width), lambda i: (0, 0)),
                  pl.BlockSpec((1, width), lambda i: (0, 0))],
        out_specs=pl.BlockSpec((tm, width), lambda i: (i, 0)),
        out_shape=jax.ShapeDtypeStruct((n_tok, width), F32),
        compiler_params=_cparams(("parallel",)),
        name="s5_glu",
    )(y, w_bf16, b.astype(F32).reshape(1, width))


def _outproj_kernel(a_ref, b_ref, w_ref, x_ref, g_ref, o_ref, *, half):
    y = (jnp.dot(a_ref[0].astype(BF16), w_ref[:half, :], preferred_element_type=F32)
         + jnp.dot(b_ref[0].astype(BF16), w_ref[half:, :], preferred_element_type=F32))
    o_ref[0] = x_ref[0] + g_ref[0] * y


def out_proj(a, b, w_bf16, x, gate, tm=512):
    bsz, seq, d = x.shape
    half = a.shape[-1]
    tm = _pick_tile(seq, (tm, 256, 128))
    return pl.pallas_call(
        functools.partial(_outproj_kernel, half=half),
        grid=(bsz, seq // tm),
        in_specs=[pl.BlockSpec((1, tm, half), lambda bb, i: (bb, i, 0)),
                  pl.BlockSpec((1, tm, half), lambda bb, i: (bb, i, 0)),
                  pl.BlockSpec((2 * half, d), lambda bb, i: (0, 0)),
                  pl.BlockSpec((1, tm, d), lambda bb, i: (bb, i, 0)),
                  pl.BlockSpec((1, 1, d), lambda bb, i: (bb, 0, 0))],
        out_specs=pl.BlockSpec((1, tm, d), lambda bb, i: (bb, i, 0)),
        out_shape=jax.ShapeDtypeStruct((bsz, seq, d), F32),
        compiler_params=_cparams(("parallel", "parallel")),
        name="out_proj",
    )(a, b, w_bf16, x, gate)


def _router_kernel(x_ref, sc_ref, sh_ref, wr_ref, br_ref, h_ref, r_ref):
    h = _norm_mod(x_ref[0], sc_ref[0], sh_ref[0])
    h_ref[0] = h.astype(BF16)
    logits = jnp.dot(h, wr_ref[...], precision=HI, preferred_element_type=F32) + br_ref[...]
    tm = logits.shape[0]
    lane = lax.broadcasted_iota(I32, (tm, LANES), 1)
    big = 4 * LANES
    coarse = jnp.where(lane < N_GROUPS, logits, -jnp.inf)
    cmax = jnp.max(coarse, axis=1, keepdims=True)
    grp = jnp.min(jnp.where(coarse == cmax, lane, big), axis=1, keepdims=True)
    p_grp = 1.0 / jnp.sum(jnp.exp(coarse - cmax), axis=1, keepdims=True)
    lo = N_GROUPS + grp * EXPERTS_PER_GROUP
    fine = jnp.where(jnp.logical_and(lane >= lo, lane < lo + EXPERTS_PER_GROUP), logits, -jnp.inf)
    v1 = jnp.max(fine, axis=1, keepdims=True)
    i1 = jnp.min(jnp.where(fine == v1, lane, big), axis=1, keepdims=True)
    fine2 = jnp.where(lane == i1, -jnp.inf, fine)
    v2 = jnp.max(fine2, axis=1, keepdims=True)
    i2 = jnp.min(jnp.where(fine2 == v2, lane, big), axis=1, keepdims=True)
    e2 = jnp.exp(v2 - v1)
    wa = p_grp / (1.0 + e2)
    wb = p_grp * e2 / (1.0 + e2)
    ida = (i1 - N_GROUPS).astype(F32)
    idb = (i2 - N_GROUPS).astype(F32)
    r_ref[0] = jnp.where(lane == 0, ida, jnp.where(lane == 1, idb,
                         jnp.where(lane == 2, wa, jnp.where(lane == 3, wb, 0.0))))


def moe_router(x, sc, sh, wg, bg, we, be, tm=512):
    bsz, seq, d = x.shape
    tm = _pick_tile(seq, (tm, 256, 128))
    n_r = N_GROUPS + N_EXPERTS
    wr = jnp.zeros((d, LANES), F32).at[:, :N_GROUPS].set(wg).at[:, N_GROUPS:n_r].set(we)
    br = jnp.zeros((1, LANES), F32).at[0, :N_GROUPS].set(bg).at[0, N_GROUPS:n_r].set(be)
    return pl.pallas_call(
        _router_kernel,
        grid=(bsz, seq // tm),
        in_specs=[pl.BlockSpec((1, tm, d), lambda b, i: (b, i, 0)),
                  pl.BlockSpec((1, 1, d), lambda b, i: (b, 0, 0)),
                  pl.BlockSpec((1, 1, d), lambda b, i: (b, 0, 0)),
                  pl.BlockSpec((d, LANES), lambda b, i: (0, 0)),
                  pl.BlockSpec((1, LANES), lambda b, i: (0, 0))],
        out_specs=[pl.BlockSpec((1, tm, d), lambda b, i: (b, i, 0)),
                   pl.BlockSpec((1, tm, LANES), lambda b, i: (b, i, 0))],
        out_shape=[jax.ShapeDtypeStruct((bsz, seq, d), BF16),
                   jax.ShapeDtypeStruct((bsz, seq, LANES), F32)],
        compiler_params=_cparams(("parallel", "parallel")),
        name="moe_router",
    )(x, sc, sh, wr, br)


MOE_SEG = 256


def _moe_kernel(erow_ref, route_ref, h_ref, x_ref, g_ref, w1_ref, w3_ref, w2_ref, o_ref,
                rrow_sc, rcol_sc, *, chunk, rows):
    e = pl.program_id(1)
    n_seg = chunk // MOE_SEG

    @pl.when(e == 0)
    def _():
        o_ref[...] = jnp.zeros(o_ref.shape, F32)
        si = lax.broadcasted_iota(I32, (MOE_SEG, MOE_SEG), 0)
        sj = lax.broadcasted_iota(I32, (MOE_SEG, MOE_SEG), 1)
        upper = jnp.where(si < sj, 1.0, 0.0).astype(BF16)
        lower = jnp.where(sj < si, 1.0, 0.0).astype(BF16)
        erow = erow_ref[0]
        eid = lax.broadcasted_iota(I32, (N_EXPERTS, chunk), 0)
        member = jnp.where(jnp.logical_or(erow[0:1] == eid, erow[1:2] == eid), 1.0, 0.0)
        base = jnp.zeros((N_EXPERTS, 1), F32)
        parts = []
        for sgi in range(n_seg):
            seg = member[:, sgi * MOE_SEG:(sgi + 1) * MOE_SEG]
            parts.append(jnp.dot(seg.astype(BF16), upper, preferred_element_type=F32) + base)
            base = base + jnp.sum(seg, axis=1, keepdims=True)
        rank = jnp.where(member > 0.0, jnp.concatenate(parts, axis=1), -1.0)
        for ee in range(N_EXPERTS):
            rrow_sc[ee] = rank[ee:ee + 1]
        route = route_ref[...]
        lane = lax.broadcasted_iota(I32, (chunk, LANES), 1).astype(F32)
        member_c = jnp.where(jnp.logical_or(route[:, 0:1] == lane, route[:, 1:2] == lane), 1.0, 0.0)
        base_c = jnp.zeros((1, LANES), F32)
        for sgi in range(n_seg):
            seg = member_c[sgi * MOE_SEG:(sgi + 1) * MOE_SEG]
            rk = jnp.dot(lower, seg.astype(BF16), preferred_element_type=F32) + base_c
            rcol_sc[sgi * MOE_SEG:(sgi + 1) * MOE_SEG, :] = jnp.where(seg > 0.0, rk, -1.0)
            base_c = base_c + jnp.sum(seg, axis=0, keepdims=True)

    rrow = rrow_sc[e]
    ef = e.astype(F32)
    lane = lax.broadcasted_iota(I32, (chunk, LANES), 1)
    rcol = jnp.sum(jnp.where(lane == e, rcol_sc[...], 0.0), axis=1, keepdims=True)
    route = route_ref[...]
    wcol = (jnp.where(route[:, 0:1] == ef, route[:, 2:3], 0.0)
            + jnp.where(route[:, 1:2] == ef, route[:, 3:4], 0.0))
    count = (jnp.max(rrow) + 1.0).astype(I32)
    n_blk = (count + rows - 1) // rows

    def block(bi, carry):
        base = (bi * rows).astype(F32)
        ridx = lax.broadcasted_iota(I32, (rows, 1), 0).astype(F32) + base
        gather = jnp.where(rrow == ridx, 1.0, 0.0).astype(BF16)
        xb = jnp.dot(gather, h_ref[...], preferred_element_type=F32).astype(BF16)
        a = jnp.dot(xb, w1_ref[0], preferred_element_type=F32)
        hid = (a * jax.nn.sigmoid(a)) * jnp.dot(xb, w3_ref[0], preferred_element_type=F32)
        yb = jnp.dot(hid.astype(BF16), w2_ref[0], preferred_element_type=F32)
        cidx = lax.broadcasted_iota(I32, (1, rows), 1).astype(F32) + base
        scatter = jnp.where(rcol == cidx, 1.0, 0.0).astype(BF16)
        o_ref[...] += wcol * jnp.dot(scatter, yb.astype(BF16), preferred_element_type=F32)
        return carry

    lax.fori_loop(0, n_blk, block, 0)

    @pl.when(e == N_EXPERTS - 1)
    def _():
        o_ref[...] = x_ref[...] + g_ref[0] * o_ref[...]


def moe_experts(h, route, x, gate, w1_bf16, w3_bf16, w2_bf16, chunk=1024, rows=128):
    bsz, seq, d = x.shape
    chunk = _pick_tile(seq, (chunk, 512, 256))
    n_tok = bsz * seq
    n_chunks = n_tok // chunk
    per_b = seq // chunk
    ff = w1_bf16.shape[-1]
    route2 = route.reshape(n_tok, LANES)
    erow = route2[:, :2].astype(I32).T.reshape(2, n_chunks, chunk).transpose(1, 0, 2)
    out = pl.pallas_call(
        functools.partial(_moe_kernel, chunk=chunk, rows=rows),
        grid=(n_chunks, N_EXPERTS),
        in_specs=[pl.BlockSpec((1, 2, chunk), lambda c, e: (c, 0, 0)),
                  pl.BlockSpec((chunk, LANES), lambda c, e: (c, 0)),
                  pl.BlockSpec((chunk, d), lambda c, e: (c, 0)),
                  pl.BlockSpec((chunk, d), lambda c, e: (c, 0)),
                  pl.BlockSpec((1, 1, d), lambda c, e: (c // per_b, 0, 0)),
                  pl.BlockSpec((1, d, ff), lambda c, e: (e, 0, 0)),
                  pl.BlockSpec((1, d, ff), lambda c, e: (e, 0, 0)),
                  pl.BlockSpec((1, ff, d), lambda c, e: (e, 0, 0))],
        out_specs=pl.BlockSpec((chunk, d), lambda c, e: (c, 0)),
        out_shape=jax.ShapeDtypeStruct((n_tok, d), F32),
        scratch_shapes=[pltpu.VMEM((N_EXPERTS, 1, chunk), F32), pltpu.VMEM((chunk, LANES), F32)],
        compiler_params=_cparams(("parallel", "arbitrary")),
        name="moe_experts",
    )(erow, route2, h.reshape(n_tok, d), x.reshape(n_tok, d), gate, w1_bf16, w3_bf16, w2_bf16)
    return out.reshape(bsz, seq, d)


def kernel(x, c, ada_w, ada_b, moe_wg, moe_bg, moe_we, moe_be, moe_w1, moe_w3, moe_w2,
           ev_w_in, ev_w_out, a_qn, a_kn, a_lq1, a_lk1, a_lq2, a_lk2, a_subln,
           b_qn, b_kn, b_idx_kn, od_w_in, od_w_out, c_qn, c_kn,
           s5_lam_re, s5_lam_im, s5_b_re, s5_b_im, s5_c_re, s5_c_im, s5_d, s5_log_dt,
           s5_glu_w, s5_glu_b):
    bsz, seq, d = x.shape
    depth = ada_w.shape[0]
    mod = ada_mod(c, ada_w, ada_b)
    rope_h = rope_lane_tables(seq, HEAD_DIM, ROT_HEAD)
    rope_i = rope_lane_tables(seq, IDX_DIM, ROT_IDX)
    qk_scale = HEAD_DIM ** -0.5 * math.log2(math.e)
    half_h, half_i = ROT_HEAD // 2, ROT_IDX // 2
    hw = A_HEADS * 2 * HEAD_DIM

    def pad_cols(w):
        n = w.shape[1]
        npad = -(-n // LANES) * LANES
        return jnp.zeros((w.shape[0], npad), F32).at[:, :n].set(w).astype(BF16)

    for l in range(depth):
        sh1, sc1, g1, sh2, sc2, g2 = [m[:, None, :] for m in jnp.split(mod[l], 6, axis=-1)]
        i = l // 2
        if l % 2 == 0:
            lam_init = 0.8 - 0.6 * math.exp(-0.3 * l)
            z = in_proj(x, sc1, sh1, pad_cols(ev_w_in[i]))
            (aq,) = prep(z, 0, hw, hd=HEAD_DIM, gain=a_qn[i], rope=rope_h, half=half_h, scale=qk_scale)
            (ak,) = prep(z, hw, hw, hd=HEAD_DIM, gain=a_kn[i], rope=rope_h, half=half_h)
            (av,) = prep(z, 2 * hw, hw, hd=HEAD_DIM)
            o_a = diff_attention(aq, ak, av, a_lq1[i], a_lk1[i], a_lq2[i], a_lk2[i], a_subln[i], lam_init,
                                 _logit_bound(a_qn[i], a_kn[i]) <= MAX_FAST_LOGIT)
            (bq,) = prep(z, 3 * hw, hw, hd=HEAD_DIM, gain=b_qn[i], rope=rope_h, half=half_h, scale=qk_scale)
            (bk,) = prep(z, 4 * hw, hw, hd=HEAD_DIM, gain=b_kn[i], rope=rope_h, half=half_h)
            (bv,) = prep(z, 5 * hw, hw, hd=HEAD_DIM)
            iq_off = 6 * hw
            ik_off = iq_off + IDX_HEADS * IDX_DIM
            (iq,) = prep(z, iq_off, IDX_HEADS * IDX_DIM, hd=IDX_DIM, rope=rope_i, half=half_i)
            (ik,) = prep(z, ik_off, LANES, hd=IDX_DIM, gain=b_idx_kn[i], rope=rope_i, half=half_i)
            iw = z[:, :, ik_off + IDX_DIM:ik_off + IDX_DIM + IDX_HEADS]
            o_b = dsa_attention(bq, bk, bv, iq, ik, iw, _logit_bound(b_qn[i], b_kn[i]) <= MAX_FAST_LOGIT)
            x = out_proj(o_a, o_b, ev_w_out[i].astype(BF16), x, g1)
        else:
            z = in_proj(x, sc1, sh1, pad_cols(od_w_in[i]))
            (cq,) = prep(z, 0, hw, hd=HEAD_DIM, gain=c_qn[i], rope=rope_h, half=half_h, scale=qk_scale,
                         out_dtypes=(F32,))
            ck32, ck = prep(z, hw, hw, hd=HEAD_DIM, gain=c_kn[i], rope=rope_h, half=half_h,
                            out_dtypes=(F32, BF16))
            (cv,) = prep(z, 2 * hw, hw, hd=HEAD_DIM)
            o_c = moba_attention(cq, block_means(ck32), ck, cv, _logit_bound(c_qn[i], c_kn[i]) <= MAX_FAST_LOGIT)
            ops = s5_operators(s5_lam_re[i], s5_lam_im[i], s5_b_re[i], s5_b_im[i], s5_c_re[i], s5_c_im[i],
                               s5_d[i], s5_log_dt[i], S5_CHUNK)
            y = s5_layer(z[:, :, 3 * hw:4 * hw], ops)
            y = glu(y.reshape(bsz * seq, hw), s5_glu_w[i].astype(BF16), s5_glu_b[i]).reshape(bsz, seq, hw)
            x = out_proj(o_c, y, od_w_out[i].astype(BF16), x, g1)
        h, route = moe_router(x, sc2, sh2, moe_wg[l], moe_bg[l], moe_we[l], moe_be[l])
        x = moe_experts(h, route, x, g2, moe_w1[l].astype(BF16), moe_w3[l].astype(BF16), moe_w2[l].astype(BF16))
    return x
```

```python
import functools
import math

import jax
import jax.numpy as jnp
from jax import lax
from jax.experimental import pallas as pl
from jax.experimental.pallas import tpu as pltpu

F32 = jnp.float32
BF16 = jnp.bfloat16
I32 = jnp.int32
HI = lax.Precision.HIGHEST

LANES = 128
NEG = -1e30
I32_MIN = -2147483648

HEAD_DIM = 64
A_HEADS = 4
B_HEADS = 8
IDX_HEADS = 8
IDX_DIM = 32
DSA_TOPK = 256
C_HEADS = 8
MOBA_BLOCK = 256
MOBA_TOPK = 3
S5_GROUP_CH = 16
S5_GROUPS = 32
S5_STATE = 64
ROPE_THETA = 500000.0
ROT_HEAD = HEAD_DIM // 4
ROT_IDX = IDX_DIM // 4
N_GROUPS = 4
EXPERTS_PER_GROUP = 8
N_EXPERTS = N_GROUPS * EXPERTS_PER_GROUP
EXPERT_FF = 512
NORM_EPS = 1e-6
S5_CHUNK = 16


def _cparams(sem, vmem_mb=48):
    return pltpu.CompilerParams(dimension_semantics=sem, vmem_limit_bytes=vmem_mb * 1024 * 1024)


def _nt_dot(a, b, precision=None):
    return lax.dot_general(a, b, (((1,), (1,)), ((), ())), precision=precision,
                           preferred_element_type=F32)


def _mod_kernel(c_ref, w_ref, b_ref, o_ref):
    c = c_ref[...]
    ca = c * jax.nn.sigmoid(c)
    o_ref[0] = jnp.dot(ca, w_ref[0], precision=HI, preferred_element_type=F32) + b_ref[0]


def ada_mod(c, ada_w, ada_b):
    depth, d, n6 = ada_w.shape
    bsz = c.shape[0]
    rows = 8
    c_pad = jnp.zeros((rows, d), F32).at[:bsz].set(c)
    out = pl.pallas_call(
        _mod_kernel,
        grid=(depth, n6 // d),
        in_specs=[pl.BlockSpec((rows, d), lambda l, n: (0, 0)),
                  pl.BlockSpec((1, d, d), lambda l, n: (l, 0, n)),
                  pl.BlockSpec((1, 1, d), lambda l, n: (l, 0, n))],
        out_specs=pl.BlockSpec((1, rows, d), lambda l, n: (l, 0, n)),
        out_shape=jax.ShapeDtypeStruct((depth, rows, n6), F32),
        compiler_params=_cparams(("parallel", "parallel")),
        name="ada_mod",
    )(c_pad, ada_w, ada_b.reshape(depth, 1, n6))
    return out[:, :bsz]


def _norm_mod(x, sc, sh):
    y = x * lax.rsqrt(jnp.mean(x * x, axis=-1, keepdims=True) + NORM_EPS)
    return y * (1.0 + sc) + sh


def _inproj_kernel(x_ref, sc_ref, sh_ref, w_ref, o_ref, h_sc):
    @pl.when(pl.program_id(2) == 0)
    def _():
        h_sc[...] = _norm_mod(x_ref[0], sc_ref[0], sh_ref[0]).astype(BF16)

    o_ref[0] = jnp.dot(h_sc[...], w_ref[...], preferred_element_type=F32)


def _pick_tile(n, prefs):
    for p in prefs:
        if n % p == 0:
            return p
    return n


def in_proj(x, sc, sh, w_bf16, tm=512):
    bsz, seq, d = x.shape
    n = w_bf16.shape[1]
    tm = _pick_tile(seq, (tm, 256, 128))
    tn = _pick_tile(n, (1152, 1024, 512, 384, 256, 128))
    return pl.pallas_call(
        _inproj_kernel,
        grid=(bsz, seq // tm, n // tn),
        in_specs=[pl.BlockSpec((1, tm, d), lambda b, i, j: (b, i, 0)),
                  pl.BlockSpec((1, 1, d), lambda b, i, j: (b, 0, 0)),
                  pl.BlockSpec((1, 1, d), lambda b, i, j: (b, 0, 0)),
                  pl.BlockSpec((d, tn), lambda b, i, j: (0, j))],
        out_specs=pl.BlockSpec((1, tm, tn), lambda b, i, j: (b, i, j)),
        out_shape=jax.ShapeDtypeStruct((bsz, seq, n), F32),
        scratch_shapes=[pltpu.VMEM((tm, d), BF16)],
        compiler_params=_cparams(("parallel", "parallel", "arbitrary")),
        name="in_proj",
    )(x, sc, sh, w_bf16)


def rope_lane_tables(seq, hd, rot):
    half = rot // 2
    pos = jnp.arange(seq, dtype=F32)
    inv = ROPE_THETA ** (-(jnp.arange(0, rot, 2, dtype=F32) / rot))
    ang = pos[:, None] * inv[None, :]
    cos, sin = jnp.cos(ang), jnp.sin(ang)
    dl = jnp.arange(LANES) % hd
    first = dl < half
    second = (dl >= half) & (dl < rot)
    idx = jnp.where(first, dl, jnp.where(second, dl - half, 0))
    cos_l = jnp.where(first | second, cos[:, idx], 1.0)
    sa_l = jnp.where(second, sin[:, idx], 0.0)
    sb_l = jnp.where(first, -sin[:, idx], 0.0)
    return cos_l.astype(F32), sa_l.astype(F32), sb_l.astype(F32)


def _prep_kernel(*refs, hd, half, do_norm, do_rope, scale, n_out):
    z_ref, g_ref, gm_ref, cos_ref, sa_ref, sb_ref = refs[:6]
    o_refs = refs[6:6 + n_out]
    x = z_ref[0]
    width = x.shape[-1]
    if do_norm:
        sq = x * x
        hi = sq.astype(BF16)
        lo = (sq - hi.astype(F32)).astype(BF16)
        gm = gm_ref[...]
        ms = (jnp.dot(hi, gm, preferred_element_type=F32)
              + jnp.dot(lo, gm, preferred_element_type=F32)) * (1.0 / hd)
        x = x * lax.rsqrt(ms + NORM_EPS) * g_ref[...]
    if do_rope:
        cos, sa, sb = cos_ref[...], sa_ref[...], sb_ref[...]
        cols = []
        for r in range(width // LANES):
            xc = x[:, r * LANES:(r + 1) * LANES]
            cols.append(xc * cos + pltpu.roll(xc, half, 1) * sa + pltpu.roll(xc, LANES - half, 1) * sb)
        x = cols[0] if len(cols) == 1 else jnp.concatenate(cols, axis=-1)
    if scale != 1.0:
        x = x * scale
    for o in o_refs:
        o[0] = x.astype(o.dtype)


def prep(z, col_off, width, *, hd, gain=None, rope=None, half=0, scale=1.0, out_dtypes=(BF16,), tm=512):
    bsz, seq, _ = z.shape
    tm = _pick_tile(seq, (tm, 256, 128))
    cb = col_off // width
    assert col_off % width == 0 and width % LANES == 0
    do_norm = gain is not None
    do_rope = rope is not None
    g = jnp.tile(gain.astype(F32), width // hd).reshape(1, width) if do_norm else jnp.ones((1, width), F32)
    gi = jnp.arange(width) // hd
    gm = (gi[:, None] == gi[None, :]).astype(BF16)
    if do_rope:
        cos_l, sa_l, sb_l = rope
    else:
        cos_l = sa_l = sb_l = jnp.zeros((seq, LANES), F32)
    outs = pl.pallas_call(
        functools.partial(_prep_kernel, hd=hd, half=half, do_norm=do_norm, do_rope=do_rope,
                          scale=scale, n_out=len(out_dtypes)),
        grid=(bsz, seq // tm),
        in_specs=[pl.BlockSpec((1, tm, width), lambda b, i: (b, i, cb)),
                  pl.BlockSpec((1, width), lambda b, i: (0, 0)),
                  pl.BlockSpec((width, width), lambda b, i: (0, 0)),
                  pl.BlockSpec((tm, LANES), lambda b, i: (i, 0)),
                  pl.BlockSpec((tm, LANES), lambda b, i: (i, 0)),
                  pl.BlockSpec((tm, LANES), lambda b, i: (i, 0))],
        out_specs=[pl.BlockSpec((1, tm, width), lambda b, i: (b, i, 0)) for _ in out_dtypes],
        out_shape=[jax.ShapeDtypeStruct((bsz, seq, width), dt) for dt in out_dtypes],
        compiler_params=_cparams(("parallel", "parallel")),
        name="prep",
    )(z, g, gm, cos_l, sa_l, sb_l)
    return outs


ATT_TILE = 256


ONES_ROWS = 16
MAX_FAST_LOGIT = 60.0


def _attn_update(s_t, vt_aug, m_ref, acc_ref, idx, fast):
    if fast:
        acc_ref[idx] += jnp.dot(vt_aug, jnp.exp2(s_t).astype(BF16), preferred_element_type=F32)
    else:
        m_prev = m_ref[idx]
        m_new = jnp.maximum(m_prev, jnp.max(s_t, axis=0, keepdims=True))
        p_t = jnp.exp2(s_t - m_new).astype(BF16)
        acc_ref[idx] = (jnp.exp2(m_prev - m_new) * acc_ref[idx]
                        + jnp.dot(vt_aug, p_t, preferred_element_type=F32))
        m_ref[idx] = m_new


def _attn_init(m_sc, acc_sc):
    m_sc[...] = jnp.full(m_sc.shape, NEG, F32)
    acc_sc[...] = jnp.zeros(acc_sc.shape, F32)


def _attn_out(acc_ref, idx, dv):
    acc = acc_ref[idx]
    return acc[:dv] / acc[dv:dv + 1]


def _vt_aug(v, dv):
    bsz, seq, width = v.shape
    n_h = width // dv
    vt = v.astype(BF16).transpose(0, 2, 1).reshape(bsz, n_h, dv, seq)
    ones = jnp.ones((bsz, n_h, ONES_ROWS, seq), BF16)
    return jnp.concatenate([vt, ones], axis=2).reshape(bsz, n_h * (dv + ONES_ROWS), seq)


def _logit_bound(gain_q, gain_k):
    return (HEAD_DIM ** 0.5 * math.log2(math.e) * 1.02) * jnp.max(jnp.abs(gain_q)) * jnp.max(jnp.abs(gain_k))


def _half_rows(x, sub):
    row = lax.broadcasted_iota(I32, (LANES, 1), 0)
    return jnp.where((row >= sub * HEAD_DIM) & (row < (sub + 1) * HEAD_DIM), x, jnp.zeros_like(x))


def _causal_t(t):
    return lax.broadcasted_iota(I32, (t, t), 0) <= lax.broadcasted_iota(I32, (t, t), 1)


def _to_t(a, dtype=None):
    a = a if dtype is None else a.astype(dtype)
    return a.transpose(0, 2, 1)


def _diff_attn_kernel(qt_ref, k_ref, vt_ref, lq1_ref, lk1_ref, lq2_ref, lk2_ref, g_ref, o_ref,
                      m_sc, acc_sc, *, t, lam_init, fast):
    i = pl.program_id(2)
    _attn_init(m_sc, acc_sc)
    qt = qt_ref[0]
    qc = [_half_rows(qt, c) for c in range(2)]

    def attend(off, mask_fn):
        k = k_ref[0, pl.ds(off, t), :]
        vt = vt_ref[0, :, pl.ds(off, t)]
        s = [mask_fn(jnp.dot(k, qc[c], preferred_element_type=F32)) for c in range(2)]
        for c in range(2):
            _attn_update(s[c], vt, m_sc, acc_sc, c, fast)

    def past_tile(jt, carry):
        attend(pl.multiple_of(jt * t, t), lambda s_t: s_t)
        return carry

    lax.fori_loop(0, i, past_tile, 0)
    causal = _causal_t(t)
    attend(pl.multiple_of(i * t, t), lambda s_t: jnp.where(causal, s_t, NEG))

    lam = (jnp.exp(jnp.sum(lq1_ref[...] * lk1_ref[...], axis=-1, keepdims=True))
           - jnp.exp(jnp.sum(lq2_ref[...] * lk2_ref[...], axis=-1, keepdims=True)) + lam_init)
    o = _attn_out(acc_sc, 0, LANES) - lam * _attn_out(acc_sc, 1, LANES)
    o = o * lax.rsqrt(jnp.mean(o * o, axis=0, keepdims=True) + NORM_EPS)
    o_ref[0] = o * g_ref[...] * (1.0 - lam_init)


DIFF_TILE = 512


def diff_attention(q, k, v, lq1, lk1, lq2, lk2, subln, lam_init, fast_ok):
    bsz, seq, width = q.shape
    n_h = width // LANES
    t = _pick_tile(seq, (DIFF_TILE, 256, 128))
    rows = LANES + ONES_ROWS
    vec = lambda a: a.astype(F32).reshape(1, -1)
    small = pl.BlockSpec((1, HEAD_DIM), lambda b, h, i: (0, 0))

    def call(fast, *args):
        return pl.pallas_call(
            functools.partial(_diff_attn_kernel, t=t, lam_init=lam_init, fast=fast),
            grid=(bsz, n_h, seq // t),
            in_specs=[pl.BlockSpec((1, LANES, t), lambda b, h, i: (b, h, i)),
                      pl.BlockSpec((1, seq, LANES), lambda b, h, i: (b, 0, h)),
                      pl.BlockSpec((1, rows, seq), lambda b, h, i: (b, h, 0)),
                      small, small, small, small,
                      pl.BlockSpec((LANES, 1), lambda b, h, i: (0, 0))],
            out_specs=pl.BlockSpec((1, LANES, t), lambda b, h, i: (b, h, i)),
            out_shape=jax.ShapeDtypeStruct((bsz, width, seq), F32),
            scratch_shapes=[pltpu.VMEM((2, 1, t), F32), pltpu.VMEM((2, rows, t), F32)],
            compiler_params=_cparams(("parallel", "parallel", "arbitrary")),
            name="diff_attn_fast" if fast else "diff_attn_safe",
        )(*args)

    args = (_to_t(q), k, _vt_aug(v, LANES), vec(lq1), vec(lk1), vec(lq2), vec(lk2),
            subln.astype(F32).reshape(-1, 1))
    o_t = lax.cond(fast_ok, functools.partial(call, True), functools.partial(call, False), *args)
    return _to_t(o_t)


def _sortable_key(score):
    bits = pltpu.bitcast(score + 0.0, I32)
    return bits ^ ((bits >> 31) & 0x7FFFFFFF)


def _dsa_kernel(qt_ref, k_ref, vt_ref, qit_ref, ki_ref, wt_ref, o_ref, kb_sc, m_sc, acc_sc,
                *, t, nk, n_sel, fast):
    i = pl.program_id(1)
    pair = pl.program_id(2)
    n_chunks = i + 1
    krow = lax.broadcasted_iota(I32, (t, t), 0)

    def count_where(pred_fn):
        def body(cidx, acc):
            g = jnp.where(pred_fn(kb_sc[cidx], cidx), 1, 0)
            return acc + jnp.sum(g.reshape(t // 8, 8, t), axis=0)
        acc = lax.fori_loop(0, n_chunks, body, jnp.zeros((8, t), I32))
        return jnp.sum(acc, axis=0, keepdims=True)

    @pl.when(pair == 0)
    def _():
        qit = qit_ref[0]
        pad = jnp.zeros((LANES - IDX_DIM, t), BF16)
        q_heads = [jnp.concatenate([qit[h * IDX_DIM:(h + 1) * IDX_DIM], pad], axis=0)
                   for h in range(IDX_HEADS)]
        wt = wt_ref[0]
        w_rows = [wt[h:h + 1] for h in range(IDX_HEADS)]
        qcol = i * t + lax.broadcasted_iota(I32, (1, t), 1)

        def score_chunk(cidx, carry):
            off = pl.multiple_of(cidx * t, t)
            ki = ki_ref[0, pl.ds(off, t), :]
            score = jnp.zeros((t, t), F32)
            for h in range(IDX_HEADS):
                lg = jnp.dot(ki, q_heads[h], preferred_element_type=F32)
                score = score + w_rows[h] * jnp.maximum(lg, 0.0)
            key = _sortable_key(score)
            kb_sc[cidx] = jnp.where(krow + off <= qcol, key, I32_MIN)
            return carry

        lax.fori_loop(0, n_chunks, score_chunk, 0)

        def key_range(cidx, mm):
            kk = kb_sc[cidx]
            lo_c = jnp.min(jnp.where(kk == I32_MIN, 2147483647, kk).reshape(t // 8, 8, t), axis=0)
            hi_c = jnp.max(kk.reshape(t // 8, 8, t), axis=0)
            return jnp.minimum(mm[0], lo_c), jnp.maximum(mm[1], hi_c)

        kmin, kmax = lax.fori_loop(0, n_chunks, key_range, (jnp.full((8, t), 2147483647, I32),
                                                            jnp.full((8, t), I32_MIN, I32)))
        n_vis = qcol + 1
        few = n_vis <= n_sel
        lo0 = jnp.where(few, I32_MIN, jnp.min(kmin, axis=0, keepdims=True))
        hi0 = jnp.where(few, I32_MIN + 1, jnp.max(kmax, axis=0, keepdims=True) + 1)

        def open_lanes(lo, hi, c_lo):
            return jnp.logical_and(c_lo != n_sel, hi > lo + 1)

        def bisect_cond(st):
            it, lo, hi, c_lo, c_hi = st
            return jnp.logical_and(it < 34, jnp.max(jnp.where(open_lanes(lo, hi, c_lo), 1, 0)) > 0)

        def bisect(st):
            it, lo, hi, c_lo, c_hi = st
            mid = (lo >> 1) + (hi >> 1) + (lo & hi & 1)
            cnt = count_where(lambda kk, cidx: kk >= mid)
            up = jnp.logical_and(open_lanes(lo, hi, c_lo), cnt >= n_sel)
            dn = jnp.logical_and(open_lanes(lo, hi, c_lo), cnt < n_sel)
            return (it + 1, jnp.where(up, mid, lo), jnp.where(dn, mid, hi),
                    jnp.where(up, cnt, c_lo), jnp.where(dn, cnt, c_hi))

        c_lo0 = jnp.where(few, n_sel, n_vis)
        _, thr, _, c_lo, c_hi = lax.while_loop(
            bisect_cond, bisect, (jnp.int32(0), lo0, hi0, c_lo0, jnp.zeros((1, t), I32)))

        need = n_sel - c_hi
        all_ties = c_lo == n_sel
        jmax0 = jnp.where(few, -1, jnp.where(all_ties, 2147483647, -1))
        unresolved = jnp.logical_and(jnp.logical_not(few), jnp.logical_not(all_ties))

        def tie_search(_):
            def step(it, lohi):
                lo, hi = lohi
                mid = (lo + hi) >> 1
                cnt = count_where(lambda kk, cidx: jnp.logical_and(kk == thr, krow + cidx * t <= mid))
                ok = cnt >= need
                return jnp.where(ok, lo, mid), jnp.where(ok, mid, hi)
            n_steps = 1 + max(1, (nk * t - 1).bit_length())
            _, hi = lax.fori_loop(0, n_steps, step, (jnp.full((1, t), -1, I32),
                                                     jnp.full((1, t), nk * t - 1, I32)))
            return jnp.where(unresolved, hi, jmax0)

        any_unresolved = jnp.max(jnp.where(unresolved, 1, 0)) > 0
        jmax = lax.cond(any_unresolved, tie_search, lambda _: jmax0, 0)

        def bias_chunk(cidx, carry):
            kk = kb_sc[cidx]
            cut = jnp.where(krow + cidx * t <= jmax, thr - 1, thr)
            bias = jnp.where(kk > cut, 0.0, NEG)
            kb_sc[cidx] = pltpu.bitcast(bias.astype(F32), I32)
            return carry

        lax.fori_loop(0, n_chunks, bias_chunk, 0)

    _attn_init(m_sc, acc_sc)
    qt = qt_ref[0]
    qh = [_half_rows(qt, sub) for sub in range(2)]
    hrows = HEAD_DIM + ONES_ROWS

    def attend(jt, n):
        off = pl.multiple_of(jt * t, t)
        k = k_ref[0, pl.ds(off, n * t), :]
        vt = vt_ref[0, :, pl.ds(off, n * t)]
        bias = jnp.concatenate([pltpu.bitcast(kb_sc[jt + c], F32) for c in range(n)], axis=0)
        s = [jnp.dot(k, qh[sub], preferred_element_type=F32) + bias for sub in range(2)]
        for sub in range(2):
            _attn_update(s[sub], vt[sub * hrows:(sub + 1) * hrows], m_sc, acc_sc, sub, fast)

    def chunk_quad(jq, carry):
        attend(4 * jq, 4)
        return carry

    lax.fori_loop(0, n_chunks // 4, chunk_quad, 0)

    @pl.when(n_chunks % 4 >= 2)
    def _():
        attend((n_chunks // 4) * 4, 2)

    @pl.when(n_chunks % 2 == 1)
    def _():
        attend(n_chunks - 1, 1)

    o_ref[0] = jnp.concatenate([_attn_out(acc_sc, sub, HEAD_DIM) for sub in range(2)], axis=0)


def dsa_attention(q, k, v, qi, ki, iw, fast_ok):
    bsz, seq, width = q.shape
    t = _pick_tile(seq, (ATT_TILE, 128))
    nq = seq // t
    n_sel = min(DSA_TOPK, seq // 4)
    prow = 2 * (HEAD_DIM + ONES_ROWS)

    def call(fast, *args):
        return pl.pallas_call(
            functools.partial(_dsa_kernel, t=t, nk=nq, n_sel=n_sel, fast=fast),
            grid=(bsz, nq, width // LANES),
            in_specs=[pl.BlockSpec((1, LANES, t), lambda b, i, p: (b, p, i)),
                      pl.BlockSpec((1, seq, LANES), lambda b, i, p: (b, 0, p)),
                      pl.BlockSpec((1, prow, seq), lambda b, i, p: (b, p, 0)),
                      pl.BlockSpec((1, IDX_HEADS * IDX_DIM, t), lambda b, i, p: (b, 0, i)),
                      pl.BlockSpec((1, seq, LANES), lambda b, i, p: (b, 0, 0)),
                      pl.BlockSpec((1, IDX_HEADS, t), lambda b, i, p: (b, 0, i))],
            out_specs=pl.BlockSpec((1, LANES, t), lambda b, i, p: (b, p, i)),
            out_shape=jax.ShapeDtypeStruct((bsz, width, seq), F32),
            scratch_shapes=[pltpu.VMEM((nq, t, t), I32), pltpu.VMEM((2, 1, t), F32),
                            pltpu.VMEM((2, HEAD_DIM + ONES_ROWS, t), F32)],
            compiler_params=_cparams(("parallel", "parallel", "arbitrary"), vmem_mb=56),
            name="dsa_attn_fast" if fast else "dsa_attn_safe",
        )(*args)

    args = (_to_t(q), k, _vt_aug(v, HEAD_DIM), _to_t(qi), ki, _to_t(iw, F32))
    o_t = lax.cond(fast_ok, functools.partial(call, True), functools.partial(call, False), *args)
    return _to_t(o_t)


def _kmean_kernel(k_ref, o_ref, *, nbb):
    x = k_ref[0]
    o_ref[0] = jnp.mean(x.reshape(nbb, MOBA_BLOCK, x.shape[-1]), axis=1)


def block_means(k_f32):
    bsz, seq, width = k_f32.shape
    nb = seq // MOBA_BLOCK
    nbb = 8 if nb % 8 == 0 else nb
    return pl.pallas_call(
        functools.partial(_kmean_kernel, nbb=nbb),
        grid=(bsz, nb // nbb),
        in_specs=[pl.BlockSpec((1, nbb * MOBA_BLOCK, width), lambda b, i: (b, i, 0))],
        out_specs=pl.BlockSpec((1, nbb, width), lambda b, i: (b, i, 0)),
        out_shape=jax.ShapeDtypeStruct((bsz, nb, width), F32),
        compiler_params=_cparams(("parallel", "parallel")),
        name="moba_kmean",
    )(k_f32)


def _moba_kernel(qt_ref, km_ref, k_ref, vt_ref, o_ref, sel_sc, m_sc, acc_sc, *, t, n_sel, nbp, fast):
    i = pl.program_id(2)
    blk = lax.broadcasted_iota(I32, (nbp, t), 0)
    _attn_init(m_sc, acc_sc)
    qt = qt_ref[0]
    qh = [_half_rows(qt, sub) for sub in range(2)]
    qb = [x.astype(BF16) for x in qh]

    for sub in range(2):
        gate = jnp.dot(km_ref[0], qh[sub], precision=HI, preferred_element_type=F32)
        g = jnp.where(blk < i, gate, -jnp.inf)
        sel = jnp.zeros((nbp, t), F32)
        for _ in range(n_sel):
            mx = jnp.max(g, axis=0, keepdims=True)
            is_mx = jnp.logical_and(g == mx, mx > -jnp.inf)
            first = jnp.min(jnp.where(is_mx, blk, 2 * nbp), axis=0, keepdims=True)
            pick = blk == first
            sel = jnp.where(pick, 1.0, sel)
            g = jnp.where(pick, -jnp.inf, g)
        sel_sc[sub] = sel

    hrows = HEAD_DIM + ONES_ROWS

    def attend(kb, n, mask_fn):
        off = pl.multiple_of(kb * t, t)
        k = k_ref[0, pl.ds(off, n * t), :]
        vt = vt_ref[0, :, pl.ds(off, n * t)]
        s = [mask_fn(jnp.dot(k, qb[sub], preferred_element_type=F32), sub) for sub in range(2)]
        for sub in range(2):
            _attn_update(s[sub], vt[sub * hrows:(sub + 1) * hrows], m_sc, acc_sc, sub, fast)

    causal = _causal_t(t)
    attend(i, 1, lambda s_t, sub: jnp.where(causal, s_t, NEG))

    def past_blocks(kb, n):
        def row_mask(s_t, sub):
            rows = [jnp.where(sel_sc[sub, pl.ds(kb + c, 1), :] > 0.0, 0.0, NEG) for c in range(n)]
            bias = jnp.concatenate([jnp.broadcast_to(r, (t, t)) for r in rows], axis=0)
            return s_t + bias
        attend(kb, n, row_mask)

    def block_quad(jq, carry):
        past_blocks(4 * jq, 4)
        return carry

    lax.fori_loop(0, i // 4, block_quad, 0)

    @pl.when(i % 4 >= 2)
    def _():
        past_blocks((i // 4) * 4, 2)

    @pl.when(i % 2 == 1)
    def _():
        past_blocks(i - 1, 1)

    o_ref[0] = jnp.concatenate([_attn_out(acc_sc, sub, HEAD_DIM) for sub in range(2)], axis=0)


def moba_attention(q_f32, kmean, k, v, fast_ok):
    bsz, seq, width = q_f32.shape
    t = MOBA_BLOCK
    nb = seq // MOBA_BLOCK
    n_sel = max(1, min(MOBA_TOPK, nb - 1))
    nbp = -(-nb // 8) * 8
    km = jnp.zeros((bsz, nbp, width), F32).at[:, :nb].set(kmean)
    prow = 2 * (HEAD_DIM + ONES_ROWS)

    def call(fast, *args):
        return pl.pallas_call(
            functools.partial(_moba_kernel, t=t, n_sel=n_sel, nbp=nbp, fast=fast),
            grid=(bsz, width // LANES, nb),
            in_specs=[pl.BlockSpec((1, LANES, t), lambda b, h, i: (b, h, i)),
                      pl.BlockSpec((1, nbp, LANES), lambda b, h, i: (b, 0, h)),
                      pl.BlockSpec((1, seq, LANES), lambda b, h, i: (b, 0, h)),
                      pl.BlockSpec((1, prow, seq), lambda b, h, i: (b, h, 0))],
            out_specs=pl.BlockSpec((1, LANES, t), lambda b, h, i: (b, h, i)),
            out_shape=jax.ShapeDtypeStruct((bsz, width, seq), F32),
            scratch_shapes=[pltpu.VMEM((2, nbp, t), F32), pltpu.VMEM((2, 1, t), F32),
                            pltpu.VMEM((2, HEAD_DIM + ONES_ROWS, t), F32)],
            compiler_params=_cparams(("parallel", "parallel", "arbitrary")),
            name="moba_attn_fast" if fast else "moba_attn_safe",
        )(*args)

    args = (_to_t(q_f32), km, k, _vt_aug(v, HEAD_DIM))
    o_t = lax.cond(fast_ok, functools.partial(call, True), functools.partial(call, False), *args)
    return _to_t(o_t)


def s5_operators(lam_re, lam_im, b_re, b_im, c_re, c_im, d_skip, log_dt, chunk):
    n_g, n_p = lam_re.shape
    n_c = d_skip.shape[-1]
    dt = jnp.exp(log_dt.astype(F32))[:, None]
    den = lam_re * lam_re + lam_im * lam_im
    mag = jnp.exp(lam_re * dt)
    abar_re = mag * jnp.cos(lam_im * dt)
    abar_im = mag * jnp.sin(lam_im * dt)
    coef_re = ((abar_re - 1.0) * lam_re + abar_im * lam_im) / den
    coef_im = (abar_im * lam_re - (abar_re - 1.0) * lam_im) / den
    bbar_re = coef_re[..., None] * b_re - coef_im[..., None] * b_im
    bbar_im = coef_re[..., None] * b_im + coef_im[..., None] * b_re
    tau = jnp.arange(chunk + 1, dtype=F32)[:, None, None]
    pmag = jnp.exp(lam_re * dt * tau)
    pw_re = pmag * jnp.cos(lam_im * dt * tau)
    pw_im = pmag * jnp.sin(lam_im * dt * tau)
    ab_re = pw_re[..., None] * bbar_re - pw_im[..., None] * bbar_im
    ab_im = pw_re[..., None] * bbar_im + pw_im[..., None] * bbar_re
    kern = (jnp.einsum('gop,tgpi->tgio', c_re, ab_re, precision=HI)
            - jnp.einsum('gop,tgpi->tgio', c_im, ab_im, precision=HI))
    t_idx = jnp.arange(chunk)
    lag = t_idx[None, :] - t_idx[:, None]
    toep = jnp.where((lag >= 0)[None, :, None, :, None],
                     kern[jnp.clip(lag, 0, chunk)].transpose(2, 0, 3, 1, 4), 0.0)
    eye = (jnp.eye(chunk)[None, :, None, :, None] * jnp.eye(n_c)[None, None, :, None, :])
    toep = toep + eye * d_skip[:, None, :, None, None]
    toep = toep.reshape(n_g, chunk * n_c, chunk * n_c)
    rev = chunk - 1 - t_idx
    w_in = jnp.concatenate([ab_re[rev].transpose(1, 0, 3, 2), ab_im[rev].transpose(1, 0, 3, 2)], axis=-1)
    w_in = w_in.reshape(n_g, chunk * n_c, 2 * n_p)
    a_re, a_im = pw_re[1:chunk + 1], pw_im[1:chunk + 1]
    st_re = (c_re[None] * a_re[:, :, None, :] - c_im[None] * a_im[:, :, None, :])
    st_im = (-c_re[None] * a_im[:, :, None, :] - c_im[None] * a_re[:, :, None, :])
    w_st = jnp.concatenate([st_re.transpose(1, 3, 0, 2), st_im.transpose(1, 3, 0, 2)], axis=1)
    w_st = w_st.reshape(n_g, 2 * n_p, chunk * n_c)
    return toep, w_in, w_st, pw_re[chunk], pw_im[chunk]


def _s5_in_kernel(u_ref, w_ref, o_ref):
    o_ref[0, 0] = jnp.dot(u_ref[0, 0], w_ref[0], precision=HI, preferred_element_type=F32)


def _s5_scan_kernel(x_ref, aa_ref, ab_ref, abs_ref, o_ref, h_sc, hs_sc, *, steps, n_p):
    @pl.when(pl.program_id(0) == 0)
    def _():
        h_sc[...] = jnp.zeros(h_sc.shape, F32)
        hs_sc[...] = jnp.zeros(hs_sc.shape, F32)

    aa, ab, abs_ = aa_ref[...], ab_ref[...], abs_ref[...]

    def body(t, carry):
        h, hs = carry
        o_ref[t] = h
        x = x_ref[t]
        xs = pltpu.roll(x, n_p, 1)
        return aa * h + ab * hs + x, aa * hs + abs_ * h + xs

    h, hs = lax.fori_loop(0, steps, body, (h_sc[...], hs_sc[...]))
    h_sc[...] = h
    hs_sc[...] = hs


def _s5_out_kernel(u_ref, h0_ref, t_ref, w_ref, o_ref):
    y = (jnp.dot(u_ref[0, 0], t_ref[0], precision=HI, preferred_element_type=F32)
         + jnp.dot(h0_ref[0, 0], w_ref[0], precision=HI, preferred_element_type=F32))
    o_ref[0, 0] = jax.nn.gelu(y)


def s5_layer(u, ops):
    toep, w_in, w_st, ac_re, ac_im = ops
    bsz, seq, width = u.shape
    n_g, n_p = ac_re.shape
    n_c = width // n_g
    chunk = S5_CHUNK
    nj = seq // chunk
    row = chunk * n_c
    ur = u.reshape(bsz, nj, chunk, n_g, n_c).transpose(0, 3, 1, 2, 4).reshape(bsz, n_g, nj, row)
    hloc = pl.pallas_call(
        _s5_in_kernel,
        grid=(bsz, n_g),
        in_specs=[pl.BlockSpec((1, 1, nj, row), lambda b, g: (b, g, 0, 0)),
                  pl.BlockSpec((1, row, 2 * n_p), lambda b, g: (g, 0, 0))],
        out_specs=pl.BlockSpec((1, 1, nj, 2 * n_p), lambda b, g: (b, g, 0, 0)),
        out_shape=jax.ShapeDtypeStruct((bsz, n_g, nj, 2 * n_p), F32),
        compiler_params=_cparams(("parallel", "parallel")),
        name="s5_in",
    )(ur, w_in)
    bg = bsz * n_g
    xs = hloc.transpose(2, 0, 1, 3).reshape(nj, bg, 2 * n_p)
    rep = lambda a: jnp.tile(a, (bsz, 1))
    aa = rep(jnp.concatenate([ac_re, ac_re], axis=-1))
    ab = rep(jnp.concatenate([-ac_im, ac_im], axis=-1))
    abs_ = rep(jnp.concatenate([ac_im, -ac_im], axis=-1))
    steps = _pick_tile(nj, (128, 64, 32, 16, 8))
    cst = pl.BlockSpec((bg, 2 * n_p), lambda t: (0, 0))
    h0 = pl.pallas_call(
        functools.partial(_s5_scan_kernel, steps=steps, n_p=n_p),
        grid=(nj // steps,),
        in_specs=[pl.BlockSpec((steps, bg, 2 * n_p), lambda t: (t, 0, 0)), cst, cst, cst],
        out_specs=pl.BlockSpec((steps, bg, 2 * n_p), lambda t: (t, 0, 0)),
        out_shape=jax.ShapeDtypeStruct((nj, bg, 2 * n_p), F32),
        scratch_shapes=[pltpu.VMEM((bg, 2 * n_p), F32), pltpu.VMEM((bg, 2 * n_p), F32)],
        compiler_params=_cparams(("arbitrary",)),
        name="s5_scan",
    )(xs, aa, ab, abs_)
    h0 = h0.reshape(nj, bsz, n_g, 2 * n_p).transpose(1, 2, 0, 3)
    y = pl.pallas_call(
        _s5_out_kernel,
        grid=(bsz, n_g),
        in_specs=[pl.BlockSpec((1, 1, nj, row), lambda b, g: (b, g, 0, 0)),
                  pl.BlockSpec((1, 1, nj, 2 * n_p), lambda b, g: (b, g, 0, 0)),
                  pl.BlockSpec((1, row, row), lambda b, g: (g, 0, 0)),
                  pl.BlockSpec((1, 2 * n_p, row), lambda b, g: (g, 0, 0))],
        out_specs=pl.BlockSpec((1, 1, nj, row), lambda b, g: (b, g, 0, 0)),
        out_shape=jax.ShapeDtypeStruct((bsz, n_g, nj, row), F32),
        compiler_params=_cparams(("parallel", "parallel")),
        name="s5_out",
    )(ur, h0, toep, w_st)
    return y.reshape(bsz, n_g, nj, chunk, n_c).transpose(0, 2, 3, 1, 4).reshape(bsz, seq, width)


def _glu_kernel(y_ref, w_ref, b_ref, o_ref):
    y = y_ref[...]
    gate = jnp.dot(y.astype(BF16), w_ref[...], preferred_element_type=F32) + b_ref[...]
    o_ref[...] = (y * jax.nn.sigmoid(gate)).astype(o_ref.dtype)


def glu(y, w_bf16, b, tm=1024):
    n_tok, width = y.shape
    tm = _pick_tile(n_tok, (tm, 512, 256, 128))
    return pl.pallas_call(
        _glu_kernel,
        grid=(n_tok // tm,),
        in_specs=[pl.BlockSpec((tm, width), lambda i: (i, 0)),
                  pl.BlockSpec((width, width), lambda i: (0, 0)),
                  pl.BlockSpec((1, width), lambda i: (0, 0))],
        out_specs=pl.BlockSpec((tm, width), lambda i: (i, 0)),
        out_shape=jax.ShapeDtypeStruct((n_tok, width), F32),
        compiler_params=_cparams(("parallel",)),
        name="s5_glu",
    )(y, w_bf16, b.astype(F32).reshape(1, width))


def _outproj_kernel(a_ref, b_ref, w_ref, x_ref, g_ref, o_ref, *, half):
    y = (jnp.dot(a_ref[0].astype(BF16), w_ref[:half, :], preferred_element_type=F32)
         + jnp.dot(b_ref[0].astype(BF16), w_ref[half:, :], preferred_element_type=F32))
    o_ref[0] = x_ref[0] + g_ref[0] * y


def out_proj(a, b, w_bf16, x, gate, tm=512):
    bsz, seq, d = x.shape
    half = a.shape[-1]
    tm = _pick_tile(seq, (tm, 256, 128))
    return pl.pallas_call(
        functools.partial(_outproj_kernel, half=half),
        grid=(bsz, seq // tm),
        in_specs=[pl.BlockSpec((1, tm, half), lambda bb, i: (bb, i, 0)),
                  pl.BlockSpec((1, tm, half), lambda bb, i: (bb, i, 0)),
                  pl.BlockSpec((2 * half, d), lambda bb, i: (0, 0)),
                  pl.BlockSpec((1, tm, d), lambda bb, i: (bb, i, 0)),
                  pl.BlockSpec((1, 1, d), lambda bb, i: (bb, 0, 0))],
        out_specs=pl.BlockSpec((1, tm, d), lambda bb, i: (bb, i, 0)),
        out_shape=jax.ShapeDtypeStruct((bsz, seq, d), F32),
        compiler_params=_cparams(("parallel", "parallel")),
        name="out_proj",
    )(a, b, w_bf16, x, gate)


def _router_kernel(x_ref, sc_ref, sh_ref, wr_ref, br_ref, h_ref, r_ref):
    h = _norm_mod(x_ref[0], sc_ref[0], sh_ref[0])
    h_ref[0] = h.astype(BF16)
    logits = jnp.dot(h, wr_ref[...], precision=HI, preferred_element_type=F32) + br_ref[...]
    tm = logits.shape[0]
    lane = lax.broadcasted_iota(I32, (tm, LANES), 1)
    big = 4 * LANES
    coarse = jnp.where(lane < N_GROUPS, logits, -jnp.inf)
    cmax = jnp.max(coarse, axis=1, keepdims=True)
    grp = jnp.min(jnp.where(coarse == cmax, lane, big), axis=1, keepdims=True)
    p_grp = 1.0 / jnp.sum(jnp.exp(coarse - cmax), axis=1, keepdims=True)
    lo = N_GROUPS + grp * EXPERTS_PER_GROUP
    fine = jnp.where(jnp.logical_and(lane >= lo, lane < lo + EXPERTS_PER_GROUP), logits, -jnp.inf)
    v1 = jnp.max(fine, axis=1, keepdims=True)
    i1 = jnp.min(jnp.where(fine == v1, lane, big), axis=1, keepdims=True)
    fine2 = jnp.where(lane == i1, -jnp.inf, fine)
    v2 = jnp.max(fine2, axis=1, keepdims=True)
    i2 = jnp.min(jnp.where(fine2 == v2, lane, big), axis=1, keepdims=True)
    e2 = jnp.exp(v2 - v1)
    wa = p_grp / (1.0 + e2)
    wb = p_grp * e2 / (1.0 + e2)
    ida = (i1 - N_GROUPS).astype(F32)
    idb = (i2 - N_GROUPS).astype(F32)
    r_ref[0] = jnp.where(lane == 0, ida, jnp.where(lane == 1, idb,
                         jnp.where(lane == 2, wa, jnp.where(lane == 3, wb, 0.0))))


def moe_router(x, sc, sh, wg, bg, we, be, tm=512):
    bsz, seq, d = x.shape
    tm = _pick_tile(seq, (tm, 256, 128))
    n_r = N_GROUPS + N_EXPERTS
    wr = jnp.zeros((d, LANES), F32).at[:, :N_GROUPS].set(wg).at[:, N_GROUPS:n_r].set(we)
    br = jnp.zeros((1, LANES), F32).at[0, :N_GROUPS].set(bg).at[0, N_GROUPS:n_r].set(be)
    return pl.pallas_call(
        _router_kernel,
        grid=(bsz, seq // tm),
        in_specs=[pl.BlockSpec((1, tm, d), lambda b, i: (b, i, 0)),
                  pl.BlockSpec((1, 1, d), lambda b, i: (b, 0, 0)),
                  pl.BlockSpec((1, 1, d), lambda b, i: (b, 0, 0)),
                  pl.BlockSpec((d, LANES), lambda b, i: (0, 0)),
                  pl.BlockSpec((1, LANES), lambda b, i: (0, 0))],
        out_specs=[pl.BlockSpec((1, tm, d), lambda b, i: (b, i, 0)),
                   pl.BlockSpec((1, tm, LANES), lambda b, i: (b, i, 0))],
        out_shape=[jax.ShapeDtypeStruct((bsz, seq, d), BF16),
                   jax.ShapeDtypeStruct((bsz, seq, LANES), F32)],
        compiler_params=_cparams(("parallel", "parallel")),
        name="moe_router",
    )(x, sc, sh, wr, br)


MOE_SEG = 256


MOE_PAD = 16


def _pad_rows(count):
    return jnp.floor((count + (MOE_PAD - 1)) * (1.0 / MOE_PAD)) * MOE_PAD


def _moe_kernel(erow_ref, route_ref, h_ref, x_ref, g_ref, w1_ref, w3_ref, w2_ref, o_ref,
                xs_sc, ys_sc, meta_sc, drow_sc, dcol_sc, *, chunk, rows, na):
    e = pl.program_id(1)
    n_seg = chunk // MOE_SEG
    lane1 = lax.broadcasted_iota(I32, (1, LANES), 1)

    def lane_scalar(row, k):
        return jnp.sum(jnp.where(lane1 == k, row, 0.0)).astype(I32)

    @pl.when(e == 0)
    def _():
        si = lax.broadcasted_iota(I32, (MOE_SEG, MOE_SEG), 0)
        sj = lax.broadcasted_iota(I32, (MOE_SEG, MOE_SEG), 1)
        upper = jnp.where(si < sj, 1.0, 0.0).astype(BF16)
        lower = jnp.where(sj < si, 1.0, 0.0).astype(BF16)
        erow = erow_ref[0]
        eid = lax.broadcasted_iota(I32, (N_EXPERTS, chunk), 0)
        sel = [erow[k:k + 1] == eid for k in range(2)]
        member = jnp.where(jnp.logical_or(sel[0], sel[1]), 1.0, 0.0)
        count = jnp.zeros((N_EXPERTS, 1), F32)
        parts = []
        for sgi in range(n_seg):
            seg = member[:, sgi * MOE_SEG:(sgi + 1) * MOE_SEG]
            parts.append(jnp.dot(seg.astype(BF16), upper, preferred_element_type=F32) + count)
            count = count + jnp.sum(seg, axis=1, keepdims=True)
        rank = jnp.concatenate(parts, axis=1)
        padded = _pad_rows(count)
        ei = lax.broadcasted_iota(I32, (N_EXPERTS, N_EXPERTS), 0)
        ej = lax.broadcasted_iota(I32, (N_EXPERTS, N_EXPERTS), 1)
        below = jnp.where(ej < ei, 1.0, 0.0).astype(BF16)
        start = jnp.dot(below, jnp.broadcast_to(padded, (N_EXPERTS, LANES)).astype(BF16),
                        preferred_element_type=F32)
        total = jnp.sum(padded, axis=0, keepdims=True)
        lane_e = lax.broadcasted_iota(I32, (N_EXPERTS, LANES), 1)
        meta_sc[...] = jnp.where(lane_e == 0, start, jnp.where(lane_e == 1, count, jnp.where(lane_e == 2, total, 0.0)))
        dest = start[:, 0:1] + rank
        for k in range(2):
            drow_sc[k] = jnp.sum(jnp.where(sel[k], dest, 0.0), axis=0, keepdims=True)
        route = route_ref[...]
        lane_c = lax.broadcasted_iota(I32, (chunk, LANES), 1)
        lane_f = lane_c.astype(F32)
        sel_c = [route[:, k:k + 1] == lane_f for k in range(2)]
        member_c = jnp.where(jnp.logical_or(sel_c[0], sel_c[1]), 1.0, 0.0)
        count_c = jnp.zeros((1, LANES), F32)
        parts_c = []
        for sgi in range(n_seg):
            seg = member_c[sgi * MOE_SEG:(sgi + 1) * MOE_SEG]
            parts_c.append(jnp.dot(lower, seg.astype(BF16), preferred_element_type=F32) + count_c)
            count_c = count_c + jnp.sum(seg, axis=0, keepdims=True)
        rank_c = jnp.concatenate(parts_c, axis=0)
        li = lax.broadcasted_iota(I32, (LANES, LANES), 0)
        lj = lax.broadcasted_iota(I32, (LANES, LANES), 1)
        before = jnp.where(li < lj, 1.0, 0.0).astype(BF16)
        start_c = jnp.dot(jnp.broadcast_to(_pad_rows(count_c), (8, LANES)).astype(BF16), before,
                          preferred_element_type=F32)[0:1]
        dest_c = start_c + rank_c
        dcols = [jnp.sum(jnp.where(sel_c[k], dest_c, 0.0), axis=1, keepdims=True) for k in range(2)]
        dcol_sc[...] = jnp.where(lane_c == 0, dcols[0], jnp.where(lane_c == 1, dcols[1], 0.0))
        ys_sc[...] = jnp.zeros(ys_sc.shape, BF16)
        xs_sc[...] = jnp.zeros(xs_sc.shape, BF16)
        d0, d1 = drow_sc[0], drow_sc[1]
        n_slabs = (jnp.sum(total).astype(I32) + MOE_SEG - 1) // MOE_SEG

        def sort_slab(s, carry):
            off = pl.multiple_of(s * MOE_SEG, MOE_SEG)
            r = (lax.broadcasted_iota(I32, (MOE_SEG, 1), 0) + off).astype(F32)
            onehot = jnp.where(jnp.logical_or(d0 == r, d1 == r), 1.0, 0.0).astype(BF16)
            xs_sc[pl.ds(off, MOE_SEG), :] = jnp.dot(onehot, h_ref[...], preferred_element_type=F32).astype(BF16)
            return carry

        lax.fori_loop(0, n_slabs, sort_slab, 0)

    meta = meta_sc[pl.ds(e, 1), :]
    first = lane_scalar(meta, 0)
    count_e = lane_scalar(meta, 1)
    padded_e = ((count_e + MOE_PAD - 1) // MOE_PAD) * MOE_PAD

    def block(bi, carry):
        r0 = pl.multiple_of(first + bi * rows, MOE_PAD)
        xb = xs_sc[pl.ds(r0, rows), :]
        a = jnp.dot(xb, w1_ref[0], preferred_element_type=F32)
        hid = (a * jax.nn.sigmoid(a)) * jnp.dot(xb, w3_ref[0], preferred_element_type=F32)
        yb = jnp.dot(hid.astype(BF16), w2_ref[0], preferred_element_type=F32)
        mine = lax.broadcasted_iota(I32, (rows, 1), 0) + bi * rows < padded_e
        ys_sc[pl.ds(r0, rows), :] = jnp.where(mine, yb, ys_sc[pl.ds(r0, rows), :].astype(F32)).astype(BF16)
        return carry

    lax.fori_loop(0, (count_e + rows - 1) // rows, block, 0)

    @pl.when(e == N_EXPERTS - 1)
    def _():
        route = route_ref[...]
        dcol = dcol_sc[...]
        d0, d1 = dcol[:, 0:1], dcol[:, 1:2]
        w0, w1 = route[:, 2:3], route[:, 3:4]
        n_slabs = (lane_scalar(meta, 2) + MOE_SEG - 1) // MOE_SEG
        o_ref[...] = jnp.zeros(o_ref.shape, F32)

        def unsort_slab(s, carry):
            off = pl.multiple_of(s * MOE_SEG, MOE_SEG)
            r = (lax.broadcasted_iota(I32, (1, MOE_SEG), 1) + off).astype(F32)
            comb = (jnp.where(d0 == r, w0, 0.0) + jnp.where(d1 == r, w1, 0.0)).astype(BF16)
            o_ref[...] += jnp.dot(comb, ys_sc[pl.ds(off, MOE_SEG), :], preferred_element_type=F32)
            return carry

        lax.fori_loop(0, n_slabs, unsort_slab, 0)
        o_ref[...] = x_ref[...] + g_ref[0] * o_ref[...]


def moe_experts(h, route, x, gate, w1_bf16, w3_bf16, w2_bf16, chunk=1024, rows=128):
    bsz, seq, d = x.shape
    chunk = _pick_tile(seq, (chunk, 512, 256))
    n_tok = bsz * seq
    n_chunks = n_tok // chunk
    per_b = seq // chunk
    ff = w1_bf16.shape[-1]
    na = -(-(2 * chunk + N_EXPERTS * (MOE_PAD - 1)) // MOE_SEG) * MOE_SEG
    route2 = route.reshape(n_tok, LANES)
    erow = route2[:, :2].astype(I32).T.reshape(2, n_chunks, chunk).transpose(1, 0, 2)
    out = pl.pallas_call(
        functools.partial(_moe_kernel, chunk=chunk, rows=rows, na=na),
        grid=(n_chunks, N_EXPERTS),
        in_specs=[pl.BlockSpec((1, 2, chunk), lambda c, e: (c, 0, 0)),
                  pl.BlockSpec((chunk, LANES), lambda c, e: (c, 0)),
                  pl.BlockSpec((chunk, d), lambda c, e: (c, 0)),
                  pl.BlockSpec((chunk, d), lambda c, e: (c, 0)),
                  pl.BlockSpec((1, 1, d), lambda c, e: (c // per_b, 0, 0)),
                  pl.BlockSpec((1, d, ff), lambda c, e: (e, 0, 0)),
                  pl.BlockSpec((1, d, ff), lambda c, e: (e, 0, 0)),
                  pl.BlockSpec((1, ff, d), lambda c, e: (e, 0, 0))],
        out_specs=pl.BlockSpec((chunk, d), lambda c, e: (c, 0)),
        out_shape=jax.ShapeDtypeStruct((n_tok, d), F32),
        scratch_shapes=[pltpu.VMEM((na + rows, d), BF16), pltpu.VMEM((na + rows, d), BF16),
                        pltpu.VMEM((N_EXPERTS, LANES), F32), pltpu.VMEM((2, 1, chunk), F32),
                        pltpu.VMEM((chunk, LANES), F32)],
        compiler_params=_cparams(("parallel", "arbitrary")),
        name="moe_experts",
    )(erow, route2, h.reshape(n_tok, d), x.reshape(n_tok, d), gate, w1_bf16, w3_bf16, w2_bf16)
    return out.reshape(bsz, seq, d)


def kernel(x, c, ada_w, ada_b, moe_wg, moe_bg, moe_we, moe_be, moe_w1, moe_w3, moe_w2,
           ev_w_in, ev_w_out, a_qn, a_kn, a_lq1, a_lk1, a_lq2, a_lk2, a_subln,
           b_qn, b_kn, b_idx_kn, od_w_in, od_w_out, c_qn, c_kn,
           s5_lam_re, s5_lam_im, s5_b_re, s5_b_im, s5_c_re, s5_c_im, s5_d, s5_log_dt,
           s5_glu_w, s5_glu_b):
    bsz, seq, d = x.shape
    depth = ada_w.shape[0]
    mod = ada_mod(c, ada_w, ada_b)
    rope_h = rope_lane_tables(seq, HEAD_DIM, ROT_HEAD)
    rope_i = rope_lane_tables(seq, IDX_DIM, ROT_IDX)
    qk_scale = HEAD_DIM ** -0.5 * math.log2(math.e)
    half_h, half_i = ROT_HEAD // 2, ROT_IDX // 2
    hw = A_HEADS * 2 * HEAD_DIM

    def pad_cols(w):
        n = w.shape[1]
        npad = -(-n // LANES) * LANES
        return jnp.zeros((w.shape[0], npad), F32).at[:, :n].set(w).astype(BF16)

    for l in range(depth):
        sh1, sc1, g1, sh2, sc2, g2 = [m[:, None, :] for m in jnp.split(mod[l], 6, axis=-1)]
        i = l // 2
        if l % 2 == 0:
            lam_init = 0.8 - 0.6 * math.exp(-0.3 * l)
            z = in_proj(x, sc1, sh1, pad_cols(ev_w_in[i]))
            (aq,) = prep(z, 0, hw, hd=HEAD_DIM, gain=a_qn[i], rope=rope_h, half=half_h, scale=qk_scale)
            (ak,) = prep(z, hw, hw, hd=HEAD_DIM, gain=a_kn[i], rope=rope_h, half=half_h)
            (av,) = prep(z, 2 * hw, hw, hd=HEAD_DIM)
            o_a = diff_attention(aq, ak, av, a_lq1[i], a_lk1[i], a_lq2[i], a_lk2[i], a_subln[i], lam_init,
                                 _logit_bound(a_qn[i], a_kn[i]) <= MAX_FAST_LOGIT)
            (bq,) = prep(z, 3 * hw, hw, hd=HEAD_DIM, gain=b_qn[i], rope=rope_h, half=half_h, scale=qk_scale)
            (bk,) = prep(z, 4 * hw, hw, hd=HEAD_DIM, gain=b_kn[i], rope=rope_h, half=half_h)
            (bv,) = prep(z, 5 * hw, hw, hd=HEAD_DIM)
            iq_off = 6 * hw
            ik_off = iq_off + IDX_HEADS * IDX_DIM
            (iq,) = prep(z, iq_off, IDX_HEADS * IDX_DIM, hd=IDX_DIM, rope=rope_i, half=half_i)
            (ik,) = prep(z, ik_off, LANES, hd=IDX_DIM, gain=b_idx_kn[i], rope=rope_i, half=half_i)
            iw = z[:, :, ik_off + IDX_DIM:ik_off + IDX_DIM + IDX_HEADS]
            o_b = dsa_attention(bq, bk, bv, iq, ik, iw, _logit_bound(b_qn[i], b_kn[i]) <= MAX_FAST_LOGIT)
            x = out_proj(o_a, o_b, ev_w_out[i].astype(BF16), x, g1)
        else:
            z = in_proj(x, sc1, sh1, pad_cols(od_w_in[i]))
            (cq,) = prep(z, 0, hw, hd=HEAD_DIM, gain=c_qn[i], rope=rope_h, half=half_h, scale=qk_scale,
                         out_dtypes=(F32,))
            ck32, ck = prep(z, hw, hw, hd=HEAD_DIM, gain=c_kn[i], rope=rope_h, half=half_h,
                            out_dtypes=(F32, BF16))
            (cv,) = prep(z, 2 * hw, hw, hd=HEAD_DIM)
            o_c = moba_attention(cq, block_means(ck32), ck, cv, _logit_bound(c_qn[i], c_kn[i]) <= MAX_FAST_LOGIT)
            ops = s5_operators(s5_lam_re[i], s5_lam_im[i], s5_b_re[i], s5_b_im[i], s5_c_re[i], s5_c_im[i],
                               s5_d[i], s5_log_dt[i], S5_CHUNK)
            y = s5_layer(z[:, :, 3 * hw:4 * hw], ops)
            y = glu(y.reshape(bsz * seq, hw), s5_glu_w[i].astype(BF16), s5_glu_b[i]).reshape(bsz, seq, hw)
            x = out_proj(o_c, y, od_w_out[i].astype(BF16), x, g1)
        h, route = moe_router(x, sc2, sh2, moe_wg[l], moe_bg[l], moe_we[l], moe_be[l])
        x = moe_experts(h, route, x, g2, moe_w1[l].astype(BF16), moe_w3[l].astype(BF16), moe_w2[l].astype(BF16))
    return x
```

```python
import functools
import math

import jax
import jax.numpy as jnp
from jax import lax
from jax.experimental import pallas as pl
from jax.experimental.pallas import tpu as pltpu

F32 = jnp.float32
BF16 = jnp.bfloat16
I32 = jnp.int32
HI = lax.Precision.HIGHEST

LANES = 128
NEG = -1e30
I32_MIN = -2147483648

HEAD_DIM = 64
A_HEADS = 4
B_HEADS = 8
IDX_HEADS = 8
IDX_DIM = 32
DSA_TOPK = 256
C_HEADS = 8
MOBA_BLOCK = 256
MOBA_TOPK = 3
S5_GROUP_CH = 16
S5_GROUPS = 32
S5_STATE = 64
ROPE_THETA = 500000.0
ROT_HEAD = HEAD_DIM // 4
ROT_IDX = IDX_DIM // 4
N_GROUPS = 4
EXPERTS_PER_GROUP = 8
N_EXPERTS = N_GROUPS * EXPERTS_PER_GROUP
EXPERT_FF = 512
NORM_EPS = 1e-6
S5_CHUNK = 16


def _cparams(sem, vmem_mb=48):
    return pltpu.CompilerParams(dimension_semantics=sem, vmem_limit_bytes=vmem_mb * 1024 * 1024)


def _nt_dot(a, b, precision=None):
    return lax.dot_general(a, b, (((1,), (1,)), ((), ())), precision=precision,
                           preferred_element_type=F32)


def _mod_kernel(c_ref, w_ref, b_ref, o_ref):
    c = c_ref[...]
    ca = c * jax.nn.sigmoid(c)
    o_ref[0] = jnp.dot(ca, w_ref[0], precision=HI, preferred_element_type=F32) + b_ref[0]


def ada_mod(c, ada_w, ada_b):
    depth, d, n6 = ada_w.shape
    bsz = c.shape[0]
    rows = 8
    c_pad = jnp.zeros((rows, d), F32).at[:bsz].set(c)
    out = pl.pallas_call(
        _mod_kernel,
        grid=(depth, n6 // d),
        in_specs=[pl.BlockSpec((rows, d), lambda l, n: (0, 0)),
                  pl.BlockSpec((1, d, d), lambda l, n: (l, 0, n)),
                  pl.BlockSpec((1, 1, d), lambda l, n: (l, 0, n))],
        out_specs=pl.BlockSpec((1, rows, d), lambda l, n: (l, 0, n)),
        out_shape=jax.ShapeDtypeStruct((depth, rows, n6), F32),
        compiler_params=_cparams(("parallel", "parallel")),
        name="ada_mod",
    )(c_pad, ada_w, ada_b.reshape(depth, 1, n6))
    return out[:, :bsz]


def _norm_mod(x, sc, sh):
    y = x * lax.rsqrt(jnp.mean(x * x, axis=-1, keepdims=True) + NORM_EPS)
    return y * (1.0 + sc) + sh


def _inproj_kernel(x_ref, sc_ref, sh_ref, w_ref, o_ref, h_sc):
    @pl.when(pl.program_id(2) == 0)
    def _():
        h_sc[...] = _norm_mod(x_ref[0], sc_ref[0], sh_ref[0]).astype(BF16)

    o_ref[0] = jnp.dot(h_sc[...], w_ref[...], preferred_element_type=F32)


def _pick_tile(n, prefs):
    for p in prefs:
        if n % p == 0:
            return p
    return n


def in_proj(x, sc, sh, w_bf16, tm=512):
    bsz, seq, d = x.shape
    n = w_bf16.shape[1]
    tm = _pick_tile(seq, (tm, 256, 128))
    tn = _pick_tile(n, (1152, 1024, 512, 384, 256, 128))
    return pl.pallas_call(
        _inproj_kernel,
        grid=(bsz, seq // tm, n // tn),
        in_specs=[pl.BlockSpec((1, tm, d), lambda b, i, j: (b, i, 0)),
                  pl.BlockSpec((1, 1, d), lambda b, i, j: (b, 0, 0)),
                  pl.BlockSpec((1, 1, d), lambda b, i, j: (b, 0, 0)),
                  pl.BlockSpec((d, tn), lambda b, i, j: (0, j))],
        out_specs=pl.BlockSpec((1, tm, tn), lambda b, i, j: (b, i, j)),
        out_shape=jax.ShapeDtypeStruct((bsz, seq, n), F32),
        scratch_shapes=[pltpu.VMEM((tm, d), BF16)],
        compiler_params=_cparams(("parallel", "parallel", "arbitrary")),
        name="in_proj",
    )(x, sc, sh, w_bf16)


def rope_lane_tables(seq, hd, rot):
    half = rot // 2
    pos = jnp.arange(seq, dtype=F32)
    inv = ROPE_THETA ** (-(jnp.arange(0, rot, 2, dtype=F32) / rot))
    ang = pos[:, None] * inv[None, :]
    cos, sin = jnp.cos(ang), jnp.sin(ang)
    dl = jnp.arange(LANES) % hd
    first = dl < half
    second = (dl >= half) & (dl < rot)
    idx = jnp.where(first, dl, jnp.where(second, dl - half, 0))
    cos_l = jnp.where(first | second, cos[:, idx], 1.0)
    sa_l = jnp.where(second, sin[:, idx], 0.0)
    sb_l = jnp.where(first, -sin[:, idx], 0.0)
    return cos_l.astype(F32), sa_l.astype(F32), sb_l.astype(F32)


def _prep_out_rows(mode, width):
    if mode == "t":
        return width
    if mode[0] == "aug":
        return (width // mode[1]) * (mode[1] + ONES_ROWS)
    return mode[2] - mode[1]


def _prep_kernel(*refs, hd, half, do_norm, do_rope, scale, outs):
    z_ref, g_ref, gm_ref, cos_ref, sa_ref, sb_ref = refs[:6]
    o_refs = refs[6:6 + len(outs)]
    x = z_ref[0]
    raw = x
    width = x.shape[-1]
    if do_norm:
        sq = x * x
        hi = sq.astype(BF16)
        lo = (sq - hi.astype(F32)).astype(BF16)
        gm = gm_ref[...]
        ms = (jnp.dot(hi, gm, preferred_element_type=F32)
              + jnp.dot(lo, gm, preferred_element_type=F32)) * (1.0 / hd)
        x = x * lax.rsqrt(ms + NORM_EPS) * g_ref[...]
    if do_rope:
        cos, sa, sb = cos_ref[...], sa_ref[...], sb_ref[...]
        cols = []
        for r in range(width // LANES):
            xc = x[:, r * LANES:(r + 1) * LANES]
            cols.append(xc * cos + pltpu.roll(xc, half, 1) * sa + pltpu.roll(xc, LANES - half, 1) * sb)
        x = cols[0] if len(cols) == 1 else jnp.concatenate(cols, axis=-1)
    if scale != 1.0:
        x = x * scale
    tm = x.shape[0]
    for o, (_, mode) in zip(o_refs, outs):
        if mode == "n":
            o[0] = x.astype(o.dtype)
        elif mode == "t":
            o[0] = x.T.astype(o.dtype)
        elif mode[0] == "aug":
            dv = mode[1]
            xt = x.T
            ones = jnp.ones((ONES_ROWS, tm), F32)
            parts = []
            for h in range(width // dv):
                parts += [xt[h * dv:(h + 1) * dv], ones]
            o[0] = jnp.concatenate(parts, axis=0).astype(o.dtype)
        else:
            o[0] = raw.T[mode[1]:mode[2]].astype(o.dtype)


def prep(z, col_off, width, *, hd, gain=None, rope=None, half=0, scale=1.0, outs=((BF16, "n"),), tm=512):
    bsz, seq, _ = z.shape
    tm = _pick_tile(seq, (tm, 256, 128))
    cb = col_off // width
    assert col_off % width == 0 and width % LANES == 0
    do_norm = gain is not None
    do_rope = rope is not None
    g = jnp.tile(gain.astype(F32), width // hd).reshape(1, width) if do_norm else jnp.ones((1, width), F32)
    gi = jnp.arange(width) // hd
    gm = (gi[:, None] == gi[None, :]).astype(BF16)
    if do_rope:
        cos_l, sa_l, sb_l = rope
    else:
        cos_l = sa_l = sb_l = jnp.zeros((seq, LANES), F32)
    out_specs, out_shape = [], []
    for dt, mode in outs:
        if mode == "n":
            out_specs.append(pl.BlockSpec((1, tm, width), lambda b, i: (b, i, 0)))
            out_shape.append(jax.ShapeDtypeStruct((bsz, seq, width), dt))
        else:
            rows = _prep_out_rows(mode, width)
            out_specs.append(pl.BlockSpec((1, rows, tm), lambda b, i: (b, 0, i)))
            out_shape.append(jax.ShapeDtypeStruct((bsz, rows, seq), dt))
    return pl.pallas_call(
        functools.partial(_prep_kernel, hd=hd, half=half, do_norm=do_norm, do_rope=do_rope,
                          scale=scale, outs=tuple(outs)),
        grid=(bsz, seq // tm),
        in_specs=[pl.BlockSpec((1, tm, width), lambda b, i: (b, i, cb)),
                  pl.BlockSpec((1, width), lambda b, i: (0, 0)),
                  pl.BlockSpec((width, width), lambda b, i: (0, 0)),
                  pl.BlockSpec((tm, LANES), lambda b, i: (i, 0)),
                  pl.BlockSpec((tm, LANES), lambda b, i: (i, 0)),
                  pl.BlockSpec((tm, LANES), lambda b, i: (i, 0))],
        out_specs=out_specs,
        out_shape=out_shape,
        compiler_params=_cparams(("parallel", "parallel")),
        name="prep",
    )(z, g, gm, cos_l, sa_l, sb_l)


ATT_TILE = 256


ONES_ROWS = 16
MAX_FAST_LOGIT = 60.0


def _attn_update(s_t, vt_aug, m_ref, acc_ref, idx, fast):
    if fast:
        acc_ref[idx] += jnp.dot(vt_aug, jnp.exp2(s_t).astype(BF16), preferred_element_type=F32)
    else:
        m_prev = m_ref[idx]
        m_new = jnp.maximum(m_prev, jnp.max(s_t, axis=0, keepdims=True))
        p_t = jnp.exp2(s_t - m_new).astype(BF16)
        acc_ref[idx] = (jnp.exp2(m_prev - m_new) * acc_ref[idx]
                        + jnp.dot(vt_aug, p_t, preferred_element_type=F32))
        m_ref[idx] = m_new


def _attn_init(m_sc, acc_sc):
    m_sc[...] = jnp.full(m_sc.shape, NEG, F32)
    acc_sc[...] = jnp.zeros(acc_sc.shape, F32)


def _attn_out(acc_ref, idx, dv):
    acc = acc_ref[idx]
    return acc[:dv] / acc[dv:dv + 1]


def _logit_bound(gain_q, gain_k):
    return (HEAD_DIM ** 0.5 * math.log2(math.e) * 1.02) * jnp.max(jnp.abs(gain_q)) * jnp.max(jnp.abs(gain_k))


def _half_rows(x, sub):
    row = lax.broadcasted_iota(I32, (LANES, 1), 0)
    return jnp.where((row >= sub * HEAD_DIM) & (row < (sub + 1) * HEAD_DIM), x, jnp.zeros_like(x))


def _causal_t(t):
    return lax.broadcasted_iota(I32, (t, t), 0) <= lax.broadcasted_iota(I32, (t, t), 1)


def _diff_attn_kernel(qt_ref, k_ref, vt_ref, lq1_ref, lk1_ref, lq2_ref, lk2_ref, g_ref, o_ref,
                      m_sc, acc_sc, *, t, lam_init, fast):
    i = pl.program_id(2)
    _attn_init(m_sc, acc_sc)
    qt = qt_ref[0]
    qc = [_half_rows(qt, c) for c in range(2)]

    def attend(off, mask_fn):
        k = k_ref[0, pl.ds(off, t), :]
        vt = vt_ref[0, :, pl.ds(off, t)]
        s = [mask_fn(jnp.dot(k, qc[c], preferred_element_type=F32)) for c in range(2)]
        for c in range(2):
            _attn_update(s[c], vt, m_sc, acc_sc, c, fast)

    def past_tile(jt, carry):
        attend(pl.multiple_of(jt * t, t), lambda s_t: s_t)
        return carry

    lax.fori_loop(0, i, past_tile, 0)
    causal = _causal_t(t)
    attend(pl.multiple_of(i * t, t), lambda s_t: jnp.where(causal, s_t, NEG))

    lam = (jnp.exp(jnp.sum(lq1_ref[...] * lk1_ref[...], axis=-1, keepdims=True))
           - jnp.exp(jnp.sum(lq2_ref[...] * lk2_ref[...], axis=-1, keepdims=True)) + lam_init)
    o = _attn_out(acc_sc, 0, LANES) - lam * _attn_out(acc_sc, 1, LANES)
    o = o * lax.rsqrt(jnp.mean(o * o, axis=0, keepdims=True) + NORM_EPS)
    o_ref[0] = o * g_ref[...] * (1.0 - lam_init)


DIFF_TILE = 512


def diff_attention(qt, k, vta, lq1, lk1, lq2, lk2, subln, lam_init, fast_ok):
    bsz, width, seq = qt.shape
    n_h = width // LANES
    t = _pick_tile(seq, (DIFF_TILE, 256, 128))
    rows = LANES + ONES_ROWS
    vec = lambda a: a.astype(F32).reshape(1, -1)
    small = pl.BlockSpec((1, HEAD_DIM), lambda b, h, i: (0, 0))

    def call(fast, *args):
        return pl.pallas_call(
            functools.partial(_diff_attn_kernel, t=t, lam_init=lam_init, fast=fast),
            grid=(bsz, n_h, seq // t),
            in_specs=[pl.BlockSpec((1, LANES, t), lambda b, h, i: (b, h, i)),
                      pl.BlockSpec((1, seq, LANES), lambda b, h, i: (b, 0, h)),
                      pl.BlockSpec((1, rows, seq), lambda b, h, i: (b, h, 0)),
                      small, small, small, small,
                      pl.BlockSpec((LANES, 1), lambda b, h, i: (0, 0))],
            out_specs=pl.BlockSpec((1, LANES, t), lambda b, h, i: (b, h, i)),
            out_shape=jax.ShapeDtypeStruct((bsz, width, seq), F32),
            scratch_shapes=[pltpu.VMEM((2, 1, t), F32), pltpu.VMEM((2, rows, t), F32)],
            compiler_params=_cparams(("parallel", "parallel", "arbitrary")),
            name="diff_attn_fast" if fast else "diff_attn_safe",
        )(*args)

    args = (qt, k, vta, vec(lq1), vec(lk1), vec(lq2), vec(lk2), subln.astype(F32).reshape(-1, 1))
    return lax.cond(fast_ok, functools.partial(call, True), functools.partial(call, False), *args)


def _sortable_key(score):
    bits = pltpu.bitcast(score + 0.0, I32)
    return bits ^ ((bits >> 31) & 0x7FFFFFFF)


def _dsa_kernel(qt_ref, k_ref, vt_ref, qit_ref, ki_ref, wt_ref, o_ref, kb_sc, m_sc, acc_sc,
                *, t, nk, n_sel, fast):
    i = pl.program_id(1)
    pair = pl.program_id(2)
    n_chunks = i + 1
    krow = lax.broadcasted_iota(I32, (t, t), 0)

    def count_where(pred_fn):
        def body(cidx, acc):
            g = jnp.where(pred_fn(kb_sc[cidx], cidx), 1, 0)
            return acc + jnp.sum(g.reshape(t // 8, 8, t), axis=0)
        acc = lax.fori_loop(0, n_chunks, body, jnp.zeros((8, t), I32))
        return jnp.sum(acc, axis=0, keepdims=True)

    @pl.when(pair == 0)
    def _():
        qit = qit_ref[0]
        pad = jnp.zeros((LANES - IDX_DIM, t), BF16)
        q_heads = [jnp.concatenate([qit[h * IDX_DIM:(h + 1) * IDX_DIM], pad], axis=0)
                   for h in range(IDX_HEADS)]
        wt = wt_ref[0]
        w_rows = [wt[h:h + 1] for h in range(IDX_HEADS)]
        qcol = i * t + lax.broadcasted_iota(I32, (1, t), 1)

        def score_chunk(cidx, carry):
            off = pl.multiple_of(cidx * t, t)
            ki = ki_ref[0, pl.ds(off, t), :]
            score = jnp.zeros((t, t), F32)
            for h in range(IDX_HEADS):
                lg = jnp.dot(ki, q_heads[h], preferred_element_type=F32)
                score = score + w_rows[h] * jnp.maximum(lg, 0.0)
            key = _sortable_key(score)
            kb_sc[cidx] = jnp.where(krow + off <= qcol, key, I32_MIN)
            return carry

        lax.fori_loop(0, n_chunks, score_chunk, 0)

        def key_range(cidx, mm):
            kk = kb_sc[cidx]
            lo_c = jnp.min(jnp.where(kk == I32_MIN, 2147483647, kk).reshape(t // 8, 8, t), axis=0)
            hi_c = jnp.max(kk.reshape(t // 8, 8, t), axis=0)
            return jnp.minimum(mm[0], lo_c), jnp.maximum(mm[1], hi_c)

        kmin, kmax = lax.fori_loop(0, n_chunks, key_range, (jnp.full((8, t), 2147483647, I32),
                                                            jnp.full((8, t), I32_MIN, I32)))
        n_vis = qcol + 1
        few = n_vis <= n_sel
        kmin = jnp.min(kmin, axis=0, keepdims=True)
        kmax = jnp.max(kmax, axis=0, keepdims=True)
        span = 1 << 25
        probe = jnp.maximum(jnp.where(kmax > I32_MIN + span, kmax - span, kmin), kmin)
        c_probe = count_where(lambda kk, cidx: kk >= probe)
        probe_ok = c_probe >= n_sel
        lo0 = jnp.where(few, I32_MIN, jnp.where(probe_ok, probe, kmin))
        hi0 = jnp.where(few, I32_MIN + 1, jnp.where(probe_ok, kmax + 1, probe))

        def open_lanes(lo, hi, c_lo):
            return jnp.logical_and(c_lo != n_sel, hi > lo + 1)

        def bisect_cond(st):
            it, lo, hi, c_lo, c_hi = st
            return jnp.logical_and(it < 34, jnp.max(jnp.where(open_lanes(lo, hi, c_lo), 1, 0)) > 0)

        def bisect(st):
            it, lo, hi, c_lo, c_hi = st
            mid = (lo >> 1) + (hi >> 1) + (lo & hi & 1)
            cnt = count_where(lambda kk, cidx: kk >= mid)
            up = jnp.logical_and(open_lanes(lo, hi, c_lo), cnt >= n_sel)
            dn = jnp.logical_and(open_lanes(lo, hi, c_lo), cnt < n_sel)
            return (it + 1, jnp.where(up, mid, lo), jnp.where(dn, mid, hi),
                    jnp.where(up, cnt, c_lo), jnp.where(dn, cnt, c_hi))

        c_lo0 = jnp.where(few, n_sel, jnp.where(probe_ok, c_probe, n_vis))
        c_hi0 = jnp.where(jnp.logical_or(few, probe_ok), 0, c_probe)
        _, thr, _, c_lo, c_hi = lax.while_loop(
            bisect_cond, bisect, (jnp.int32(0), lo0, hi0, c_lo0, c_hi0))

        need = n_sel - c_hi
        all_ties = c_lo == n_sel
        jmax0 = jnp.where(few, -1, jnp.where(all_ties, 2147483647, -1))
        unresolved = jnp.logical_and(jnp.logical_not(few), jnp.logical_not(all_ties))

        def tie_search(_):
            def step(it, lohi):
                lo, hi = lohi
                mid = (lo + hi) >> 1
                cnt = count_where(lambda kk, cidx: jnp.logical_and(kk == thr, krow + cidx * t <= mid))
                ok = cnt >= need
                return jnp.where(ok, lo, mid), jnp.where(ok, mid, hi)
            n_steps = 1 + max(1, (nk * t - 1).bit_length())
            _, hi = lax.fori_loop(0, n_steps, step, (jnp.full((1, t), -1, I32),
                                                     jnp.full((1, t), nk * t - 1, I32)))
            return jnp.where(unresolved, hi, jmax0)

        any_unresolved = jnp.max(jnp.where(unresolved, 1, 0)) > 0
        jmax = lax.cond(any_unresolved, tie_search, lambda _: jmax0, 0)

        def bias_chunk(cidx, carry):
            kk = kb_sc[cidx]
            cut = jnp.where(krow + cidx * t <= jmax, thr - 1, thr)
            bias = jnp.where(kk > cut, 0.0, NEG)
            kb_sc[cidx] = pltpu.bitcast(bias.astype(F32), I32)
            return carry

        lax.fori_loop(0, n_chunks, bias_chunk, 0)

    _attn_init(m_sc, acc_sc)
    qt = qt_ref[0]
    qh = [_half_rows(qt, sub) for sub in range(2)]
    hrows = HEAD_DIM + ONES_ROWS

    def attend(jt, n):
        off = pl.multiple_of(jt * t, t)
        k = k_ref[0, pl.ds(off, n * t), :]
        vt = vt_ref[0, :, pl.ds(off, n * t)]
        bias = jnp.concatenate([pltpu.bitcast(kb_sc[jt + c], F32) for c in range(n)], axis=0)
        s = [jnp.dot(k, qh[sub], preferred_element_type=F32) + bias for sub in range(2)]
        for sub in range(2):
            _attn_update(s[sub], vt[sub * hrows:(sub + 1) * hrows], m_sc, acc_sc, sub, fast)

    def chunk_quad(jq, carry):
        attend(4 * jq, 4)
        return carry

    lax.fori_loop(0, n_chunks // 4, chunk_quad, 0)

    @pl.when(n_chunks % 4 >= 2)
    def _():
        attend((n_chunks // 4) * 4, 2)

    @pl.when(n_chunks % 2 == 1)
    def _():
        attend(n_chunks - 1, 1)

    o_ref[0] = jnp.concatenate([_attn_out(acc_sc, sub, HEAD_DIM) for sub in range(2)], axis=0)


def dsa_attention(qt, k, vta, qit, ki, iwt, fast_ok):
    bsz, width, seq = qt.shape
    t = _pick_tile(seq, (ATT_TILE, 128))
    nq = seq // t
    n_sel = min(DSA_TOPK, seq // 4)
    prow = 2 * (HEAD_DIM + ONES_ROWS)

    def call(fast, *args):
        return pl.pallas_call(
            functools.partial(_dsa_kernel, t=t, nk=nq, n_sel=n_sel, fast=fast),
            grid=(bsz, nq, width // LANES),
            in_specs=[pl.BlockSpec((1, LANES, t), lambda b, i, p: (b, p, i)),
                      pl.BlockSpec((1, seq, LANES), lambda b, i, p: (b, 0, p)),
                      pl.BlockSpec((1, prow, seq), lambda b, i, p: (b, p, 0)),
                      pl.BlockSpec((1, IDX_HEADS * IDX_DIM, t), lambda b, i, p: (b, 0, i)),
                      pl.BlockSpec((1, seq, LANES), lambda b, i, p: (b, 0, 0)),
                      pl.BlockSpec((1, IDX_HEADS, t), lambda b, i, p: (b, 0, i))],
            out_specs=pl.BlockSpec((1, LANES, t), lambda b, i, p: (b, p, i)),
            out_shape=jax.ShapeDtypeStruct((bsz, width, seq), F32),
            scratch_shapes=[pltpu.VMEM((nq, t, t), I32), pltpu.VMEM((2, 1, t), F32),
                            pltpu.VMEM((2, HEAD_DIM + ONES_ROWS, t), F32)],
            compiler_params=_cparams(("parallel", "parallel", "arbitrary"), vmem_mb=56),
            name="dsa_attn_fast" if fast else "dsa_attn_safe",
        )(*args)

    args = (qt, k, vta, qit, ki, iwt)
    return lax.cond(fast_ok, functools.partial(call, True), functools.partial(call, False), *args)


def _kmean_kernel(k_ref, o_ref, *, nbb):
    x = k_ref[0]
    o_ref[0] = jnp.mean(x.reshape(nbb, MOBA_BLOCK, x.shape[-1]), axis=1)


def block_means(k_f32):
    bsz, seq, width = k_f32.shape
    nb = seq // MOBA_BLOCK
    nbb = 8 if nb % 8 == 0 else nb
    return pl.pallas_call(
        functools.partial(_kmean_kernel, nbb=nbb),
        grid=(bsz, nb // nbb),
        in_specs=[pl.BlockSpec((1, nbb * MOBA_BLOCK, width), lambda b, i: (b, i, 0))],
        out_specs=pl.BlockSpec((1, nbb, width), lambda b, i: (b, i, 0)),
        out_shape=jax.ShapeDtypeStruct((bsz, nb, width), F32),
        compiler_params=_cparams(("parallel", "parallel")),
        name="moba_kmean",
    )(k_f32)


def _moba_kernel(qt_ref, km_ref, k_ref, vt_ref, o_ref, sel_sc, m_sc, acc_sc, *, t, n_sel, nbp, fast):
    i = pl.program_id(2)
    blk = lax.broadcasted_iota(I32, (nbp, t), 0)
    _attn_init(m_sc, acc_sc)
    qt = qt_ref[0]
    qh = [_half_rows(qt, sub) for sub in range(2)]
    qb = [x.astype(BF16) for x in qh]

    for sub in range(2):
        gate = jnp.dot(km_ref[0], qh[sub], precision=HI, preferred_element_type=F32)
        g = jnp.where(blk < i, gate, -jnp.inf)
        sel = jnp.zeros((nbp, t), F32)
        for _ in range(n_sel):
            mx = jnp.max(g, axis=0, keepdims=True)
            is_mx = jnp.logical_and(g == mx, mx > -jnp.inf)
            first = jnp.min(jnp.where(is_mx, blk, 2 * nbp), axis=0, keepdims=True)
            pick = blk == first
            sel = jnp.where(pick, 1.0, sel)
            g = jnp.where(pick, -jnp.inf, g)
        sel_sc[sub] = sel

    hrows = HEAD_DIM + ONES_ROWS

    def attend(kb, n, mask_fn):
        off = pl.multiple_of(kb * t, t)
        k = k_ref[0, pl.ds(off, n * t), :]
        vt = vt_ref[0, :, pl.ds(off, n * t)]
        s = [mask_fn(jnp.dot(k, qb[sub], preferred_element_type=F32), sub) for sub in range(2)]
        for sub in range(2):
            _attn_update(s[sub], vt[sub * hrows:(sub + 1) * hrows], m_sc, acc_sc, sub, fast)

    causal = _causal_t(t)
    attend(i, 1, lambda s_t, sub: jnp.where(causal, s_t, NEG))

    def past_blocks(kb, n):
        def row_mask(s_t, sub):
            rows = [jnp.where(sel_sc[sub, pl.ds(kb + c, 1), :] > 0.0, 0.0, NEG) for c in range(n)]
            bias = jnp.concatenate([jnp.broadcast_to(r, (t, t)) for r in rows], axis=0)
            return s_t + bias
        attend(kb, n, row_mask)

    def block_quad(jq, carry):
        past_blocks(4 * jq, 4)
        return carry

    lax.fori_loop(0, i // 4, block_quad, 0)

    @pl.when(i % 4 >= 2)
    def _():
        past_blocks((i // 4) * 4, 2)

    @pl.when(i % 2 == 1)
    def _():
        past_blocks(i - 1, 1)

    o_ref[0] = jnp.concatenate([_attn_out(acc_sc, sub, HEAD_DIM) for sub in range(2)], axis=0)


def moba_attention(qt_f32, kmean, k, vta, fast_ok):
    bsz, width, seq = qt_f32.shape
    t = MOBA_BLOCK
    nb = seq // MOBA_BLOCK
    n_sel = max(1, min(MOBA_TOPK, nb - 1))
    nbp = -(-nb // 8) * 8
    km = jnp.zeros((bsz, nbp, width), F32).at[:, :nb].set(kmean)
    prow = 2 * (HEAD_DIM + ONES_ROWS)

    def call(fast, *args):
        return pl.pallas_call(
            functools.partial(_moba_kernel, t=t, n_sel=n_sel, nbp=nbp, fast=fast),
            grid=(bsz, width // LANES, nb),
            in_specs=[pl.BlockSpec((1, LANES, t), lambda b, h, i: (b, h, i)),
                      pl.BlockSpec((1, nbp, LANES), lambda b, h, i: (b, 0, h)),
                      pl.BlockSpec((1, seq, LANES), lambda b, h, i: (b, 0, h)),
                      pl.BlockSpec((1, prow, seq), lambda b, h, i: (b, h, 0))],
            out_specs=pl.BlockSpec((1, LANES, t), lambda b, h, i: (b, h, i)),
            out_shape=jax.ShapeDtypeStruct((bsz, width, seq), F32),
            scratch_shapes=[pltpu.VMEM((2, nbp, t), F32), pltpu.VMEM((2, 1, t), F32),
                            pltpu.VMEM((2, HEAD_DIM + ONES_ROWS, t), F32)],
            compiler_params=_cparams(("parallel", "parallel", "arbitrary")),
            name="moba_attn_fast" if fast else "moba_attn_safe",
        )(*args)

    args = (qt_f32, km, k, vta)
    return lax.cond(fast_ok, functools.partial(call, True), functools.partial(call, False), *args)


def s5_operators(lam_re, lam_im, b_re, b_im, c_re, c_im, d_skip, log_dt, chunk):
    n_g, n_p = lam_re.shape
    n_c = d_skip.shape[-1]
    dt = jnp.exp(log_dt.astype(F32))[:, None]
    den = lam_re * lam_re + lam_im * lam_im
    mag = jnp.exp(lam_re * dt)
    abar_re = mag * jnp.cos(lam_im * dt)
    abar_im = mag * jnp.sin(lam_im * dt)
    coef_re = ((abar_re - 1.0) * lam_re + abar_im * lam_im) / den
    coef_im = (abar_im * lam_re - (abar_re - 1.0) * lam_im) / den
    bbar_re = coef_re[..., None] * b_re - coef_im[..., None] * b_im
    bbar_im = coef_re[..., None] * b_im + coef_im[..., None] * b_re
    tau = jnp.arange(chunk + 1, dtype=F32)[:, None, None]
    pmag = jnp.exp(lam_re * dt * tau)
    pw_re = pmag * jnp.cos(lam_im * dt * tau)
    pw_im = pmag * jnp.sin(lam_im * dt * tau)
    ab_re = pw_re[..., None] * bbar_re - pw_im[..., None] * bbar_im
    ab_im = pw_re[..., None] * bbar_im + pw_im[..., None] * bbar_re
    kern = (jnp.einsum('gop,tgpi->tgio', c_re, ab_re, precision=HI)
            - jnp.einsum('gop,tgpi->tgio', c_im, ab_im, precision=HI))
    t_idx = jnp.arange(chunk)
    lag = t_idx[None, :] - t_idx[:, None]
    toep = jnp.where((lag >= 0)[None, :, None, :, None],
                     kern[jnp.clip(lag, 0, chunk)].transpose(2, 0, 3, 1, 4), 0.0)
    eye = (jnp.eye(chunk)[None, :, None, :, None] * jnp.eye(n_c)[None, None, :, None, :])
    toep = toep + eye * d_skip[:, None, :, None, None]
    toep = toep.reshape(n_g, chunk * n_c, chunk * n_c)
    rev = chunk - 1 - t_idx
    w_in = jnp.concatenate([ab_re[rev].transpose(1, 0, 3, 2), ab_im[rev].transpose(1, 0, 3, 2)], axis=-1)
    w_in = w_in.reshape(n_g, chunk * n_c, 2 * n_p)
    a_re, a_im = pw_re[1:chunk + 1], pw_im[1:chunk + 1]
    st_re = (c_re[None] * a_re[:, :, None, :] - c_im[None] * a_im[:, :, None, :])
    st_im = (-c_re[None] * a_im[:, :, None, :] - c_im[None] * a_re[:, :, None, :])
    w_st = jnp.concatenate([st_re.transpose(1, 3, 0, 2), st_im.transpose(1, 3, 0, 2)], axis=1)
    w_st = w_st.reshape(n_g, 2 * n_p, chunk * n_c)
    return toep, w_in, w_st, pw_re[chunk], pw_im[chunk]


def _s5_in_kernel(u_ref, w_ref, o_ref):
    o_ref[0, 0] = jnp.dot(u_ref[0, 0], w_ref[0], precision=HI, preferred_element_type=F32)


def _s5_scan_kernel(x_ref, aa_ref, ab_ref, abs_ref, o_ref, h_sc, hs_sc, *, steps, n_p):
    @pl.when(pl.program_id(0) == 0)
    def _():
        h_sc[...] = jnp.zeros(h_sc.shape, F32)
        hs_sc[...] = jnp.zeros(hs_sc.shape, F32)

    aa, ab, abs_ = aa_ref[...], ab_ref[...], abs_ref[...]

    def body(t, carry):
        h, hs = carry
        o_ref[t] = h
        x = x_ref[t]
        xs = pltpu.roll(x, n_p, 1)
        return aa * h + ab * hs + x, aa * hs + abs_ * h + xs

    h, hs = lax.fori_loop(0, steps, body, (h_sc[...], hs_sc[...]))
    h_sc[...] = h
    hs_sc[...] = hs


def _s5_out_kernel(u_ref, h0_ref, t_ref, w_ref, o_ref):
    y = (jnp.dot(u_ref[0, 0], t_ref[0], precision=HI, preferred_element_type=F32)
         + jnp.dot(h0_ref[0, 0], w_ref[0], precision=HI, preferred_element_type=F32))
    o_ref[0, 0] = jax.nn.gelu(y)


def s5_layer(u, ops):
    toep, w_in, w_st, ac_re, ac_im = ops
    bsz, seq, width = u.shape
    n_g, n_p = ac_re.shape
    n_c = width // n_g
    chunk = S5_CHUNK
    nj = seq // chunk
    row = chunk * n_c
    ur = u.reshape(bsz, nj, chunk, n_g, n_c).transpose(0, 3, 1, 2, 4).reshape(bsz, n_g, nj, row)
    hloc = pl.pallas_call(
        _s5_in_kernel,
        grid=(bsz, n_g),
        in_specs=[pl.BlockSpec((1, 1, nj, row), lambda b, g: (b, g, 0, 0)),
                  pl.BlockSpec((1, row, 2 * n_p), lambda b, g: (g, 0, 0))],
        out_specs=pl.BlockSpec((1, 1, nj, 2 * n_p), lambda b, g: (b, g, 0, 0)),
        out_shape=jax.ShapeDtypeStruct((bsz, n_g, nj, 2 * n_p), F32),
        compiler_params=_cparams(("parallel", "parallel")),
        name="s5_in",
    )(ur, w_in)
    bg = bsz * n_g
    xs = hloc.transpose(2, 0, 1, 3).reshape(nj, bg, 2 * n_p)
    rep = lambda a: jnp.tile(a, (bsz, 1))
    aa = rep(jnp.concatenate([ac_re, ac_re], axis=-1))
    ab = rep(jnp.concatenate([-ac_im, ac_im], axis=-1))
    abs_ = rep(jnp.concatenate([ac_im, -ac_im], axis=-1))
    steps = _pick_tile(nj, (128, 64, 32, 16, 8))
    cst = pl.BlockSpec((bg, 2 * n_p), lambda t: (0, 0))
    h0 = pl.pallas_call(
        functools.partial(_s5_scan_kernel, steps=steps, n_p=n_p),
        grid=(nj // steps,),
        in_specs=[pl.BlockSpec((steps, bg, 2 * n_p), lambda t: (t, 0, 0)), cst, cst, cst],
        out_specs=pl.BlockSpec((steps, bg, 2 * n_p), lambda t: (t, 0, 0)),
        out_shape=jax.ShapeDtypeStruct((nj, bg, 2 * n_p), F32),
        scratch_shapes=[pltpu.VMEM((bg, 2 * n_p), F32), pltpu.VMEM((bg, 2 * n_p), F32)],
        compiler_params=_cparams(("arbitrary",)),
        name="s5_scan",
    )(xs, aa, ab, abs_)
    h0 = h0.reshape(nj, bsz, n_g, 2 * n_p).transpose(1, 2, 0, 3)
    y = pl.pallas_call(
        _s5_out_kernel,
        grid=(bsz, n_g),
        in_specs=[pl.BlockSpec((1, 1, nj, row), lambda b, g: (b, g, 0, 0)),
                  pl.BlockSpec((1, 1, nj, 2 * n_p), lambda b, g: (b, g, 0, 0)),
                  pl.BlockSpec((1, row, row), lambda b, g: (g, 0, 0)),
                  pl.BlockSpec((1, 2 * n_p, row), lambda b, g: (g, 0, 0))],
        out_specs=pl.BlockSpec((1, 1, nj, row), lambda b, g: (b, g, 0, 0)),
        out_shape=jax.ShapeDtypeStruct((bsz, n_g, nj, row), F32),
        compiler_params=_cparams(("parallel", "parallel")),
        name="s5_out",
    )(ur, h0, toep, w_st)
    return y.reshape(bsz, n_g, nj, chunk, n_c).transpose(0, 2, 3, 1, 4).reshape(bsz, seq, width)


def _glu_kernel(y_ref, w_ref, b_ref, o_ref):
    y = y_ref[...]
    gate = jnp.dot(y.astype(BF16), w_ref[...], preferred_element_type=F32) + b_ref[...]
    o_ref[...] = (y * jax.nn.sigmoid(gate)).astype(o_ref.dtype)


def glu(y, w_bf16, b, tm=1024):
    n_tok, width = y.shape
    tm = _pick_tile(n_tok, (tm, 512, 256, 128))
    return pl.pallas_call(
        _glu_kernel,
        grid=(n_tok // tm,),
        in_specs=[pl.BlockSpec((tm, width), lambda i: (i, 0)),
                  pl.BlockSpec((width, width), lambda i: (0, 0)),
                  pl.BlockSpec((1, width), lambda i: (0, 0))],
        out_specs=pl.BlockSpec((tm, width), lambda i: (i, 0)),
        out_shape=jax.ShapeDtypeStruct((n_tok, width), F32),
        compiler_params=_cparams(("parallel",)),
        name="s5_glu",
    )(y, w_bf16, b.astype(F32).reshape(1, width))


def _outproj_kernel(a_ref, b_ref, w_ref, x_ref, g_ref, o_ref, *, half, a_t, b_t):
    a = a_ref[0].T if a_t else a_ref[0]
    b = b_ref[0].T if b_t else b_ref[0]
    y = (jnp.dot(a.astype(BF16), w_ref[:half, :], preferred_element_type=F32)
         + jnp.dot(b.astype(BF16), w_ref[half:, :], preferred_element_type=F32))
    o_ref[0] = x_ref[0] + g_ref[0] * y


def out_proj(a, b, w_bf16, x, gate, a_t=False, b_t=False, tm=512):
    bsz, seq, d = x.shape
    half = w_bf16.shape[0] // 2
    tm = _pick_tile(seq, (tm, 256, 128))

    def spec(transposed):
        if transposed:
            return pl.BlockSpec((1, half, tm), lambda bb, i: (bb, 0, i))
        return pl.BlockSpec((1, tm, half), lambda bb, i: (bb, i, 0))

    return pl.pallas_call(
        functools.partial(_outproj_kernel, half=half, a_t=a_t, b_t=b_t),
        grid=(bsz, seq // tm),
        in_specs=[spec(a_t), spec(b_t),
                  pl.BlockSpec((2 * half, d), lambda bb, i: (0, 0)),
                  pl.BlockSpec((1, tm, d), lambda bb, i: (bb, i, 0)),
                  pl.BlockSpec((1, 1, d), lambda bb, i: (bb, 0, 0))],
        out_specs=pl.BlockSpec((1, tm, d), lambda bb, i: (bb, i, 0)),
        out_shape=jax.ShapeDtypeStruct((bsz, seq, d), F32),
        compiler_params=_cparams(("parallel", "parallel")),
        name="out_proj",
    )(a, b, w_bf16, x, gate)


def _router_kernel(x_ref, sc_ref, sh_ref, wr_ref, br_ref, h_ref, r_ref):
    h = _norm_mod(x_ref[0], sc_ref[0], sh_ref[0])
    h_ref[0] = h.astype(BF16)
    logits = jnp.dot(h, wr_ref[...], precision=HI, preferred_element_type=F32) + br_ref[...]
    tm = logits.shape[0]
    lane = lax.broadcasted_iota(I32, (tm, LANES), 1)
    big = 4 * LANES
    coarse = jnp.where(lane < N_GROUPS, logits, -jnp.inf)
    cmax = jnp.max(coarse, axis=1, keepdims=True)
    grp = jnp.min(jnp.where(coarse == cmax, lane, big), axis=1, keepdims=True)
    p_grp = 1.0 / jnp.sum(jnp.exp(coarse - cmax), axis=1, keepdims=True)
    lo = N_GROUPS + grp * EXPERTS_PER_GROUP
    fine = jnp.where(jnp.logical_and(lane >= lo, lane < lo + EXPERTS_PER_GROUP), logits, -jnp.inf)
    v1 = jnp.max(fine, axis=1, keepdims=True)
    i1 = jnp.min(jnp.where(fine == v1, lane, big), axis=1, keepdims=True)
    fine2 = jnp.where(lane == i1, -jnp.inf, fine)
    v2 = jnp.max(fine2, axis=1, keepdims=True)
    i2 = jnp.min(jnp.where(fine2 == v2, lane, big), axis=1, keepdims=True)
    e2 = jnp.exp(v2 - v1)
    wa = p_grp / (1.0 + e2)
    wb = p_grp * e2 / (1.0 + e2)
    ida = (i1 - N_GROUPS).astype(F32)
    idb = (i2 - N_GROUPS).astype(F32)
    r_ref[0] = jnp.where(lane == 0, ida, jnp.where(lane == 1, idb,
                         jnp.where(lane == 2, wa, jnp.where(lane == 3, wb, 0.0))))


def moe_router(x, sc, sh, wg, bg, we, be, tm=512):
    bsz, seq, d = x.shape
    tm = _pick_tile(seq, (tm, 256, 128))
    n_r = N_GROUPS + N_EXPERTS
    wr = jnp.zeros((d, LANES), F32).at[:, :N_GROUPS].set(wg).at[:, N_GROUPS:n_r].set(we)
    br = jnp.zeros((1, LANES), F32).at[0, :N_GROUPS].set(bg).at[0, N_GROUPS:n_r].set(be)
    return pl.pallas_call(
        _router_kernel,
        grid=(bsz, seq // tm),
        in_specs=[pl.BlockSpec((1, tm, d), lambda b, i: (b, i, 0)),
                  pl.BlockSpec((1, 1, d), lambda b, i: (b, 0, 0)),
                  pl.BlockSpec((1, 1, d), lambda b, i: (b, 0, 0)),
                  pl.BlockSpec((d, LANES), lambda b, i: (0, 0)),
                  pl.BlockSpec((1, LANES), lambda b, i: (0, 0))],
        out_specs=[pl.BlockSpec((1, tm, d), lambda b, i: (b, i, 0)),
                   pl.BlockSpec((1, tm, LANES), lambda b, i: (b, i, 0))],
        out_shape=[jax.ShapeDtypeStruct((bsz, seq, d), BF16),
                   jax.ShapeDtypeStruct((bsz, seq, LANES), F32)],
        compiler_params=_cparams(("parallel", "parallel")),
        name="moe_router",
    )(x, sc, sh, wr, br)


MOE_SEG = 256


MOE_PAD = 16


def _pad_rows(count):
    return jnp.floor((count + (MOE_PAD - 1)) * (1.0 / MOE_PAD)) * MOE_PAD


def _moe_kernel(erow_ref, route_ref, h_ref, x_ref, g_ref, w1_ref, w3_ref, w2_ref, o_ref,
                xs_sc, ys_sc, meta_sc, drow_sc, dcol_sc, *, chunk, rows, na):
    e = pl.program_id(1)
    n_seg = chunk // MOE_SEG
    lane1 = lax.broadcasted_iota(I32, (1, LANES), 1)

    def lane_scalar(row, k):
        return jnp.sum(jnp.where(lane1 == k, row, 0.0)).astype(I32)

    @pl.when(e == 0)
    def _():
        si = lax.broadcasted_iota(I32, (MOE_SEG, MOE_SEG), 0)
        sj = lax.broadcasted_iota(I32, (MOE_SEG, MOE_SEG), 1)
        upper = jnp.where(si < sj, 1.0, 0.0).astype(BF16)
        lower = jnp.where(sj < si, 1.0, 0.0).astype(BF16)
        erow = erow_ref[0]
        eid = lax.broadcasted_iota(I32, (N_EXPERTS, chunk), 0)
        sel = [erow[k:k + 1] == eid for k in range(2)]
        member = jnp.where(jnp.logical_or(sel[0], sel[1]), 1.0, 0.0)
        count = jnp.zeros((N_EXPERTS, 1), F32)
        parts = []
        for sgi in range(n_seg):
            seg = member[:, sgi * MOE_SEG:(sgi + 1) * MOE_SEG]
            parts.append(jnp.dot(seg.astype(BF16), upper, preferred_element_type=F32) + count)
            count = count + jnp.sum(seg, axis=1, keepdims=True)
        rank = jnp.concatenate(parts, axis=1)
        padded = _pad_rows(count)
        ei = lax.broadcasted_iota(I32, (N_EXPERTS, N_EXPERTS), 0)
        ej = lax.broadcasted_iota(I32, (N_EXPERTS, N_EXPERTS), 1)
        below = jnp.where(ej < ei, 1.0, 0.0).astype(BF16)
        start = jnp.dot(below, jnp.broadcast_to(padded, (N_EXPERTS, LANES)).astype(BF16),
                        preferred_element_type=F32)
        total = jnp.sum(padded, axis=0, keepdims=True)
        lane_e = lax.broadcasted_iota(I32, (N_EXPERTS, LANES), 1)
        meta_sc[...] = jnp.where(lane_e == 0, start, jnp.where(lane_e == 1, count, jnp.where(lane_e == 2, total, 0.0)))
        dest = start[:, 0:1] + rank
        for k in range(2):
            drow_sc[k] = jnp.sum(jnp.where(sel[k], dest, 0.0), axis=0, keepdims=True)
        route = route_ref[...]
        lane_c = lax.broadcasted_iota(I32, (chunk, LANES), 1)
        lane_f = lane_c.astype(F32)
        sel_c = [route[:, k:k + 1] == lane_f for k in range(2)]
        member_c = jnp.where(jnp.logical_or(sel_c[0], sel_c[1]), 1.0, 0.0)
        count_c = jnp.zeros((1, LANES), F32)
        parts_c = []
        for sgi in range(n_seg):
            seg = member_c[sgi * MOE_SEG:(sgi + 1) * MOE_SEG]
            parts_c.append(jnp.dot(lower, seg.astype(BF16), preferred_element_type=F32) + count_c)
            count_c = count_c + jnp.sum(seg, axis=0, keepdims=True)
        rank_c = jnp.concatenate(parts_c, axis=0)
        li = lax.broadcasted_iota(I32, (LANES, LANES), 0)
        lj = lax.broadcasted_iota(I32, (LANES, LANES), 1)
        before = jnp.where(li < lj, 1.0, 0.0).astype(BF16)
        start_c = jnp.dot(jnp.broadcast_to(_pad_rows(count_c), (8, LANES)).astype(BF16), before,
                          preferred_element_type=F32)[0:1]
        dest_c = start_c + rank_c
        dcols = [jnp.sum(jnp.where(sel_c[k], dest_c, 0.0), axis=1, keepdims=True) for k in range(2)]
        dcol_sc[...] = jnp.where(lane_c == 0, dcols[0], jnp.where(lane_c == 1, dcols[1], 0.0))
        ys_sc[...] = jnp.zeros(ys_sc.shape, BF16)
        xs_sc[...] = jnp.zeros(xs_sc.shape, BF16)
        d0, d1 = drow_sc[0], drow_sc[1]
        n_slabs = (jnp.sum(total).astype(I32) + MOE_SEG - 1) // MOE_SEG

        def sort_slab(s, carry):
            off = pl.multiple_of(s * MOE_SEG, MOE_SEG)
            r = (lax.broadcasted_iota(I32, (MOE_SEG, 1), 0) + off).astype(F32)
            onehot = jnp.where(jnp.logical_or(d0 == r, d1 == r), 1.0, 0.0).astype(BF16)
            xs_sc[pl.ds(off, MOE_SEG), :] = jnp.dot(onehot, h_ref[...], preferred_element_type=F32).astype(BF16)
            return carry

        lax.fori_loop(0, n_slabs, sort_slab, 0)

    meta = meta_sc[pl.ds(e, 1), :]
    first = lane_scalar(meta, 0)
    count_e = lane_scalar(meta, 1)
    padded_e = ((count_e + MOE_PAD - 1) // MOE_PAD) * MOE_PAD

    def block(bi, carry):
        r0 = pl.multiple_of(first + bi * rows, MOE_PAD)
        xb = xs_sc[pl.ds(r0, rows), :]
        a = jnp.dot(xb, w1_ref[0], preferred_element_type=F32)
        hid = (a * jax.nn.sigmoid(a)) * jnp.dot(xb, w3_ref[0], preferred_element_type=F32)
        yb = jnp.dot(hid.astype(BF16), w2_ref[0], preferred_element_type=F32)
        mine = lax.broadcasted_iota(I32, (rows, 1), 0) + bi * rows < padded_e
        ys_sc[pl.ds(r0, rows), :] = jnp.where(mine, yb, ys_sc[pl.ds(r0, rows), :].astype(F32)).astype(BF16)
        return carry

    lax.fori_loop(0, (count_e + rows - 1) // rows, block, 0)

    @pl.when(e == N_EXPERTS - 1)
    def _():
        route = route_ref[...]
        dcol = dcol_sc[...]
        d0, d1 = dcol[:, 0:1], dcol[:, 1:2]
        w0, w1 = route[:, 2:3], route[:, 3:4]
        r = lax.broadcasted_iota(I32, (1, na), 1).astype(F32)
        comb = (jnp.where(d0 == r, w0, 0.0) + jnp.where(d1 == r, w1, 0.0)).astype(BF16)
        moe = jnp.dot(comb, ys_sc[:na, :], preferred_element_type=F32)
        o_ref[...] = x_ref[...] + g_ref[0] * moe


def moe_experts(h, route, x, gate, w1_bf16, w3_bf16, w2_bf16, chunk=1024, rows=128):
    bsz, seq, d = x.shape
    chunk = _pick_tile(seq, (chunk, 512, 256))
    n_tok = bsz * seq
    n_chunks = n_tok // chunk
    per_b = seq // chunk
    ff = w1_bf16.shape[-1]
    na = -(-(2 * chunk + N_EXPERTS * (MOE_PAD - 1)) // MOE_SEG) * MOE_SEG
    route2 = route.reshape(n_tok, LANES)
    erow = route2[:, :2].astype(I32).T.reshape(2, n_chunks, chunk).transpose(1, 0, 2)
    out = pl.pallas_call(
        functools.partial(_moe_kernel, chunk=chunk, rows=rows, na=na),
        grid=(n_chunks, N_EXPERTS),
        in_specs=[pl.BlockSpec((1, 2, chunk), lambda c, e: (c, 0, 0)),
                  pl.BlockSpec((chunk, LANES), lambda c, e: (c, 0)),
                  pl.BlockSpec((chunk, d), lambda c, e: (c, 0)),
                  pl.BlockSpec((chunk, d), lambda c, e: (c, 0)),
                  pl.BlockSpec((1, 1, d), lambda c, e: (c // per_b, 0, 0)),
                  pl.BlockSpec((1, d, ff), lambda c, e: (e, 0, 0)),
                  pl.BlockSpec((1, d, ff), lambda c, e: (e, 0, 0)),
                  pl.BlockSpec((1, ff, d), lambda c, e: (e, 0, 0))],
        out_specs=pl.BlockSpec((chunk, d), lambda c, e: (c, 0)),
        out_shape=jax.ShapeDtypeStruct((n_tok, d), F32),
        scratch_shapes=[pltpu.VMEM((na + rows, d), BF16), pltpu.VMEM((na + rows, d), BF16),
                        pltpu.VMEM((N_EXPERTS, LANES), F32), pltpu.VMEM((2, 1, chunk), F32),
                        pltpu.VMEM((chunk, LANES), F32)],
        compiler_params=_cparams(("parallel", "arbitrary")),
        name="moe_experts",
    )(erow, route2, h.reshape(n_tok, d), x.reshape(n_tok, d), gate, w1_bf16, w3_bf16, w2_bf16)
    return out.reshape(bsz, seq, d)


def kernel(x, c, ada_w, ada_b, moe_wg, moe_bg, moe_we, moe_be, moe_w1, moe_w3, moe_w2,
           ev_w_in, ev_w_out, a_qn, a_kn, a_lq1, a_lk1, a_lq2, a_lk2, a_subln,
           b_qn, b_kn, b_idx_kn, od_w_in, od_w_out, c_qn, c_kn,
           s5_lam_re, s5_lam_im, s5_b_re, s5_b_im, s5_c_re, s5_c_im, s5_d, s5_log_dt,
           s5_glu_w, s5_glu_b):
    bsz, seq, d = x.shape
    depth = ada_w.shape[0]
    mod = ada_mod(c, ada_w, ada_b)
    rope_h = rope_lane_tables(seq, HEAD_DIM, ROT_HEAD)
    rope_i = rope_lane_tables(seq, IDX_DIM, ROT_IDX)
    qk_scale = HEAD_DIM ** -0.5 * math.log2(math.e)
    half_h, half_i = ROT_HEAD // 2, ROT_IDX // 2
    hw = A_HEADS * 2 * HEAD_DIM

    def pad_cols(w):
        n = w.shape[1]
        npad = -(-n // LANES) * LANES
        return jnp.zeros((w.shape[0], npad), F32).at[:, :n].set(w).astype(BF16)

    for l in range(depth):
        sh1, sc1, g1, sh2, sc2, g2 = [m[:, None, :] for m in jnp.split(mod[l], 6, axis=-1)]
        i = l // 2
        if l % 2 == 0:
            lam_init = 0.8 - 0.6 * math.exp(-0.3 * l)
            z = in_proj(x, sc1, sh1, pad_cols(ev_w_in[i]))
            q_t, v_a, v_b = ((BF16, "t"),), ((BF16, ("aug", 2 * HEAD_DIM)),), ((BF16, ("aug", HEAD_DIM)),)
            (aq,) = prep(z, 0, hw, hd=HEAD_DIM, gain=a_qn[i], rope=rope_h, half=half_h, scale=qk_scale, outs=q_t)
            (ak,) = prep(z, hw, hw, hd=HEAD_DIM, gain=a_kn[i], rope=rope_h, half=half_h)
            (av,) = prep(z, 2 * hw, hw, hd=HEAD_DIM, outs=v_a)
            o_a = diff_attention(aq, ak, av, a_lq1[i], a_lk1[i], a_lq2[i], a_lk2[i], a_subln[i], lam_init,
                                 _logit_bound(a_qn[i], a_kn[i]) <= MAX_FAST_LOGIT)
            (bq,) = prep(z, 3 * hw, hw, hd=HEAD_DIM, gain=b_qn[i], rope=rope_h, half=half_h, scale=qk_scale, outs=q_t)
            (bk,) = prep(z, 4 * hw, hw, hd=HEAD_DIM, gain=b_kn[i], rope=rope_h, half=half_h)
            (bv,) = prep(z, 5 * hw, hw, hd=HEAD_DIM, outs=v_b)
            iq_off = 6 * hw
            ik_off = iq_off + IDX_HEADS * IDX_DIM
            (iq,) = prep(z, iq_off, IDX_HEADS * IDX_DIM, hd=IDX_DIM, rope=rope_i, half=half_i, outs=q_t)
            ik, iw = prep(z, ik_off, LANES, hd=IDX_DIM, gain=b_idx_kn[i], rope=rope_i, half=half_i,
                          outs=((BF16, "n"), (F32, ("rawt", IDX_DIM, IDX_DIM + IDX_HEADS))))
            o_b = dsa_attention(bq, bk, bv, iq, ik, iw, _logit_bound(b_qn[i], b_kn[i]) <= MAX_FAST_LOGIT)
            x = out_proj(o_a, o_b, ev_w_out[i].astype(BF16), x, g1, a_t=True, b_t=True)
        else:
            z = in_proj(x, sc1, sh1, pad_cols(od_w_in[i]))
            (cq,) = prep(z, 0, hw, hd=HEAD_DIM, gain=c_qn[i], rope=rope_h, half=half_h, scale=qk_scale,
                         outs=((F32, "t"),))
            ck32, ck = prep(z, hw, hw, hd=HEAD_DIM, gain=c_kn[i], rope=rope_h, half=half_h,
                            outs=((F32, "n"), (BF16, "n")))
            (cv,) = prep(z, 2 * hw, hw, hd=HEAD_DIM, outs=((BF16, ("aug", HEAD_DIM)),))
            o_c = moba_attention(cq, block_means(ck32), ck, cv, _logit_bound(c_qn[i], c_kn[i]) <= MAX_FAST_LOGIT)
            ops = s5_operators(s5_lam_re[i], s5_lam_im[i], s5_b_re[i], s5_b_im[i], s5_c_re[i], s5_c_im[i],
                               s5_d[i], s5_log_dt[i], S5_CHUNK)
            y = s5_layer(z[:, :, 3 * hw:4 * hw], ops)
            y = glu(y.reshape(bsz * seq, hw), s5_glu_w[i].astype(BF16), s5_glu_b[i]).reshape(bsz, seq, hw)
            x = out_proj(o_c, y, od_w_out[i].astype(BF16), x, g1, a_t=True)
        h, route = moe_router(x, sc2, sh2, moe_wg[l], moe_bg[l], moe_we[l], moe_be[l])
        x = moe_experts(h, route, x, g2, moe_w1[l].astype(BF16), moe_w3[l].astype(BF16), moe_w2[l].astype(BF16))
    return x
```

```python
import functools
import math

import jax
import jax.numpy as jnp
from jax import lax
from jax.experimental import pallas as pl
from jax.experimental.pallas import tpu as pltpu

F32 = jnp.float32
BF16 = jnp.bfloat16
I32 = jnp.int32
HI = lax.Precision.HIGHEST

LANES = 128
NEG = -1e30
I32_MIN = -2147483648

HEAD_DIM = 64
A_HEADS = 4
B_HEADS = 8
IDX_HEADS = 8
IDX_DIM = 32
DSA_TOPK = 256
C_HEADS = 8
MOBA_BLOCK = 256
MOBA_TOPK = 3
S5_GROUP_CH = 16
S5_GROUPS = 32
S5_STATE = 64
ROPE_THETA = 500000.0
ROT_HEAD = HEAD_DIM // 4
ROT_IDX = IDX_DIM // 4
N_GROUPS = 4
EXPERTS_PER_GROUP = 8
N_EXPERTS = N_GROUPS * EXPERTS_PER_GROUP
EXPERT_FF = 512
NORM_EPS = 1e-6
S5_CHUNK = 16


def _cparams(sem, vmem_mb=48):
    return pltpu.CompilerParams(dimension_semantics=sem, vmem_limit_bytes=vmem_mb * 1024 * 1024)


def _nt_dot(a, b, precision=None):
    return lax.dot_general(a, b, (((1,), (1,)), ((), ())), precision=precision,
                           preferred_element_type=F32)


def _mod_kernel(c_ref, w_ref, b_ref, o_ref):
    c = c_ref[...]
    ca = c * jax.nn.sigmoid(c)
    o_ref[0] = jnp.dot(ca, w_ref[0], precision=HI, preferred_element_type=F32) + b_ref[0]


def ada_mod(c, ada_w, ada_b):
    depth, d, n6 = ada_w.shape
    bsz = c.shape[0]
    rows = 8
    c_pad = jnp.zeros((rows, d), F32).at[:bsz].set(c)
    out = pl.pallas_call(
        _mod_kernel,
        grid=(depth, n6 // d),
        in_specs=[pl.BlockSpec((rows, d), lambda l, n: (0, 0)),
                  pl.BlockSpec((1, d, d), lambda l, n: (l, 0, n)),
                  pl.BlockSpec((1, 1, d), lambda l, n: (l, 0, n))],
        out_specs=pl.BlockSpec((1, rows, d), lambda l, n: (l, 0, n)),
        out_shape=jax.ShapeDtypeStruct((depth, rows, n6), F32),
        compiler_params=_cparams(("parallel", "parallel")),
        name="ada_mod",
    )(c_pad, ada_w, ada_b.reshape(depth, 1, n6))
    return out[:, :bsz]


def _norm_mod(x, sc, sh):
    y = x * lax.rsqrt(jnp.mean(x * x, axis=-1, keepdims=True) + NORM_EPS)
    return y * (1.0 + sc) + sh


def _inproj_kernel(x_ref, sc_ref, sh_ref, w_ref, o_ref, h_sc):
    @pl.when(pl.program_id(2) == 0)
    def _():
        h_sc[...] = _norm_mod(x_ref[0], sc_ref[0], sh_ref[0]).astype(BF16)

    o_ref[0] = jnp.dot(h_sc[...], w_ref[...], preferred_element_type=F32)


def _pick_tile(n, prefs):
    for p in prefs:
        if n % p == 0:
            return p
    return n


def in_proj(x, sc, sh, w_bf16, tm=512):
    bsz, seq, d = x.shape
    n = w_bf16.shape[1]
    tm = _pick_tile(seq, (tm, 256, 128))
    tn = _pick_tile(n, (1152, 1024, 512, 384, 256, 128))
    return pl.pallas_call(
        _inproj_kernel,
        grid=(bsz, seq // tm, n // tn),
        in_specs=[pl.BlockSpec((1, tm, d), lambda b, i, j: (b, i, 0)),
                  pl.BlockSpec((1, 1, d), lambda b, i, j: (b, 0, 0)),
                  pl.BlockSpec((1, 1, d), lambda b, i, j: (b, 0, 0)),
                  pl.BlockSpec((d, tn), lambda b, i, j: (0, j))],
        out_specs=pl.BlockSpec((1, tm, tn), lambda b, i, j: (b, i, j)),
        out_shape=jax.ShapeDtypeStruct((bsz, seq, n), F32),
        scratch_shapes=[pltpu.VMEM((tm, d), BF16)],
        compiler_params=_cparams(("parallel", "parallel", "arbitrary")),
        name="in_proj",
    )(x, sc, sh, w_bf16)


def rope_lane_tables(seq, hd, rot):
    half = rot // 2
    pos = jnp.arange(seq, dtype=F32)
    inv = ROPE_THETA ** (-(jnp.arange(0, rot, 2, dtype=F32) / rot))
    ang = pos[:, None] * inv[None, :]
    cos, sin = jnp.cos(ang), jnp.sin(ang)
    dl = jnp.arange(LANES) % hd
    first = dl < half
    second = (dl >= half) & (dl < rot)
    idx = jnp.where(first, dl, jnp.where(second, dl - half, 0))
    cos_l = jnp.where(first | second, cos[:, idx], 1.0)
    sa_l = jnp.where(second, sin[:, idx], 0.0)
    sb_l = jnp.where(first, -sin[:, idx], 0.0)
    return cos_l.astype(F32), sa_l.astype(F32), sb_l.astype(F32)


def _prep_out_rows(mode, width):
    if mode == "t":
        return width
    if mode[0] == "aug":
        return (width // mode[1]) * (mode[1] + ONES_ROWS)
    return mode[2] - mode[1]


def _prep_kernel(*refs, hd, half, do_norm, do_rope, scale, outs):
    z_ref, g_ref, gm_ref, cos_ref, sa_ref, sb_ref = refs[:6]
    o_refs = refs[6:6 + len(outs)]
    x = z_ref[0]
    raw = x
    width = x.shape[-1]
    if do_norm:
        sq = x * x
        hi = sq.astype(BF16)
        lo = (sq - hi.astype(F32)).astype(BF16)
        gm = gm_ref[...]
        ms = (jnp.dot(hi, gm, preferred_element_type=F32)
              + jnp.dot(lo, gm, preferred_element_type=F32)) * (1.0 / hd)
        x = x * lax.rsqrt(ms + NORM_EPS) * g_ref[...]
    if do_rope:
        cos, sa, sb = cos_ref[...], sa_ref[...], sb_ref[...]
        cols = []
        for r in range(width // LANES):
            xc = x[:, r * LANES:(r + 1) * LANES]
            cols.append(xc * cos + pltpu.roll(xc, half, 1) * sa + pltpu.roll(xc, LANES - half, 1) * sb)
        x = cols[0] if len(cols) == 1 else jnp.concatenate(cols, axis=-1)
    if scale != 1.0:
        x = x * scale
    tm = x.shape[0]
    for o, (_, mode) in zip(o_refs, outs):
        if mode == "n":
            o[0] = x.astype(o.dtype)
        elif mode == "t":
            o[0] = x.T.astype(o.dtype)
        elif mode[0] == "aug":
            dv = mode[1]
            xt = x.T
            ones = jnp.ones((ONES_ROWS, tm), F32)
            parts = []
            for h in range(width // dv):
                parts += [xt[h * dv:(h + 1) * dv], ones]
            o[0] = jnp.concatenate(parts, axis=0).astype(o.dtype)
        else:
            o[0] = raw.T[mode[1]:mode[2]].astype(o.dtype)


def prep(z, col_off, width, *, hd, gain=None, rope=None, half=0, scale=1.0, outs=((BF16, "n"),), tm=512):
    bsz, seq, _ = z.shape
    tm = _pick_tile(seq, (tm, 256, 128))
    cb = col_off // width
    assert col_off % width == 0 and width % LANES == 0
    do_norm = gain is not None
    do_rope = rope is not None
    g = jnp.tile(gain.astype(F32), width // hd).reshape(1, width) if do_norm else jnp.ones((1, width), F32)
    gi = jnp.arange(width) // hd
    gm = (gi[:, None] == gi[None, :]).astype(BF16)
    if do_rope:
        cos_l, sa_l, sb_l = rope
    else:
        cos_l = sa_l = sb_l = jnp.zeros((seq, LANES), F32)
    out_specs, out_shape = [], []
    for dt, mode in outs:
        if mode == "n":
            out_specs.append(pl.BlockSpec((1, tm, width), lambda b, i: (b, i, 0)))
            out_shape.append(jax.ShapeDtypeStruct((bsz, seq, width), dt))
        else:
            rows = _prep_out_rows(mode, width)
            out_specs.append(pl.BlockSpec((1, rows, tm), lambda b, i: (b, 0, i)))
            out_shape.append(jax.ShapeDtypeStruct((bsz, rows, seq), dt))
    return pl.pallas_call(
        functools.partial(_prep_kernel, hd=hd, half=half, do_norm=do_norm, do_rope=do_rope,
                          scale=scale, outs=tuple(outs)),
        grid=(bsz, seq // tm),
        in_specs=[pl.BlockSpec((1, tm, width), lambda b, i: (b, i, cb)),
                  pl.BlockSpec((1, width), lambda b, i: (0, 0)),
                  pl.BlockSpec((width, width), lambda b, i: (0, 0)),
                  pl.BlockSpec((tm, LANES), lambda b, i: (i, 0)),
                  pl.BlockSpec((tm, LANES), lambda b, i: (i, 0)),
                  pl.BlockSpec((tm, LANES), lambda b, i: (i, 0))],
        out_specs=out_specs,
        out_shape=out_shape,
        compiler_params=_cparams(("parallel", "parallel")),
        name="prep",
    )(z, g, gm, cos_l, sa_l, sb_l)


ATT_TILE = 256


ONES_ROWS = 16
MAX_FAST_LOGIT = 60.0


def _attn_update(s_t, vt_aug, m_ref, acc_ref, idx, fast):
    if fast:
        acc_ref[idx] += jnp.dot(vt_aug, jnp.exp2(s_t).astype(BF16), preferred_element_type=F32)
    else:
        m_prev = m_ref[idx]
        m_new = jnp.maximum(m_prev, jnp.max(s_t, axis=0, keepdims=True))
        p_t = jnp.exp2(s_t - m_new).astype(BF16)
        acc_ref[idx] = (jnp.exp2(m_prev - m_new) * acc_ref[idx]
                        + jnp.dot(vt_aug, p_t, preferred_element_type=F32))
        m_ref[idx] = m_new


def _attn_init(m_sc, acc_sc):
    m_sc[...] = jnp.full(m_sc.shape, NEG, F32)
    acc_sc[...] = jnp.zeros(acc_sc.shape, F32)


def _attn_out(acc_ref, idx, dv):
    acc = acc_ref[idx]
    return acc[:dv] / acc[dv:dv + 1]


def _logit_bound(gain_q, gain_k):
    return (HEAD_DIM ** 0.5 * math.log2(math.e) * 1.02) * jnp.max(jnp.abs(gain_q)) * jnp.max(jnp.abs(gain_k))


def _half_rows(x, sub):
    row = lax.broadcasted_iota(I32, (LANES, 1), 0)
    return jnp.where((row >= sub * HEAD_DIM) & (row < (sub + 1) * HEAD_DIM), x, jnp.zeros_like(x))


def _causal_t(t):
    return lax.broadcasted_iota(I32, (t, t), 0) <= lax.broadcasted_iota(I32, (t, t), 1)


def _diff_attn_kernel(qt_ref, k_ref, vt_ref, lq1_ref, lk1_ref, lq2_ref, lk2_ref, g_ref, o_ref,
                      m_sc, acc_sc, *, t, lam_init, fast):
    i = pl.program_id(2)
    _attn_init(m_sc, acc_sc)
    qt = qt_ref[0]
    qc = [_half_rows(qt, c) for c in range(2)]

    def attend(off, mask_fn):
        k = k_ref[0, pl.ds(off, t), :]
        vt = vt_ref[0, :, pl.ds(off, t)]
        s = [mask_fn(jnp.dot(k, qc[c], preferred_element_type=F32)) for c in range(2)]
        for c in range(2):
            _attn_update(s[c], vt, m_sc, acc_sc, c, fast)

    def past_tile(jt, carry):
        attend(pl.multiple_of(jt * t, t), lambda s_t: s_t)
        return carry

    lax.fori_loop(0, i, past_tile, 0)
    causal = _causal_t(t)
    attend(pl.multiple_of(i * t, t), lambda s_t: jnp.where(causal, s_t, NEG))

    lam = (jnp.exp(jnp.sum(lq1_ref[...] * lk1_ref[...], axis=-1, keepdims=True))
           - jnp.exp(jnp.sum(lq2_ref[...] * lk2_ref[...], axis=-1, keepdims=True)) + lam_init)
    o = _attn_out(acc_sc, 0, LANES) - lam * _attn_out(acc_sc, 1, LANES)
    o = o * lax.rsqrt(jnp.mean(o * o, axis=0, keepdims=True) + NORM_EPS)
    o_ref[0] = o * g_ref[...] * (1.0 - lam_init)


DIFF_TILE = 512


def diff_attention(qt, k, vta, lq1, lk1, lq2, lk2, subln, lam_init, fast_ok):
    bsz, width, seq = qt.shape
    n_h = width // LANES
    t = _pick_tile(seq, (DIFF_TILE, 256, 128))
    rows = LANES + ONES_ROWS
    vec = lambda a: a.astype(F32).reshape(1, -1)
    small = pl.BlockSpec((1, HEAD_DIM), lambda b, h, i: (0, 0))

    def call(fast, *args):
        return pl.pallas_call(
            functools.partial(_diff_attn_kernel, t=t, lam_init=lam_init, fast=fast),
            grid=(bsz, n_h, seq // t),
            in_specs=[pl.BlockSpec((1, LANES, t), lambda b, h, i: (b, h, i)),
                      pl.BlockSpec((1, seq, LANES), lambda b, h, i: (b, 0, h)),
                      pl.BlockSpec((1, rows, seq), lambda b, h, i: (b, h, 0)),
                      small, small, small, small,
                      pl.BlockSpec((LANES, 1), lambda b, h, i: (0, 0))],
            out_specs=pl.BlockSpec((1, LANES, t), lambda b, h, i: (b, h, i)),
            out_shape=jax.ShapeDtypeStruct((bsz, width, seq), F32),
            scratch_shapes=[pltpu.VMEM((2, 1, t), F32), pltpu.VMEM((2, rows, t), F32)],
            compiler_params=_cparams(("parallel", "parallel", "arbitrary")),
            name="diff_attn_fast" if fast else "diff_attn_safe",
        )(*args)

    args = (qt, k, vta, vec(lq1), vec(lk1), vec(lq2), vec(lk2), subln.astype(F32).reshape(-1, 1))
    return lax.cond(fast_ok, functools.partial(call, True), functools.partial(call, False), *args)


def _sortable_key(score):
    bits = pltpu.bitcast(score + 0.0, I32)
    return bits ^ ((bits >> 31) & 0x7FFFFFFF)


def _dsa_kernel(qt_ref, k_ref, vt_ref, qit_ref, ki_ref, wt_ref, o_ref, kb_sc, m_sc, acc_sc,
                *, t, nk, n_sel, fast):
    i = pl.program_id(1)
    pair = pl.program_id(2)
    n_chunks = i + 1
    krow = lax.broadcasted_iota(I32, (t, t), 0)

    def count_where(pred_fn):
        def chunk_count(cidx):
            g = jnp.where(pred_fn(kb_sc[cidx], cidx), 1, 0)
            return jnp.sum(g.reshape(t // 8, 8, t), axis=0)

        def body(jp, acc):
            return acc + chunk_count(2 * jp) + chunk_count(2 * jp + 1)

        acc = lax.fori_loop(0, n_chunks // 2, body, jnp.zeros((8, t), I32))
        last = chunk_count(n_chunks - 1)
        acc = acc + jnp.where(n_chunks % 2 == 1, last, 0)
        return jnp.sum(acc, axis=0, keepdims=True)

    @pl.when(pair == 0)
    def _():
        qit = qit_ref[0]
        pad = jnp.zeros((LANES - IDX_DIM, t), BF16)
        q_heads = [jnp.concatenate([qit[h * IDX_DIM:(h + 1) * IDX_DIM], pad], axis=0)
                   for h in range(IDX_HEADS)]
        wt = wt_ref[0]
        w_rows = [wt[h:h + 1] for h in range(IDX_HEADS)]
        qcol = i * t + lax.broadcasted_iota(I32, (1, t), 1)

        def score_chunk(cidx, mm):
            off = pl.multiple_of(cidx * t, t)
            ki = ki_ref[0, pl.ds(off, t), :]
            score = jnp.zeros((t, t), F32)
            for h in range(IDX_HEADS):
                lg = jnp.dot(ki, q_heads[h], preferred_element_type=F32)
                score = score + w_rows[h] * jnp.maximum(lg, 0.0)
            key = _sortable_key(score)
            visible = krow + off <= qcol
            kb_sc[cidx] = jnp.where(visible, key, I32_MIN)
            lo_c = jnp.min(jnp.where(visible, key, 2147483647).reshape(t // 8, 8, t), axis=0)
            hi_c = jnp.max(jnp.where(visible, key, I32_MIN).reshape(t // 8, 8, t), axis=0)
            return jnp.minimum(mm[0], lo_c), jnp.maximum(mm[1], hi_c)

        kmin, kmax = lax.fori_loop(0, n_chunks, score_chunk, (jnp.full((8, t), 2147483647, I32),
                                                              jnp.full((8, t), I32_MIN, I32)))
        n_vis = qcol + 1
        few = n_vis <= n_sel
        kmin = jnp.min(kmin, axis=0, keepdims=True)
        kmax = jnp.max(kmax, axis=0, keepdims=True)
        span = 1 << 25
        probe = jnp.maximum(jnp.where(kmax > I32_MIN + span, kmax - span, kmin), kmin)
        c_probe = count_where(lambda kk, cidx: kk >= probe)
        probe_ok = c_probe >= n_sel
        lo0 = jnp.where(few, I32_MIN, jnp.where(probe_ok, probe, kmin))
        hi0 = jnp.where(few, I32_MIN + 1, jnp.where(probe_ok, kmax + 1, probe))

        def open_lanes(lo, hi, c_lo):
            return jnp.logical_and(c_lo != n_sel, hi > lo + 1)

        def bisect_cond(st):
            it, lo, hi, c_lo, c_hi = st
            return jnp.logical_and(it < 34, jnp.max(jnp.where(open_lanes(lo, hi, c_lo), 1, 0)) > 0)

        def bisect(st):
            it, lo, hi, c_lo, c_hi = st
            mid = (lo >> 1) + (hi >> 1) + (lo & hi & 1)
            cnt = count_where(lambda kk, cidx: kk >= mid)
            up = jnp.logical_and(open_lanes(lo, hi, c_lo), cnt >= n_sel)
            dn = jnp.logical_and(open_lanes(lo, hi, c_lo), cnt < n_sel)
            return (it + 1, jnp.where(up, mid, lo), jnp.where(dn, mid, hi),
                    jnp.where(up, cnt, c_lo), jnp.where(dn, cnt, c_hi))

        c_lo0 = jnp.where(few, n_sel, jnp.where(probe_ok, c_probe, n_vis))
        c_hi0 = jnp.where(jnp.logical_or(few, probe_ok), 0, c_probe)
        _, thr, _, c_lo, c_hi = lax.while_loop(
            bisect_cond, bisect, (jnp.int32(0), lo0, hi0, c_lo0, c_hi0))

        need = n_sel - c_hi
        all_ties = c_lo == n_sel
        jmax0 = jnp.where(few, -1, jnp.where(all_ties, 2147483647, -1))
        unresolved = jnp.logical_and(jnp.logical_not(few), jnp.logical_not(all_ties))

        def tie_search(_):
            def step(it, lohi):
                lo, hi = lohi
                mid = (lo + hi) >> 1
                cnt = count_where(lambda kk, cidx: jnp.logical_and(kk == thr, krow + cidx * t <= mid))
                ok = cnt >= need
                return jnp.where(ok, lo, mid), jnp.where(ok, mid, hi)
            n_steps = 1 + max(1, (nk * t - 1).bit_length())
            _, hi = lax.fori_loop(0, n_steps, step, (jnp.full((1, t), -1, I32),
                                                     jnp.full((1, t), nk * t - 1, I32)))
            return jnp.where(unresolved, hi, jmax0)

        any_unresolved = jnp.max(jnp.where(unresolved, 1, 0)) > 0
        jmax = lax.cond(any_unresolved, tie_search, lambda _: jmax0, 0)

        def bias_chunk(cidx, carry):
            kk = kb_sc[cidx]
            cut = jnp.where(krow + cidx * t <= jmax, thr - 1, thr)
            bias = jnp.where(kk > cut, 0.0, NEG)
            kb_sc[cidx] = pltpu.bitcast(bias.astype(F32), I32)
            return carry

        lax.fori_loop(0, n_chunks, bias_chunk, 0)

    _attn_init(m_sc, acc_sc)
    qt = qt_ref[0]
    qh = [_half_rows(qt, sub) for sub in range(2)]
    hrows = HEAD_DIM + ONES_ROWS

    def attend(jt, n):
        off = pl.multiple_of(jt * t, t)
        k = k_ref[0, pl.ds(off, n * t), :]
        vt = vt_ref[0, :, pl.ds(off, n * t)]
        bias = jnp.concatenate([pltpu.bitcast(kb_sc[jt + c], F32) for c in range(n)], axis=0)
        s = [jnp.dot(k, qh[sub], preferred_element_type=F32) + bias for sub in range(2)]
        for sub in range(2):
            _attn_update(s[sub], vt[sub * hrows:(sub + 1) * hrows], m_sc, acc_sc, sub, fast)

    def chunk_quad(jq, carry):
        attend(4 * jq, 4)
        return carry

    lax.fori_loop(0, n_chunks // 4, chunk_quad, 0)

    @pl.when(n_chunks % 4 >= 2)
    def _():
        attend((n_chunks // 4) * 4, 2)

    @pl.when(n_chunks % 2 == 1)
    def _():
        attend(n_chunks - 1, 1)

    o_ref[0] = jnp.concatenate([_attn_out(acc_sc, sub, HEAD_DIM) for sub in range(2)], axis=0)


def dsa_attention(qt, k, vta, qit, ki, iwt, fast_ok):
    bsz, width, seq = qt.shape
    t = _pick_tile(seq, (ATT_TILE, 128))
    nq = seq // t
    n_sel = min(DSA_TOPK, seq // 4)
    prow = 2 * (HEAD_DIM + ONES_ROWS)

    def call(fast, *args):
        return pl.pallas_call(
            functools.partial(_dsa_kernel, t=t, nk=nq, n_sel=n_sel, fast=fast),
            grid=(bsz, nq, width // LANES),
            in_specs=[pl.BlockSpec((1, LANES, t), lambda b, i, p: (b, p, i)),
                      pl.BlockSpec((1, seq, LANES), lambda b, i, p: (b, 0, p)),
                      pl.BlockSpec((1, prow, seq), lambda b, i, p: (b, p, 0)),
                      pl.BlockSpec((1, IDX_HEADS * IDX_DIM, t), lambda b, i, p: (b, 0, i)),
                      pl.BlockSpec((1, seq, LANES), lambda b, i, p: (b, 0, 0)),
                      pl.BlockSpec((1, IDX_HEADS, t), lambda b, i, p: (b, 0, i))],
            out_specs=pl.BlockSpec((1, LANES, t), lambda b, i, p: (b, p, i)),
            out_shape=jax.ShapeDtypeStruct((bsz, width, seq), F32),
            scratch_shapes=[pltpu.VMEM((nq, t, t), I32), pltpu.VMEM((2, 1, t), F32),
                            pltpu.VMEM((2, HEAD_DIM + ONES_ROWS, t), F32)],
            compiler_params=_cparams(("parallel", "parallel", "arbitrary"), vmem_mb=56),
            name="dsa_attn_fast" if fast else "dsa_attn_safe",
        )(*args)

    args = (qt, k, vta, qit, ki, iwt)
    return lax.cond(fast_ok, functools.partial(call, True), functools.partial(call, False), *args)


def _kmean_kernel(k_ref, o_ref, *, nbb):
    x = k_ref[0]
    o_ref[0] = jnp.mean(x.reshape(nbb, MOBA_BLOCK, x.shape[-1]), axis=1)


def block_means(k_f32):
    bsz, seq, width = k_f32.shape
    nb = seq // MOBA_BLOCK
    nbb = 8 if nb % 8 == 0 else nb
    return pl.pallas_call(
        functools.partial(_kmean_kernel, nbb=nbb),
        grid=(bsz, nb // nbb),
        in_specs=[pl.BlockSpec((1, nbb * MOBA_BLOCK, width), lambda b, i: (b, i, 0))],
        out_specs=pl.BlockSpec((1, nbb, width), lambda b, i: (b, i, 0)),
        out_shape=jax.ShapeDtypeStruct((bsz, nb, width), F32),
        compiler_params=_cparams(("parallel", "parallel")),
        name="moba_kmean",
    )(k_f32)


def _moba_select_kernel(qt_ref, km_ref, o_ref, *, ts, n_sel, nbp):
    j = pl.program_id(2)
    blk = lax.broadcasted_iota(I32, (nbp, ts), 0)
    own = jnp.right_shift(j * ts + lax.broadcasted_iota(I32, (1, ts), 1), MOBA_BLOCK.bit_length() - 1)
    qt = qt_ref[0]
    for sub in range(2):
        gate = jnp.dot(km_ref[0], _half_rows(qt, sub), precision=HI, preferred_element_type=F32)
        g = jnp.where(blk < own, gate, -jnp.inf)
        sel = jnp.zeros((nbp, ts), F32)
        for _ in range(n_sel):
            mx = jnp.max(g, axis=0, keepdims=True)
            is_mx = jnp.logical_and(g == mx, mx > -jnp.inf)
            first = jnp.min(jnp.where(is_mx, blk, 2 * nbp), axis=0, keepdims=True)
            pick = blk == first
            sel = jnp.where(pick, 1.0, sel)
            g = jnp.where(pick, -jnp.inf, g)
        o_ref[0, sub] = sel


def _moba_kernel(qt_ref, sel_ref, k_ref, vt_ref, o_ref, m_sc, acc_sc, *, t, fast):
    i = pl.program_id(2)
    _attn_init(m_sc, acc_sc)
    qt = qt_ref[0]
    qb = [_half_rows(qt, sub).astype(BF16) for sub in range(2)]
    hrows = HEAD_DIM + ONES_ROWS

    def attend(kb, n, mask_fn):
        off = pl.multiple_of(kb * t, t)
        k = k_ref[0, pl.ds(off, n * t), :]
        vt = vt_ref[0, :, pl.ds(off, n * t)]
        s = [mask_fn(jnp.dot(k, qb[sub], preferred_element_type=F32), sub) for sub in range(2)]
        for sub in range(2):
            _attn_update(s[sub], vt[sub * hrows:(sub + 1) * hrows], m_sc, acc_sc, sub, fast)

    causal = _causal_t(t)
    attend(i, 1, lambda s_t, sub: jnp.where(causal, s_t, NEG))

    def past_blocks(kb, n):
        def row_mask(s_t, sub):
            rows = [jnp.where(sel_ref[0, sub, pl.ds(kb + c, 1), :] > 0.0, 0.0, NEG) for c in range(n)]
            bias = jnp.concatenate([jnp.broadcast_to(r, (t, t)) for r in rows], axis=0)
            return s_t + bias
        attend(kb, n, row_mask)

    def block_quad(jq, carry):
        past_blocks(4 * jq, 4)
        return carry

    lax.fori_loop(0, i // 4, block_quad, 0)

    @pl.when(i % 4 >= 2)
    def _():
        past_blocks((i // 4) * 4, 2)

    @pl.when(i % 2 == 1)
    def _():
        past_blocks(i - 1, 1)

    o_ref[0] = jnp.concatenate([_attn_out(acc_sc, sub, HEAD_DIM) for sub in range(2)], axis=0)


def moba_attention(qt_f32, kmean, k, vta, fast_ok):
    bsz, width, seq = qt_f32.shape
    t = MOBA_BLOCK
    nb = seq // MOBA_BLOCK
    n_sel = max(1, min(MOBA_TOPK, nb - 1))
    nbp = -(-nb // 8) * 8
    km = jnp.zeros((bsz, nbp, width), F32).at[:, :nb].set(kmean)
    prow = 2 * (HEAD_DIM + ONES_ROWS)
    n_pairs = width // LANES
    ts = _pick_tile(seq, (1024, 512, 256))
    sel = pl.pallas_call(
        functools.partial(_moba_select_kernel, ts=ts, n_sel=n_sel, nbp=nbp),
        grid=(bsz, n_pairs, seq // ts),
        in_specs=[pl.BlockSpec((1, LANES, ts), lambda b, h, j: (b, h, j)),
                  pl.BlockSpec((1, nbp, LANES), lambda b, h, j: (b, 0, h))],
        out_specs=pl.BlockSpec((1, 2, nbp, ts), lambda b, h, j: (b, h, 0, j)),
        out_shape=jax.ShapeDtypeStruct((bsz, 2 * n_pairs, nbp, seq), F32),
        compiler_params=_cparams(("parallel", "parallel", "parallel")),
        name="moba_select",
    )(qt_f32, km)

    def call(fast, *args):
        return pl.pallas_call(
            functools.partial(_moba_kernel, t=t, fast=fast),
            grid=(bsz, n_pairs, nb),
            in_specs=[pl.BlockSpec((1, LANES, t), lambda b, h, i: (b, h, i)),
                      pl.BlockSpec((1, 2, nbp, t), lambda b, h, i: (b, h, 0, i)),
                      pl.BlockSpec((1, seq, LANES), lambda b, h, i: (b, 0, h)),
                      pl.BlockSpec((1, prow, seq), lambda b, h, i: (b, h, 0))],
            out_specs=pl.BlockSpec((1, LANES, t), lambda b, h, i: (b, h, i)),
            out_shape=jax.ShapeDtypeStruct((bsz, width, seq), F32),
            scratch_shapes=[pltpu.VMEM((2, 1, t), F32), pltpu.VMEM((2, HEAD_DIM + ONES_ROWS, t), F32)],
            compiler_params=_cparams(("parallel", "parallel", "arbitrary")),
            name="moba_attn_fast" if fast else "moba_attn_safe",
        )(*args)

    args = (qt_f32, sel, k, vta)
    return lax.cond(fast_ok, functools.partial(call, True), functools.partial(call, False), *args)


def s5_operators(lam_re, lam_im, b_re, b_im, c_re, c_im, d_skip, log_dt, chunk):
    n_g, n_p = lam_re.shape
    n_c = d_skip.shape[-1]
    dt = jnp.exp(log_dt.astype(F32))[:, None]
    den = lam_re * lam_re + lam_im * lam_im
    mag = jnp.exp(lam_re * dt)
    abar_re = mag * jnp.cos(lam_im * dt)
    abar_im = mag * jnp.sin(lam_im * dt)
    coef_re = ((abar_re - 1.0) * lam_re + abar_im * lam_im) / den
    coef_im = (abar_im * lam_re - (abar_re - 1.0) * lam_im) / den
    bbar_re = coef_re[..., None] * b_re - coef_im[..., None] * b_im
    bbar_im = coef_re[..., None] * b_im + coef_im[..., None] * b_re
    tau = jnp.arange(chunk + 1, dtype=F32)[:, None, None]
    pmag = jnp.exp(lam_re * dt * tau)
    pw_re = pmag * jnp.cos(lam_im * dt * tau)
    pw_im = pmag * jnp.sin(lam_im * dt * tau)
    ab_re = pw_re[..., None] * bbar_re - pw_im[..., None] * bbar_im
    ab_im = pw_re[..., None] * bbar_im + pw_im[..., None] * bbar_re
    kern = (jnp.einsum('gop,tgpi->tgio', c_re, ab_re, precision=HI)
            - jnp.einsum('gop,tgpi->tgio', c_im, ab_im, precision=HI))
    t_idx = jnp.arange(chunk)
    lag = t_idx[None, :] - t_idx[:, None]
    toep = jnp.where((lag >= 0)[None, :, None, :, None],
                     kern[jnp.clip(lag, 0, chunk)].transpose(2, 0, 3, 1, 4), 0.0)
    eye = (jnp.eye(chunk)[None, :, None, :, None] * jnp.eye(n_c)[None, None, :, None, :])
    toep = toep + eye * d_skip[:, None, :, None, None]
    toep = toep.reshape(n_g, chunk * n_c, chunk * n_c)
    rev = chunk - 1 - t_idx
    w_in = jnp.concatenate([ab_re[rev].transpose(1, 0, 3, 2), ab_im[rev].transpose(1, 0, 3, 2)], axis=-1)
    w_in = w_in.reshape(n_g, chunk * n_c, 2 * n_p)
    a_re, a_im = pw_re[1:chunk + 1], pw_im[1:chunk + 1]
    st_re = (c_re[None] * a_re[:, :, None, :] - c_im[None] * a_im[:, :, None, :])
    st_im = (-c_re[None] * a_im[:, :, None, :] - c_im[None] * a_re[:, :, None, :])
    w_st = jnp.concatenate([st_re.transpose(1, 3, 0, 2), st_im.transpose(1, 3, 0, 2)], axis=1)
    w_st = w_st.reshape(n_g, 2 * n_p, chunk * n_c)
    return toep, w_in, w_st, pw_re[chunk], pw_im[chunk]


def _s5_in_kernel(u_ref, w_ref, o_ref):
    o_ref[0, 0] = jnp.dot(u_ref[0, 0], w_ref[0], precision=HI, preferred_element_type=F32)


def _s5_scan_kernel(x_ref, aa_ref, ab_ref, abs_ref, o_ref, h_sc, hs_sc, *, steps, n_p):
    @pl.when(pl.program_id(0) == 0)
    def _():
        h_sc[...] = jnp.zeros(h_sc.shape, F32)
        hs_sc[...] = jnp.zeros(hs_sc.shape, F32)

    aa, ab, abs_ = aa_ref[...], ab_ref[...], abs_ref[...]

    def body(t, carry):
        h, hs = carry
        o_ref[t] = h
        x = x_ref[t]
        xs = pltpu.roll(x, n_p, 1)
        return aa * h + ab * hs + x, aa * hs + abs_ * h + xs

    h, hs = lax.fori_loop(0, steps, body, (h_sc[...], hs_sc[...]))
    h_sc[...] = h
    hs_sc[...] = hs


def _s5_out_kernel(u_ref, h0_ref, t_ref, w_ref, o_ref):
    y = (jnp.dot(u_ref[0, 0], t_ref[0], precision=HI, preferred_element_type=F32)
         + jnp.dot(h0_ref[0, 0], w_ref[0], precision=HI, preferred_element_type=F32))
    o_ref[0, 0] = jax.nn.gelu(y)


def s5_layer(u, ops):
    toep, w_in, w_st, ac_re, ac_im = ops
    bsz, seq, width = u.shape
    n_g, n_p = ac_re.shape
    n_c = width // n_g
    chunk = S5_CHUNK
    nj = seq // chunk
    row = chunk * n_c
    ur = u.reshape(bsz, nj, chunk, n_g, n_c).transpose(0, 3, 1, 2, 4).reshape(bsz, n_g, nj, row)
    hloc = pl.pallas_call(
        _s5_in_kernel,
        grid=(bsz, n_g),
        in_specs=[pl.BlockSpec((1, 1, nj, row), lambda b, g: (b, g, 0, 0)),
                  pl.BlockSpec((1, row, 2 * n_p), lambda b, g: (g, 0, 0))],
        out_specs=pl.BlockSpec((1, 1, nj, 2 * n_p), lambda b, g: (b, g, 0, 0)),
        out_shape=jax.ShapeDtypeStruct((bsz, n_g, nj, 2 * n_p), F32),
        compiler_params=_cparams(("parallel", "parallel")),
        name="s5_in",
    )(ur, w_in)
    bg = bsz * n_g
    xs = hloc.transpose(2, 0, 1, 3).reshape(nj, bg, 2 * n_p)
    rep = lambda a: jnp.tile(a, (bsz, 1))
    aa = rep(jnp.concatenate([ac_re, ac_re], axis=-1))
    ab = rep(jnp.concatenate([-ac_im, ac_im], axis=-1))
    abs_ = rep(jnp.concatenate([ac_im, -ac_im], axis=-1))
    steps = _pick_tile(nj, (128, 64, 32, 16, 8))
    cst = pl.BlockSpec((bg, 2 * n_p), lambda t: (0, 0))
    h0 = pl.pallas_call(
        functools.partial(_s5_scan_kernel, steps=steps, n_p=n_p),
        grid=(nj // steps,),
        in_specs=[pl.BlockSpec((steps, bg, 2 * n_p), lambda t: (t, 0, 0)), cst, cst, cst],
        out_specs=pl.BlockSpec((steps, bg, 2 * n_p), lambda t: (t, 0, 0)),
        out_shape=jax.ShapeDtypeStruct((nj, bg, 2 * n_p), F32),
        scratch_shapes=[pltpu.VMEM((bg, 2 * n_p), F32), pltpu.VMEM((bg, 2 * n_p), F32)],
        compiler_params=_cparams(("arbitrary",)),
        name="s5_scan",
    )(xs, aa, ab, abs_)
    h0 = h0.reshape(nj, bsz, n_g, 2 * n_p).transpose(1, 2, 0, 3)
    y = pl.pallas_call(
        _s5_out_kernel,
        grid=(bsz, n_g),
        in_specs=[pl.BlockSpec((1, 1, nj, row), lambda b, g: (b, g, 0, 0)),
                  pl.BlockSpec((1, 1, nj, 2 * n_p), lambda b, g: (b, g, 0, 0)),
                  pl.BlockSpec((1, row, row), lambda b, g: (g, 0, 0)),
                  pl.BlockSpec((1, 2 * n_p, row), lambda b, g: (g, 0, 0))],
        out_specs=pl.BlockSpec((1, 1, nj, row), lambda b, g: (b, g, 0, 0)),
        out_shape=jax.ShapeDtypeStruct((bsz, n_g, nj, row), F32),
        compiler_params=_cparams(("parallel", "parallel")),
        name="s5_out",
    )(ur, h0, toep, w_st)
    return y.reshape(bsz, n_g, nj, chunk, n_c).transpose(0, 2, 3, 1, 4).reshape(bsz, seq, width)


def _glu_kernel(y_ref, w_ref, b_ref, o_ref):
    y = y_ref[...]
    gate = jnp.dot(y.astype(BF16), w_ref[...], preferred_element_type=F32) + b_ref[...]
    o_ref[...] = (y * jax.nn.sigmoid(gate)).astype(o_ref.dtype)


def glu(y, w_bf16, b, tm=1024):
    n_tok, width = y.shape
    tm = _pick_tile(n_tok, (tm, 512, 256, 128))
    return pl.pallas_call(
        _glu_kernel,
        grid=(n_tok // tm,),
        in_specs=[pl.BlockSpec((tm, width), lambda i: (i, 0)),
                  pl.BlockSpec((width, width), lambda i: (0, 0)),
                  pl.BlockSpec((1, width), lambda i: (0, 0))],
        out_specs=pl.BlockSpec((tm, width), lambda i: (i, 0)),
        out_shape=jax.ShapeDtypeStruct((n_tok, width), F32),
        compiler_params=_cparams(("parallel",)),
        name="s5_glu",
    )(y, w_bf16, b.astype(F32).reshape(1, width))


def _outproj_kernel(a_ref, b_ref, w_ref, x_ref, g_ref, o_ref, *, half, a_t, b_t):
    a = a_ref[0].T if a_t else a_ref[0]
    b = b_ref[0].T if b_t else b_ref[0]
    y = (jnp.dot(a.astype(BF16), w_ref[:half, :], preferred_element_type=F32)
         + jnp.dot(b.astype(BF16), w_ref[half:, :], preferred_element_type=F32))
    o_ref[0] = x_ref[0] + g_ref[0] * y


def out_proj(a, b, w_bf16, x, gate, a_t=False, b_t=False, tm=512):
    bsz, seq, d = x.shape
    half = w_bf16.shape[0] // 2
    tm = _pick_tile(seq, (tm, 256, 128))

    def spec(transposed):
        if transposed:
            return pl.BlockSpec((1, half, tm), lambda bb, i: (bb, 0, i))
        return pl.BlockSpec((1, tm, half), lambda bb, i: (bb, i, 0))

    return pl.pallas_call(
        functools.partial(_outproj_kernel, half=half, a_t=a_t, b_t=b_t),
        grid=(bsz, seq // tm),
        in_specs=[spec(a_t), spec(b_t),
                  pl.BlockSpec((2 * half, d), lambda bb, i: (0, 0)),
                  pl.BlockSpec((1, tm, d), lambda bb, i: (bb, i, 0)),
                  pl.BlockSpec((1, 1, d), lambda bb, i: (bb, 0, 0))],
        out_specs=pl.BlockSpec((1, tm, d), lambda bb, i: (bb, i, 0)),
        out_shape=jax.ShapeDtypeStruct((bsz, seq, d), F32),
        compiler_params=_cparams(("parallel", "parallel")),
        name="out_proj",
    )(a, b, w_bf16, x, gate)


def _router_kernel(x_ref, sc_ref, sh_ref, wr_ref, br_ref, h_ref, r_ref):
    h = _norm_mod(x_ref[0], sc_ref[0], sh_ref[0])
    h_ref[0] = h.astype(BF16)
    logits = jnp.dot(h, wr_ref[...], precision=HI, preferred_element_type=F32) + br_ref[...]
    tm = logits.shape[0]
    lane = lax.broadcasted_iota(I32, (tm, LANES), 1)
    big = 4 * LANES
    coarse = jnp.where(lane < N_GROUPS, logits, -jnp.inf)
    cmax = jnp.max(coarse, axis=1, keepdims=True)
    grp = jnp.min(jnp.where(coarse == cmax, lane, big), axis=1, keepdims=True)
    p_grp = 1.0 / jnp.sum(jnp.exp(coarse - cmax), axis=1, keepdims=True)
    lo = N_GROUPS + grp * EXPERTS_PER_GROUP
    fine = jnp.where(jnp.logical_and(lane >= lo, lane < lo + EXPERTS_PER_GROUP), logits, -jnp.inf)
    v1 = jnp.max(fine, axis=1, keepdims=True)
    i1 = jnp.min(jnp.where(fine == v1, lane, big), axis=1, keepdims=True)
    fine2 = jnp.where(lane == i1, -jnp.inf, fine)
    v2 = jnp.max(fine2, axis=1, keepdims=True)
    i2 = jnp.min(jnp.where(fine2 == v2, lane, big), axis=1, keepdims=True)
    e2 = jnp.exp(v2 - v1)
    wa = p_grp / (1.0 + e2)
    wb = p_grp * e2 / (1.0 + e2)
    ida = (i1 - N_GROUPS).astype(F32)
    idb = (i2 - N_GROUPS).astype(F32)
    r_ref[0] = jnp.where(lane == 0, ida, jnp.where(lane == 1, idb,
                         jnp.where(lane == 2, wa, jnp.where(lane == 3, wb, 0.0))))


def moe_router(x, sc, sh, wg, bg, we, be, tm=512):
    bsz, seq, d = x.shape
    tm = _pick_tile(seq, (tm, 256, 128))
    n_r = N_GROUPS + N_EXPERTS
    wr = jnp.zeros((d, LANES), F32).at[:, :N_GROUPS].set(wg).at[:, N_GROUPS:n_r].set(we)
    br = jnp.zeros((1, LANES), F32).at[0, :N_GROUPS].set(bg).at[0, N_GROUPS:n_r].set(be)
    return pl.pallas_call(
        _router_kernel,
        grid=(bsz, seq // tm),
        in_specs=[pl.BlockSpec((1, tm, d), lambda b, i: (b, i, 0)),
                  pl.BlockSpec((1, 1, d), lambda b, i: (b, 0, 0)),
                  pl.BlockSpec((1, 1, d), lambda b, i: (b, 0, 0)),
                  pl.BlockSpec((d, LANES), lambda b, i: (0, 0)),
                  pl.BlockSpec((1, LANES), lambda b, i: (0, 0))],
        out_specs=[pl.BlockSpec((1, tm, d), lambda b, i: (b, i, 0)),
                   pl.BlockSpec((1, tm, LANES), lambda b, i: (b, i, 0))],
        out_shape=[jax.ShapeDtypeStruct((bsz, seq, d), BF16),
                   jax.ShapeDtypeStruct((bsz, seq, LANES), F32)],
        compiler_params=_cparams(("parallel", "parallel")),
        name="moe_router",
    )(x, sc, sh, wr, br)


MOE_SEG = 256


MOE_PAD = 16


def _pad_rows(count):
    return jnp.floor((count + (MOE_PAD - 1)) * (1.0 / MOE_PAD)) * MOE_PAD


def _moe_kernel(erow_ref, route_ref, h_ref, x_ref, g_ref, w1_ref, w3_ref, w2_ref, o_ref,
                xs_sc, ys_sc, meta_sc, drow_sc, dcol_sc, *, chunk, rows, na):
    e = pl.program_id(1)
    n_seg = chunk // MOE_SEG
    lane1 = lax.broadcasted_iota(I32, (1, LANES), 1)

    def lane_scalar(row, k):
        return jnp.sum(jnp.where(lane1 == k, row, 0.0)).astype(I32)

    @pl.when(e == 0)
    def _():
        si = lax.broadcasted_iota(I32, (MOE_SEG, MOE_SEG), 0)
        sj = lax.broadcasted_iota(I32, (MOE_SEG, MOE_SEG), 1)
        upper = jnp.where(si < sj, 1.0, 0.0).astype(BF16)
        lower = jnp.where(sj < si, 1.0, 0.0).astype(BF16)
        erow = erow_ref[0]
        eid = lax.broadcasted_iota(I32, (N_EXPERTS, chunk), 0)
        sel = [erow[k:k + 1] == eid for k in range(2)]
        member = jnp.where(jnp.logical_or(sel[0], sel[1]), 1.0, 0.0)
        count = jnp.zeros((N_EXPERTS, 1), F32)
        parts = []
        for sgi in range(n_seg):
            seg = member[:, sgi * MOE_SEG:(sgi + 1) * MOE_SEG]
            parts.append(jnp.dot(seg.astype(BF16), upper, preferred_element_type=F32) + count)
            count = count + jnp.sum(seg, axis=1, keepdims=True)
        rank = jnp.concatenate(parts, axis=1)
        padded = _pad_rows(count)
        ei = lax.broadcasted_iota(I32, (N_EXPERTS, N_EXPERTS), 0)
        ej = lax.broadcasted_iota(I32, (N_EXPERTS, N_EXPERTS), 1)
        below = jnp.where(ej < ei, 1.0, 0.0).astype(BF16)
        start = jnp.dot(below, jnp.broadcast_to(padded, (N_EXPERTS, LANES)).astype(BF16),
                        preferred_element_type=F32)
        total = jnp.sum(padded, axis=0, keepdims=True)
        lane_e = lax.broadcasted_iota(I32, (N_EXPERTS, LANES), 1)
        meta_sc[...] = jnp.where(lane_e == 0, start, jnp.where(lane_e == 1, count, jnp.where(lane_e == 2, total, 0.0)))
        dest = start[:, 0:1] + rank
        for k in range(2):
            drow_sc[k] = jnp.sum(jnp.where(sel[k], dest, 0.0), axis=0, keepdims=True)
        route = route_ref[...]
        lane_c = lax.broadcasted_iota(I32, (chunk, LANES), 1)
        lane_f = lane_c.astype(F32)
        sel_c = [route[:, k:k + 1] == lane_f for k in range(2)]
        member_c = jnp.where(jnp.logical_or(sel_c[0], sel_c[1]), 1.0, 0.0)
        count_c = jnp.zeros((1, LANES), F32)
        parts_c = []
        for sgi in range(n_seg):
            seg = member_c[sgi * MOE_SEG:(sgi + 1) * MOE_SEG]
            parts_c.append(jnp.dot(lower, seg.astype(BF16), preferred_element_type=F32) + count_c)
            count_c = count_c + jnp.sum(seg, axis=0, keepdims=True)
        rank_c = jnp.concatenate(parts_c, axis=0)
        li = lax.broadcasted_iota(I32, (LANES, LANES), 0)
        lj = lax.broadcasted_iota(I32, (LANES, LANES), 1)
        before = jnp.where(li < lj, 1.0, 0.0).astype(BF16)
        start_c = jnp.dot(jnp.broadcast_to(_pad_rows(count_c), (8, LANES)).astype(BF16), before,
                          preferred_element_type=F32)[0:1]
        dest_c = start_c + rank_c
        dcols = [jnp.sum(jnp.where(sel_c[k], dest_c, 0.0), axis=1, keepdims=True) for k in range(2)]
        dcol_sc[...] = jnp.where(lane_c == 0, dcols[0], jnp.where(lane_c == 1, dcols[1], 0.0))
        ys_sc[...] = jnp.zeros(ys_sc.shape, BF16)
        xs_sc[...] = jnp.zeros(xs_sc.shape, BF16)
        d0, d1 = drow_sc[0], drow_sc[1]
        n_slabs = (jnp.sum(total).astype(I32) + MOE_SEG - 1) // MOE_SEG

        def sort_slab(s, carry):
            off = pl.multiple_of(s * MOE_SEG, MOE_SEG)
            r = (lax.broadcasted_iota(I32, (MOE_SEG, 1), 0) + off).astype(F32)
            onehot = jnp.where(jnp.logical_or(d0 == r, d1 == r), 1.0, 0.0).astype(BF16)
            xs_sc[pl.ds(off, MOE_SEG), :] = jnp.dot(onehot, h_ref[...], preferred_element_type=F32).astype(BF16)
            return carry

        lax.fori_loop(0, n_slabs, sort_slab, 0)

    meta = meta_sc[pl.ds(e, 1), :]
    first = lane_scalar(meta, 0)
    count_e = lane_scalar(meta, 1)
    padded_e = ((count_e + MOE_PAD - 1) // MOE_PAD) * MOE_PAD

    def block(bi, carry):
        r0 = pl.multiple_of(first + bi * rows, MOE_PAD)
        xb = xs_sc[pl.ds(r0, rows), :]
        a = jnp.dot(xb, w1_ref[0], preferred_element_type=F32)
        hid = (a * jax.nn.sigmoid(a)) * jnp.dot(xb, w3_ref[0], preferred_element_type=F32)
        yb = jnp.dot(hid.astype(BF16), w2_ref[0], preferred_element_type=F32)
        mine = lax.broadcasted_iota(I32, (rows, 1), 0) + bi * rows < padded_e
        ys_sc[pl.ds(r0, rows), :] = jnp.where(mine, yb, ys_sc[pl.ds(r0, rows), :].astype(F32)).astype(BF16)
        return carry

    lax.fori_loop(0, (count_e + rows - 1) // rows, block, 0)

    @pl.when(e == N_EXPERTS - 1)
    def _():
        route = route_ref[...]
        dcol = dcol_sc[...]
        d0, d1 = dcol[:, 0:1], dcol[:, 1:2]
        w0, w1 = route[:, 2:3], route[:, 3:4]
        r = lax.broadcasted_iota(I32, (1, na), 1).astype(F32)
        comb = (jnp.where(d0 == r, w0, 0.0) + jnp.where(d1 == r, w1, 0.0)).astype(BF16)
        moe = jnp.dot(comb, ys_sc[:na, :], preferred_element_type=F32)
        o_ref[...] = x_ref[...] + g_ref[0] * moe


def moe_experts(h, route, x, gate, w1_bf16, w3_bf16, w2_bf16, chunk=1024, rows=128):
    bsz, seq, d = x.shape
    chunk = _pick_tile(seq, (chunk, 512, 256))
    n_tok = bsz * seq
    n_chunks = n_tok // chunk
    per_b = seq // chunk
    ff = w1_bf16.shape[-1]
    na = -(-(2 * chunk + N_EXPERTS * (MOE_PAD - 1)) // MOE_SEG) * MOE_SEG
    route2 = route.reshape(n_tok, LANES)
    erow = route2[:, :2].astype(I32).T.reshape(2, n_chunks, chunk).transpose(1, 0, 2)
    out = pl.pallas_call(
        functools.partial(_moe_kernel, chunk=chunk, rows=rows, na=na),
        grid=(n_chunks, N_EXPERTS),
        in_specs=[pl.BlockSpec((1, 2, chunk), lambda c, e: (c, 0, 0)),
                  pl.BlockSpec((chunk, LANES), lambda c, e: (c, 0)),
                  pl.BlockSpec((chunk, d), lambda c, e: (c, 0)),
                  pl.BlockSpec((chunk, d), lambda c, e: (c, 0)),
                  pl.BlockSpec((1, 1, d), lambda c, e: (c // per_b, 0, 0)),
                  pl.BlockSpec((1, d, ff), lambda c, e: (e, 0, 0)),
                  pl.BlockSpec((1, d, ff), lambda c, e: (e, 0, 0)),
                  pl.BlockSpec((1, ff, d), lambda c, e: (e, 0, 0))],
        out_specs=pl.BlockSpec((chunk, d), lambda c, e: (c, 0)),
        out_shape=jax.ShapeDtypeStruct((n_tok, d), F32),
        scratch_shapes=[pltpu.VMEM((na + rows, d), BF16), pltpu.VMEM((na + rows, d), BF16),
                        pltpu.VMEM((N_EXPERTS, LANES), F32), pltpu.VMEM((2, 1, chunk), F32),
                        pltpu.VMEM((chunk, LANES), F32)],
        compiler_params=_cparams(("parallel", "arbitrary")),
        name="moe_experts",
    )(erow, route2, h.reshape(n_tok, d), x.reshape(n_tok, d), gate, w1_bf16, w3_bf16, w2_bf16)
    return out.reshape(bsz, seq, d)


def kernel(x, c, ada_w, ada_b, moe_wg, moe_bg, moe_we, moe_be, moe_w1, moe_w3, moe_w2,
           ev_w_in, ev_w_out, a_qn, a_kn, a_lq1, a_lk1, a_lq2, a_lk2, a_subln,
           b_qn, b_kn, b_idx_kn, od_w_in, od_w_out, c_qn, c_kn,
           s5_lam_re, s5_lam_im, s5_b_re, s5_b_im, s5_c_re, s5_c_im, s5_d, s5_log_dt,
           s5_glu_w, s5_glu_b):
    bsz, seq, d = x.shape
    depth = ada_w.shape[0]
    mod = ada_mod(c, ada_w, ada_b)
    rope_h = rope_lane_tables(seq, HEAD_DIM, ROT_HEAD)
    rope_i = rope_lane_tables(seq, IDX_DIM, ROT_IDX)
    qk_scale = HEAD_DIM ** -0.5 * math.log2(math.e)
    half_h, half_i = ROT_HEAD // 2, ROT_IDX // 2
    hw = A_HEADS * 2 * HEAD_DIM

    def pad_cols(w):
        n = w.shape[1]
        npad = -(-n // LANES) * LANES
        return jnp.zeros((w.shape[0], npad), F32).at[:, :n].set(w).astype(BF16)

    for l in range(depth):
        sh1, sc1, g1, sh2, sc2, g2 = [m[:, None, :] for m in jnp.split(mod[l], 6, axis=-1)]
        i = l // 2
        if l % 2 == 0:
            lam_init = 0.8 - 0.6 * math.exp(-0.3 * l)
            z = in_proj(x, sc1, sh1, pad_cols(ev_w_in[i]))
            q_t, v_a, v_b = ((BF16, "t"),), ((BF16, ("aug", 2 * HEAD_DIM)),), ((BF16, ("aug", HEAD_DIM)),)
            (aq,) = prep(z, 0, hw, hd=HEAD_DIM, gain=a_qn[i], rope=rope_h, half=half_h, scale=qk_scale, outs=q_t)
            (ak,) = prep(z, hw, hw, hd=HEAD_DIM, gain=a_kn[i], rope=rope_h, half=half_h)
            (av,) = prep(z, 2 * hw, hw, hd=HEAD_DIM, outs=v_a)
            o_a = diff_attention(aq, ak, av, a_lq1[i], a_lk1[i], a_lq2[i], a_lk2[i], a_subln[i], lam_init,
                                 _logit_bound(a_qn[i], a_kn[i]) <= MAX_FAST_LOGIT)
            (bq,) = prep(z, 3 * hw, hw, hd=HEAD_DIM, gain=b_qn[i], rope=rope_h, half=half_h, scale=qk_scale, outs=q_t)
            (bk,) = prep(z, 4 * hw, hw, hd=HEAD_DIM, gain=b_kn[i], rope=rope_h, half=half_h)
            (bv,) = prep(z, 5 * hw, hw, hd=HEAD_DIM, outs=v_b)
            iq_off = 6 * hw
            ik_off = iq_off + IDX_HEADS * IDX_DIM
            (iq,) = prep(z, iq_off, IDX_HEADS * IDX_DIM, hd=IDX_DIM, rope=rope_i, half=half_i, outs=q_t)
            ik, iw = prep(z, ik_off, LANES, hd=IDX_DIM, gain=b_idx_kn[i], rope=rope_i, half=half_i,
                          outs=((BF16, "n"), (F32, ("rawt", IDX_DIM, IDX_DIM + IDX_HEADS))))
            o_b = dsa_attention(bq, bk, bv, iq, ik, iw, _logit_bound(b_qn[i], b_kn[i]) <= MAX_FAST_LOGIT)
            x = out_proj(o_a, o_b, ev_w_out[i].astype(BF16), x, g1, a_t=True, b_t=True)
        else:
            z = in_proj(x, sc1, sh1, pad_cols(od_w_in[i]))
            (cq,) = prep(z, 0, hw, hd=HEAD_DIM, gain=c_qn[i], rope=rope_h, half=half_h, scale=qk_scale,
                         outs=((F32, "t"),))
            ck32, ck = prep(z, hw, hw, hd=HEAD_DIM, gain=c_kn[i], rope=rope_h, half=half_h,
                            outs=((F32, "n"), (BF16, "n")))
            (cv,) = prep(z, 2 * hw, hw, hd=HEAD_DIM, outs=((BF16, ("aug", HEAD_DIM)),))
            o_c = moba_attention(cq, block_means(ck32), ck, cv, _logit_bound(c_qn[i], c_kn[i]) <= MAX_FAST_LOGIT)
            ops = s5_operators(s5_lam_re[i], s5_lam_im[i], s5_b_re[i], s5_b_im[i], s5_c_re[i], s5_c_im[i],
                               s5_d[i], s5_log_dt[i], S5_CHUNK)
            y = s5_layer(z[:, :, 3 * hw:4 * hw], ops)
            y = glu(y.reshape(bsz * seq, hw), s5_glu_w[i].astype(BF16), s5_glu_b[i]).reshape(bsz, seq, hw)
            x = out_proj(o_c, y, od_w_out[i].astype(BF16), x, g1, a_t=True)
        h, route = moe_router(x, sc2, sh2, moe_wg[l], moe_bg[l], moe_we[l], moe_be[l])
        x = moe_experts(h, route, x, g2, moe_w1[l].astype(BF16), moe_w3[l].astype(BF16), moe_w2[l].astype(BF16))
    return x
```

```python
import functools
import math

import jax
import jax.numpy as jnp
from jax import lax
from jax.experimental import pallas as pl
from jax.experimental.pallas import tpu as pltpu

F32 = jnp.float32
BF16 = jnp.bfloat16
I32 = jnp.int32
HI = lax.Precision.HIGHEST

LANES = 128
NEG = -1e30
I32_MIN = -2147483648

HEAD_DIM = 64
A_HEADS = 4
B_HEADS = 8
IDX_HEADS = 8
IDX_DIM = 32
DSA_TOPK = 256
C_HEADS = 8
MOBA_BLOCK = 256
MOBA_TOPK = 3
S5_GROUP_CH = 16
S5_GROUPS = 32
S5_STATE = 64
ROPE_THETA = 500000.0
ROT_HEAD = HEAD_DIM // 4
ROT_IDX = IDX_DIM // 4
N_GROUPS = 4
EXPERTS_PER_GROUP = 8
N_EXPERTS = N_GROUPS * EXPERTS_PER_GROUP
EXPERT_FF = 512
NORM_EPS = 1e-6
S5_CHUNK = 16


def _cparams(sem, vmem_mb=48):
    return pltpu.CompilerParams(dimension_semantics=sem, vmem_limit_bytes=vmem_mb * 1024 * 1024)


def _nt_dot(a, b, precision=None):
    return lax.dot_general(a, b, (((1,), (1,)), ((), ())), precision=precision,
                           preferred_element_type=F32)


def _mod_kernel(c_ref, w_ref, b_ref, o_ref):
    c = c_ref[...]
    ca = c * jax.nn.sigmoid(c)
    o_ref[0] = jnp.dot(ca, w_ref[0], precision=HI, preferred_element_type=F32) + b_ref[0]


def ada_mod(c, ada_w, ada_b):
    depth, d, n6 = ada_w.shape
    bsz = c.shape[0]
    rows = 8
    c_pad = jnp.zeros((rows, d), F32).at[:bsz].set(c)
    out = pl.pallas_call(
        _mod_kernel,
        grid=(depth, n6 // d),
        in_specs=[pl.BlockSpec((rows, d), lambda l, n: (0, 0)),
                  pl.BlockSpec((1, d, d), lambda l, n: (l, 0, n)),
                  pl.BlockSpec((1, 1, d), lambda l, n: (l, 0, n))],
        out_specs=pl.BlockSpec((1, rows, d), lambda l, n: (l, 0, n)),
        out_shape=jax.ShapeDtypeStruct((depth, rows, n6), F32),
        compiler_params=_cparams(("parallel", "parallel")),
        name="ada_mod",
    )(c_pad, ada_w, ada_b.reshape(depth, 1, n6))
    return out[:, :bsz]


def _norm_mod(x, sc, sh):
    y = x * lax.rsqrt(jnp.mean(x * x, axis=-1, keepdims=True) + NORM_EPS)
    return y * (1.0 + sc) + sh


def _inproj_kernel(x_ref, sc_ref, sh_ref, w_ref, o_ref, h_sc):
    @pl.when(pl.program_id(2) == 0)
    def _():
        h_sc[...] = _norm_mod(x_ref[0], sc_ref[0], sh_ref[0]).astype(BF16)

    o_ref[0] = jnp.dot(h_sc[...], w_ref[...], preferred_element_type=F32)


def _pick_tile(n, prefs):
    for p in prefs:
        if n % p == 0:
            return p
    return n


def in_proj(x, sc, sh, w_bf16, tm=512):
    bsz, seq, d = x.shape
    n = w_bf16.shape[1]
    tm = _pick_tile(seq, (tm, 256, 128))
    tn = _pick_tile(n, (1152, 1024, 512, 384, 256, 128))
    return pl.pallas_call(
        _inproj_kernel,
        grid=(bsz, seq // tm, n // tn),
        in_specs=[pl.BlockSpec((1, tm, d), lambda b, i, j: (b, i, 0)),
                  pl.BlockSpec((1, 1, d), lambda b, i, j: (b, 0, 0)),
                  pl.BlockSpec((1, 1, d), lambda b, i, j: (b, 0, 0)),
                  pl.BlockSpec((d, tn), lambda b, i, j: (0, j))],
        out_specs=pl.BlockSpec((1, tm, tn), lambda b, i, j: (b, i, j)),
        out_shape=jax.ShapeDtypeStruct((bsz, seq, n), F32),
        scratch_shapes=[pltpu.VMEM((tm, d), BF16)],
        compiler_params=_cparams(("parallel", "parallel", "arbitrary")),
        name="in_proj",
    )(x, sc, sh, w_bf16)


def rope_lane_tables(seq, hd, rot):
    half = rot // 2
    pos = jnp.arange(seq, dtype=F32)
    inv = ROPE_THETA ** (-(jnp.arange(0, rot, 2, dtype=F32) / rot))
    ang = pos[:, None] * inv[None, :]
    cos, sin = jnp.cos(ang), jnp.sin(ang)
    dl = jnp.arange(LANES) % hd
    first = dl < half
    second = (dl >= half) & (dl < rot)
    idx = jnp.where(first, dl, jnp.where(second, dl - half, 0))
    cos_l = jnp.where(first | second, cos[:, idx], 1.0)
    sa_l = jnp.where(second, sin[:, idx], 0.0)
    sb_l = jnp.where(first, -sin[:, idx], 0.0)
    return cos_l.astype(F32), sa_l.astype(F32), sb_l.astype(F32)


def _prep_out_rows(mode, width):
    if mode == "t":
        return width
    if mode[0] == "aug":
        return (width // mode[1]) * (mode[1] + ONES_ROWS)
    return mode[2] - mode[1]


def _prep_kernel(*refs, hd, half, do_norm, do_rope, scale, outs):
    z_ref, g_ref, gm_ref, cos_ref, sa_ref, sb_ref = refs[:6]
    o_refs = refs[6:6 + len(outs)]
    x = z_ref[0]
    raw = x
    width = x.shape[-1]
    if do_norm:
        sq = x * x
        hi = sq.astype(BF16)
        lo = (sq - hi.astype(F32)).astype(BF16)
        gm = gm_ref[...]
        ms = (jnp.dot(hi, gm, preferred_element_type=F32)
              + jnp.dot(lo, gm, preferred_element_type=F32)) * (1.0 / hd)
        x = x * lax.rsqrt(ms + NORM_EPS) * g_ref[...]
    if do_rope:
        cos, sa, sb = cos_ref[...], sa_ref[...], sb_ref[...]
        cols = []
        for r in range(width // LANES):
            xc = x[:, r * LANES:(r + 1) * LANES]
            cols.append(xc * cos + pltpu.roll(xc, half, 1) * sa + pltpu.roll(xc, LANES - half, 1) * sb)
        x = cols[0] if len(cols) == 1 else jnp.concatenate(cols, axis=-1)
    if scale != 1.0:
        x = x * scale
    tm = x.shape[0]
    for o, (_, mode) in zip(o_refs, outs):
        if mode == "n":
            o[0] = x.astype(o.dtype)
        elif mode == "t":
            o[0] = x.T.astype(o.dtype)
        elif mode[0] == "aug":
            dv = mode[1]
            xt = x.T
            ones = jnp.ones((ONES_ROWS, tm), F32)
            parts = []
            for h in range(width // dv):
                parts += [xt[h * dv:(h + 1) * dv], ones]
            o[0] = jnp.concatenate(parts, axis=0).astype(o.dtype)
        else:
            o[0] = raw.T[mode[1]:mode[2]].astype(o.dtype)


def prep(z, col_off, width, *, hd, gain=None, rope=None, half=0, scale=1.0, outs=((BF16, "n"),), tm=512):
    bsz, seq, _ = z.shape
    tm = _pick_tile(seq, (tm, 256, 128))
    cb = col_off // width
    assert col_off % width == 0 and width % LANES == 0
    do_norm = gain is not None
    do_rope = rope is not None
    g = jnp.tile(gain.astype(F32), width // hd).reshape(1, width) if do_norm else jnp.ones((1, width), F32)
    gi = jnp.arange(width) // hd
    gm = (gi[:, None] == gi[None, :]).astype(BF16)
    if do_rope:
        cos_l, sa_l, sb_l = rope
    else:
        cos_l = sa_l = sb_l = jnp.zeros((seq, LANES), F32)
    out_specs, out_shape = [], []
    for dt, mode in outs:
        if mode == "n":
            out_specs.append(pl.BlockSpec((1, tm, width), lambda b, i: (b, i, 0)))
            out_shape.append(jax.ShapeDtypeStruct((bsz, seq, width), dt))
        else:
            rows = _prep_out_rows(mode, width)
            out_specs.append(pl.BlockSpec((1, rows, tm), lambda b, i: (b, 0, i)))
            out_shape.append(jax.ShapeDtypeStruct((bsz, rows, seq), dt))
    return pl.pallas_call(
        functools.partial(_prep_kernel, hd=hd, half=half, do_norm=do_norm, do_rope=do_rope,
                          scale=scale, outs=tuple(outs)),
        grid=(bsz, seq // tm),
        in_specs=[pl.BlockSpec((1, tm, width), lambda b, i: (b, i, cb)),
                  pl.BlockSpec((1, width), lambda b, i: (0, 0)),
                  pl.BlockSpec((width, width), lambda b, i: (0, 0)),
                  pl.BlockSpec((tm, LANES), lambda b, i: (i, 0)),
                  pl.BlockSpec((tm, LANES), lambda b, i: (i, 0)),
                  pl.BlockSpec((tm, LANES), lambda b, i: (i, 0))],
        out_specs=out_specs,
        out_shape=out_shape,
        compiler_params=_cparams(("parallel", "parallel")),
        name="prep",
    )(z, g, gm, cos_l, sa_l, sb_l)


ATT_TILE = 256
ATT_GROUP = 8


def _for_tile_groups(n_tiles, body):
    def group(jg, carry):
        body(ATT_GROUP * jg, ATT_GROUP)
        return carry

    lax.fori_loop(0, n_tiles // ATT_GROUP, group, 0)
    size = ATT_GROUP // 2
    while size >= 1:
        def tail(size=size):
            body((n_tiles // (2 * size)) * (2 * size), size)
        pl.when((n_tiles // size) % 2 == 1)(tail)
        size //= 2


ONES_ROWS = 16
MAX_FAST_LOGIT = 60.0


def _attn_update(s_t, vt_aug, m_ref, acc_ref, idx, fast):
    if fast:
        acc_ref[idx] += jnp.dot(vt_aug, jnp.exp2(s_t).astype(BF16), preferred_element_type=F32)
    else:
        m_prev = m_ref[idx]
        m_new = jnp.maximum(m_prev, jnp.max(s_t, axis=0, keepdims=True))
        p_t = jnp.exp2(s_t - m_new).astype(BF16)
        acc_ref[idx] = (jnp.exp2(m_prev - m_new) * acc_ref[idx]
                        + jnp.dot(vt_aug, p_t, preferred_element_type=F32))
        m_ref[idx] = m_new


def _attn_init(m_sc, acc_sc):
    m_sc[...] = jnp.full(m_sc.shape, NEG, F32)
    acc_sc[...] = jnp.zeros(acc_sc.shape, F32)


def _attn_out(acc_ref, idx, dv):
    acc = acc_ref[idx]
    return acc[:dv] / acc[dv:dv + 1]


def _logit_bound(gain_q, gain_k):
    return (HEAD_DIM ** 0.5 * math.log2(math.e) * 1.02) * jnp.max(jnp.abs(gain_q)) * jnp.max(jnp.abs(gain_k))


def _half_rows(x, sub):
    row = lax.broadcasted_iota(I32, (LANES, 1), 0)
    return jnp.where((row >= sub * HEAD_DIM) & (row < (sub + 1) * HEAD_DIM), x, jnp.zeros_like(x))


def _causal_t(t):
    return lax.broadcasted_iota(I32, (t, t), 0) <= lax.broadcasted_iota(I32, (t, t), 1)


def _diff_attn_kernel(qt_ref, k_ref, vt_ref, lq1_ref, lk1_ref, lq2_ref, lk2_ref, g_ref, o_ref,
                      m_sc, acc_sc, *, t, lam_init, fast):
    i = pl.program_id(2)
    _attn_init(m_sc, acc_sc)
    qt = qt_ref[0]
    qc = [_half_rows(qt, c) for c in range(2)]

    def attend(jt, n, mask_fn):
        off = pl.multiple_of(jt * t, t)
        k = k_ref[0, pl.ds(off, n * t), :]
        vt = vt_ref[0, :, pl.ds(off, n * t)]
        s = [mask_fn(jnp.dot(k, qc[c], preferred_element_type=F32)) for c in range(2)]
        for c in range(2):
            _attn_update(s[c], vt, m_sc, acc_sc, c, fast)

    def past_pair(jp, carry):
        attend(2 * jp, 2, lambda s_t: s_t)
        return carry

    lax.fori_loop(0, i // 2, past_pair, 0)

    @pl.when(i % 2 == 1)
    def _():
        attend(i - 1, 1, lambda s_t: s_t)

    causal = _causal_t(t)
    attend(i, 1, lambda s_t: jnp.where(causal, s_t, NEG))

    lam = (jnp.exp(jnp.sum(lq1_ref[...] * lk1_ref[...], axis=-1, keepdims=True))
           - jnp.exp(jnp.sum(lq2_ref[...] * lk2_ref[...], axis=-1, keepdims=True)) + lam_init)
    o = _attn_out(acc_sc, 0, LANES) - lam * _attn_out(acc_sc, 1, LANES)
    o = o * lax.rsqrt(jnp.mean(o * o, axis=0, keepdims=True) + NORM_EPS)
    o_ref[0] = o * g_ref[...] * (1.0 - lam_init)


DIFF_TILE = 512


def diff_attention(qt, k, vta, lq1, lk1, lq2, lk2, subln, lam_init, fast_ok):
    bsz, width, seq = qt.shape
    n_h = width // LANES
    t = _pick_tile(seq, (DIFF_TILE, 256, 128))
    rows = LANES + ONES_ROWS
    vec = lambda a: a.astype(F32).reshape(1, -1)
    small = pl.BlockSpec((1, HEAD_DIM), lambda b, h, i: (0, 0))

    def call(fast, *args):
        return pl.pallas_call(
            functools.partial(_diff_attn_kernel, t=t, lam_init=lam_init, fast=fast),
            grid=(bsz, n_h, seq // t),
            in_specs=[pl.BlockSpec((1, LANES, t), lambda b, h, i: (b, h, i)),
                      pl.BlockSpec((1, seq, LANES), lambda b, h, i: (b, 0, h)),
                      pl.BlockSpec((1, rows, seq), lambda b, h, i: (b, h, 0)),
                      small, small, small, small,
                      pl.BlockSpec((LANES, 1), lambda b, h, i: (0, 0))],
            out_specs=pl.BlockSpec((1, LANES, t), lambda b, h, i: (b, h, i)),
            out_shape=jax.ShapeDtypeStruct((bsz, width, seq), F32),
            scratch_shapes=[pltpu.VMEM((2, 1, t), F32), pltpu.VMEM((2, rows, t), F32)],
            compiler_params=_cparams(("parallel", "parallel", "arbitrary")),
            name="diff_attn_fast" if fast else "diff_attn_safe",
        )(*args)

    args = (qt, k, vta, vec(lq1), vec(lk1), vec(lq2), vec(lk2), subln.astype(F32).reshape(-1, 1))
    return lax.cond(fast_ok, functools.partial(call, True), functools.partial(call, False), *args)


def _sortable_key(score):
    bits = pltpu.bitcast(score + 0.0, I32)
    return bits ^ ((bits >> 31) & 0x7FFFFFFF)


def _dsa_kernel(qt_ref, k_ref, vt_ref, qit_ref, ki_ref, wt_ref, o_ref, kb_sc, m_sc, acc_sc,
                *, t, nk, n_sel, fast):
    i = pl.program_id(1)
    pair = pl.program_id(2)
    n_chunks = i + 1
    krow = lax.broadcasted_iota(I32, (t, t), 0)

    def count_where(pred_fn):
        def chunk_count(cidx):
            g = jnp.where(pred_fn(kb_sc[cidx], cidx), 1, 0)
            return jnp.sum(g.reshape(t // 8, 8, t), axis=0)

        def body(jp, acc):
            return acc + chunk_count(2 * jp) + chunk_count(2 * jp + 1)

        acc = lax.fori_loop(0, n_chunks // 2, body, jnp.zeros((8, t), I32))
        last = chunk_count(n_chunks - 1)
        acc = acc + jnp.where(n_chunks % 2 == 1, last, 0)
        return jnp.sum(acc, axis=0, keepdims=True)

    @pl.when(pair == 0)
    def _():
        qit = qit_ref[0]
        pad = jnp.zeros((LANES - IDX_DIM, t), BF16)
        q_heads = [jnp.concatenate([qit[h * IDX_DIM:(h + 1) * IDX_DIM], pad], axis=0)
                   for h in range(IDX_HEADS)]
        wt = wt_ref[0]
        w_rows = [wt[h:h + 1] for h in range(IDX_HEADS)]
        qcol = i * t + lax.broadcasted_iota(I32, (1, t), 1)

        def score_chunk(cidx, mm):
            off = pl.multiple_of(cidx * t, t)
            ki = ki_ref[0, pl.ds(off, t), :]
            score = jnp.zeros((t, t), F32)
            for h in range(IDX_HEADS):
                lg = jnp.dot(ki, q_heads[h], preferred_element_type=F32)
                score = score + w_rows[h] * jnp.maximum(lg, 0.0)
            key = _sortable_key(score)
            visible = krow + off <= qcol
            kb_sc[cidx] = jnp.where(visible, key, I32_MIN)
            lo_c = jnp.min(jnp.where(visible, key, 2147483647).reshape(t // 8, 8, t), axis=0)
            hi_c = jnp.max(jnp.where(visible, key, I32_MIN).reshape(t // 8, 8, t), axis=0)
            return jnp.minimum(mm[0], lo_c), jnp.maximum(mm[1], hi_c)

        kmin, kmax = lax.fori_loop(0, n_chunks, score_chunk, (jnp.full((8, t), 2147483647, I32),
                                                              jnp.full((8, t), I32_MIN, I32)))
        n_vis = qcol + 1
        few = n_vis <= n_sel
        kmin = jnp.min(kmin, axis=0, keepdims=True)
        kmax = jnp.max(kmax, axis=0, keepdims=True)
        span = 1 << 25
        probe = jnp.maximum(jnp.where(kmax > I32_MIN + span, kmax - span, kmin), kmin)
        c_probe = count_where(lambda kk, cidx: kk >= probe)
        probe_ok = c_probe >= n_sel
        lo0 = jnp.where(few, I32_MIN, jnp.where(probe_ok, probe, kmin))
        hi0 = jnp.where(few, I32_MIN + 1, jnp.where(probe_ok, kmax + 1, probe))

        def open_lanes(lo, hi, c_lo):
            return jnp.logical_and(c_lo != n_sel, hi > lo + 1)

        def bisect_cond(st):
            it, lo, hi, c_lo, c_hi = st
            return jnp.logical_and(it < 34, jnp.max(jnp.where(open_lanes(lo, hi, c_lo), 1, 0)) > 0)

        def bisect(st):
            it, lo, hi, c_lo, c_hi = st
            mid = (lo >> 1) + (hi >> 1) + (lo & hi & 1)
            cnt = count_where(lambda kk, cidx: kk >= mid)
            up = jnp.logical_and(open_lanes(lo, hi, c_lo), cnt >= n_sel)
            dn = jnp.logical_and(open_lanes(lo, hi, c_lo), cnt < n_sel)
            return (it + 1, jnp.where(up, mid, lo), jnp.where(dn, mid, hi),
                    jnp.where(up, cnt, c_lo), jnp.where(dn, cnt, c_hi))

        c_lo0 = jnp.where(few, n_sel, jnp.where(probe_ok, c_probe, n_vis))
        c_hi0 = jnp.where(jnp.logical_or(few, probe_ok), 0, c_probe)
        _, thr, _, c_lo, c_hi = lax.while_loop(
            bisect_cond, bisect, (jnp.int32(0), lo0, hi0, c_lo0, c_hi0))

        need = n_sel - c_hi
        all_ties = c_lo == n_sel
        jmax0 = jnp.where(few, -1, jnp.where(all_ties, 2147483647, -1))
        unresolved = jnp.logical_and(jnp.logical_not(few), jnp.logical_not(all_ties))

        def tie_search(_):
            def step(it, lohi):
                lo, hi = lohi
                mid = (lo + hi) >> 1
                cnt = count_where(lambda kk, cidx: jnp.logical_and(kk == thr, krow + cidx * t <= mid))
                ok = cnt >= need
                return jnp.where(ok, lo, mid), jnp.where(ok, mid, hi)
            n_steps = 1 + max(1, (nk * t - 1).bit_length())
            _, hi = lax.fori_loop(0, n_steps, step, (jnp.full((1, t), -1, I32),
                                                     jnp.full((1, t), nk * t - 1, I32)))
            return jnp.where(unresolved, hi, jmax0)

        any_unresolved = jnp.max(jnp.where(unresolved, 1, 0)) > 0
        jmax = lax.cond(any_unresolved, tie_search, lambda _: jmax0, 0)

        def bias_chunk(cidx, carry):
            kk = kb_sc[cidx]
            cut = jnp.where(krow + cidx * t <= jmax, thr - 1, thr)
            bias = jnp.where(kk > cut, 0.0, NEG)
            kb_sc[cidx] = pltpu.bitcast(bias.astype(F32), I32)
            return carry

        lax.fori_loop(0, n_chunks, bias_chunk, 0)

    _attn_init(m_sc, acc_sc)
    qt = qt_ref[0]
    qh = [_half_rows(qt, sub) for sub in range(2)]
    hrows = HEAD_DIM + ONES_ROWS

    def attend(jt, n):
        off = pl.multiple_of(jt * t, t)
        k = k_ref[0, pl.ds(off, n * t), :]
        vt = vt_ref[0, :, pl.ds(off, n * t)]
        bias = jnp.concatenate([pltpu.bitcast(kb_sc[jt + c], F32) for c in range(n)], axis=0)
        s = [jnp.dot(k, qh[sub], preferred_element_type=F32) + bias for sub in range(2)]
        for sub in range(2):
            _attn_update(s[sub], vt[sub * hrows:(sub + 1) * hrows], m_sc, acc_sc, sub, fast)

    _for_tile_groups(n_chunks, attend)
    o_ref[0] = jnp.concatenate([_attn_out(acc_sc, sub, HEAD_DIM) for sub in range(2)], axis=0)


def dsa_attention(qt, k, vta, qit, ki, iwt, fast_ok):
    bsz, width, seq = qt.shape
    t = _pick_tile(seq, (ATT_TILE, 128))
    nq = seq // t
    n_sel = min(DSA_TOPK, seq // 4)
    prow = 2 * (HEAD_DIM + ONES_ROWS)

    def call(fast, *args):
        return pl.pallas_call(
            functools.partial(_dsa_kernel, t=t, nk=nq, n_sel=n_sel, fast=fast),
            grid=(bsz, nq, width // LANES),
            in_specs=[pl.BlockSpec((1, LANES, t), lambda b, i, p: (b, p, i)),
                      pl.BlockSpec((1, seq, LANES), lambda b, i, p: (b, 0, p)),
                      pl.BlockSpec((1, prow, seq), lambda b, i, p: (b, p, 0)),
                      pl.BlockSpec((1, IDX_HEADS * IDX_DIM, t), lambda b, i, p: (b, 0, i)),
                      pl.BlockSpec((1, seq, LANES), lambda b, i, p: (b, 0, 0)),
                      pl.BlockSpec((1, IDX_HEADS, t), lambda b, i, p: (b, 0, i))],
            out_specs=pl.BlockSpec((1, LANES, t), lambda b, i, p: (b, p, i)),
            out_shape=jax.ShapeDtypeStruct((bsz, width, seq), F32),
            scratch_shapes=[pltpu.VMEM((nq, t, t), I32), pltpu.VMEM((2, 1, t), F32),
                            pltpu.VMEM((2, HEAD_DIM + ONES_ROWS, t), F32)],
            compiler_params=_cparams(("parallel", "parallel", "arbitrary"), vmem_mb=56),
            name="dsa_attn_fast" if fast else "dsa_attn_safe",
        )(*args)

    args = (qt, k, vta, qit, ki, iwt)
    return lax.cond(fast_ok, functools.partial(call, True), functools.partial(call, False), *args)


def _kmean_kernel(k_ref, o_ref, *, nbb):
    x = k_ref[0]
    o_ref[0] = jnp.mean(x.reshape(nbb, MOBA_BLOCK, x.shape[-1]), axis=1)


def block_means(k_f32):
    bsz, seq, width = k_f32.shape
    nb = seq // MOBA_BLOCK
    nbb = 8 if nb % 8 == 0 else nb
    return pl.pallas_call(
        functools.partial(_kmean_kernel, nbb=nbb),
        grid=(bsz, nb // nbb),
        in_specs=[pl.BlockSpec((1, nbb * MOBA_BLOCK, width), lambda b, i: (b, i, 0))],
        out_specs=pl.BlockSpec((1, nbb, width), lambda b, i: (b, i, 0)),
        out_shape=jax.ShapeDtypeStruct((bsz, nb, width), F32),
        compiler_params=_cparams(("parallel", "parallel")),
        name="moba_kmean",
    )(k_f32)


def _moba_select_kernel(qt_ref, km_ref, o_ref, *, ts, n_sel, nbp):
    j = pl.program_id(2)
    blk = lax.broadcasted_iota(I32, (nbp, ts), 0)
    own = jnp.right_shift(j * ts + lax.broadcasted_iota(I32, (1, ts), 1), MOBA_BLOCK.bit_length() - 1)
    qt = qt_ref[0]
    for sub in range(2):
        gate = jnp.dot(km_ref[0], _half_rows(qt, sub), precision=HI, preferred_element_type=F32)
        g = jnp.where(blk < own, gate, -jnp.inf)
        sel = jnp.zeros((nbp, ts), F32)
        for _ in range(n_sel):
            mx = jnp.max(g, axis=0, keepdims=True)
            is_mx = jnp.logical_and(g == mx, mx > -jnp.inf)
            first = jnp.min(jnp.where(is_mx, blk, 2 * nbp), axis=0, keepdims=True)
            pick = blk == first
            sel = jnp.where(pick, 1.0, sel)
            g = jnp.where(pick, -jnp.inf, g)
        o_ref[0, sub] = sel


def _moba_kernel(qt_ref, sel_ref, k_ref, vt_ref, o_ref, m_sc, acc_sc, *, t, fast):
    i = pl.program_id(2)
    _attn_init(m_sc, acc_sc)
    qt = qt_ref[0]
    qb = [_half_rows(qt, sub).astype(BF16) for sub in range(2)]
    hrows = HEAD_DIM + ONES_ROWS

    def attend(kb, n, mask_fn):
        off = pl.multiple_of(kb * t, t)
        k = k_ref[0, pl.ds(off, n * t), :]
        vt = vt_ref[0, :, pl.ds(off, n * t)]
        s = [mask_fn(jnp.dot(k, qb[sub], preferred_element_type=F32), sub) for sub in range(2)]
        for sub in range(2):
            _attn_update(s[sub], vt[sub * hrows:(sub + 1) * hrows], m_sc, acc_sc, sub, fast)

    causal = _causal_t(t)
    attend(i, 1, lambda s_t, sub: jnp.where(causal, s_t, NEG))

    def past_blocks(kb, n):
        def row_mask(s_t, sub):
            rows = [jnp.where(sel_ref[0, sub, pl.ds(kb + c, 1), :] > 0.0, 0.0, NEG) for c in range(n)]
            bias = jnp.concatenate([jnp.broadcast_to(r, (t, t)) for r in rows], axis=0)
            return s_t + bias
        attend(kb, n, row_mask)

    _for_tile_groups(i, past_blocks)
    o_ref[0] = jnp.concatenate([_attn_out(acc_sc, sub, HEAD_DIM) for sub in range(2)], axis=0)


def moba_attention(qt_f32, kmean, k, vta, fast_ok):
    bsz, width, seq = qt_f32.shape
    t = MOBA_BLOCK
    nb = seq // MOBA_BLOCK
    n_sel = max(1, min(MOBA_TOPK, nb - 1))
    nbp = -(-nb // 8) * 8
    km = jnp.zeros((bsz, nbp, width), F32).at[:, :nb].set(kmean)
    prow = 2 * (HEAD_DIM + ONES_ROWS)
    n_pairs = width // LANES
    ts = _pick_tile(seq, (1024, 512, 256))
    sel = pl.pallas_call(
        functools.partial(_moba_select_kernel, ts=ts, n_sel=n_sel, nbp=nbp),
        grid=(bsz, n_pairs, seq // ts),
        in_specs=[pl.BlockSpec((1, LANES, ts), lambda b, h, j: (b, h, j)),
                  pl.BlockSpec((1, nbp, LANES), lambda b, h, j: (b, 0, h))],
        out_specs=pl.BlockSpec((1, 2, nbp, ts), lambda b, h, j: (b, h, 0, j)),
        out_shape=jax.ShapeDtypeStruct((bsz, 2 * n_pairs, nbp, seq), F32),
        compiler_params=_cparams(("parallel", "parallel", "parallel")),
        name="moba_select",
    )(qt_f32, km)

    def call(fast, *args):
        return pl.pallas_call(
            functools.partial(_moba_kernel, t=t, fast=fast),
            grid=(bsz, n_pairs, nb),
            in_specs=[pl.BlockSpec((1, LANES, t), lambda b, h, i: (b, h, i)),
                      pl.BlockSpec((1, 2, nbp, t), lambda b, h, i: (b, h, 0, i)),
                      pl.BlockSpec((1, seq, LANES), lambda b, h, i: (b, 0, h)),
                      pl.BlockSpec((1, prow, seq), lambda b, h, i: (b, h, 0))],
            out_specs=pl.BlockSpec((1, LANES, t), lambda b, h, i: (b, h, i)),
            out_shape=jax.ShapeDtypeStruct((bsz, width, seq), F32),
            scratch_shapes=[pltpu.VMEM((2, 1, t), F32), pltpu.VMEM((2, HEAD_DIM + ONES_ROWS, t), F32)],
            compiler_params=_cparams(("parallel", "parallel", "arbitrary")),
            name="moba_attn_fast" if fast else "moba_attn_safe",
        )(*args)

    args = (qt_f32, sel, k, vta)
    return lax.cond(fast_ok, functools.partial(call, True), functools.partial(call, False), *args)


def s5_operators(lam_re, lam_im, b_re, b_im, c_re, c_im, d_skip, log_dt, chunk):
    n_g, n_p = lam_re.shape
    n_c = d_skip.shape[-1]
    dt = jnp.exp(log_dt.astype(F32))[:, None]
    den = lam_re * lam_re + lam_im * lam_im
    mag = jnp.exp(lam_re * dt)
    abar_re = mag * jnp.cos(lam_im * dt)
    abar_im = mag * jnp.sin(lam_im * dt)
    coef_re = ((abar_re - 1.0) * lam_re + abar_im * lam_im) / den
    coef_im = (abar_im * lam_re - (abar_re - 1.0) * lam_im) / den
    bbar_re = coef_re[..., None] * b_re - coef_im[..., None] * b_im
    bbar_im = coef_re[..., None] * b_im + coef_im[..., None] * b_re
    tau = jnp.arange(chunk + 1, dtype=F32)[:, None, None]
    pmag = jnp.exp(lam_re * dt * tau)
    pw_re = pmag * jnp.cos(lam_im * dt * tau)
    pw_im = pmag * jnp.sin(lam_im * dt * tau)
    ab_re = pw_re[..., None] * bbar_re - pw_im[..., None] * bbar_im
    ab_im = pw_re[..., None] * bbar_im + pw_im[..., None] * bbar_re
    kern = (jnp.einsum('gop,tgpi->tgio', c_re, ab_re, precision=HI)
            - jnp.einsum('gop,tgpi->tgio', c_im, ab_im, precision=HI))
    t_idx = jnp.arange(chunk)
    lag = t_idx[None, :] - t_idx[:, None]
    toep = jnp.where((lag >= 0)[None, :, None, :, None],
                     kern[jnp.clip(lag, 0, chunk)].transpose(2, 0, 3, 1, 4), 0.0)
    eye = (jnp.eye(chunk)[None, :, None, :, None] * jnp.eye(n_c)[None, None, :, None, :])
    toep = toep + eye * d_skip[:, None, :, None, None]
    toep = toep.reshape(n_g, chunk * n_c, chunk * n_c)
    rev = chunk - 1 - t_idx
    w_in = jnp.concatenate([ab_re[rev].transpose(1, 0, 3, 2), ab_im[rev].transpose(1, 0, 3, 2)], axis=-1)
    w_in = w_in.reshape(n_g, chunk * n_c, 2 * n_p)
    a_re, a_im = pw_re[1:chunk + 1], pw_im[1:chunk + 1]
    st_re = (c_re[None] * a_re[:, :, None, :] - c_im[None] * a_im[:, :, None, :])
    st_im = (-c_re[None] * a_im[:, :, None, :] - c_im[None] * a_re[:, :, None, :])
    w_st = jnp.concatenate([st_re.transpose(1, 3, 0, 2), st_im.transpose(1, 3, 0, 2)], axis=1)
    w_st = w_st.reshape(n_g, 2 * n_p, chunk * n_c)
    return toep, w_in, w_st, pw_re[chunk], pw_im[chunk]


def _s5_in_kernel(u_ref, w_ref, o_ref):
    o_ref[0, 0] = jnp.dot(u_ref[0, 0], w_ref[0], precision=HI, preferred_element_type=F32)


def _s5_scan_kernel(x_ref, aa_ref, ab_ref, abs_ref, o_ref, h_sc, hs_sc, *, steps, n_p):
    @pl.when(pl.program_id(0) == 0)
    def _():
        h_sc[...] = jnp.zeros(h_sc.shape, F32)
        hs_sc[...] = jnp.zeros(hs_sc.shape, F32)

    aa, ab, abs_ = aa_ref[...], ab_ref[...], abs_ref[...]

    def body(t, carry):
        h, hs = carry
        o_ref[t] = h
        x = x_ref[t]
        xs = pltpu.roll(x, n_p, 1)
        return aa * h + ab * hs + x, aa * hs + abs_ * h + xs

    h, hs = lax.fori_loop(0, steps, body, (h_sc[...], hs_sc[...]))
    h_sc[...] = h
    hs_sc[...] = hs


def _s5_out_kernel(u_ref, h0_ref, t_ref, w_ref, o_ref):
    y = (jnp.dot(u_ref[0, 0], t_ref[0], precision=HI, preferred_element_type=F32)
         + jnp.dot(h0_ref[0, 0], w_ref[0], precision=HI, preferred_element_type=F32))
    o_ref[0, 0] = jax.nn.gelu(y)


def s5_layer(u, ops):
    toep, w_in, w_st, ac_re, ac_im = ops
    bsz, seq, width = u.shape
    n_g, n_p = ac_re.shape
    n_c = width // n_g
    chunk = S5_CHUNK
    nj = seq // chunk
    row = chunk * n_c
    ur = u.reshape(bsz, nj, chunk, n_g, n_c).transpose(0, 3, 1, 2, 4).reshape(bsz, n_g, nj, row)
    hloc = pl.pallas_call(
        _s5_in_kernel,
        grid=(bsz, n_g),
        in_specs=[pl.BlockSpec((1, 1, nj, row), lambda b, g: (b, g, 0, 0)),
                  pl.BlockSpec((1, row, 2 * n_p), lambda b, g: (g, 0, 0))],
        out_specs=pl.BlockSpec((1, 1, nj, 2 * n_p), lambda b, g: (b, g, 0, 0)),
        out_shape=jax.ShapeDtypeStruct((bsz, n_g, nj, 2 * n_p), F32),
        compiler_params=_cparams(("parallel", "parallel")),
        name="s5_in",
    )(ur, w_in)
    bg = bsz * n_g
    xs = hloc.transpose(2, 0, 1, 3).reshape(nj, bg, 2 * n_p)
    rep = lambda a: jnp.tile(a, (bsz, 1))
    aa = rep(jnp.concatenate([ac_re, ac_re], axis=-1))
    ab = rep(jnp.concatenate([-ac_im, ac_im], axis=-1))
    abs_ = rep(jnp.concatenate([ac_im, -ac_im], axis=-1))
    steps = _pick_tile(nj, (128, 64, 32, 16, 8))
    cst = pl.BlockSpec((bg, 2 * n_p), lambda t: (0, 0))
    h0 = pl.pallas_call(
        functools.partial(_s5_scan_kernel, steps=steps, n_p=n_p),
        grid=(nj // steps,),
        in_specs=[pl.BlockSpec((steps, bg, 2 * n_p), lambda t: (t, 0, 0)), cst, cst, cst],
        out_specs=pl.BlockSpec((steps, bg, 2 * n_p), lambda t: (t, 0, 0)),
        out_shape=jax.ShapeDtypeStruct((nj, bg, 2 * n_p), F32),
        scratch_shapes=[pltpu.VMEM((bg, 2 * n_p), F32), pltpu.VMEM((bg, 2 * n_p), F32)],
        compiler_params=_cparams(("arbitrary",)),
        name="s5_scan",
    )(xs, aa, ab, abs_)
    h0 = h0.reshape(nj, bsz, n_g, 2 * n_p).transpose(1, 2, 0, 3)
    y = pl.pallas_call(
        _s5_out_kernel,
        grid=(bsz, n_g),
        in_specs=[pl.BlockSpec((1, 1, nj, row), lambda b, g: (b, g, 0, 0)),
                  pl.BlockSpec((1, 1, nj, 2 * n_p), lambda b, g: (b, g, 0, 0)),
                  pl.BlockSpec((1, row, row), lambda b, g: (g, 0, 0)),
                  pl.BlockSpec((1, 2 * n_p, row), lambda b, g: (g, 0, 0))],
        out_specs=pl.BlockSpec((1, 1, nj, row), lambda b, g: (b, g, 0, 0)),
        out_shape=jax.ShapeDtypeStruct((bsz, n_g, nj, row), F32),
        compiler_params=_cparams(("parallel", "parallel")),
        name="s5_out",
    )(ur, h0, toep, w_st)
    return y.reshape(bsz, n_g, nj, chunk, n_c).transpose(0, 2, 3, 1, 4).reshape(bsz, seq, width)


def _glu_kernel(y_ref, w_ref, b_ref, o_ref):
    y = y_ref[...]
    gate = jnp.dot(y.astype(BF16), w_ref[...], preferred_element_type=F32) + b_ref[...]
    o_ref[...] = (y * jax.nn.sigmoid(gate)).astype(o_ref.dtype)


def glu(y, w_bf16, b, tm=1024):
    n_tok, width = y.shape
    tm = _pick_tile(n_tok, (tm, 512, 256, 128))
    return pl.pallas_call(
        _glu_kernel,
        grid=(n_tok // tm,),
        in_specs=[pl.BlockSpec((tm, width), lambda i: (i, 0)),
                  pl.BlockSpec((width, width), lambda i: (0, 0)),
                  pl.BlockSpec((1, width), lambda i: (0, 0))],
        out_specs=pl.BlockSpec((tm, width), lambda i: (i, 0)),
        out_shape=jax.ShapeDtypeStruct((n_tok, width), F32),
        compiler_params=_cparams(("parallel",)),
        name="s5_glu",
    )(y, w_bf16, b.astype(F32).reshape(1, width))


def _outproj_kernel(a_ref, b_ref, w_ref, x_ref, g_ref, o_ref, *, half, a_t, b_t):
    a = a_ref[0].T if a_t else a_ref[0]
    b = b_ref[0].T if b_t else b_ref[0]
    y = (jnp.dot(a.astype(BF16), w_ref[:half, :], preferred_element_type=F32)
         + jnp.dot(b.astype(BF16), w_ref[half:, :], preferred_element_type=F32))
    o_ref[0] = x_ref[0] + g_ref[0] * y


def out_proj(a, b, w_bf16, x, gate, a_t=False, b_t=False, tm=512):
    bsz, seq, d = x.shape
    half = w_bf16.shape[0] // 2
    tm = _pick_tile(seq, (tm, 256, 128))

    def spec(transposed):
        if transposed:
            return pl.BlockSpec((1, half, tm), lambda bb, i: (bb, 0, i))
        return pl.BlockSpec((1, tm, half), lambda bb, i: (bb, i, 0))

    return pl.pallas_call(
        functools.partial(_outproj_kernel, half=half, a_t=a_t, b_t=b_t),
        grid=(bsz, seq // tm),
        in_specs=[spec(a_t), spec(b_t),
                  pl.BlockSpec((2 * half, d), lambda bb, i: (0, 0)),
                  pl.BlockSpec((1, tm, d), lambda bb, i: (bb, i, 0)),
                  pl.BlockSpec((1, 1, d), lambda bb, i: (bb, 0, 0))],
        out_specs=pl.BlockSpec((1, tm, d), lambda bb, i: (bb, i, 0)),
        out_shape=jax.ShapeDtypeStruct((bsz, seq, d), F32),
        compiler_params=_cparams(("parallel", "parallel")),
        name="out_proj",
    )(a, b, w_bf16, x, gate)


def _router_kernel(x_ref, sc_ref, sh_ref, wr_ref, br_ref, h_ref, r_ref):
    h = _norm_mod(x_ref[0], sc_ref[0], sh_ref[0])
    h_ref[0] = h.astype(BF16)
    logits = jnp.dot(h, wr_ref[...], precision=HI, preferred_element_type=F32) + br_ref[...]
    tm = logits.shape[0]
    lane = lax.broadcasted_iota(I32, (tm, LANES), 1)
    big = 4 * LANES
    coarse = jnp.where(lane < N_GROUPS, logits, -jnp.inf)
    cmax = jnp.max(coarse, axis=1, keepdims=True)
    grp = jnp.min(jnp.where(coarse == cmax, lane, big), axis=1, keepdims=True)
    p_grp = 1.0 / jnp.sum(jnp.exp(coarse - cmax), axis=1, keepdims=True)
    lo = N_GROUPS + grp * EXPERTS_PER_GROUP
    fine = jnp.where(jnp.logical_and(lane >= lo, lane < lo + EXPERTS_PER_GROUP), logits, -jnp.inf)
    v1 = jnp.max(fine, axis=1, keepdims=True)
    i1 = jnp.min(jnp.where(fine == v1, lane, big), axis=1, keepdims=True)
    fine2 = jnp.where(lane == i1, -jnp.inf, fine)
    v2 = jnp.max(fine2, axis=1, keepdims=True)
    i2 = jnp.min(jnp.where(fine2 == v2, lane, big), axis=1, keepdims=True)
    e2 = jnp.exp(v2 - v1)
    wa = p_grp / (1.0 + e2)
    wb = p_grp * e2 / (1.0 + e2)
    ida = (i1 - N_GROUPS).astype(F32)
    idb = (i2 - N_GROUPS).astype(F32)
    r_ref[0] = jnp.where(lane == 0, ida, jnp.where(lane == 1, idb,
                         jnp.where(lane == 2, wa, jnp.where(lane == 3, wb, 0.0))))


def moe_router(x, sc, sh, wg, bg, we, be, tm=512):
    bsz, seq, d = x.shape
    tm = _pick_tile(seq, (tm, 256, 128))
    n_r = N_GROUPS + N_EXPERTS
    wr = jnp.zeros((d, LANES), F32).at[:, :N_GROUPS].set(wg).at[:, N_GROUPS:n_r].set(we)
    br = jnp.zeros((1, LANES), F32).at[0, :N_GROUPS].set(bg).at[0, N_GROUPS:n_r].set(be)
    return pl.pallas_call(
        _router_kernel,
        grid=(bsz, seq // tm),
        in_specs=[pl.BlockSpec((1, tm, d), lambda b, i: (b, i, 0)),
                  pl.BlockSpec((1, 1, d), lambda b, i: (b, 0, 0)),
                  pl.BlockSpec((1, 1, d), lambda b, i: (b, 0, 0)),
                  pl.BlockSpec((d, LANES), lambda b, i: (0, 0)),
                  pl.BlockSpec((1, LANES), lambda b, i: (0, 0))],
        out_specs=[pl.BlockSpec((1, tm, d), lambda b, i: (b, i, 0)),
                   pl.BlockSpec((1, tm, LANES), lambda b, i: (b, i, 0))],
        out_shape=[jax.ShapeDtypeStruct((bsz, seq, d), BF16),
                   jax.ShapeDtypeStruct((bsz, seq, LANES), F32)],
        compiler_params=_cparams(("parallel", "parallel")),
        name="moe_router",
    )(x, sc, sh, wr, br)


MOE_SEG = 256


MOE_PAD = 16


def _pad_rows(count):
    return jnp.floor((count + (MOE_PAD - 1)) * (1.0 / MOE_PAD)) * MOE_PAD


def _moe_kernel(erow_ref, route_ref, h_ref, x_ref, g_ref, w1_ref, w3_ref, w2_ref, o_ref,
                xs_sc, ys_sc, meta_sc, drow_sc, dcol_sc, *, chunk, rows, na):
    e = pl.program_id(1)
    n_seg = chunk // MOE_SEG
    lane1 = lax.broadcasted_iota(I32, (1, LANES), 1)

    def lane_scalar(row, k):
        return jnp.sum(jnp.where(lane1 == k, row, 0.0)).astype(I32)

    @pl.when(e == 0)
    def _():
        si = lax.broadcasted_iota(I32, (MOE_SEG, MOE_SEG), 0)
        sj = lax.broadcasted_iota(I32, (MOE_SEG, MOE_SEG), 1)
        upper = jnp.where(si < sj, 1.0, 0.0).astype(BF16)
        lower = jnp.where(sj < si, 1.0, 0.0).astype(BF16)
        erow = erow_ref[0]
        eid = lax.broadcasted_iota(I32, (N_EXPERTS, chunk), 0)
        sel = [erow[k:k + 1] == eid for k in range(2)]
        member = jnp.where(jnp.logical_or(sel[0], sel[1]), 1.0, 0.0)
        count = jnp.zeros((N_EXPERTS, 1), F32)
        parts = []
        for sgi in range(n_seg):
            seg = member[:, sgi * MOE_SEG:(sgi + 1) * MOE_SEG]
            parts.append(jnp.dot(seg.astype(BF16), upper, preferred_element_type=F32) + count)
            count = count + jnp.sum(seg, axis=1, keepdims=True)
        rank = jnp.concatenate(parts, axis=1)
        padded = _pad_rows(count)
        ei = lax.broadcasted_iota(I32, (N_EXPERTS, N_EXPERTS), 0)
        ej = lax.broadcasted_iota(I32, (N_EXPERTS, N_EXPERTS), 1)
        below = jnp.where(ej < ei, 1.0, 0.0).astype(BF16)
        start = jnp.dot(below, jnp.broadcast_to(padded, (N_EXPERTS, LANES)).astype(BF16),
                        preferred_element_type=F32)
        total = jnp.sum(padded, axis=0, keepdims=True)
        lane_e = lax.broadcasted_iota(I32, (N_EXPERTS, LANES), 1)
        meta_sc[...] = jnp.where(lane_e == 0, start, jnp.where(lane_e == 1, count, jnp.where(lane_e == 2, total, 0.0)))
        dest = start[:, 0:1] + rank
        for k in range(2):
            drow_sc[k] = jnp.sum(jnp.where(sel[k], dest, 0.0), axis=0, keepdims=True)
        route = route_ref[...]
        lane_c = lax.broadcasted_iota(I32, (chunk, LANES), 1)
        lane_f = lane_c.astype(F32)
        sel_c = [route[:, k:k + 1] == lane_f for k in range(2)]
        member_c = jnp.where(jnp.logical_or(sel_c[0], sel_c[1]), 1.0, 0.0)
        count_c = jnp.zeros((1, LANES), F32)
        parts_c = []
        for sgi in range(n_seg):
            seg = member_c[sgi * MOE_SEG:(sgi + 1) * MOE_SEG]
            parts_c.append(jnp.dot(lower, seg.astype(BF16), preferred_element_type=F32) + count_c)
            count_c = count_c + jnp.sum(seg, axis=0, keepdims=True)
        rank_c = jnp.concatenate(parts_c, axis=0)
        li = lax.broadcasted_iota(I32, (LANES, LANES), 0)
        lj = lax.broadcasted_iota(I32, (LANES, LANES), 1)
        before = jnp.where(li < lj, 1.0, 0.0).astype(BF16)
        start_c = jnp.dot(jnp.broadcast_to(_pad_rows(count_c), (8, LANES)).astype(BF16), before,
                          preferred_element_type=F32)[0:1]
        dest_c = start_c + rank_c
        dcols = [jnp.sum(jnp.where(sel_c[k], dest_c, 0.0), axis=1, keepdims=True) for k in range(2)]
        dcol_sc[...] = jnp.where(lane_c == 0, dcols[0], jnp.where(lane_c == 1, dcols[1], 0.0))
        ys_sc[...] = jnp.zeros(ys_sc.shape, BF16)
        xs_sc[...] = jnp.zeros(xs_sc.shape, BF16)
        d0, d1 = drow_sc[0], drow_sc[1]
        n_slabs = (jnp.sum(total).astype(I32) + MOE_SEG - 1) // MOE_SEG

        def sort_slab(s, carry):
            off = pl.multiple_of(s * MOE_SEG, MOE_SEG)
            r = (lax.broadcasted_iota(I32, (MOE_SEG, 1), 0) + off).astype(F32)
            onehot = jnp.where(jnp.logical_or(d0 == r, d1 == r), 1.0, 0.0).astype(BF16)
            xs_sc[pl.ds(off, MOE_SEG), :] = jnp.dot(onehot, h_ref[...], preferred_element_type=F32).astype(BF16)
            return carry

        lax.fori_loop(0, n_slabs, sort_slab, 0)

    meta = meta_sc[pl.ds(e, 1), :]
    first = lane_scalar(meta, 0)
    count_e = lane_scalar(meta, 1)
    padded_e = ((count_e + MOE_PAD - 1) // MOE_PAD) * MOE_PAD

    def block(bi, carry):
        r0 = pl.multiple_of(first + bi * rows, MOE_PAD)
        xb = xs_sc[pl.ds(r0, rows), :]
        a = jnp.dot(xb, w1_ref[0], preferred_element_type=F32)
        hid = (a * jax.nn.sigmoid(a)) * jnp.dot(xb, w3_ref[0], preferred_element_type=F32)
        yb = jnp.dot(hid.astype(BF16), w2_ref[0], preferred_element_type=F32)
        mine = lax.broadcasted_iota(I32, (rows, 1), 0) + bi * rows < padded_e
        ys_sc[pl.ds(r0, rows), :] = jnp.where(mine, yb, ys_sc[pl.ds(r0, rows), :].astype(F32)).astype(BF16)
        return carry

    lax.fori_loop(0, (count_e + rows - 1) // rows, block, 0)

    @pl.when(e == N_EXPERTS - 1)
    def _():
        route = route_ref[...]
        dcol = dcol_sc[...]
        d0, d1 = dcol[:, 0:1], dcol[:, 1:2]
        w0, w1 = route[:, 2:3], route[:, 3:4]
        r = lax.broadcasted_iota(I32, (1, na), 1).astype(F32)
        comb = (jnp.where(d0 == r, w0, 0.0) + jnp.where(d1 == r, w1, 0.0)).astype(BF16)
        moe = jnp.dot(comb, ys_sc[:na, :], preferred_element_type=F32)
        o_ref[...] = x_ref[...] + g_ref[0] * moe


def moe_experts(h, route, x, gate, w1_bf16, w3_bf16, w2_bf16, chunk=1024, rows=128):
    bsz, seq, d = x.shape
    chunk = _pick_tile(seq, (chunk, 512, 256))
    n_tok = bsz * seq
    n_chunks = n_tok // chunk
    per_b = seq // chunk
    ff = w1_bf16.shape[-1]
    na = -(-(2 * chunk + N_EXPERTS * (MOE_PAD - 1)) // MOE_SEG) * MOE_SEG
    route2 = route.reshape(n_tok, LANES)
    erow = route2[:, :2].astype(I32).T.reshape(2, n_chunks, chunk).transpose(1, 0, 2)
    out = pl.pallas_call(
        functools.partial(_moe_kernel, chunk=chunk, rows=rows, na=na),
        grid=(n_chunks, N_EXPERTS),
        in_specs=[pl.BlockSpec((1, 2, chunk), lambda c, e: (c, 0, 0)),
                  pl.BlockSpec((chunk, LANES), lambda c, e: (c, 0)),
                  pl.BlockSpec((chunk, d), lambda c, e: (c, 0)),
                  pl.BlockSpec((chunk, d), lambda c, e: (c, 0)),
                  pl.BlockSpec((1, 1, d), lambda c, e: (c // per_b, 0, 0)),
                  pl.BlockSpec((1, d, ff), lambda c, e: (e, 0, 0)),
                  pl.BlockSpec((1, d, ff), lambda c, e: (e, 0, 0)),
                  pl.BlockSpec((1, ff, d), lambda c, e: (e, 0, 0))],
        out_specs=pl.BlockSpec((chunk, d), lambda c, e: (c, 0)),
        out_shape=jax.ShapeDtypeStruct((n_tok, d), F32),
        scratch_shapes=[pltpu.VMEM((na + rows, d), BF16), pltpu.VMEM((na + rows, d), BF16),
                        pltpu.VMEM((N_EXPERTS, LANES), F32), pltpu.VMEM((2, 1, chunk), F32),
                        pltpu.VMEM((chunk, LANES), F32)],
        compiler_params=_cparams(("parallel", "arbitrary")),
        name="moe_experts",
    )(erow, route2, h.reshape(n_tok, d), x.reshape(n_tok, d), gate, w1_bf16, w3_bf16, w2_bf16)
    return out.reshape(bsz, seq, d)


def kernel(x, c, ada_w, ada_b, moe_wg, moe_bg, moe_we, moe_be, moe_w1, moe_w3, moe_w2,
           ev_w_in, ev_w_out, a_qn, a_kn, a_lq1, a_lk1, a_lq2, a_lk2, a_subln,
           b_qn, b_kn, b_idx_kn, od_w_in, od_w_out, c_qn, c_kn,
           s5_lam_re, s5_lam_im, s5_b_re, s5_b_im, s5_c_re, s5_c_im, s5_d, s5_log_dt,
           s5_glu_w, s5_glu_b):
    bsz, seq, d = x.shape
    depth = ada_w.shape[0]
    mod = ada_mod(c, ada_w, ada_b)
    rope_h = rope_lane_tables(seq, HEAD_DIM, ROT_HEAD)
    rope_i = rope_lane_tables(seq, IDX_DIM, ROT_IDX)
    qk_scale = HEAD_DIM ** -0.5 * math.log2(math.e)
    half_h, half_i = ROT_HEAD // 2, ROT_IDX // 2
    hw = A_HEADS * 2 * HEAD_DIM

    def pad_cols(w):
        n = w.shape[1]
        npad = -(-n // LANES) * LANES
        return jnp.zeros((w.shape[0], npad), F32).at[:, :n].set(w).astype(BF16)

    for l in range(depth):
        sh1, sc1, g1, sh2, sc2, g2 = [m[:, None, :] for m in jnp.split(mod[l], 6, axis=-1)]
        i = l // 2
        if l % 2 == 0:
            lam_init = 0.8 - 0.6 * math.exp(-0.3 * l)
            z = in_proj(x, sc1, sh1, pad_cols(ev_w_in[i]))
            q_t, v_a, v_b = ((BF16, "t"),), ((BF16, ("aug", 2 * HEAD_DIM)),), ((BF16, ("aug", HEAD_DIM)),)
            (aq,) = prep(z, 0, hw, hd=HEAD_DIM, gain=a_qn[i], rope=rope_h, half=half_h, scale=qk_scale, outs=q_t)
            (ak,) = prep(z, hw, hw, hd=HEAD_DIM, gain=a_kn[i], rope=rope_h, half=half_h)
            (av,) = prep(z, 2 * hw, hw, hd=HEAD_DIM, outs=v_a)
            o_a = diff_attention(aq, ak, av, a_lq1[i], a_lk1[i], a_lq2[i], a_lk2[i], a_subln[i], lam_init,
                                 _logit_bound(a_qn[i], a_kn[i]) <= MAX_FAST_LOGIT)
            (bq,) = prep(z, 3 * hw, hw, hd=HEAD_DIM, gain=b_qn[i], rope=rope_h, half=half_h, scale=qk_scale, outs=q_t)
            (bk,) = prep(z, 4 * hw, hw, hd=HEAD_DIM, gain=b_kn[i], rope=rope_h, half=half_h)
            (bv,) = prep(z, 5 * hw, hw, hd=HEAD_DIM, outs=v_b)
            iq_off = 6 * hw
            ik_off = iq_off + IDX_HEADS * IDX_DIM
            (iq,) = prep(z, iq_off, IDX_HEADS * IDX_DIM, hd=IDX_DIM, rope=rope_i, half=half_i, outs=q_t)
            ik, iw = prep(z, ik_off, LANES, hd=IDX_DIM, gain=b_idx_kn[i], rope=rope_i, half=half_i,
                          outs=((BF16, "n"), (F32, ("rawt", IDX_DIM, IDX_DIM + IDX_HEADS))))
            o_b = dsa_attention(bq, bk, bv, iq, ik, iw, _logit_bound(b_qn[i], b_kn[i]) <= MAX_FAST_LOGIT)
            x = out_proj(o_a, o_b, ev_w_out[i].astype(BF16), x, g1, a_t=True, b_t=True)
        else:
            z = in_proj(x, sc1, sh1, pad_cols(od_w_in[i]))
            (cq,) = prep(z, 0, hw, hd=HEAD_DIM, gain=c_qn[i], rope=rope_h, half=half_h, scale=qk_scale,
                         outs=((F32, "t"),))
            ck32, ck = prep(z, hw, hw, hd=HEAD_DIM, gain=c_kn[i], rope=rope_h, half=half_h,
                            outs=((F32, "n"), (BF16, "n")))
            (cv,) = prep(z, 2 * hw, hw, hd=HEAD_DIM, outs=((BF16, ("aug", HEAD_DIM)),))
            o_c = moba_attention(cq, block_means(ck32), ck, cv, _logit_bound(c_qn[i], c_kn[i]) <= MAX_FAST_LOGIT)
            ops = s5_operators(s5_lam_re[i], s5_lam_im[i], s5_b_re[i], s5_b_im[i], s5_c_re[i], s5_c_im[i],
                               s5_d[i], s5_log_dt[i], S5_CHUNK)
            y = s5_layer(z[:, :, 3 * hw:4 * hw], ops)
            y = glu(y.reshape(bsz * seq, hw), s5_glu_w[i].astype(BF16), s5_glu_b[i]).reshape(bsz, seq, hw)
            x = out_proj(o_c, y, od_w_out[i].astype(BF16), x, g1, a_t=True)
        h, route = moe_router(x, sc2, sh2, moe_wg[l], moe_bg[l], moe_we[l], moe_be[l])
        x = moe_experts(h, route, x, g2, moe_w1[l].astype(BF16), moe_w3[l].astype(BF16), moe_w2[l].astype(BF16))
    return x
```

```python
import functools
import math

import jax
import jax.numpy as jnp
from jax import lax
from jax.experimental import pallas as pl
from jax.experimental.pallas import tpu as pltpu

F32 = jnp.float32
BF16 = jnp.bfloat16
I32 = jnp.int32
HI = lax.Precision.HIGHEST

LANES = 128
NEG = -1e30
I32_MIN = -2147483648

HEAD_DIM = 64
A_HEADS = 4
B_HEADS = 8
IDX_HEADS = 8
IDX_DIM = 32
DSA_TOPK = 256
C_HEADS = 8
MOBA_BLOCK = 256
MOBA_TOPK = 3
S5_GROUP_CH = 16
S5_GROUPS = 32
S5_STATE = 64
ROPE_THETA = 500000.0
ROT_HEAD = HEAD_DIM // 4
ROT_IDX = IDX_DIM // 4
N_GROUPS = 4
EXPERTS_PER_GROUP = 8
N_EXPERTS = N_GROUPS * EXPERTS_PER_GROUP
EXPERT_FF = 512
NORM_EPS = 1e-6
S5_CHUNK = 16


def _cparams(sem, vmem_mb=48):
    return pltpu.CompilerParams(dimension_semantics=sem, vmem_limit_bytes=vmem_mb * 1024 * 1024)


def _nt_dot(a, b, precision=None):
    return lax.dot_general(a, b, (((1,), (1,)), ((), ())), precision=precision,
                           preferred_element_type=F32)


def _mod_kernel(c_ref, w_ref, b_ref, o_ref):
    c = c_ref[...]
    ca = c * jax.nn.sigmoid(c)
    o_ref[0] = jnp.dot(ca, w_ref[0], precision=HI, preferred_element_type=F32) + b_ref[0]


def ada_mod(c, ada_w, ada_b):
    depth, d, n6 = ada_w.shape
    bsz = c.shape[0]
    rows = 8
    c_pad = jnp.zeros((rows, d), F32).at[:bsz].set(c)
    out = pl.pallas_call(
        _mod_kernel,
        grid=(depth, n6 // d),
        in_specs=[pl.BlockSpec((rows, d), lambda l, n: (0, 0)),
                  pl.BlockSpec((1, d, d), lambda l, n: (l, 0, n)),
                  pl.BlockSpec((1, 1, d), lambda l, n: (l, 0, n))],
        out_specs=pl.BlockSpec((1, rows, d), lambda l, n: (l, 0, n)),
        out_shape=jax.ShapeDtypeStruct((depth, rows, n6), F32),
        compiler_params=_cparams(("parallel", "parallel")),
        name="ada_mod",
    )(c_pad, ada_w, ada_b.reshape(depth, 1, n6))
    return out[:, :bsz]


def _norm_mod(x, sc, sh):
    y = x * lax.rsqrt(jnp.mean(x * x, axis=-1, keepdims=True) + NORM_EPS)
    return y * (1.0 + sc) + sh


def _inproj_kernel(x_ref, sc_ref, sh_ref, w_ref, o_ref, h_sc):
    @pl.when(pl.program_id(2) == 0)
    def _():
        h_sc[...] = _norm_mod(x_ref[0], sc_ref[0], sh_ref[0]).astype(BF16)

    o_ref[0] = jnp.dot(h_sc[...], w_ref[...], preferred_element_type=F32)


def _pick_tile(n, prefs):
    for p in prefs:
        if n % p == 0:
            return p
    return n


def in_proj(x, sc, sh, w_bf16, tm=512):
    bsz, seq, d = x.shape
    n = w_bf16.shape[1]
    tm = _pick_tile(seq, (tm, 256, 128))
    tn = _pick_tile(n, (1152, 1024, 512, 384, 256, 128))
    return pl.pallas_call(
        _inproj_kernel,
        grid=(bsz, seq // tm, n // tn),
        in_specs=[pl.BlockSpec((1, tm, d), lambda b, i, j: (b, i, 0)),
                  pl.BlockSpec((1, 1, d), lambda b, i, j: (b, 0, 0)),
                  pl.BlockSpec((1, 1, d), lambda b, i, j: (b, 0, 0)),
                  pl.BlockSpec((d, tn), lambda b, i, j: (0, j))],
        out_specs=pl.BlockSpec((1, tm, tn), lambda b, i, j: (b, i, j)),
        out_shape=jax.ShapeDtypeStruct((bsz, seq, n), F32),
        scratch_shapes=[pltpu.VMEM((tm, d), BF16)],
        compiler_params=_cparams(("parallel", "parallel", "arbitrary")),
        name="in_proj",
    )(x, sc, sh, w_bf16)


def rope_lane_tables(seq, hd, rot):
    half = rot // 2
    pos = jnp.arange(seq, dtype=F32)
    inv = ROPE_THETA ** (-(jnp.arange(0, rot, 2, dtype=F32) / rot))
    ang = pos[:, None] * inv[None, :]
    cos, sin = jnp.cos(ang), jnp.sin(ang)
    dl = jnp.arange(LANES) % hd
    first = dl < half
    second = (dl >= half) & (dl < rot)
    idx = jnp.where(first, dl, jnp.where(second, dl - half, 0))
    cos_l = jnp.where(first | second, cos[:, idx], 1.0)
    sa_l = jnp.where(second, sin[:, idx], 0.0)
    sb_l = jnp.where(first, -sin[:, idx], 0.0)
    return cos_l.astype(F32), sa_l.astype(F32), sb_l.astype(F32)


def _prep_out_rows(mode, width):
    if mode == "t":
        return width
    if mode[0] == "aug":
        return (width // mode[1]) * (mode[1] + ONES_ROWS)
    return mode[2] - mode[1]


def _prep_kernel(*refs, hd, half, do_norm, do_rope, scale, outs):
    z_ref, g_ref, gm_ref, cos_ref, sa_ref, sb_ref = refs[:6]
    o_refs = refs[6:6 + len(outs)]
    x = z_ref[0]
    raw = x
    width = x.shape[-1]
    if do_norm:
        ms = jnp.dot((x * x).astype(BF16), gm_ref[...], preferred_element_type=F32) * (1.0 / hd)
        x = x * lax.rsqrt(ms + NORM_EPS) * g_ref[...]
    if do_rope:
        cos, sa, sb = cos_ref[...], sa_ref[...], sb_ref[...]
        cols = []
        for r in range(width // LANES):
            xc = x[:, r * LANES:(r + 1) * LANES]
            cols.append(xc * cos + pltpu.roll(xc, half, 1) * sa + pltpu.roll(xc, LANES - half, 1) * sb)
        x = cols[0] if len(cols) == 1 else jnp.concatenate(cols, axis=-1)
    if scale != 1.0:
        x = x * scale
    tm = x.shape[0]
    for o, (_, mode) in zip(o_refs, outs):
        if mode == "n":
            o[0] = x.astype(o.dtype)
        elif mode == "t":
            o[0] = x.T.astype(o.dtype)
        elif mode[0] == "aug":
            dv = mode[1]
            xt = x.T
            ones = jnp.ones((ONES_ROWS, tm), F32)
            parts = []
            for h in range(width // dv):
                parts += [xt[h * dv:(h + 1) * dv], ones]
            o[0] = jnp.concatenate(parts, axis=0).astype(o.dtype)
        else:
            o[0] = raw.T[mode[1]:mode[2]].astype(o.dtype)


def prep(z, col_off, width, *, hd, gain=None, rope=None, half=0, scale=1.0, outs=((BF16, "n"),), tm=512):
    bsz, seq, _ = z.shape
    tm = _pick_tile(seq, (tm, 256, 128))
    cb = col_off // width
    assert col_off % width == 0 and width % LANES == 0
    do_norm = gain is not None
    do_rope = rope is not None
    g = jnp.tile(gain.astype(F32), width // hd).reshape(1, width) if do_norm else jnp.ones((1, width), F32)
    gi = jnp.arange(width) // hd
    gm = (gi[:, None] == gi[None, :]).astype(BF16)
    if do_rope:
        cos_l, sa_l, sb_l = rope
    else:
        cos_l = sa_l = sb_l = jnp.zeros((seq, LANES), F32)
    out_specs, out_shape = [], []
    for dt, mode in outs:
        if mode == "n":
            out_specs.append(pl.BlockSpec((1, tm, width), lambda b, i: (b, i, 0)))
            out_shape.append(jax.ShapeDtypeStruct((bsz, seq, width), dt))
        else:
            rows = _prep_out_rows(mode, width)
            out_specs.append(pl.BlockSpec((1, rows, tm), lambda b, i: (b, 0, i)))
            out_shape.append(jax.ShapeDtypeStruct((bsz, rows, seq), dt))
    return pl.pallas_call(
        functools.partial(_prep_kernel, hd=hd, half=half, do_norm=do_norm, do_rope=do_rope,
                          scale=scale, outs=tuple(outs)),
        grid=(bsz, seq // tm),
        in_specs=[pl.BlockSpec((1, tm, width), lambda b, i: (b, i, cb)),
                  pl.BlockSpec((1, width), lambda b, i: (0, 0)),
                  pl.BlockSpec((width, width), lambda b, i: (0, 0)),
                  pl.BlockSpec((tm, LANES), lambda b, i: (i, 0)),
                  pl.BlockSpec((tm, LANES), lambda b, i: (i, 0)),
                  pl.BlockSpec((tm, LANES), lambda b, i: (i, 0))],
        out_specs=out_specs,
        out_shape=out_shape,
        compiler_params=_cparams(("parallel", "parallel")),
        name="prep",
    )(z, g, gm, cos_l, sa_l, sb_l)


ATT_TILE = 256
ATT_GROUP = 8


def _for_tile_groups(n_tiles, body, group_size=ATT_GROUP):
    def group(jg, carry):
        body(group_size * jg, group_size)
        return carry

    lax.fori_loop(0, n_tiles // group_size, group, 0)
    size = group_size // 2
    while size >= 1:
        def tail(size=size):
            body((n_tiles // (2 * size)) * (2 * size), size)
        pl.when((n_tiles // size) % 2 == 1)(tail)
        size //= 2


ONES_ROWS = 16
MAX_FAST_LOGIT = 60.0


def _attn_update(s_t, vt_aug, m_ref, acc_ref, idx, fast):
    if fast:
        acc_ref[idx] += jnp.dot(vt_aug, jnp.exp2(s_t).astype(BF16), preferred_element_type=F32)
    else:
        m_prev = m_ref[idx]
        m_new = jnp.maximum(m_prev, jnp.max(s_t, axis=0, keepdims=True))
        p_t = jnp.exp2(s_t - m_new).astype(BF16)
        acc_ref[idx] = (jnp.exp2(m_prev - m_new) * acc_ref[idx]
                        + jnp.dot(vt_aug, p_t, preferred_element_type=F32))
        m_ref[idx] = m_new


def _attn_init(m_sc, acc_sc):
    m_sc[...] = jnp.full(m_sc.shape, NEG, F32)
    acc_sc[...] = jnp.zeros(acc_sc.shape, F32)


def _attn_out(acc_ref, idx, dv):
    acc = acc_ref[idx]
    return acc[:dv] / acc[dv:dv + 1]


def _logit_bound(gain_q, gain_k):
    return (HEAD_DIM ** 0.5 * math.log2(math.e) * 1.02) * jnp.max(jnp.abs(gain_q)) * jnp.max(jnp.abs(gain_k))


def _half_rows(x, sub):
    row = lax.broadcasted_iota(I32, (LANES, 1), 0)
    return jnp.where((row >= sub * HEAD_DIM) & (row < (sub + 1) * HEAD_DIM), x, jnp.zeros_like(x))


def _causal_t(t):
    return lax.broadcasted_iota(I32, (t, t), 0) <= lax.broadcasted_iota(I32, (t, t), 1)


def _diff_attn_kernel(qt_ref, k_ref, vt_ref, lq1_ref, lk1_ref, lq2_ref, lk2_ref, g_ref, o_ref,
                      m_sc, acc_sc, *, t, lam_init, fast):
    i = pl.program_id(2)
    _attn_init(m_sc, acc_sc)
    qt = qt_ref[0]
    qc = [_half_rows(qt, c) for c in range(2)]

    def attend(jt, n, mask_fn):
        off = pl.multiple_of(jt * t, t)
        k = k_ref[0, pl.ds(off, n * t), :]
        vt = vt_ref[0, :, pl.ds(off, n * t)]
        s = [mask_fn(jnp.dot(k, qc[c], preferred_element_type=F32)) for c in range(2)]
        for c in range(2):
            _attn_update(s[c], vt, m_sc, acc_sc, c, fast)

    _for_tile_groups(i, lambda jt, n: attend(jt, n, lambda s_t: s_t), DIFF_GROUP)
    causal = _causal_t(t)
    attend(i, 1, lambda s_t: jnp.where(causal, s_t, NEG))

    lam = (jnp.exp(jnp.sum(lq1_ref[...] * lk1_ref[...], axis=-1, keepdims=True))
           - jnp.exp(jnp.sum(lq2_ref[...] * lk2_ref[...], axis=-1, keepdims=True)) + lam_init)
    o = _attn_out(acc_sc, 0, LANES) - lam * _attn_out(acc_sc, 1, LANES)
    o = o * lax.rsqrt(jnp.mean(o * o, axis=0, keepdims=True) + NORM_EPS)
    o_ref[0] = o * g_ref[...] * (1.0 - lam_init)


DIFF_TILE = 512
DIFF_GROUP = 4


def diff_attention(qt, k, vta, lq1, lk1, lq2, lk2, subln, lam_init, fast_ok):
    bsz, width, seq = qt.shape
    n_h = width // LANES
    t = _pick_tile(seq, (DIFF_TILE, 256, 128))
    rows = LANES + ONES_ROWS
    vec = lambda a: a.astype(F32).reshape(1, -1)
    small = pl.BlockSpec((1, HEAD_DIM), lambda b, h, i: (0, 0))

    def call(fast, *args):
        return pl.pallas_call(
            functools.partial(_diff_attn_kernel, t=t, lam_init=lam_init, fast=fast),
            grid=(bsz, n_h, seq // t),
            in_specs=[pl.BlockSpec((1, LANES, t), lambda b, h, i: (b, h, i)),
                      pl.BlockSpec((1, seq, LANES), lambda b, h, i: (b, 0, h)),
                      pl.BlockSpec((1, rows, seq), lambda b, h, i: (b, h, 0)),
                      small, small, small, small,
                      pl.BlockSpec((LANES, 1), lambda b, h, i: (0, 0))],
            out_specs=pl.BlockSpec((1, LANES, t), lambda b, h, i: (b, h, i)),
            out_shape=jax.ShapeDtypeStruct((bsz, width, seq), F32),
            scratch_shapes=[pltpu.VMEM((2, 1, t), F32), pltpu.VMEM((2, rows, t), F32)],
            compiler_params=_cparams(("parallel", "parallel", "arbitrary")),
            name="diff_attn_fast" if fast else "diff_attn_safe",
        )(*args)

    args = (qt, k, vta, vec(lq1), vec(lk1), vec(lq2), vec(lk2), subln.astype(F32).reshape(-1, 1))
    return lax.cond(fast_ok, functools.partial(call, True), functools.partial(call, False), *args)


def _sortable_key(score):
    bits = pltpu.bitcast(score + 0.0, I32)
    return bits ^ ((bits >> 31) & 0x7FFFFFFF)


def _dsa_kernel(qt_ref, k_ref, vt_ref, qit_ref, ki_ref, wt_ref, o_ref, kb_sc, m_sc, acc_sc,
                *, t, nk, n_sel, fast):
    i = pl.program_id(1)
    pair = pl.program_id(2)
    n_chunks = i + 1
    krow = lax.broadcasted_iota(I32, (t, t), 0)

    def count_where(pred_fn):
        def chunk_count(cidx):
            g = jnp.where(pred_fn(kb_sc[cidx], cidx), 1, 0)
            return jnp.sum(g.reshape(t // 8, 8, t), axis=0)

        def body(jp, acc):
            return acc + chunk_count(2 * jp) + chunk_count(2 * jp + 1)

        acc = lax.fori_loop(0, n_chunks // 2, body, jnp.zeros((8, t), I32))
        last = chunk_count(n_chunks - 1)
        acc = acc + jnp.where(n_chunks % 2 == 1, last, 0)
        return jnp.sum(acc, axis=0, keepdims=True)

    @pl.when(pair == 0)
    def _():
        qit = qit_ref[0]
        pad = jnp.zeros((LANES - IDX_DIM, t), BF16)
        q_heads = [jnp.concatenate([qit[h * IDX_DIM:(h + 1) * IDX_DIM], pad], axis=0)
                   for h in range(IDX_HEADS)]
        wt = wt_ref[0]
        w_rows = [wt[h:h + 1] for h in range(IDX_HEADS)]
        qcol = i * t + lax.broadcasted_iota(I32, (1, t), 1)

        def score_chunk(cidx, mm):
            off = pl.multiple_of(cidx * t, t)
            ki = ki_ref[0, pl.ds(off, t), :]
            score = jnp.zeros((t, t), F32)
            for h in range(IDX_HEADS):
                lg = jnp.dot(ki, q_heads[h], preferred_element_type=F32)
                score = score + w_rows[h] * jnp.maximum(lg, 0.0)
            key = _sortable_key(score)
            visible = krow + off <= qcol
            kb_sc[cidx] = jnp.where(visible, key, I32_MIN)
            lo_c = jnp.min(jnp.where(visible, key, 2147483647).reshape(t // 8, 8, t), axis=0)
            hi_c = jnp.max(jnp.where(visible, key, I32_MIN).reshape(t // 8, 8, t), axis=0)
            return jnp.minimum(mm[0], lo_c), jnp.maximum(mm[1], hi_c)

        kmin, kmax = lax.fori_loop(0, n_chunks, score_chunk, (jnp.full((8, t), 2147483647, I32),
                                                              jnp.full((8, t), I32_MIN, I32)))
        n_vis = qcol + 1
        few = n_vis <= n_sel
        kmin = jnp.min(kmin, axis=0, keepdims=True)
        kmax = jnp.max(kmax, axis=0, keepdims=True)
        span = 1 << 25
        probe = jnp.maximum(jnp.where(kmax > I32_MIN + span, kmax - span, kmin), kmin)
        c_probe = count_where(lambda kk, cidx: kk >= probe)
        probe_ok = c_probe >= n_sel
        lo0 = jnp.where(few, I32_MIN, jnp.where(probe_ok, probe, kmin))
        hi0 = jnp.where(few, I32_MIN + 1, jnp.where(probe_ok, kmax + 1, probe))

        def open_lanes(lo, hi, c_lo):
            return jnp.logical_and(c_lo != n_sel, hi > lo + 1)

        def bisect_cond(st):
            it, lo, hi, c_lo, c_hi = st
            return jnp.logical_and(it < 34, jnp.max(jnp.where(open_lanes(lo, hi, c_lo), 1, 0)) > 0)

        def bisect(st):
            it, lo, hi, c_lo, c_hi = st
            mid = (lo >> 1) + (hi >> 1) + (lo & hi & 1)
            cnt = count_where(lambda kk, cidx: kk >= mid)
            up = jnp.logical_and(open_lanes(lo, hi, c_lo), cnt >= n_sel)
            dn = jnp.logical_and(open_lanes(lo, hi, c_lo), cnt < n_sel)
            return (it + 1, jnp.where(up, mid, lo), jnp.where(dn, mid, hi),
                    jnp.where(up, cnt, c_lo), jnp.where(dn, cnt, c_hi))

        c_lo0 = jnp.where(few, n_sel, jnp.where(probe_ok, c_probe, n_vis))
        c_hi0 = jnp.where(jnp.logical_or(few, probe_ok), 0, c_probe)
        _, thr, _, c_lo, c_hi = lax.while_loop(
            bisect_cond, bisect, (jnp.int32(0), lo0, hi0, c_lo0, c_hi0))

        need = n_sel - c_hi
        all_ties = c_lo == n_sel
        jmax0 = jnp.where(few, -1, jnp.where(all_ties, 2147483647, -1))
        unresolved = jnp.logical_and(jnp.logical_not(few), jnp.logical_not(all_ties))

        def tie_search(_):
            def step(it, lohi):
                lo, hi = lohi
                mid = (lo + hi) >> 1
                cnt = count_where(lambda kk, cidx: jnp.logical_and(kk == thr, krow + cidx * t <= mid))
                ok = cnt >= need
                return jnp.where(ok, lo, mid), jnp.where(ok, mid, hi)
            n_steps = 1 + max(1, (nk * t - 1).bit_length())
            _, hi = lax.fori_loop(0, n_steps, step, (jnp.full((1, t), -1, I32),
                                                     jnp.full((1, t), nk * t - 1, I32)))
            return jnp.where(unresolved, hi, jmax0)

        any_unresolved = jnp.max(jnp.where(unresolved, 1, 0)) > 0
        jmax = lax.cond(any_unresolved, tie_search, lambda _: jmax0, 0)

        def bias_chunk(cidx, carry):
            kk = kb_sc[cidx]
            cut = jnp.where(krow + cidx * t <= jmax, thr - 1, thr)
            bias = jnp.where(kk > cut, 0.0, NEG)
            kb_sc[cidx] = pltpu.bitcast(bias.astype(F32), I32)
            return carry

        lax.fori_loop(0, n_chunks, bias_chunk, 0)

    _attn_init(m_sc, acc_sc)
    qt = qt_ref[0]
    qh = [_half_rows(qt, sub) for sub in range(2)]
    hrows = HEAD_DIM + ONES_ROWS

    def attend(jt, n):
        off = pl.multiple_of(jt * t, t)
        k = k_ref[0, pl.ds(off, n * t), :]
        vt = vt_ref[0, :, pl.ds(off, n * t)]
        bias = jnp.concatenate([pltpu.bitcast(kb_sc[jt + c], F32) for c in range(n)], axis=0)
        s = [jnp.dot(k, qh[sub], preferred_element_type=F32) + bias for sub in range(2)]
        for sub in range(2):
            _attn_update(s[sub], vt[sub * hrows:(sub + 1) * hrows], m_sc, acc_sc, sub, fast)

    _for_tile_groups(n_chunks, attend)
    o_ref[0] = jnp.concatenate([_attn_out(acc_sc, sub, HEAD_DIM) for sub in range(2)], axis=0)


def dsa_attention(qt, k, vta, qit, ki, iwt, fast_ok):
    bsz, width, seq = qt.shape
    t = _pick_tile(seq, (ATT_TILE, 128))
    nq = seq // t
    n_sel = min(DSA_TOPK, seq // 4)
    prow = 2 * (HEAD_DIM + ONES_ROWS)

    def call(fast, *args):
        return pl.pallas_call(
            functools.partial(_dsa_kernel, t=t, nk=nq, n_sel=n_sel, fast=fast),
            grid=(bsz, nq, width // LANES),
            in_specs=[pl.BlockSpec((1, LANES, t), lambda b, i, p: (b, p, i)),
                      pl.BlockSpec((1, seq, LANES), lambda b, i, p: (b, 0, p)),
                      pl.BlockSpec((1, prow, seq), lambda b, i, p: (b, p, 0)),
                      pl.BlockSpec((1, IDX_HEADS * IDX_DIM, t), lambda b, i, p: (b, 0, i)),
                      pl.BlockSpec((1, seq, LANES), lambda b, i, p: (b, 0, 0)),
                      pl.BlockSpec((1, IDX_HEADS, t), lambda b, i, p: (b, 0, i))],
            out_specs=pl.BlockSpec((1, LANES, t), lambda b, i, p: (b, p, i)),
            out_shape=jax.ShapeDtypeStruct((bsz, width, seq), F32),
            scratch_shapes=[pltpu.VMEM((nq, t, t), I32), pltpu.VMEM((2, 1, t), F32),
                            pltpu.VMEM((2, HEAD_DIM + ONES_ROWS, t), F32)],
            compiler_params=_cparams(("parallel", "parallel", "arbitrary"), vmem_mb=56),
            name="dsa_attn_fast" if fast else "dsa_attn_safe",
        )(*args)

    args = (qt, k, vta, qit, ki, iwt)
    return lax.cond(fast_ok, functools.partial(call, True), functools.partial(call, False), *args)


def _kmean_kernel(k_ref, o_ref, *, nbb):
    x = k_ref[0]
    o_ref[0] = jnp.mean(x.reshape(nbb, MOBA_BLOCK, x.shape[-1]), axis=1)


def block_means(k_f32):
    bsz, seq, width = k_f32.shape
    nb = seq // MOBA_BLOCK
    nbb = 8 if nb % 8 == 0 else nb
    return pl.pallas_call(
        functools.partial(_kmean_kernel, nbb=nbb),
        grid=(bsz, nb // nbb),
        in_specs=[pl.BlockSpec((1, nbb * MOBA_BLOCK, width), lambda b, i: (b, i, 0))],
        out_specs=pl.BlockSpec((1, nbb, width), lambda b, i: (b, i, 0)),
        out_shape=jax.ShapeDtypeStruct((bsz, nb, width), F32),
        compiler_params=_cparams(("parallel", "parallel")),
        name="moba_kmean",
    )(k_f32)


def _moba_select_kernel(qt_ref, km_ref, o_ref, *, ts, n_sel, nbp):
    j = pl.program_id(2)
    blk = lax.broadcasted_iota(I32, (nbp, ts), 0)
    own = jnp.right_shift(j * ts + lax.broadcasted_iota(I32, (1, ts), 1), MOBA_BLOCK.bit_length() - 1)
    qt = qt_ref[0]
    for sub in range(2):
        gate = jnp.dot(km_ref[0], _half_rows(qt, sub), precision=HI, preferred_element_type=F32)
        g = jnp.where(blk < own, gate, -jnp.inf)
        sel = jnp.zeros((nbp, ts), F32)
        for _ in range(n_sel):
            mx = jnp.max(g, axis=0, keepdims=True)
            is_mx = jnp.logical_and(g == mx, mx > -jnp.inf)
            first = jnp.min(jnp.where(is_mx, blk, 2 * nbp), axis=0, keepdims=True)
            pick = blk == first
            sel = jnp.where(pick, 1.0, sel)
            g = jnp.where(pick, -jnp.inf, g)
        o_ref[0, sub] = sel


def _moba_kernel(qt_ref, sel_ref, k_ref, vt_ref, o_ref, m_sc, acc_sc, *, t, fast):
    i = pl.program_id(2)
    _attn_init(m_sc, acc_sc)
    qt = qt_ref[0]
    qb = [_half_rows(qt, sub).astype(BF16) for sub in range(2)]
    hrows = HEAD_DIM + ONES_ROWS

    def attend(kb, n, mask_fn):
        off = pl.multiple_of(kb * t, t)
        k = k_ref[0, pl.ds(off, n * t), :]
        vt = vt_ref[0, :, pl.ds(off, n * t)]
        s = [mask_fn(jnp.dot(k, qb[sub], preferred_element_type=F32), sub) for sub in range(2)]
        for sub in range(2):
            _attn_update(s[sub], vt[sub * hrows:(sub + 1) * hrows], m_sc, acc_sc, sub, fast)

    causal = _causal_t(t)
    attend(i, 1, lambda s_t, sub: jnp.where(causal, s_t, NEG))

    def past_blocks(kb, n):
        def row_mask(s_t, sub):
            rows = [jnp.where(sel_ref[0, sub, pl.ds(kb + c, 1), :] > 0.0, 0.0, NEG) for c in range(n)]
            bias = jnp.concatenate([jnp.broadcast_to(r, (t, t)) for r in rows], axis=0)
            return s_t + bias
        attend(kb, n, row_mask)

    _for_tile_groups(i, past_blocks)
    o_ref[0] = jnp.concatenate([_attn_out(acc_sc, sub, HEAD_DIM) for sub in range(2)], axis=0)


def moba_attention(qt_f32, kmean, k, vta, fast_ok):
    bsz, width, seq = qt_f32.shape
    t = MOBA_BLOCK
    nb = seq // MOBA_BLOCK
    n_sel = max(1, min(MOBA_TOPK, nb - 1))
    nbp = -(-nb // 8) * 8
    km = jnp.zeros((bsz, nbp, width), F32).at[:, :nb].set(kmean)
    prow = 2 * (HEAD_DIM + ONES_ROWS)
    n_pairs = width // LANES
    ts = _pick_tile(seq, (1024, 512, 256))
    sel = pl.pallas_call(
        functools.partial(_moba_select_kernel, ts=ts, n_sel=n_sel, nbp=nbp),
        grid=(bsz, n_pairs, seq // ts),
        in_specs=[pl.BlockSpec((1, LANES, ts), lambda b, h, j: (b, h, j)),
                  pl.BlockSpec((1, nbp, LANES), lambda b, h, j: (b, 0, h))],
        out_specs=pl.BlockSpec((1, 2, nbp, ts), lambda b, h, j: (b, h, 0, j)),
        out_shape=jax.ShapeDtypeStruct((bsz, 2 * n_pairs, nbp, seq), F32),
        compiler_params=_cparams(("parallel", "parallel", "parallel")),
        name="moba_select",
    )(qt_f32, km)

    def call(fast, *args):
        return pl.pallas_call(
            functools.partial(_moba_kernel, t=t, fast=fast),
            grid=(bsz, n_pairs, nb),
            in_specs=[pl.BlockSpec((1, LANES, t), lambda b, h, i: (b, h, i)),
                      pl.BlockSpec((1, 2, nbp, t), lambda b, h, i: (b, h, 0, i)),
                      pl.BlockSpec((1, seq, LANES), lambda b, h, i: (b, 0, h)),
                      pl.BlockSpec((1, prow, seq), lambda b, h, i: (b, h, 0))],
            out_specs=pl.BlockSpec((1, LANES, t), lambda b, h, i: (b, h, i)),
            out_shape=jax.ShapeDtypeStruct((bsz, width, seq), F32),
            scratch_shapes=[pltpu.VMEM((2, 1, t), F32), pltpu.VMEM((2, HEAD_DIM + ONES_ROWS, t), F32)],
            compiler_params=_cparams(("parallel", "parallel", "arbitrary")),
            name="moba_attn_fast" if fast else "moba_attn_safe",
        )(*args)

    args = (qt_f32, sel, k, vta)
    return lax.cond(fast_ok, functools.partial(call, True), functools.partial(call, False), *args)


def s5_operators(lam_re, lam_im, b_re, b_im, c_re, c_im, d_skip, log_dt, chunk):
    n_g, n_p = lam_re.shape
    n_c = d_skip.shape[-1]
    dt = jnp.exp(log_dt.astype(F32))[:, None]
    den = lam_re * lam_re + lam_im * lam_im
    mag = jnp.exp(lam_re * dt)
    abar_re = mag * jnp.cos(lam_im * dt)
    abar_im = mag * jnp.sin(lam_im * dt)
    coef_re = ((abar_re - 1.0) * lam_re + abar_im * lam_im) / den
    coef_im = (abar_im * lam_re - (abar_re - 1.0) * lam_im) / den
    bbar_re = coef_re[..., None] * b_re - coef_im[..., None] * b_im
    bbar_im = coef_re[..., None] * b_im + coef_im[..., None] * b_re
    tau = jnp.arange(chunk + 1, dtype=F32)[:, None, None]
    pmag = jnp.exp(lam_re * dt * tau)
    pw_re = pmag * jnp.cos(lam_im * dt * tau)
    pw_im = pmag * jnp.sin(lam_im * dt * tau)
    ab_re = pw_re[..., None] * bbar_re - pw_im[..., None] * bbar_im
    ab_im = pw_re[..., None] * bbar_im + pw_im[..., None] * bbar_re
    kern = (jnp.einsum('gop,tgpi->tgio', c_re, ab_re, precision=HI)
            - jnp.einsum('gop,tgpi->tgio', c_im, ab_im, precision=HI))
    t_idx = jnp.arange(chunk)
    lag = t_idx[None, :] - t_idx[:, None]
    toep = jnp.where((lag >= 0)[None, :, None, :, None],
                     kern[jnp.clip(lag, 0, chunk)].transpose(2, 0, 3, 1, 4), 0.0)
    eye = (jnp.eye(chunk)[None, :, None, :, None] * jnp.eye(n_c)[None, None, :, None, :])
    toep = toep + eye * d_skip[:, None, :, None, None]
    toep = toep.reshape(n_g, chunk * n_c, chunk * n_c)
    rev = chunk - 1 - t_idx
    w_in = jnp.concatenate([ab_re[rev].transpose(1, 0, 3, 2), ab_im[rev].transpose(1, 0, 3, 2)], axis=-1)
    w_in = w_in.reshape(n_g, chunk * n_c, 2 * n_p)
    a_re, a_im = pw_re[1:chunk + 1], pw_im[1:chunk + 1]
    st_re = (c_re[None] * a_re[:, :, None, :] - c_im[None] * a_im[:, :, None, :])
    st_im = (-c_re[None] * a_im[:, :, None, :] - c_im[None] * a_re[:, :, None, :])
    w_st = jnp.concatenate([st_re.transpose(1, 3, 0, 2), st_im.transpose(1, 3, 0, 2)], axis=1)
    w_st = w_st.reshape(n_g, 2 * n_p, chunk * n_c)
    return toep, w_in, w_st, pw_re[chunk], pw_im[chunk]


def _s5_in_kernel(u_ref, w_ref, o_ref):
    o_ref[0, 0] = jnp.dot(u_ref[0, 0], w_ref[0], precision=HI, preferred_element_type=F32)


def _s5_scan_kernel(x_ref, aa_ref, ab_ref, abs_ref, o_ref, h_sc, hs_sc, *, steps, n_p):
    @pl.when(pl.program_id(0) == 0)
    def _():
        h_sc[...] = jnp.zeros(h_sc.shape, F32)
        hs_sc[...] = jnp.zeros(hs_sc.shape, F32)

    aa, ab, abs_ = aa_ref[...], ab_ref[...], abs_ref[...]

    def body(t, carry):
        h, hs = carry
        o_ref[t] = h
        x = x_ref[t]
        xs = pltpu.roll(x, n_p, 1)
        return aa * h + ab * hs + x, aa * hs + abs_ * h + xs

    h, hs = lax.fori_loop(0, steps, body, (h_sc[...], hs_sc[...]))
    h_sc[...] = h
    hs_sc[...] = hs


def _s5_out_kernel(u_ref, h0_ref, t_ref, w_ref, o_ref):
    y = (jnp.dot(u_ref[0, 0], t_ref[0], precision=HI, preferred_element_type=F32)
         + jnp.dot(h0_ref[0, 0], w_ref[0], precision=HI, preferred_element_type=F32))
    o_ref[0, 0] = jax.nn.gelu(y)


def s5_layer(u, ops):
    toep, w_in, w_st, ac_re, ac_im = ops
    bsz, seq, width = u.shape
    n_g, n_p = ac_re.shape
    n_c = width // n_g
    chunk = S5_CHUNK
    nj = seq // chunk
    row = chunk * n_c
    ur = u.reshape(bsz, nj, chunk, n_g, n_c).transpose(0, 3, 1, 2, 4).reshape(bsz, n_g, nj, row)
    hloc = pl.pallas_call(
        _s5_in_kernel,
        grid=(bsz, n_g),
        in_specs=[pl.BlockSpec((1, 1, nj, row), lambda b, g: (b, g, 0, 0)),
                  pl.BlockSpec((1, row, 2 * n_p), lambda b, g: (g, 0, 0))],
        out_specs=pl.BlockSpec((1, 1, nj, 2 * n_p), lambda b, g: (b, g, 0, 0)),
        out_shape=jax.ShapeDtypeStruct((bsz, n_g, nj, 2 * n_p), F32),
        compiler_params=_cparams(("parallel", "parallel")),
        name="s5_in",
    )(ur, w_in)
    bg = bsz * n_g
    xs = hloc.transpose(2, 0, 1, 3).reshape(nj, bg, 2 * n_p)
    rep = lambda a: jnp.tile(a, (bsz, 1))
    aa = rep(jnp.concatenate([ac_re, ac_re], axis=-1))
    ab = rep(jnp.concatenate([-ac_im, ac_im], axis=-1))
    abs_ = rep(jnp.concatenate([ac_im, -ac_im], axis=-1))
    steps = _pick_tile(nj, (128, 64, 32, 16, 8))
    cst = pl.BlockSpec((bg, 2 * n_p), lambda t: (0, 0))
    h0 = pl.pallas_call(
        functools.partial(_s5_scan_kernel, steps=steps, n_p=n_p),
        grid=(nj // steps,),
        in_specs=[pl.BlockSpec((steps, bg, 2 * n_p), lambda t: (t, 0, 0)), cst, cst, cst],
        out_specs=pl.BlockSpec((steps, bg, 2 * n_p), lambda t: (t, 0, 0)),
        out_shape=jax.ShapeDtypeStruct((nj, bg, 2 * n_p), F32),
        scratch_shapes=[pltpu.VMEM((bg, 2 * n_p), F32), pltpu.VMEM((bg, 2 * n_p), F32)],
        compiler_params=_cparams(("arbitrary",)),
        name="s5_scan",
    )(xs, aa, ab, abs_)
    h0 = h0.reshape(nj, bsz, n_g, 2 * n_p).transpose(1, 2, 0, 3)
    y = pl.pallas_call(
        _s5_out_kernel,
        grid=(bsz, n_g),
        in_specs=[pl.BlockSpec((1, 1, nj, row), lambda b, g: (b, g, 0, 0)),
                  pl.BlockSpec((1, 1, nj, 2 * n_p), lambda b, g: (b, g, 0, 0)),
                  pl.BlockSpec((1, row, row), lambda b, g: (g, 0, 0)),
                  pl.BlockSpec((1, 2 * n_p, row), lambda b, g: (g, 0, 0))],
        out_specs=pl.BlockSpec((1, 1, nj, row), lambda b, g: (b, g, 0, 0)),
        out_shape=jax.ShapeDtypeStruct((bsz, n_g, nj, row), F32),
        compiler_params=_cparams(("parallel", "parallel")),
        name="s5_out",
    )(ur, h0, toep, w_st)
    return y.reshape(bsz, n_g, nj, chunk, n_c).transpose(0, 2, 3, 1, 4).reshape(bsz, seq, width)


def _glu_kernel(y_ref, w_ref, b_ref, o_ref):
    y = y_ref[...]
    gate = jnp.dot(y.astype(BF16), w_ref[...], preferred_element_type=F32) + b_ref[...]
    o_ref[...] = (y * jax.nn.sigmoid(gate)).astype(o_ref.dtype)


def glu(y, w_bf16, b, tm=1024):
    n_tok, width = y.shape
    tm = _pick_tile(n_tok, (tm, 512, 256, 128))
    return pl.pallas_call(
        _glu_kernel,
        grid=(n_tok // tm,),
        in_specs=[pl.BlockSpec((tm, width), lambda i: (i, 0)),
                  pl.BlockSpec((width, width), lambda i: (0, 0)),
                  pl.BlockSpec((1, width), lambda i: (0, 0))],
        out_specs=pl.BlockSpec((tm, width), lambda i: (i, 0)),
        out_shape=jax.ShapeDtypeStruct((n_tok, width), F32),
        compiler_params=_cparams(("parallel",)),
        name="s5_glu",
    )(y, w_bf16, b.astype(F32).reshape(1, width))


def _outproj_kernel(a_ref, b_ref, w_ref, x_ref, g_ref, o_ref, *, half, a_t, b_t):
    a = a_ref[0].T if a_t else a_ref[0]
    b = b_ref[0].T if b_t else b_ref[0]
    y = (jnp.dot(a.astype(BF16), w_ref[:half, :], preferred_element_type=F32)
         + jnp.dot(b.astype(BF16), w_ref[half:, :], preferred_element_type=F32))
    o_ref[0] = x_ref[0] + g_ref[0] * y


def out_proj(a, b, w_bf16, x, gate, a_t=False, b_t=False, tm=512):
    bsz, seq, d = x.shape
    half = w_bf16.shape[0] // 2
    tm = _pick_tile(seq, (tm, 256, 128))

    def spec(transposed):
        if transposed:
            return pl.BlockSpec((1, half, tm), lambda bb, i: (bb, 0, i))
        return pl.BlockSpec((1, tm, half), lambda bb, i: (bb, i, 0))

    return pl.pallas_call(
        functools.partial(_outproj_kernel, half=half, a_t=a_t, b_t=b_t),
        grid=(bsz, seq // tm),
        in_specs=[spec(a_t), spec(b_t),
                  pl.BlockSpec((2 * half, d), lambda bb, i: (0, 0)),
                  pl.BlockSpec((1, tm, d), lambda bb, i: (bb, i, 0)),
                  pl.BlockSpec((1, 1, d), lambda bb, i: (bb, 0, 0))],
        out_specs=pl.BlockSpec((1, tm, d), lambda bb, i: (bb, i, 0)),
        out_shape=jax.ShapeDtypeStruct((bsz, seq, d), F32),
        compiler_params=_cparams(("parallel", "parallel")),
        name="out_proj",
    )(a, b, w_bf16, x, gate)


def _router_kernel(x_ref, sc_ref, sh_ref, wr_ref, br_ref, h_ref, r_ref):
    h = _norm_mod(x_ref[0], sc_ref[0], sh_ref[0])
    h_ref[0] = h.astype(BF16)
    logits = jnp.dot(h, wr_ref[...], precision=HI, preferred_element_type=F32) + br_ref[...]
    tm = logits.shape[0]
    lane = lax.broadcasted_iota(I32, (tm, LANES), 1)
    big = 4 * LANES
    coarse = jnp.where(lane < N_GROUPS, logits, -jnp.inf)
    cmax = jnp.max(coarse, axis=1, keepdims=True)
    grp = jnp.min(jnp.where(coarse == cmax, lane, big), axis=1, keepdims=True)
    p_grp = 1.0 / jnp.sum(jnp.exp(coarse - cmax), axis=1, keepdims=True)
    lo = N_GROUPS + grp * EXPERTS_PER_GROUP
    fine = jnp.where(jnp.logical_and(lane >= lo, lane < lo + EXPERTS_PER_GROUP), logits, -jnp.inf)
    v1 = jnp.max(fine, axis=1, keepdims=True)
    i1 = jnp.min(jnp.where(fine == v1, lane, big), axis=1, keepdims=True)
    fine2 = jnp.where(lane == i1, -jnp.inf, fine)
    v2 = jnp.max(fine2, axis=1, keepdims=True)
    i2 = jnp.min(jnp.where(fine2 == v2, lane, big), axis=1, keepdims=True)
    e2 = jnp.exp(v2 - v1)
    wa = p_grp / (1.0 + e2)
    wb = p_grp * e2 / (1.0 + e2)
    ida = (i1 - N_GROUPS).astype(F32)
    idb = (i2 - N_GROUPS).astype(F32)
    r_ref[0] = jnp.where(lane == 0, ida, jnp.where(lane == 1, idb,
                         jnp.where(lane == 2, wa, jnp.where(lane == 3, wb, 0.0))))


def moe_router(x, sc, sh, wg, bg, we, be, tm=512):
    bsz, seq, d = x.shape
    tm = _pick_tile(seq, (tm, 256, 128))
    n_r = N_GROUPS + N_EXPERTS
    wr = jnp.zeros((d, LANES), F32).at[:, :N_GROUPS].set(wg).at[:, N_GROUPS:n_r].set(we)
    br = jnp.zeros((1, LANES), F32).at[0, :N_GROUPS].set(bg).at[0, N_GROUPS:n_r].set(be)
    return pl.pallas_call(
        _router_kernel,
        grid=(bsz, seq // tm),
        in_specs=[pl.BlockSpec((1, tm, d), lambda b, i: (b, i, 0)),
                  pl.BlockSpec((1, 1, d), lambda b, i: (b, 0, 0)),
                  pl.BlockSpec((1, 1, d), lambda b, i: (b, 0, 0)),
                  pl.BlockSpec((d, LANES), lambda b, i: (0, 0)),
                  pl.BlockSpec((1, LANES), lambda b, i: (0, 0))],
        out_specs=[pl.BlockSpec((1, tm, d), lambda b, i: (b, i, 0)),
                   pl.BlockSpec((1, tm, LANES), lambda b, i: (b, i, 0))],
        out_shape=[jax.ShapeDtypeStruct((bsz, seq, d), BF16),
                   jax.ShapeDtypeStruct((bsz, seq, LANES), F32)],
        compiler_params=_cparams(("parallel", "parallel")),
        name="moe_router",
    )(x, sc, sh, wr, br)


MOE_SEG = 256


MOE_PAD = 16
MOE_EXPERTS_PER_STEP = 1


def _pad_rows(count):
    return jnp.floor((count + (MOE_PAD - 1)) * (1.0 / MOE_PAD)) * MOE_PAD


def _moe_kernel(erow_ref, route_ref, h_ref, x_ref, g_ref, w1_ref, w3_ref, w2_ref, o_ref,
                xs_sc, ys_sc, meta_sc, drow_sc, dcol_sc, *, chunk, rows, na):
    step = pl.program_id(1)
    n_seg = chunk // MOE_SEG
    lane1 = lax.broadcasted_iota(I32, (1, LANES), 1)

    def lane_scalar(row, k):
        return jnp.sum(jnp.where(lane1 == k, row, 0.0)).astype(I32)

    @pl.when(step == 0)
    def _():
        si = lax.broadcasted_iota(I32, (MOE_SEG, MOE_SEG), 0)
        sj = lax.broadcasted_iota(I32, (MOE_SEG, MOE_SEG), 1)
        upper = jnp.where(si < sj, 1.0, 0.0).astype(BF16)
        lower = jnp.where(sj < si, 1.0, 0.0).astype(BF16)
        erow = erow_ref[0]
        eid = lax.broadcasted_iota(I32, (N_EXPERTS, chunk), 0)
        sel = [erow[k:k + 1] == eid for k in range(2)]
        member = jnp.where(jnp.logical_or(sel[0], sel[1]), 1.0, 0.0)
        count = jnp.zeros((N_EXPERTS, 1), F32)
        parts = []
        for sgi in range(n_seg):
            seg = member[:, sgi * MOE_SEG:(sgi + 1) * MOE_SEG]
            parts.append(jnp.dot(seg.astype(BF16), upper, preferred_element_type=F32) + count)
            count = count + jnp.sum(seg, axis=1, keepdims=True)
        rank = jnp.concatenate(parts, axis=1)
        padded = _pad_rows(count)
        ei = lax.broadcasted_iota(I32, (N_EXPERTS, N_EXPERTS), 0)
        ej = lax.broadcasted_iota(I32, (N_EXPERTS, N_EXPERTS), 1)
        below = jnp.where(ej < ei, 1.0, 0.0).astype(BF16)
        start = jnp.dot(below, jnp.broadcast_to(padded, (N_EXPERTS, LANES)).astype(BF16),
                        preferred_element_type=F32)
        total = jnp.sum(padded, axis=0, keepdims=True)
        lane_e = lax.broadcasted_iota(I32, (N_EXPERTS, LANES), 1)
        meta_sc[...] = jnp.where(lane_e == 0, start, jnp.where(lane_e == 1, count, jnp.where(lane_e == 2, total, 0.0)))
        dest = start[:, 0:1] + rank
        for k in range(2):
            drow_sc[k] = jnp.sum(jnp.where(sel[k], dest, 0.0), axis=0, keepdims=True)
        route = route_ref[...]
        lane_c = lax.broadcasted_iota(I32, (chunk, LANES), 1)
        lane_f = lane_c.astype(F32)
        sel_c = [route[:, k:k + 1] == lane_f for k in range(2)]
        member_c = jnp.where(jnp.logical_or(sel_c[0], sel_c[1]), 1.0, 0.0)
        count_c = jnp.zeros((1, LANES), F32)
        parts_c = []
        for sgi in range(n_seg):
            seg = member_c[sgi * MOE_SEG:(sgi + 1) * MOE_SEG]
            parts_c.append(jnp.dot(lower, seg.astype(BF16), preferred_element_type=F32) + count_c)
            count_c = count_c + jnp.sum(seg, axis=0, keepdims=True)
        rank_c = jnp.concatenate(parts_c, axis=0)
        li = lax.broadcasted_iota(I32, (LANES, LANES), 0)
        lj = lax.broadcasted_iota(I32, (LANES, LANES), 1)
        before = jnp.where(li < lj, 1.0, 0.0).astype(BF16)
        start_c = jnp.dot(jnp.broadcast_to(_pad_rows(count_c), (8, LANES)).astype(BF16), before,
                          preferred_element_type=F32)[0:1]
        dest_c = start_c + rank_c
        dcols = [jnp.sum(jnp.where(sel_c[k], dest_c, 0.0), axis=1, keepdims=True) for k in range(2)]
        dcol_sc[...] = jnp.where(lane_c == 0, dcols[0], jnp.where(lane_c == 1, dcols[1], 0.0))
        ys_sc[...] = jnp.zeros(ys_sc.shape, BF16)
        xs_sc[...] = jnp.zeros(xs_sc.shape, BF16)
        d0, d1 = drow_sc[0], drow_sc[1]
        n_slabs = (jnp.sum(total).astype(I32) + MOE_SEG - 1) // MOE_SEG

        def sort_slab(s, carry):
            off = pl.multiple_of(s * MOE_SEG, MOE_SEG)
            r = (lax.broadcasted_iota(I32, (MOE_SEG, 1), 0) + off).astype(F32)
            onehot = jnp.where(jnp.logical_or(d0 == r, d1 == r), 1.0, 0.0).astype(BF16)
            xs_sc[pl.ds(off, MOE_SEG), :] = jnp.dot(onehot, h_ref[...], preferred_element_type=F32).astype(BF16)
            return carry

        lax.fori_loop(0, n_slabs, sort_slab, 0)

    for slot in range(MOE_EXPERTS_PER_STEP):
        meta = meta_sc[pl.ds(step * MOE_EXPERTS_PER_STEP + slot, 1), :]
        first = lane_scalar(meta, 0)
        count_e = lane_scalar(meta, 1)
        padded_e = ((count_e + MOE_PAD - 1) // MOE_PAD) * MOE_PAD

        def block(bi, carry, slot=slot, first=first, padded_e=padded_e):
            r0 = pl.multiple_of(first + bi * rows, MOE_PAD)
            xb = xs_sc[pl.ds(r0, rows), :]
            a = jnp.dot(xb, w1_ref[slot], preferred_element_type=F32)
            hid = (a * jax.nn.sigmoid(a)) * jnp.dot(xb, w3_ref[slot], preferred_element_type=F32)
            yb = jnp.dot(hid.astype(BF16), w2_ref[slot], preferred_element_type=F32)
            mine = lax.broadcasted_iota(I32, (rows, 1), 0) + bi * rows < padded_e
            ys_sc[pl.ds(r0, rows), :] = jnp.where(mine, yb, ys_sc[pl.ds(r0, rows), :].astype(F32)).astype(BF16)
            return carry

        lax.fori_loop(0, (count_e + rows - 1) // rows, block, 0)

    @pl.when(step == N_EXPERTS // MOE_EXPERTS_PER_STEP - 1)
    def _():
        route = route_ref[...]
        dcol = dcol_sc[...]
        d0, d1 = dcol[:, 0:1], dcol[:, 1:2]
        w0, w1 = route[:, 2:3], route[:, 3:4]
        r = lax.broadcasted_iota(I32, (1, na), 1).astype(F32)
        comb = (jnp.where(d0 == r, w0, 0.0) + jnp.where(d1 == r, w1, 0.0)).astype(BF16)
        moe = jnp.dot(comb, ys_sc[:na, :], preferred_element_type=F32)
        o_ref[...] = x_ref[...] + g_ref[0] * moe


def moe_experts(h, route, x, gate, w1_bf16, w3_bf16, w2_bf16, chunk=1024, rows=128):
    bsz, seq, d = x.shape
    chunk = _pick_tile(seq, (chunk, 512, 256))
    n_tok = bsz * seq
    n_chunks = n_tok // chunk
    per_b = seq // chunk
    ff = w1_bf16.shape[-1]
    na = -(-(2 * chunk + N_EXPERTS * (MOE_PAD - 1)) // MOE_SEG) * MOE_SEG
    route2 = route.reshape(n_tok, LANES)
    erow = route2[:, :2].astype(I32).T.reshape(2, n_chunks, chunk).transpose(1, 0, 2)
    eps = MOE_EXPERTS_PER_STEP
    out = pl.pallas_call(
        functools.partial(_moe_kernel, chunk=chunk, rows=rows, na=na),
        grid=(n_chunks, N_EXPERTS // eps),
        in_specs=[pl.BlockSpec((1, 2, chunk), lambda c, e: (c, 0, 0)),
                  pl.BlockSpec((chunk, LANES), lambda c, e: (c, 0)),
                  pl.BlockSpec((chunk, d), lambda c, e: (c, 0)),
                  pl.BlockSpec((chunk, d), lambda c, e: (c, 0)),
                  pl.BlockSpec((1, 1, d), lambda c, e: (c // per_b, 0, 0)),
                  pl.BlockSpec((eps, d, ff), lambda c, e: (e, 0, 0)),
                  pl.BlockSpec((eps, d, ff), lambda c, e: (e, 0, 0)),
                  pl.BlockSpec((eps, ff, d), lambda c, e: (e, 0, 0))],
        out_specs=pl.BlockSpec((chunk, d), lambda c, e: (c, 0)),
        out_shape=jax.ShapeDtypeStruct((n_tok, d), F32),
        scratch_shapes=[pltpu.VMEM((na + rows, d), BF16), pltpu.VMEM((na + rows, d), BF16),
                        pltpu.VMEM((N_EXPERTS, LANES), F32), pltpu.VMEM((2, 1, chunk), F32),
                        pltpu.VMEM((chunk, LANES), F32)],
        compiler_params=_cparams(("parallel", "arbitrary")),
        name="moe_experts",
    )(erow, route2, h.reshape(n_tok, d), x.reshape(n_tok, d), gate, w1_bf16, w3_bf16, w2_bf16)
    return out.reshape(bsz, seq, d)


def kernel(x, c, ada_w, ada_b, moe_wg, moe_bg, moe_we, moe_be, moe_w1, moe_w3, moe_w2,
           ev_w_in, ev_w_out, a_qn, a_kn, a_lq1, a_lk1, a_lq2, a_lk2, a_subln,
           b_qn, b_kn, b_idx_kn, od_w_in, od_w_out, c_qn, c_kn,
           s5_lam_re, s5_lam_im, s5_b_re, s5_b_im, s5_c_re, s5_c_im, s5_d, s5_log_dt,
           s5_glu_w, s5_glu_b):
    bsz, seq, d = x.shape
    depth = ada_w.shape[0]
    mod = ada_mod(c, ada_w, ada_b)
    rope_h = rope_lane_tables(seq, HEAD_DIM, ROT_HEAD)
    rope_i = rope_lane_tables(seq, IDX_DIM, ROT_IDX)
    qk_scale = HEAD_DIM ** -0.5 * math.log2(math.e)
    half_h, half_i = ROT_HEAD // 2, ROT_IDX // 2
    hw = A_HEADS * 2 * HEAD_DIM

    def pad_cols(w):
        n = w.shape[1]
        npad = -(-n // LANES) * LANES
        return jnp.zeros((w.shape[0], npad), F32).at[:, :n].set(w).astype(BF16)

    for l in range(depth):
        sh1, sc1, g1, sh2, sc2, g2 = [m[:, None, :] for m in jnp.split(mod[l], 6, axis=-1)]
        i = l // 2
        if l % 2 == 0:
            lam_init = 0.8 - 0.6 * math.exp(-0.3 * l)
            z = in_proj(x, sc1, sh1, pad_cols(ev_w_in[i]))
            q_t, v_a, v_b = ((BF16, "t"),), ((BF16, ("aug", 2 * HEAD_DIM)),), ((BF16, ("aug", HEAD_DIM)),)
            (aq,) = prep(z, 0, hw, hd=HEAD_DIM, gain=a_qn[i], rope=rope_h, half=half_h, scale=qk_scale, outs=q_t)
            (ak,) = prep(z, hw, hw, hd=HEAD_DIM, gain=a_kn[i], rope=rope_h, half=half_h)
            (av,) = prep(z, 2 * hw, hw, hd=HEAD_DIM, outs=v_a)
            o_a = diff_attention(aq, ak, av, a_lq1[i], a_lk1[i], a_lq2[i], a_lk2[i], a_subln[i], lam_init,
                                 _logit_bound(a_qn[i], a_kn[i]) <= MAX_FAST_LOGIT)
            (bq,) = prep(z, 3 * hw, hw, hd=HEAD_DIM, gain=b_qn[i], rope=rope_h, half=half_h, scale=qk_scale, outs=q_t)
            (bk,) = prep(z, 4 * hw, hw, hd=HEAD_DIM, gain=b_kn[i], rope=rope_h, half=half_h)
            (bv,) = prep(z, 5 * hw, hw, hd=HEAD_DIM, outs=v_b)
            iq_off = 6 * hw
            ik_off = iq_off + IDX_HEADS * IDX_DIM
            (iq,) = prep(z, iq_off, IDX_HEADS * IDX_DIM, hd=IDX_DIM, rope=rope_i, half=half_i, outs=q_t)
            ik, iw = prep(z, ik_off, LANES, hd=IDX_DIM, gain=b_idx_kn[i], rope=rope_i, half=half_i,
                          outs=((BF16, "n"), (F32, ("rawt", IDX_DIM, IDX_DIM + IDX_HEADS))))
            o_b = dsa_attention(bq, bk, bv, iq, ik, iw, _logit_bound(b_qn[i], b_kn[i]) <= MAX_FAST_LOGIT)
            x = out_proj(o_a, o_b, ev_w_out[i].astype(BF16), x, g1, a_t=True, b_t=True)
        else:
            z = in_proj(x, sc1, sh1, pad_cols(od_w_in[i]))
            (cq,) = prep(z, 0, hw, hd=HEAD_DIM, gain=c_qn[i], rope=rope_h, half=half_h, scale=qk_scale,
                         outs=((F32, "t"),))
            ck32, ck = prep(z, hw, hw, hd=HEAD_DIM, gain=c_kn[i], rope=rope_h, half=half_h,
                            outs=((F32, "n"), (BF16, "n")))
            (cv,) = prep(z, 2 * hw, hw, hd=HEAD_DIM, outs=((BF16, ("aug", HEAD_DIM)),))
            o_c = moba_attention(cq, block_means(ck32), ck, cv, _logit_bound(c_qn[i], c_kn[i]) <= MAX_FAST_LOGIT)
            ops = s5_operators(s5_lam_re[i], s5_lam_im[i], s5_b_re[i], s5_b_im[i], s5_c_re[i], s5_c_im[i],
                               s5_d[i], s5_log_dt[i], S5_CHUNK)
            y = s5_layer(z[:, :, 3 * hw:4 * hw], ops)
            y = glu(y.reshape(bsz * seq, hw), s5_glu_w[i].astype(BF16), s5_glu_b[i]).reshape(bsz, seq, hw)
            x = out_proj(o_c, y, od_w_out[i].astype(BF16), x, g1, a_t=True)
        h, route = moe_router(x, sc2, sh2, moe_wg[l], moe_bg[l], moe_we[l], moe_be[l])
        x = moe_experts(h, route, x, g2, moe_w1[l].astype(BF16), moe_w3[l].astype(BF16), moe_w2[l].astype(BF16))
    return x
```

```python
import functools
import math

import jax
import jax.numpy as jnp
from jax import lax
from jax.experimental import pallas as pl
from jax.experimental.pallas import tpu as pltpu

F32 = jnp.float32
BF16 = jnp.bfloat16
I32 = jnp.int32
HI = lax.Precision.HIGHEST

LANES = 128
NEG = -1e30
I32_MIN = -2147483648

HEAD_DIM = 64
A_HEADS = 4
B_HEADS = 8
IDX_HEADS = 8
IDX_DIM = 32
DSA_TOPK = 256
C_HEADS = 8
MOBA_BLOCK = 256
MOBA_TOPK = 3
S5_GROUP_CH = 16
S5_GROUPS = 32
S5_STATE = 64
ROPE_THETA = 500000.0
ROT_HEAD = HEAD_DIM // 4
ROT_IDX = IDX_DIM // 4
N_GROUPS = 4
EXPERTS_PER_GROUP = 8
N_EXPERTS = N_GROUPS * EXPERTS_PER_GROUP
EXPERT_FF = 512
NORM_EPS = 1e-6
S5_CHUNK = 16


def _cparams(sem, vmem_mb=48):
    return pltpu.CompilerParams(dimension_semantics=sem, vmem_limit_bytes=vmem_mb * 1024 * 1024)


def _dot3(a, b):
    a_hi = a.astype(BF16)
    a_lo = (a - a_hi.astype(F32)).astype(BF16)
    b_hi = b.astype(BF16)
    b_lo = (b - b_hi.astype(F32)).astype(BF16)
    dot = functools.partial(jnp.dot, preferred_element_type=F32)
    return dot(a_hi, b_hi) + (dot(a_hi, b_lo) + dot(a_lo, b_hi))


def _mod_kernel(c_ref, w_ref, b_ref, o_ref):
    c = c_ref[...]
    ca = c * jax.nn.sigmoid(c)
    o_ref[0] = jnp.dot(ca, w_ref[0], precision=HI, preferred_element_type=F32) + b_ref[0]


def ada_mod(c, ada_w, ada_b):
    depth, d, n6 = ada_w.shape
    bsz = c.shape[0]
    rows = 8
    c_pad = jnp.zeros((rows, d), F32).at[:bsz].set(c)
    out = pl.pallas_call(
        _mod_kernel,
        grid=(depth, n6 // d),
        in_specs=[pl.BlockSpec((rows, d), lambda l, n: (0, 0)),
                  pl.BlockSpec((1, d, d), lambda l, n: (l, 0, n)),
                  pl.BlockSpec((1, 1, d), lambda l, n: (l, 0, n))],
        out_specs=pl.BlockSpec((1, rows, d), lambda l, n: (l, 0, n)),
        out_shape=jax.ShapeDtypeStruct((depth, rows, n6), F32),
        compiler_params=_cparams(("parallel", "parallel")),
        name="ada_mod",
    )(c_pad, ada_w, ada_b.reshape(depth, 1, n6))
    return out[:, :bsz]


def _norm_mod(x, sc, sh):
    y = x * lax.rsqrt(jnp.mean(x * x, axis=-1, keepdims=True) + NORM_EPS)
    return y * (1.0 + sc) + sh


def _inproj_kernel(x_ref, sc_ref, sh_ref, w_ref, o_ref, h_sc):
    @pl.when(pl.program_id(2) == 0)
    def _():
        h_sc[...] = _norm_mod(x_ref[0], sc_ref[0], sh_ref[0]).astype(BF16)

    o_ref[0] = jnp.dot(h_sc[...], w_ref[...], preferred_element_type=F32)


def _pick_tile(n, prefs):
    for p in prefs:
        if n % p == 0:
            return p
    return n


def in_proj(x, sc, sh, w_bf16, tm=512):
    bsz, seq, d = x.shape
    n = w_bf16.shape[1]
    tm = _pick_tile(seq, (tm, 256, 128))
    tn = _pick_tile(n, (1152, 1024, 512, 384, 256, 128))
    return pl.pallas_call(
        _inproj_kernel,
        grid=(bsz, seq // tm, n // tn),
        in_specs=[pl.BlockSpec((1, tm, d), lambda b, i, j: (b, i, 0)),
                  pl.BlockSpec((1, 1, d), lambda b, i, j: (b, 0, 0)),
                  pl.BlockSpec((1, 1, d), lambda b, i, j: (b, 0, 0)),
                  pl.BlockSpec((d, tn), lambda b, i, j: (0, j))],
        out_specs=pl.BlockSpec((1, tm, tn), lambda b, i, j: (b, i, j)),
        out_shape=jax.ShapeDtypeStruct((bsz, seq, n), F32),
        scratch_shapes=[pltpu.VMEM((tm, d), BF16)],
        compiler_params=_cparams(("parallel", "parallel", "arbitrary")),
        name="in_proj",
    )(x, sc, sh, w_bf16)


def rope_lane_tables(seq, hd, rot):
    half = rot // 2
    pos = jnp.arange(seq, dtype=F32)
    inv = ROPE_THETA ** (-(jnp.arange(0, rot, 2, dtype=F32) / rot))
    ang = pos[:, None] * inv[None, :]
    cos, sin = jnp.cos(ang), jnp.sin(ang)
    dl = jnp.arange(LANES) % hd
    first = dl < half
    second = (dl >= half) & (dl < rot)
    idx = jnp.where(first, dl, jnp.where(second, dl - half, 0))
    cos_l = jnp.where(first | second, cos[:, idx], 1.0)
    sa_l = jnp.where(second, sin[:, idx], 0.0)
    sb_l = jnp.where(first, -sin[:, idx], 0.0)
    return cos_l.astype(F32), sa_l.astype(F32), sb_l.astype(F32)


def _prep_out_rows(mode, width):
    if mode == "t":
        return width
    if mode[0] == "aug":
        return (width // mode[1]) * (mode[1] + ONES_ROWS)
    return mode[2] - mode[1]


def _prep_kernel(*refs, hd, half, do_norm, do_rope, scale, outs):
    z_ref, g_ref, gm_ref, cos_ref, sa_ref, sb_ref = refs[:6]
    o_refs = refs[6:6 + len(outs)]
    x = z_ref[0]
    raw = x
    width = x.shape[-1]
    if do_norm:
        ms = jnp.dot((x * x).astype(BF16), gm_ref[...], preferred_element_type=F32) * (1.0 / hd)
        x = x * lax.rsqrt(ms + NORM_EPS) * g_ref[...]
    if do_rope:
        cos, sa, sb = cos_ref[...], sa_ref[...], sb_ref[...]
        cols = []
        for r in range(width // LANES):
            xc = x[:, r * LANES:(r + 1) * LANES]
            cols.append(xc * cos + pltpu.roll(xc, half, 1) * sa + pltpu.roll(xc, LANES - half, 1) * sb)
        x = cols[0] if len(cols) == 1 else jnp.concatenate(cols, axis=-1)
    if scale != 1.0:
        x = x * scale
    tm = x.shape[0]
    for o, (_, mode) in zip(o_refs, outs):
        if mode == "n":
            o[0] = x.astype(o.dtype)
        elif mode == "t":
            o[0] = x.T.astype(o.dtype)
        elif mode[0] == "aug":
            dv = mode[1]
            xt = x.T
            ones = jnp.ones((ONES_ROWS, tm), F32)
            parts = []
            for h in range(width // dv):
                parts += [xt[h * dv:(h + 1) * dv], ones]
            o[0] = jnp.concatenate(parts, axis=0).astype(o.dtype)
        else:
            o[0] = raw.T[mode[1]:mode[2]].astype(o.dtype)


def prep(z, col_off, width, *, hd, gain=None, rope=None, half=0, scale=1.0, outs=((BF16, "n"),), tm=512):
    bsz, seq, _ = z.shape
    tm = _pick_tile(seq, (tm, 256, 128))
    cb = col_off // width
    assert col_off % width == 0 and width % LANES == 0
    do_norm = gain is not None
    do_rope = rope is not None
    g = jnp.tile(gain.astype(F32), width // hd).reshape(1, width) if do_norm else jnp.ones((1, width), F32)
    gi = jnp.arange(width) // hd
    gm = (gi[:, None] == gi[None, :]).astype(BF16)
    if do_rope:
        cos_l, sa_l, sb_l = rope
    else:
        cos_l = sa_l = sb_l = jnp.zeros((seq, LANES), F32)
    out_specs, out_shape = [], []
    for dt, mode in outs:
        if mode == "n":
            out_specs.append(pl.BlockSpec((1, tm, width), lambda b, i: (b, i, 0)))
            out_shape.append(jax.ShapeDtypeStruct((bsz, seq, width), dt))
        else:
            rows = _prep_out_rows(mode, width)
            out_specs.append(pl.BlockSpec((1, rows, tm), lambda b, i: (b, 0, i)))
            out_shape.append(jax.ShapeDtypeStruct((bsz, rows, seq), dt))
    return pl.pallas_call(
        functools.partial(_prep_kernel, hd=hd, half=half, do_norm=do_norm, do_rope=do_rope,
                          scale=scale, outs=tuple(outs)),
        grid=(bsz, seq // tm),
        in_specs=[pl.BlockSpec((1, tm, width), lambda b, i: (b, i, cb)),
                  pl.BlockSpec((1, width), lambda b, i: (0, 0)),
                  pl.BlockSpec((width, width), lambda b, i: (0, 0)),
                  pl.BlockSpec((tm, LANES), lambda b, i: (i, 0)),
                  pl.BlockSpec((tm, LANES), lambda b, i: (i, 0)),
                  pl.BlockSpec((tm, LANES), lambda b, i: (i, 0))],
        out_specs=out_specs,
        out_shape=out_shape,
        compiler_params=_cparams(("parallel", "parallel")),
        name="prep",
    )(z, g, gm, cos_l, sa_l, sb_l)


ATT_TILE = 256
ATT_GROUP = 8
MOBA_GROUP = 16


def _for_tile_groups(n_tiles, body, group_size=ATT_GROUP):
    def group(jg, carry):
        body(group_size * jg, group_size)
        return carry

    lax.fori_loop(0, n_tiles // group_size, group, 0)
    size = group_size // 2
    while size >= 1:
        def tail(size=size):
            body((n_tiles // (2 * size)) * (2 * size), size)
        pl.when((n_tiles // size) % 2 == 1)(tail)
        size //= 2


ONES_ROWS = 16
MAX_FAST_LOGIT = 60.0


def _attn_update(s_t, vt_aug, m_ref, acc_ref, idx, fast):
    if fast:
        acc_ref[idx] += jnp.dot(vt_aug, jnp.exp2(s_t).astype(BF16), preferred_element_type=F32)
    else:
        m_prev = m_ref[idx]
        m_new = jnp.maximum(m_prev, jnp.max(s_t, axis=0, keepdims=True))
        p_t = jnp.exp2(s_t - m_new).astype(BF16)
        acc_ref[idx] = (jnp.exp2(m_prev - m_new) * acc_ref[idx]
                        + jnp.dot(vt_aug, p_t, preferred_element_type=F32))
        m_ref[idx] = m_new


def _attn_init(m_sc, acc_sc):
    m_sc[...] = jnp.full(m_sc.shape, NEG, F32)
    acc_sc[...] = jnp.zeros(acc_sc.shape, F32)


def _attn_out(acc_ref, idx, dv):
    acc = acc_ref[idx]
    return acc[:dv] / acc[dv:dv + 1]


def _logit_bound(gain_q, gain_k):
    return (HEAD_DIM ** 0.5 * math.log2(math.e) * 1.02) * jnp.max(jnp.abs(gain_q)) * jnp.max(jnp.abs(gain_k))


def _half_rows(x, sub):
    row = lax.broadcasted_iota(I32, (LANES, 1), 0)
    return jnp.where((row >= sub * HEAD_DIM) & (row < (sub + 1) * HEAD_DIM), x, jnp.zeros_like(x))


def _causal_t(t):
    return lax.broadcasted_iota(I32, (t, t), 0) <= lax.broadcasted_iota(I32, (t, t), 1)


def _diff_attn_kernel(qt_ref, k_ref, vt_ref, lq1_ref, lk1_ref, lq2_ref, lk2_ref, g_ref, o_ref,
                      m_sc, acc_sc, *, t, lam_init, fast):
    i = pl.program_id(2)
    _attn_init(m_sc, acc_sc)
    qt = qt_ref[0]
    qc = [_half_rows(qt, c) for c in range(2)]

    def attend(jt, n, mask_fn):
        off = pl.multiple_of(jt * t, t)
        k = k_ref[0, pl.ds(off, n * t), :]
        vt = vt_ref[0, :, pl.ds(off, n * t)]
        s = [mask_fn(jnp.dot(k, qc[c], preferred_element_type=F32)) for c in range(2)]
        for c in range(2):
            _attn_update(s[c], vt, m_sc, acc_sc, c, fast)

    _for_tile_groups(i, lambda jt, n: attend(jt, n, lambda s_t: s_t), DIFF_GROUP)
    causal = _causal_t(t)
    attend(i, 1, lambda s_t: jnp.where(causal, s_t, NEG))

    lam = (jnp.exp(jnp.sum(lq1_ref[...] * lk1_ref[...], axis=-1, keepdims=True))
           - jnp.exp(jnp.sum(lq2_ref[...] * lk2_ref[...], axis=-1, keepdims=True)) + lam_init)
    o = _attn_out(acc_sc, 0, LANES) - lam * _attn_out(acc_sc, 1, LANES)
    o = o * lax.rsqrt(jnp.mean(o * o, axis=0, keepdims=True) + NORM_EPS)
    o_ref[0] = o * g_ref[...] * (1.0 - lam_init)


DIFF_TILE = 512
DIFF_GROUP = 4


def diff_attention(qt, k, vta, lq1, lk1, lq2, lk2, subln, lam_init, fast_ok):
    bsz, width, seq = qt.shape
    n_h = width // LANES
    t = _pick_tile(seq, (DIFF_TILE, 256, 128))
    rows = LANES + ONES_ROWS
    vec = lambda a: a.astype(F32).reshape(1, -1)
    small = pl.BlockSpec((1, HEAD_DIM), lambda b, h, i: (0, 0))

    def call(fast, *args):
        return pl.pallas_call(
            functools.partial(_diff_attn_kernel, t=t, lam_init=lam_init, fast=fast),
            grid=(bsz, n_h, seq // t),
            in_specs=[pl.BlockSpec((1, LANES, t), lambda b, h, i: (b, h, i)),
                      pl.BlockSpec((1, seq, LANES), lambda b, h, i: (b, 0, h)),
                      pl.BlockSpec((1, rows, seq), lambda b, h, i: (b, h, 0)),
                      small, small, small, small,
                      pl.BlockSpec((LANES, 1), lambda b, h, i: (0, 0))],
            out_specs=pl.BlockSpec((1, LANES, t), lambda b, h, i: (b, h, i)),
            out_shape=jax.ShapeDtypeStruct((bsz, width, seq), F32),
            scratch_shapes=[pltpu.VMEM((2, 1, t), F32), pltpu.VMEM((2, rows, t), F32)],
            compiler_params=_cparams(("parallel", "parallel", "arbitrary")),
            name="diff_attn_fast" if fast else "diff_attn_safe",
        )(*args)

    args = (qt, k, vta, vec(lq1), vec(lk1), vec(lq2), vec(lk2), subln.astype(F32).reshape(-1, 1))
    return lax.cond(fast_ok, functools.partial(call, True), functools.partial(call, False), *args)


def _sortable_key(score):
    bits = pltpu.bitcast(score + 0.0, I32)
    return bits ^ ((bits >> 31) & 0x7FFFFFFF)


def _dsa_kernel(qt_ref, k_ref, vt_ref, qit_ref, ki_ref, wt_ref, o_ref, kb_sc, m_sc, acc_sc,
                *, t, nk, n_sel, fast):
    i = pl.program_id(1)
    pair = pl.program_id(2)
    n_chunks = i + 1
    krow = lax.broadcasted_iota(I32, (t, t), 0)

    def count_where(pred_fn):
        def chunk_count(cidx):
            g = jnp.where(pred_fn(kb_sc[cidx], cidx), 1, 0)
            return jnp.sum(g.reshape(t // 8, 8, t), axis=0)

        def body(jp, acc):
            return acc + chunk_count(2 * jp) + chunk_count(2 * jp + 1)

        acc = lax.fori_loop(0, n_chunks // 2, body, jnp.zeros((8, t), I32))
        last = chunk_count(n_chunks - 1)
        acc = acc + jnp.where(n_chunks % 2 == 1, last, 0)
        return jnp.sum(acc, axis=0, keepdims=True)

    @pl.when(pair == 0)
    def _():
        qit = qit_ref[0]
        pad = jnp.zeros((LANES - IDX_DIM, t), BF16)
        q_heads = [jnp.concatenate([qit[h * IDX_DIM:(h + 1) * IDX_DIM], pad], axis=0)
                   for h in range(IDX_HEADS)]
        wt = wt_ref[0]
        w_rows = [wt[h:h + 1] for h in range(IDX_HEADS)]
        qcol = i * t + lax.broadcasted_iota(I32, (1, t), 1)

        def score_chunk(cidx, mm):
            off = pl.multiple_of(cidx * t, t)
            ki = ki_ref[0, pl.ds(off, t), :]
            score = jnp.zeros((t, t), F32)
            for h in range(IDX_HEADS):
                lg = jnp.dot(ki, q_heads[h], preferred_element_type=F32)
                score = score + w_rows[h] * jnp.maximum(lg, 0.0)
            key = _sortable_key(score)
            visible = krow + off <= qcol
            kb_sc[cidx] = jnp.where(visible, key, I32_MIN)
            lo_c = jnp.min(jnp.where(visible, key, 2147483647).reshape(t // 8, 8, t), axis=0)
            hi_c = jnp.max(jnp.where(visible, key, I32_MIN).reshape(t // 8, 8, t), axis=0)
            return jnp.minimum(mm[0], lo_c), jnp.maximum(mm[1], hi_c)

        kmin, kmax = lax.fori_loop(0, n_chunks, score_chunk, (jnp.full((8, t), 2147483647, I32),
                                                              jnp.full((8, t), I32_MIN, I32)))
        n_vis = qcol + 1
        few = n_vis <= n_sel
        kmin = jnp.min(kmin, axis=0, keepdims=True)
        kmax = jnp.max(kmax, axis=0, keepdims=True)
        span = 1 << 25
        probe = jnp.maximum(jnp.where(kmax > I32_MIN + span, kmax - span, kmin), kmin)
        c_probe = count_where(lambda kk, cidx: kk >= probe)
        probe_ok = c_probe >= n_sel
        lo0 = jnp.where(few, I32_MIN, jnp.where(probe_ok, probe, kmin))
        hi0 = jnp.where(few, I32_MIN + 1, jnp.where(probe_ok, kmax + 1, probe))

        def open_lanes(lo, hi, c_lo):
            return jnp.logical_and(c_lo != n_sel, hi > lo + 1)

        def bisect_cond(st):
            it, lo, hi, c_lo, c_hi = st
            return jnp.logical_and(it < 34, jnp.max(jnp.where(open_lanes(lo, hi, c_lo), 1, 0)) > 0)

        def bisect(st):
            it, lo, hi, c_lo, c_hi = st
            mid = (lo >> 1) + (hi >> 1) + (lo & hi & 1)
            cnt = count_where(lambda kk, cidx: kk >= mid)
            up = jnp.logical_and(open_lanes(lo, hi, c_lo), cnt >= n_sel)
            dn = jnp.logical_and(open_lanes(lo, hi, c_lo), cnt < n_sel)
            return (it + 1, jnp.where(up, mid, lo), jnp.where(dn, mid, hi),
                    jnp.where(up, cnt, c_lo), jnp.where(dn, cnt, c_hi))

        c_lo0 = jnp.where(few, n_sel, jnp.where(probe_ok, c_probe, n_vis))
        c_hi0 = jnp.where(jnp.logical_or(few, probe_ok), 0, c_probe)
        _, thr, _, c_lo, c_hi = lax.while_loop(
            bisect_cond, bisect, (jnp.int32(0), lo0, hi0, c_lo0, c_hi0))

        need = n_sel - c_hi
        all_ties = c_lo == n_sel
        jmax0 = jnp.where(few, -1, jnp.where(all_ties, 2147483647, -1))
        unresolved = jnp.logical_and(jnp.logical_not(few), jnp.logical_not(all_ties))

        def tie_search(_):
            def step(it, lohi):
                lo, hi = lohi
                mid = (lo + hi) >> 1
                cnt = count_where(lambda kk, cidx: jnp.logical_and(kk == thr, krow + cidx * t <= mid))
                ok = cnt >= need
                return jnp.where(ok, lo, mid), jnp.where(ok, mid, hi)
            n_steps = 1 + max(1, (nk * t - 1).bit_length())
            _, hi = lax.fori_loop(0, n_steps, step, (jnp.full((1, t), -1, I32),
                                                     jnp.full((1, t), nk * t - 1, I32)))
            return jnp.where(unresolved, hi, jmax0)

        any_unresolved = jnp.max(jnp.where(unresolved, 1, 0)) > 0
        jmax = lax.cond(any_unresolved, tie_search, lambda _: jmax0, 0)

        def bias_chunk(cidx, carry):
            kk = kb_sc[cidx]
            cut = jnp.where(krow + cidx * t <= jmax, thr - 1, thr)
            bias = jnp.where(kk > cut, 0.0, NEG)
            kb_sc[cidx] = pltpu.bitcast(bias.astype(F32), I32)
            return carry

        lax.fori_loop(0, n_chunks, bias_chunk, 0)

    _attn_init(m_sc, acc_sc)
    qt = qt_ref[0]
    qh = [_half_rows(qt, sub) for sub in range(2)]
    hrows = HEAD_DIM + ONES_ROWS

    def attend(jt, n):
        off = pl.multiple_of(jt * t, t)
        k = k_ref[0, pl.ds(off, n * t), :]
        vt = vt_ref[0, :, pl.ds(off, n * t)]
        bias = jnp.concatenate([pltpu.bitcast(kb_sc[jt + c], F32) for c in range(n)], axis=0)
        s = [jnp.dot(k, qh[sub], preferred_element_type=F32) + bias for sub in range(2)]
        for sub in range(2):
            _attn_update(s[sub], vt[sub * hrows:(sub + 1) * hrows], m_sc, acc_sc, sub, fast)

    _for_tile_groups(n_chunks, attend)
    o_ref[0] = jnp.concatenate([_attn_out(acc_sc, sub, HEAD_DIM) for sub in range(2)], axis=0)


def dsa_attention(qt, k, vta, qit, ki, iwt, fast_ok):
    bsz, width, seq = qt.shape
    t = _pick_tile(seq, (ATT_TILE, 128))
    nq = seq // t
    n_sel = min(DSA_TOPK, seq // 4)
    prow = 2 * (HEAD_DIM + ONES_ROWS)

    def call(fast, *args):
        return pl.pallas_call(
            functools.partial(_dsa_kernel, t=t, nk=nq, n_sel=n_sel, fast=fast),
            grid=(bsz, nq, width // LANES),
            in_specs=[pl.BlockSpec((1, LANES, t), lambda b, i, p: (b, p, i)),
                      pl.BlockSpec((1, seq, LANES), lambda b, i, p: (b, 0, p)),
                      pl.BlockSpec((1, prow, seq), lambda b, i, p: (b, p, 0)),
                      pl.BlockSpec((1, IDX_HEADS * IDX_DIM, t), lambda b, i, p: (b, 0, i)),
                      pl.BlockSpec((1, seq, LANES), lambda b, i, p: (b, 0, 0)),
                      pl.BlockSpec((1, IDX_HEADS, t), lambda b, i, p: (b, 0, i))],
            out_specs=pl.BlockSpec((1, LANES, t), lambda b, i, p: (b, p, i)),
            out_shape=jax.ShapeDtypeStruct((bsz, width, seq), F32),
            scratch_shapes=[pltpu.VMEM((nq, t, t), I32), pltpu.VMEM((2, 1, t), F32),
                            pltpu.VMEM((2, HEAD_DIM + ONES_ROWS, t), F32)],
            compiler_params=_cparams(("parallel", "parallel", "arbitrary"), vmem_mb=56),
            name="dsa_attn_fast" if fast else "dsa_attn_safe",
        )(*args)

    args = (qt, k, vta, qit, ki, iwt)
    return lax.cond(fast_ok, functools.partial(call, True), functools.partial(call, False), *args)


def _kmean_kernel(k_ref, o_ref, *, nbb):
    x = k_ref[0]
    o_ref[0] = jnp.mean(x.reshape(nbb, MOBA_BLOCK, x.shape[-1]), axis=1)


def block_means(k_f32):
    bsz, seq, width = k_f32.shape
    nb = seq // MOBA_BLOCK
    nbb = 8 if nb % 8 == 0 else nb
    return pl.pallas_call(
        functools.partial(_kmean_kernel, nbb=nbb),
        grid=(bsz, nb // nbb),
        in_specs=[pl.BlockSpec((1, nbb * MOBA_BLOCK, width), lambda b, i: (b, i, 0))],
        out_specs=pl.BlockSpec((1, nbb, width), lambda b, i: (b, i, 0)),
        out_shape=jax.ShapeDtypeStruct((bsz, nb, width), F32),
        compiler_params=_cparams(("parallel", "parallel")),
        name="moba_kmean",
    )(k_f32)


def _moba_select_kernel(qt_ref, km_ref, o_ref, *, ts, n_sel, nbp):
    j = pl.program_id(2)
    blk = lax.broadcasted_iota(I32, (nbp, ts), 0)
    own = jnp.right_shift(j * ts + lax.broadcasted_iota(I32, (1, ts), 1), MOBA_BLOCK.bit_length() - 1)
    qt = qt_ref[0]
    for sub in range(2):
        gate = _dot3(km_ref[0], _half_rows(qt, sub))
        g = jnp.where(blk < own, gate, -jnp.inf)
        sel = jnp.zeros((nbp, ts), F32)
        for _ in range(n_sel):
            mx = jnp.max(g, axis=0, keepdims=True)
            is_mx = jnp.logical_and(g == mx, mx > -jnp.inf)
            first = jnp.min(jnp.where(is_mx, blk, 2 * nbp), axis=0, keepdims=True)
            pick = blk == first
            sel = jnp.where(pick, 1.0, sel)
            g = jnp.where(pick, -jnp.inf, g)
        o_ref[0, sub] = sel


def _moba_kernel(qt_ref, sel_ref, k_ref, vt_ref, o_ref, m_sc, acc_sc, *, t, fast):
    i = pl.program_id(2)
    _attn_init(m_sc, acc_sc)
    qt = qt_ref[0]
    qb = [_half_rows(qt, sub).astype(BF16) for sub in range(2)]
    hrows = HEAD_DIM + ONES_ROWS

    def attend(kb, n, mask_fn):
        off = pl.multiple_of(kb * t, t)
        k = k_ref[0, pl.ds(off, n * t), :]
        vt = vt_ref[0, :, pl.ds(off, n * t)]
        s = [mask_fn(jnp.dot(k, qb[sub], preferred_element_type=F32), sub) for sub in range(2)]
        for sub in range(2):
            _attn_update(s[sub], vt[sub * hrows:(sub + 1) * hrows], m_sc, acc_sc, sub, fast)

    causal = _causal_t(t)
    attend(i, 1, lambda s_t, sub: jnp.where(causal, s_t, NEG))

    def past_blocks(kb, n):
        def row_mask(s_t, sub):
            rows = [jnp.where(sel_ref[0, sub, pl.ds(kb + c, 1), :] > 0.0, 0.0, NEG) for c in range(n)]
            bias = jnp.concatenate([jnp.broadcast_to(r, (t, t)) for r in rows], axis=0)
            return s_t + bias
        attend(kb, n, row_mask)

    _for_tile_groups(i, past_blocks, MOBA_GROUP)
    o_ref[0] = jnp.concatenate([_attn_out(acc_sc, sub, HEAD_DIM) for sub in range(2)], axis=0)


def moba_attention(qt_f32, kmean, k, vta, fast_ok):
    bsz, width, seq = qt_f32.shape
    t = MOBA_BLOCK
    nb = seq // MOBA_BLOCK
    n_sel = max(1, min(MOBA_TOPK, nb - 1))
    nbp = -(-nb // 8) * 8
    km = jnp.zeros((bsz, nbp, width), F32).at[:, :nb].set(kmean)
    prow = 2 * (HEAD_DIM + ONES_ROWS)
    n_pairs = width // LANES
    ts = _pick_tile(seq, (1024, 512, 256))
    sel = pl.pallas_call(
        functools.partial(_moba_select_kernel, ts=ts, n_sel=n_sel, nbp=nbp),
        grid=(bsz, n_pairs, seq // ts),
        in_specs=[pl.BlockSpec((1, LANES, ts), lambda b, h, j: (b, h, j)),
                  pl.BlockSpec((1, nbp, LANES), lambda b, h, j: (b, 0, h))],
        out_specs=pl.BlockSpec((1, 2, nbp, ts), lambda b, h, j: (b, h, 0, j)),
        out_shape=jax.ShapeDtypeStruct((bsz, 2 * n_pairs, nbp, seq), F32),
        compiler_params=_cparams(("parallel", "parallel", "parallel")),
        name="moba_select",
    )(qt_f32, km)

    def call(fast, *args):
        return pl.pallas_call(
            functools.partial(_moba_kernel, t=t, fast=fast),
            grid=(bsz, n_pairs, nb),
            in_specs=[pl.BlockSpec((1, LANES, t), lambda b, h, i: (b, h, i)),
                      pl.BlockSpec((1, 2, nbp, t), lambda b, h, i: (b, h, 0, i)),
                      pl.BlockSpec((1, seq, LANES), lambda b, h, i: (b, 0, h)),
                      pl.BlockSpec((1, prow, seq), lambda b, h, i: (b, h, 0))],
            out_specs=pl.BlockSpec((1, LANES, t), lambda b, h, i: (b, h, i)),
            out_shape=jax.ShapeDtypeStruct((bsz, width, seq), F32),
            scratch_shapes=[pltpu.VMEM((2, 1, t), F32), pltpu.VMEM((2, HEAD_DIM + ONES_ROWS, t), F32)],
            compiler_params=_cparams(("parallel", "parallel", "arbitrary")),
            name="moba_attn_fast" if fast else "moba_attn_safe",
        )(*args)

    args = (qt_f32, sel, k, vta)
    return lax.cond(fast_ok, functools.partial(call, True), functools.partial(call, False), *args)


def s5_operators(lam_re, lam_im, b_re, b_im, c_re, c_im, d_skip, log_dt, chunk):
    n_g, n_p = lam_re.shape
    n_c = d_skip.shape[-1]
    dt = jnp.exp(log_dt.astype(F32))[:, None]
    den = lam_re * lam_re + lam_im * lam_im
    mag = jnp.exp(lam_re * dt)
    abar_re = mag * jnp.cos(lam_im * dt)
    abar_im = mag * jnp.sin(lam_im * dt)
    coef_re = ((abar_re - 1.0) * lam_re + abar_im * lam_im) / den
    coef_im = (abar_im * lam_re - (abar_re - 1.0) * lam_im) / den
    bbar_re = coef_re[..., None] * b_re - coef_im[..., None] * b_im
    bbar_im = coef_re[..., None] * b_im + coef_im[..., None] * b_re
    tau = jnp.arange(chunk + 1, dtype=F32)[:, None, None]
    pmag = jnp.exp(lam_re * dt * tau)
    pw_re = pmag * jnp.cos(lam_im * dt * tau)
    pw_im = pmag * jnp.sin(lam_im * dt * tau)
    ab_re = pw_re[..., None] * bbar_re - pw_im[..., None] * bbar_im
    ab_im = pw_re[..., None] * bbar_im + pw_im[..., None] * bbar_re
    kern = (jnp.einsum('gop,tgpi->tgio', c_re, ab_re, precision=HI)
            - jnp.einsum('gop,tgpi->tgio', c_im, ab_im, precision=HI))
    t_idx = jnp.arange(chunk)
    lag = t_idx[None, :] - t_idx[:, None]
    toep = jnp.where((lag >= 0)[None, :, None, :, None],
                     kern[jnp.clip(lag, 0, chunk)].transpose(2, 0, 3, 1, 4), 0.0)
    eye = (jnp.eye(chunk)[None, :, None, :, None] * jnp.eye(n_c)[None, None, :, None, :])
    toep = toep + eye * d_skip[:, None, :, None, None]
    toep = toep.reshape(n_g, chunk * n_c, chunk * n_c)
    rev = chunk - 1 - t_idx
    w_in = jnp.concatenate([ab_re[rev].transpose(1, 0, 3, 2), ab_im[rev].transpose(1, 0, 3, 2)], axis=-1)
    w_in = w_in.reshape(n_g, chunk * n_c, 2 * n_p)
    a_re, a_im = pw_re[1:chunk + 1], pw_im[1:chunk + 1]
    st_re = (c_re[None] * a_re[:, :, None, :] - c_im[None] * a_im[:, :, None, :])
    st_im = (-c_re[None] * a_im[:, :, None, :] - c_im[None] * a_re[:, :, None, :])
    w_st = jnp.concatenate([st_re.transpose(1, 3, 0, 2), st_im.transpose(1, 3, 0, 2)], axis=1)
    w_st = w_st.reshape(n_g, 2 * n_p, chunk * n_c)
    return toep, w_in, w_st, pw_re[chunk], pw_im[chunk]


def _s5_in_kernel(u_ref, w_ref, o_ref):
    o_ref[0, 0] = _dot3(u_ref[0, 0], w_ref[0])


def _s5_scan_kernel(x_ref, aa_ref, ab_ref, abs_ref, o_ref, h_sc, hs_sc, *, steps, n_p):
    @pl.when(pl.program_id(0) == 0)
    def _():
        h_sc[...] = jnp.zeros(h_sc.shape, F32)
        hs_sc[...] = jnp.zeros(hs_sc.shape, F32)

    aa, ab, abs_ = aa_ref[...], ab_ref[...], abs_ref[...]

    def body(t, carry):
        h, hs = carry
        o_ref[t] = h
        x = x_ref[t]
        xs = pltpu.roll(x, n_p, 1)
        return aa * h + ab * hs + x, aa * hs + abs_ * h + xs

    h, hs = lax.fori_loop(0, steps, body, (h_sc[...], hs_sc[...]))
    h_sc[...] = h
    hs_sc[...] = hs


def _s5_out_kernel(u_ref, h0_ref, t_ref, w_ref, o_ref):
    y = _dot3(u_ref[0, 0], t_ref[0]) + _dot3(h0_ref[0, 0], w_ref[0])
    o_ref[0, 0] = jax.nn.gelu(y)


def s5_layer(u, ops):
    toep, w_in, w_st, ac_re, ac_im = ops
    bsz, seq, width = u.shape
    n_g, n_p = ac_re.shape
    n_c = width // n_g
    chunk = S5_CHUNK
    nj = seq // chunk
    row = chunk * n_c
    ur = u.reshape(bsz, nj, chunk, n_g, n_c).transpose(0, 3, 1, 2, 4).reshape(bsz, n_g, nj, row)
    hloc = pl.pallas_call(
        _s5_in_kernel,
        grid=(bsz, n_g),
        in_specs=[pl.BlockSpec((1, 1, nj, row), lambda b, g: (b, g, 0, 0)),
                  pl.BlockSpec((1, row, 2 * n_p), lambda b, g: (g, 0, 0))],
        out_specs=pl.BlockSpec((1, 1, nj, 2 * n_p), lambda b, g: (b, g, 0, 0)),
        out_shape=jax.ShapeDtypeStruct((bsz, n_g, nj, 2 * n_p), F32),
        compiler_params=_cparams(("parallel", "parallel")),
        name="s5_in",
    )(ur, w_in)
    bg = bsz * n_g
    xs = hloc.transpose(2, 0, 1, 3).reshape(nj, bg, 2 * n_p)
    rep = lambda a: jnp.tile(a, (bsz, 1))
    aa = rep(jnp.concatenate([ac_re, ac_re], axis=-1))
    ab = rep(jnp.concatenate([-ac_im, ac_im], axis=-1))
    abs_ = rep(jnp.concatenate([ac_im, -ac_im], axis=-1))
    steps = _pick_tile(nj, (128, 64, 32, 16, 8))
    cst = pl.BlockSpec((bg, 2 * n_p), lambda t: (0, 0))
    h0 = pl.pallas_call(
        functools.partial(_s5_scan_kernel, steps=steps, n_p=n_p),
        grid=(nj // steps,),
        in_specs=[pl.BlockSpec((steps, bg, 2 * n_p), lambda t: (t, 0, 0)), cst, cst, cst],
        out_specs=pl.BlockSpec((steps, bg, 2 * n_p), lambda t: (t, 0, 0)),
        out_shape=jax.ShapeDtypeStruct((nj, bg, 2 * n_p), F32),
        scratch_shapes=[pltpu.VMEM((bg, 2 * n_p), F32), pltpu.VMEM((bg, 2 * n_p), F32)],
        compiler_params=_cparams(("arbitrary",)),
        name="s5_scan",
    )(xs, aa, ab, abs_)
    h0 = h0.reshape(nj, bsz, n_g, 2 * n_p).transpose(1, 2, 0, 3)
    y = pl.pallas_call(
        _s5_out_kernel,
        grid=(bsz, n_g),
        in_specs=[pl.BlockSpec((1, 1, nj, row), lambda b, g: (b, g, 0, 0)),
                  pl.BlockSpec((1, 1, nj, 2 * n_p), lambda b, g: (b, g, 0, 0)),
                  pl.BlockSpec((1, row, row), lambda b, g: (g, 0, 0)),
                  pl.BlockSpec((1, 2 * n_p, row), lambda b, g: (g, 0, 0))],
        out_specs=pl.BlockSpec((1, 1, nj, row), lambda b, g: (b, g, 0, 0)),
        out_shape=jax.ShapeDtypeStruct((bsz, n_g, nj, row), F32),
        compiler_params=_cparams(("parallel", "parallel")),
        name="s5_out",
    )(ur, h0, toep, w_st)
    return y.reshape(bsz, n_g, nj, chunk, n_c).transpose(0, 2, 3, 1, 4).reshape(bsz, seq, width)


def _glu_kernel(y_ref, w_ref, b_ref, o_ref):
    y = y_ref[...]
    gate = jnp.dot(y.astype(BF16), w_ref[...], preferred_element_type=F32) + b_ref[...]
    o_ref[...] = (y * jax.nn.sigmoid(gate)).astype(o_ref.dtype)


def glu(y, w_bf16, b, tm=1024):
    n_tok, width = y.shape
    tm = _pick_tile(n_tok, (tm, 512, 256, 128))
    return pl.pallas_call(
        _glu_kernel,
        grid=(n_tok // tm,),
        in_specs=[pl.BlockSpec((tm, width), lambda i: (i, 0)),
                  pl.BlockSpec((width, width), lambda i: (0, 0)),
                  pl.BlockSpec((1, width), lambda i: (0, 0))],
        out_specs=pl.BlockSpec((tm, width), lambda i: (i, 0)),
        out_shape=jax.ShapeDtypeStruct((n_tok, width), F32),
        compiler_params=_cparams(("parallel",)),
        name="s5_glu",
    )(y, w_bf16, b.astype(F32).reshape(1, width))


def _outproj_kernel(a_ref, b_ref, w_ref, x_ref, g_ref, o_ref, *, half, a_t, b_t):
    a = a_ref[0].T if a_t else a_ref[0]
    b = b_ref[0].T if b_t else b_ref[0]
    y = (jnp.dot(a.astype(BF16), w_ref[:half, :], preferred_element_type=F32)
         + jnp.dot(b.astype(BF16), w_ref[half:, :], preferred_element_type=F32))
    o_ref[0] = x_ref[0] + g_ref[0] * y


def out_proj(a, b, w_bf16, x, gate, a_t=False, b_t=False, tm=512):
    bsz, seq, d = x.shape
    half = w_bf16.shape[0] // 2
    tm = _pick_tile(seq, (tm, 256, 128))

    def spec(transposed):
        if transposed:
            return pl.BlockSpec((1, half, tm), lambda bb, i: (bb, 0, i))
        return pl.BlockSpec((1, tm, half), lambda bb, i: (bb, i, 0))

    return pl.pallas_call(
        functools.partial(_outproj_kernel, half=half, a_t=a_t, b_t=b_t),
        grid=(bsz, seq // tm),
        in_specs=[spec(a_t), spec(b_t),
                  pl.BlockSpec((2 * half, d), lambda bb, i: (0, 0)),
                  pl.BlockSpec((1, tm, d), lambda bb, i: (bb, i, 0)),
                  pl.BlockSpec((1, 1, d), lambda bb, i: (bb, 0, 0))],
        out_specs=pl.BlockSpec((1, tm, d), lambda bb, i: (bb, i, 0)),
        out_shape=jax.ShapeDtypeStruct((bsz, seq, d), F32),
        compiler_params=_cparams(("parallel", "parallel")),
        name="out_proj",
    )(a, b, w_bf16, x, gate)


def _router_kernel(x_ref, sc_ref, sh_ref, wr_ref, br_ref, h_ref, r_ref):
    h = _norm_mod(x_ref[0], sc_ref[0], sh_ref[0])
    h_ref[0] = h.astype(BF16)
    logits = _dot3(h, wr_ref[...]) + br_ref[...]
    tm = logits.shape[0]
    lane = lax.broadcasted_iota(I32, (tm, LANES), 1)
    big = 4 * LANES
    coarse = jnp.where(lane < N_GROUPS, logits, -jnp.inf)
    cmax = jnp.max(coarse, axis=1, keepdims=True)
    grp = jnp.min(jnp.where(coarse == cmax, lane, big), axis=1, keepdims=True)
    p_grp = 1.0 / jnp.sum(jnp.exp(coarse - cmax), axis=1, keepdims=True)
    lo = N_GROUPS + grp * EXPERTS_PER_GROUP
    fine = jnp.where(jnp.logical_and(lane >= lo, lane < lo + EXPERTS_PER_GROUP), logits, -jnp.inf)
    v1 = jnp.max(fine, axis=1, keepdims=True)
    i1 = jnp.min(jnp.where(fine == v1, lane, big), axis=1, keepdims=True)
    fine2 = jnp.where(lane == i1, -jnp.inf, fine)
    v2 = jnp.max(fine2, axis=1, keepdims=True)
    i2 = jnp.min(jnp.where(fine2 == v2, lane, big), axis=1, keepdims=True)
    e2 = jnp.exp(v2 - v1)
    wa = p_grp / (1.0 + e2)
    wb = p_grp * e2 / (1.0 + e2)
    ida = (i1 - N_GROUPS).astype(F32)
    idb = (i2 - N_GROUPS).astype(F32)
    r_ref[0] = jnp.where(lane == 0, ida, jnp.where(lane == 1, idb,
                         jnp.where(lane == 2, wa, jnp.where(lane == 3, wb, 0.0))))


def moe_router(x, sc, sh, wg, bg, we, be, tm=512):
    bsz, seq, d = x.shape
    tm = _pick_tile(seq, (tm, 256, 128))
    n_r = N_GROUPS + N_EXPERTS
    wr = jnp.zeros((d, LANES), F32).at[:, :N_GROUPS].set(wg).at[:, N_GROUPS:n_r].set(we)
    br = jnp.zeros((1, LANES), F32).at[0, :N_GROUPS].set(bg).at[0, N_GROUPS:n_r].set(be)
    return pl.pallas_call(
        _router_kernel,
        grid=(bsz, seq // tm),
        in_specs=[pl.BlockSpec((1, tm, d), lambda b, i: (b, i, 0)),
                  pl.BlockSpec((1, 1, d), lambda b, i: (b, 0, 0)),
                  pl.BlockSpec((1, 1, d), lambda b, i: (b, 0, 0)),
                  pl.BlockSpec((d, LANES), lambda b, i: (0, 0)),
                  pl.BlockSpec((1, LANES), lambda b, i: (0, 0))],
        out_specs=[pl.BlockSpec((1, tm, d), lambda b, i: (b, i, 0)),
                   pl.BlockSpec((1, tm, LANES), lambda b, i: (b, i, 0))],
        out_shape=[jax.ShapeDtypeStruct((bsz, seq, d), BF16),
                   jax.ShapeDtypeStruct((bsz, seq, LANES), F32)],
        compiler_params=_cparams(("parallel", "parallel")),
        name="moe_router",
    )(x, sc, sh, wr, br)


MOE_SEG = 256


MOE_PAD = 16
MOE_EXPERTS_PER_STEP = 1


def _pad_rows(count):
    return jnp.floor((count + (MOE_PAD - 1)) * (1.0 / MOE_PAD)) * MOE_PAD


def _moe_kernel(erow_ref, route_ref, h_ref, x_ref, g_ref, w1_ref, w3_ref, w2_ref, o_ref,
                xs_sc, ys_sc, meta_sc, drow_sc, dcol_sc, *, chunk, rows, na):
    step = pl.program_id(1)
    n_seg = chunk // MOE_SEG
    lane1 = lax.broadcasted_iota(I32, (1, LANES), 1)

    def lane_scalar(row, k):
        return jnp.sum(jnp.where(lane1 == k, row, 0.0)).astype(I32)

    @pl.when(step == 0)
    def _():
        si = lax.broadcasted_iota(I32, (MOE_SEG, MOE_SEG), 0)
        sj = lax.broadcasted_iota(I32, (MOE_SEG, MOE_SEG), 1)
        upper = jnp.where(si < sj, 1.0, 0.0).astype(BF16)
        lower = jnp.where(sj < si, 1.0, 0.0).astype(BF16)
        erow = erow_ref[0]
        eid = lax.broadcasted_iota(I32, (N_EXPERTS, chunk), 0)
        sel = [erow[k:k + 1] == eid for k in range(2)]
        member = jnp.where(jnp.logical_or(sel[0], sel[1]), 1.0, 0.0)
        count = jnp.zeros((N_EXPERTS, 1), F32)
        parts = []
        for sgi in range(n_seg):
            seg = member[:, sgi * MOE_SEG:(sgi + 1) * MOE_SEG]
            parts.append(jnp.dot(seg.astype(BF16), upper, preferred_element_type=F32) + count)
            count = count + jnp.sum(seg, axis=1, keepdims=True)
        rank = jnp.concatenate(parts, axis=1)
        padded = _pad_rows(count)
        ei = lax.broadcasted_iota(I32, (N_EXPERTS, N_EXPERTS), 0)
        ej = lax.broadcasted_iota(I32, (N_EXPERTS, N_EXPERTS), 1)
        below = jnp.where(ej < ei, 1.0, 0.0).astype(BF16)
        start = jnp.dot(below, jnp.broadcast_to(padded, (N_EXPERTS, LANES)).astype(BF16),
                        preferred_element_type=F32)
        total = jnp.sum(padded, axis=0, keepdims=True)
        lane_e = lax.broadcasted_iota(I32, (N_EXPERTS, LANES), 1)
        meta_sc[...] = jnp.where(lane_e == 0, start, jnp.where(lane_e == 1, count, jnp.where(lane_e == 2, total, 0.0)))
        dest = start[:, 0:1] + rank
        for k in range(2):
            drow_sc[k] = jnp.sum(jnp.where(sel[k], dest, 0.0), axis=0, keepdims=True)
        route = route_ref[...]
        lane_c = lax.broadcasted_iota(I32, (chunk, LANES), 1)
        lane_f = lane_c.astype(F32)
        sel_c = [route[:, k:k + 1] == lane_f for k in range(2)]
        member_c = jnp.where(jnp.logical_or(sel_c[0], sel_c[1]), 1.0, 0.0)
        count_c = jnp.zeros((1, LANES), F32)
        parts_c = []
        for sgi in range(n_seg):
            seg = member_c[sgi * MOE_SEG:(sgi + 1) * MOE_SEG]
            parts_c.append(jnp.dot(lower, seg.astype(BF16), preferred_element_type=F32) + count_c)
            count_c = count_c + jnp.sum(seg, axis=0, keepdims=True)
        rank_c = jnp.concatenate(parts_c, axis=0)
        li = lax.broadcasted_iota(I32, (LANES, LANES), 0)
        lj = lax.broadcasted_iota(I32, (LANES, LANES), 1)
        before = jnp.where(li < lj, 1.0, 0.0).astype(BF16)
        start_c = jnp.dot(jnp.broadcast_to(_pad_rows(count_c), (8, LANES)).astype(BF16), before,
                          preferred_element_type=F32)[0:1]
        dest_c = start_c + rank_c
        dcols = [jnp.sum(jnp.where(sel_c[k], dest_c, 0.0), axis=1, keepdims=True) for k in range(2)]
        dcol_sc[...] = jnp.where(lane_c == 0, dcols[0], jnp.where(lane_c == 1, dcols[1], 0.0))
        ys_sc[...] = jnp.zeros(ys_sc.shape, BF16)
        xs_sc[...] = jnp.zeros(xs_sc.shape, BF16)
        d0, d1 = drow_sc[0], drow_sc[1]
        n_slabs = (jnp.sum(total).astype(I32) + MOE_SEG - 1) // MOE_SEG

        def sort_slab(s, carry):
            off = pl.multiple_of(s * MOE_SEG, MOE_SEG)
            r = (lax.broadcasted_iota(I32, (MOE_SEG, 1), 0) + off).astype(F32)
            onehot = jnp.where(jnp.logical_or(d0 == r, d1 == r), 1.0, 0.0).astype(BF16)
            xs_sc[pl.ds(off, MOE_SEG), :] = jnp.dot(onehot, h_ref[...], preferred_element_type=F32).astype(BF16)
            return carry

        lax.fori_loop(0, n_slabs, sort_slab, 0)

    for slot in range(MOE_EXPERTS_PER_STEP):
        meta = meta_sc[pl.ds(step * MOE_EXPERTS_PER_STEP + slot, 1), :]
        first = lane_scalar(meta, 0)
        count_e = lane_scalar(meta, 1)
        padded_e = ((count_e + MOE_PAD - 1) // MOE_PAD) * MOE_PAD

        def block(bi, carry, slot=slot, first=first, padded_e=padded_e):
            r0 = pl.multiple_of(first + bi * rows, MOE_PAD)
            xb = xs_sc[pl.ds(r0, rows), :]
            a = jnp.dot(xb, w1_ref[slot], preferred_element_type=F32)
            hid = (a * jax.nn.sigmoid(a)) * jnp.dot(xb, w3_ref[slot], preferred_element_type=F32)
            yb = jnp.dot(hid.astype(BF16), w2_ref[slot], preferred_element_type=F32)
            mine = lax.broadcasted_iota(I32, (rows, 1), 0) + bi * rows < padded_e
            ys_sc[pl.ds(r0, rows), :] = jnp.where(mine, yb, ys_sc[pl.ds(r0, rows), :].astype(F32)).astype(BF16)
            return carry

        lax.fori_loop(0, (count_e + rows - 1) // rows, block, 0)

    @pl.when(step == N_EXPERTS // MOE_EXPERTS_PER_STEP - 1)
    def _():
        route = route_ref[...]
        dcol = dcol_sc[...]
        d0, d1 = dcol[:, 0:1], dcol[:, 1:2]
        w0, w1 = route[:, 2:3], route[:, 3:4]
        r = lax.broadcasted_iota(I32, (1, na), 1).astype(F32)
        comb = (jnp.where(d0 == r, w0, 0.0) + jnp.where(d1 == r, w1, 0.0)).astype(BF16)
        moe = jnp.dot(comb, ys_sc[:na, :], preferred_element_type=F32)
        o_ref[...] = x_ref[...] + g_ref[0] * moe


def moe_experts(h, route, x, gate, w1_bf16, w3_bf16, w2_bf16, chunk=1024, rows=128):
    bsz, seq, d = x.shape
    chunk = _pick_tile(seq, (chunk, 512, 256))
    n_tok = bsz * seq
    n_chunks = n_tok // chunk
    per_b = seq // chunk
    ff = w1_bf16.shape[-1]
    na = -(-(2 * chunk + N_EXPERTS * (MOE_PAD - 1)) // MOE_SEG) * MOE_SEG
    route2 = route.reshape(n_tok, LANES)
    erow = route2[:, :2].astype(I32).T.reshape(2, n_chunks, chunk).transpose(1, 0, 2)
    eps = MOE_EXPERTS_PER_STEP
    out = pl.pallas_call(
        functools.partial(_moe_kernel, chunk=chunk, rows=rows, na=na),
        grid=(n_chunks, N_EXPERTS // eps),
        in_specs=[pl.BlockSpec((1, 2, chunk), lambda c, e: (c, 0, 0)),
                  pl.BlockSpec((chunk, LANES), lambda c, e: (c, 0)),
                  pl.BlockSpec((chunk, d), lambda c, e: (c, 0)),
                  pl.BlockSpec((chunk, d), lambda c, e: (c, 0)),
                  pl.BlockSpec((1, 1, d), lambda c, e: (c // per_b, 0, 0)),
                  pl.BlockSpec((eps, d, ff), lambda c, e: (e, 0, 0)),
                  pl.BlockSpec((eps, d, ff), lambda c, e: (e, 0, 0)),
                  pl.BlockSpec((eps, ff, d), lambda c, e: (e, 0, 0))],
        out_specs=pl.BlockSpec((chunk, d), lambda c, e: (c, 0)),
        out_shape=jax.ShapeDtypeStruct((n_tok, d), F32),
        scratch_shapes=[pltpu.VMEM((na + rows, d), BF16), pltpu.VMEM((na + rows, d), BF16),
                        pltpu.VMEM((N_EXPERTS, LANES), F32), pltpu.VMEM((2, 1, chunk), F32),
                        pltpu.VMEM((chunk, LANES), F32)],
        compiler_params=_cparams(("parallel", "arbitrary")),
        name="moe_experts",
    )(erow, route2, h.reshape(n_tok, d), x.reshape(n_tok, d), gate, w1_bf16, w3_bf16, w2_bf16)
    return out.reshape(bsz, seq, d)


def kernel(x, c, ada_w, ada_b, moe_wg, moe_bg, moe_we, moe_be, moe_w1, moe_w3, moe_w2,
           ev_w_in, ev_w_out, a_qn, a_kn, a_lq1, a_lk1, a_lq2, a_lk2, a_subln,
           b_qn, b_kn, b_idx_kn, od_w_in, od_w_out, c_qn, c_kn,
           s5_lam_re, s5_lam_im, s5_b_re, s5_b_im, s5_c_re, s5_c_im, s5_d, s5_log_dt,
           s5_glu_w, s5_glu_b):
    bsz, seq, d = x.shape
    depth = ada_w.shape[0]
    mod = ada_mod(c, ada_w, ada_b)
    rope_h = rope_lane_tables(seq, HEAD_DIM, ROT_HEAD)
    rope_i = rope_lane_tables(seq, IDX_DIM, ROT_IDX)
    qk_scale = HEAD_DIM ** -0.5 * math.log2(math.e)
    half_h, half_i = ROT_HEAD // 2, ROT_IDX // 2
    hw = A_HEADS * 2 * HEAD_DIM

    def pad_cols(w):
        n = w.shape[1]
        npad = -(-n // LANES) * LANES
        return jnp.zeros((w.shape[0], npad), F32).at[:, :n].set(w).astype(BF16)

    for l in range(depth):
        sh1, sc1, g1, sh2, sc2, g2 = [m[:, None, :] for m in jnp.split(mod[l], 6, axis=-1)]
        i = l // 2
        if l % 2 == 0:
            lam_init = 0.8 - 0.6 * math.exp(-0.3 * l)
            z = in_proj(x, sc1, sh1, pad_cols(ev_w_in[i]))
            q_t, v_a, v_b = ((BF16, "t"),), ((BF16, ("aug", 2 * HEAD_DIM)),), ((BF16, ("aug", HEAD_DIM)),)
            (aq,) = prep(z, 0, hw, hd=HEAD_DIM, gain=a_qn[i], rope=rope_h, half=half_h, scale=qk_scale, outs=q_t)
            (ak,) = prep(z, hw, hw, hd=HEAD_DIM, gain=a_kn[i], rope=rope_h, half=half_h)
            (av,) = prep(z, 2 * hw, hw, hd=HEAD_DIM, outs=v_a)
            o_a = diff_attention(aq, ak, av, a_lq1[i], a_lk1[i], a_lq2[i], a_lk2[i], a_subln[i], lam_init,
                                 _logit_bound(a_qn[i], a_kn[i]) <= MAX_FAST_LOGIT)
            (bq,) = prep(z, 3 * hw, hw, hd=HEAD_DIM, gain=b_qn[i], rope=rope_h, half=half_h, scale=qk_scale, outs=q_t)
            (bk,) = prep(z, 4 * hw, hw, hd=HEAD_DIM, gain=b_kn[i], rope=rope_h, half=half_h)
            (bv,) = prep(z, 5 * hw, hw, hd=HEAD_DIM, outs=v_b)
            iq_off = 6 * hw
            ik_off = iq_off + IDX_HEADS * IDX_DIM
            (iq,) = prep(z, iq_off, IDX_HEADS * IDX_DIM, hd=IDX_DIM, rope=rope_i, half=half_i, outs=q_t)
            ik, iw = prep(z, ik_off, LANES, hd=IDX_DIM, gain=b_idx_kn[i], rope=rope_i, half=half_i,
                          outs=((BF16, "n"), (F32, ("rawt", IDX_DIM, IDX_DIM + IDX_HEADS))))
            o_b = dsa_attention(bq, bk, bv, iq, ik, iw, _logit_bound(b_qn[i], b_kn[i]) <= MAX_FAST_LOGIT)
            x = out_proj(o_a, o_b, ev_w_out[i].astype(BF16), x, g1, a_t=True, b_t=True)
        else:
            z = in_proj(x, sc1, sh1, pad_cols(od_w_in[i]))
            (cq,) = prep(z, 0, hw, hd=HEAD_DIM, gain=c_qn[i], rope=rope_h, half=half_h, scale=qk_scale,
                         outs=((F32, "t"),))
            ck32, ck = prep(z, hw, hw, hd=HEAD_DIM, gain=c_kn[i], rope=rope_h, half=half_h,
                            outs=((F32, "n"), (BF16, "n")))
            (cv,) = prep(z, 2 * hw, hw, hd=HEAD_DIM, outs=((BF16, ("aug", HEAD_DIM)),))
            o_c = moba_attention(cq, block_means(ck32), ck, cv, _logit_bound(c_qn[i], c_kn[i]) <= MAX_FAST_LOGIT)
            ops = s5_operators(s5_lam_re[i], s5_lam_im[i], s5_b_re[i], s5_b_im[i], s5_c_re[i], s5_c_im[i],
                               s5_d[i], s5_log_dt[i], S5_CHUNK)
            y = s5_layer(z[:, :, 3 * hw:4 * hw], ops)
            y = glu(y.reshape(bsz * seq, hw), s5_glu_w[i].astype(BF16), s5_glu_b[i]).reshape(bsz, seq, hw)
            x = out_proj(o_c, y, od_w_out[i].astype(BF16), x, g1, a_t=True)
        h, route = moe_router(x, sc2, sh2, moe_wg[l], moe_bg[l], moe_we[l], moe_be[l])
        x = moe_experts(h, route, x, g2, moe_w1[l].astype(BF16), moe_w3[l].astype(BF16), moe_w2[l].astype(BF16))
    return x
```

```python
import functools
import math

import jax
import jax.numpy as jnp
from jax import lax
from jax.experimental import pallas as pl
from jax.experimental.pallas import tpu as pltpu

F32 = jnp.float32
BF16 = jnp.bfloat16
I32 = jnp.int32
HI = lax.Precision.HIGHEST

LANES = 128
NEG = -1e30
I32_MIN = -2147483648

HEAD_DIM = 64
A_HEADS = 4
B_HEADS = 8
IDX_HEADS = 8
IDX_DIM = 32
DSA_TOPK = 256
C_HEADS = 8
MOBA_BLOCK = 256
MOBA_TOPK = 3
S5_GROUP_CH = 16
S5_GROUPS = 32
S5_STATE = 64
ROPE_THETA = 500000.0
ROT_HEAD = HEAD_DIM // 4
ROT_IDX = IDX_DIM // 4
N_GROUPS = 4
EXPERTS_PER_GROUP = 8
N_EXPERTS = N_GROUPS * EXPERTS_PER_GROUP
EXPERT_FF = 512
NORM_EPS = 1e-6
S5_CHUNK = 16


def _cparams(sem, vmem_mb=48):
    return pltpu.CompilerParams(dimension_semantics=sem, vmem_limit_bytes=vmem_mb * 1024 * 1024)


def _dot3(a, b):
    a_hi = a.astype(BF16)
    a_lo = (a - a_hi.astype(F32)).astype(BF16)
    b_hi = b.astype(BF16)
    b_lo = (b - b_hi.astype(F32)).astype(BF16)
    dot = functools.partial(jnp.dot, preferred_element_type=F32)
    return dot(a_hi, b_hi) + (dot(a_hi, b_lo) + dot(a_lo, b_hi))


def _mod_kernel(c_ref, w_ref, b_ref, o_ref):
    c = c_ref[...]
    ca = c * jax.nn.sigmoid(c)
    o_ref[0] = jnp.dot(ca, w_ref[0], precision=HI, preferred_element_type=F32) + b_ref[0]


def ada_mod(c, ada_w, ada_b):
    depth, d, n6 = ada_w.shape
    bsz = c.shape[0]
    rows = 8
    c_pad = jnp.zeros((rows, d), F32).at[:bsz].set(c)
    out = pl.pallas_call(
        _mod_kernel,
        grid=(depth, n6 // d),
        in_specs=[pl.BlockSpec((rows, d), lambda l, n: (0, 0)),
                  pl.BlockSpec((1, d, d), lambda l, n: (l, 0, n)),
                  pl.BlockSpec((1, 1, d), lambda l, n: (l, 0, n))],
        out_specs=pl.BlockSpec((1, rows, d), lambda l, n: (l, 0, n)),
        out_shape=jax.ShapeDtypeStruct((depth, rows, n6), F32),
        compiler_params=_cparams(("parallel", "parallel")),
        name="ada_mod",
    )(c_pad, ada_w, ada_b.reshape(depth, 1, n6))
    return out[:, :bsz]


def _norm_mod(x, sc, sh):
    y = x * lax.rsqrt(jnp.mean(x * x, axis=-1, keepdims=True) + NORM_EPS)
    return y * (1.0 + sc) + sh


def _inproj_kernel(x_ref, sc_ref, sh_ref, w_ref, o_ref, h_sc):
    @pl.when(pl.program_id(2) == 0)
    def _():
        h_sc[...] = _norm_mod(x_ref[0], sc_ref[0], sh_ref[0]).astype(BF16)

    o_ref[0] = jnp.dot(h_sc[...], w_ref[...], preferred_element_type=F32)


def _pick_tile(n, prefs):
    for p in prefs:
        if n % p == 0:
            return p
    return n


def in_proj(x, sc, sh, w_bf16, tm=512):
    bsz, seq, d = x.shape
    n = w_bf16.shape[1]
    tm = _pick_tile(seq, (tm, 256, 128))
    tn = n if n <= 4096 else _pick_tile(n, (1152, 1024, 512, 384, 256, 128))
    return pl.pallas_call(
        _inproj_kernel,
        grid=(bsz, seq // tm, n // tn),
        in_specs=[pl.BlockSpec((1, tm, d), lambda b, i, j: (b, i, 0)),
                  pl.BlockSpec((1, 1, d), lambda b, i, j: (b, 0, 0)),
                  pl.BlockSpec((1, 1, d), lambda b, i, j: (b, 0, 0)),
                  pl.BlockSpec((d, tn), lambda b, i, j: (0, j))],
        out_specs=pl.BlockSpec((1, tm, tn), lambda b, i, j: (b, i, j)),
        out_shape=jax.ShapeDtypeStruct((bsz, seq, n), F32),
        scratch_shapes=[pltpu.VMEM((tm, d), BF16)],
        compiler_params=_cparams(("parallel", "parallel", "arbitrary")),
        name="in_proj",
    )(x, sc, sh, w_bf16)


def rope_lane_tables(seq, hd, rot):
    half = rot // 2
    pos = jnp.arange(seq, dtype=F32)
    inv = ROPE_THETA ** (-(jnp.arange(0, rot, 2, dtype=F32) / rot))
    ang = pos[:, None] * inv[None, :]
    cos, sin = jnp.cos(ang), jnp.sin(ang)
    dl = jnp.arange(LANES) % hd
    first = dl < half
    second = (dl >= half) & (dl < rot)
    idx = jnp.where(first, dl, jnp.where(second, dl - half, 0))
    cos_l = jnp.where(first | second, cos[:, idx], 1.0)
    sa_l = jnp.where(second, sin[:, idx], 0.0)
    sb_l = jnp.where(first, -sin[:, idx], 0.0)
    return cos_l.astype(F32), sa_l.astype(F32), sb_l.astype(F32)


def _prep_out_rows(mode, width):
    if mode == "t":
        return width
    if mode[0] == "aug":
        return (width // mode[1]) * (mode[1] + ONES_ROWS)
    return mode[2] - mode[1]


def _prep_kernel(*refs, hd, half, do_norm, do_rope, scale, outs):
    z_ref, g_ref, gm_ref, cos_ref, sa_ref, sb_ref = refs[:6]
    o_refs = refs[6:6 + len(outs)]
    x = z_ref[0]
    raw = x
    width = x.shape[-1]
    if do_norm:
        ms = jnp.dot((x * x).astype(BF16), gm_ref[...], preferred_element_type=F32) * (1.0 / hd)
        x = x * lax.rsqrt(ms + NORM_EPS) * g_ref[...]
    if do_rope:
        cos, sa, sb = cos_ref[...], sa_ref[...], sb_ref[...]
        cols = []
        for r in range(width // LANES):
            xc = x[:, r * LANES:(r + 1) * LANES]
            cols.append(xc * cos + pltpu.roll(xc, half, 1) * sa + pltpu.roll(xc, LANES - half, 1) * sb)
        x = cols[0] if len(cols) == 1 else jnp.concatenate(cols, axis=-1)
    if scale != 1.0:
        x = x * scale
    tm = x.shape[0]
    for o, (_, mode) in zip(o_refs, outs):
        if mode == "n":
            o[0] = x.astype(o.dtype)
        elif mode == "t":
            o[0] = x.T.astype(o.dtype)
        elif mode[0] == "aug":
            dv = mode[1]
            xt = x.T
            ones = jnp.ones((ONES_ROWS, tm), F32)
            parts = []
            for h in range(width // dv):
                parts += [xt[h * dv:(h + 1) * dv], ones]
            o[0] = jnp.concatenate(parts, axis=0).astype(o.dtype)
        else:
            o[0] = raw.T[mode[1]:mode[2]].astype(o.dtype)


def prep(z, col_off, width, *, hd, gain=None, rope=None, half=0, scale=1.0, outs=((BF16, "n"),), tm=512):
    bsz, seq, _ = z.shape
    tm = _pick_tile(seq, (tm, 256, 128))
    cb = col_off // width
    assert col_off % width == 0 and width % LANES == 0
    do_norm = gain is not None
    do_rope = rope is not None
    g = jnp.tile(gain.astype(F32), width // hd).reshape(1, width) if do_norm else jnp.ones((1, width), F32)
    gi = jnp.arange(width) // hd
    gm = (gi[:, None] == gi[None, :]).astype(BF16)
    if do_rope:
        cos_l, sa_l, sb_l = rope
    else:
        cos_l = sa_l = sb_l = jnp.zeros((seq, LANES), F32)
    out_specs, out_shape = [], []
    for dt, mode in outs:
        if mode == "n":
            out_specs.append(pl.BlockSpec((1, tm, width), lambda b, i: (b, i, 0)))
            out_shape.append(jax.ShapeDtypeStruct((bsz, seq, width), dt))
        else:
            rows = _prep_out_rows(mode, width)
            out_specs.append(pl.BlockSpec((1, rows, tm), lambda b, i: (b, 0, i)))
            out_shape.append(jax.ShapeDtypeStruct((bsz, rows, seq), dt))
    return pl.pallas_call(
        functools.partial(_prep_kernel, hd=hd, half=half, do_norm=do_norm, do_rope=do_rope,
                          scale=scale, outs=tuple(outs)),
        grid=(bsz, seq // tm),
        in_specs=[pl.BlockSpec((1, tm, width), lambda b, i: (b, i, cb)),
                  pl.BlockSpec((1, width), lambda b, i: (0, 0)),
                  pl.BlockSpec((width, width), lambda b, i: (0, 0)),
                  pl.BlockSpec((tm, LANES), lambda b, i: (i, 0)),
                  pl.BlockSpec((tm, LANES), lambda b, i: (i, 0)),
                  pl.BlockSpec((tm, LANES), lambda b, i: (i, 0))],
        out_specs=out_specs,
        out_shape=out_shape,
        compiler_params=_cparams(("parallel", "parallel")),
        name="prep",
    )(z, g, gm, cos_l, sa_l, sb_l)


ATT_TILE = 256
ATT_GROUP = 8
MOBA_GROUP = 16


def _for_tile_groups(n_tiles, body, group_size=ATT_GROUP):
    def group(jg, carry):
        body(group_size * jg, group_size)
        return carry

    lax.fori_loop(0, n_tiles // group_size, group, 0)
    size = group_size // 2
    while size >= 1:
        def tail(size=size):
            body((n_tiles // (2 * size)) * (2 * size), size)
        pl.when((n_tiles // size) % 2 == 1)(tail)
        size //= 2


ONES_ROWS = 16
MAX_FAST_LOGIT = 60.0


def _attn_update(s_t, vt_aug, m_ref, acc_ref, idx, fast):
    if fast:
        acc_ref[idx] += jnp.dot(vt_aug, jnp.exp2(s_t).astype(BF16), preferred_element_type=F32)
    else:
        m_prev = m_ref[idx]
        m_new = jnp.maximum(m_prev, jnp.max(s_t, axis=0, keepdims=True))
        p_t = jnp.exp2(s_t - m_new).astype(BF16)
        acc_ref[idx] = (jnp.exp2(m_prev - m_new) * acc_ref[idx]
                        + jnp.dot(vt_aug, p_t, preferred_element_type=F32))
        m_ref[idx] = m_new


def _attn_init(m_sc, acc_sc):
    m_sc[...] = jnp.full(m_sc.shape, NEG, F32)
    acc_sc[...] = jnp.zeros(acc_sc.shape, F32)


def _attn_out(acc_ref, idx, dv):
    acc = acc_ref[idx]
    return acc[:dv] / acc[dv:dv + 1]


def _logit_bound(gain_q, gain_k):
    return (HEAD_DIM ** 0.5 * math.log2(math.e) * 1.02) * jnp.max(jnp.abs(gain_q)) * jnp.max(jnp.abs(gain_k))


def _half_rows(x, sub):
    row = lax.broadcasted_iota(I32, (LANES, 1), 0)
    return jnp.where((row >= sub * HEAD_DIM) & (row < (sub + 1) * HEAD_DIM), x, jnp.zeros_like(x))


def _causal_t(t):
    return lax.broadcasted_iota(I32, (t, t), 0) <= lax.broadcasted_iota(I32, (t, t), 1)


def _diff_attn_kernel(qt_ref, k_ref, vt_ref, lq1_ref, lk1_ref, lq2_ref, lk2_ref, g_ref, o_ref,
                      m_sc, acc_sc, *, t, lam_init, fast):
    i = pl.program_id(2)
    _attn_init(m_sc, acc_sc)
    qt = qt_ref[0]
    qc = [_half_rows(qt, c) for c in range(2)]

    def attend(jt, n, mask_fn):
        off = pl.multiple_of(jt * t, t)
        k = k_ref[0, pl.ds(off, n * t), :]
        vt = vt_ref[0, :, pl.ds(off, n * t)]
        s = [mask_fn(jnp.dot(k, qc[c], preferred_element_type=F32)) for c in range(2)]
        for c in range(2):
            _attn_update(s[c], vt, m_sc, acc_sc, c, fast)

    _for_tile_groups(i, lambda jt, n: attend(jt, n, lambda s_t: s_t), DIFF_GROUP)
    causal = _causal_t(t)
    attend(i, 1, lambda s_t: jnp.where(causal, s_t, NEG))

    lam = (jnp.exp(jnp.sum(lq1_ref[...] * lk1_ref[...], axis=-1, keepdims=True))
           - jnp.exp(jnp.sum(lq2_ref[...] * lk2_ref[...], axis=-1, keepdims=True)) + lam_init)
    o = _attn_out(acc_sc, 0, LANES) - lam * _attn_out(acc_sc, 1, LANES)
    o = o * lax.rsqrt(jnp.mean(o * o, axis=0, keepdims=True) + NORM_EPS)
    o_ref[0] = o * g_ref[...] * (1.0 - lam_init)


DIFF_TILE = 512
DIFF_GROUP = 4


def diff_attention(qt, k, vta, lq1, lk1, lq2, lk2, subln, lam_init, fast_ok):
    bsz, width, seq = qt.shape
    n_h = width // LANES
    t = _pick_tile(seq, (DIFF_TILE, 256, 128))
    rows = LANES + ONES_ROWS
    vec = lambda a: a.astype(F32).reshape(1, -1)
    small = pl.BlockSpec((1, HEAD_DIM), lambda b, h, i: (0, 0))

    def call(fast, *args):
        return pl.pallas_call(
            functools.partial(_diff_attn_kernel, t=t, lam_init=lam_init, fast=fast),
            grid=(bsz, n_h, seq // t),
            in_specs=[pl.BlockSpec((1, LANES, t), lambda b, h, i: (b, h, i)),
                      pl.BlockSpec((1, seq, LANES), lambda b, h, i: (b, 0, h)),
                      pl.BlockSpec((1, rows, seq), lambda b, h, i: (b, h, 0)),
                      small, small, small, small,
                      pl.BlockSpec((LANES, 1), lambda b, h, i: (0, 0))],
            out_specs=pl.BlockSpec((1, LANES, t), lambda b, h, i: (b, h, i)),
            out_shape=jax.ShapeDtypeStruct((bsz, width, seq), F32),
            scratch_shapes=[pltpu.VMEM((2, 1, t), F32), pltpu.VMEM((2, rows, t), F32)],
            compiler_params=_cparams(("parallel", "parallel", "arbitrary")),
            name="diff_attn_fast" if fast else "diff_attn_safe",
        )(*args)

    args = (qt, k, vta, vec(lq1), vec(lk1), vec(lq2), vec(lk2), subln.astype(F32).reshape(-1, 1))
    return lax.cond(fast_ok, functools.partial(call, True), functools.partial(call, False), *args)


def _sortable_key(score):
    bits = pltpu.bitcast(score + 0.0, I32)
    return bits ^ ((bits >> 31) & 0x7FFFFFFF)


def _dsa_kernel(qt_ref, k_ref, vt_ref, qit_ref, ki_ref, wt_ref, o_ref, kb_sc, m_sc, acc_sc,
                *, t, nk, n_sel, fast):
    i = pl.program_id(1)
    pair = pl.program_id(2)
    n_chunks = i + 1
    krow = lax.broadcasted_iota(I32, (t, t), 0)

    def count_where(pred_fn):
        def chunk_count(cidx):
            g = jnp.where(pred_fn(kb_sc[cidx], cidx), 1, 0)
            return jnp.sum(g.reshape(t // 8, 8, t), axis=0)

        def body(jp, acc):
            return acc + chunk_count(2 * jp) + chunk_count(2 * jp + 1)

        acc = lax.fori_loop(0, n_chunks // 2, body, jnp.zeros((8, t), I32))
        last = chunk_count(n_chunks - 1)
        acc = acc + jnp.where(n_chunks % 2 == 1, last, 0)
        return jnp.sum(acc, axis=0, keepdims=True)

    @pl.when(pair == 0)
    def _():
        qit = qit_ref[0]
        pad = jnp.zeros((LANES - IDX_DIM, t), BF16)
        q_heads = [jnp.concatenate([qit[h * IDX_DIM:(h + 1) * IDX_DIM], pad], axis=0)
                   for h in range(IDX_HEADS)]
        wt = wt_ref[0]
        w_rows = [wt[h:h + 1] for h in range(IDX_HEADS)]
        qcol = i * t + lax.broadcasted_iota(I32, (1, t), 1)

        def score_chunk(cidx, mm):
            off = pl.multiple_of(cidx * t, t)
            ki = ki_ref[0, pl.ds(off, t), :]
            score = jnp.zeros((t, t), F32)
            for h in range(IDX_HEADS):
                lg = jnp.dot(ki, q_heads[h], preferred_element_type=F32)
                score = score + w_rows[h] * jnp.maximum(lg, 0.0)
            key = _sortable_key(score)
            visible = krow + off <= qcol
            kb_sc[cidx] = jnp.where(visible, key, I32_MIN)
            lo_c = jnp.min(jnp.where(visible, key, 2147483647).reshape(t // 8, 8, t), axis=0)
            hi_c = jnp.max(jnp.where(visible, key, I32_MIN).reshape(t // 8, 8, t), axis=0)
            return jnp.minimum(mm[0], lo_c), jnp.maximum(mm[1], hi_c)

        kmin, kmax = lax.fori_loop(0, n_chunks, score_chunk, (jnp.full((8, t), 2147483647, I32),
                                                              jnp.full((8, t), I32_MIN, I32)))
        n_vis = qcol + 1
        few = n_vis <= n_sel
        kmin = jnp.min(kmin, axis=0, keepdims=True)
        kmax = jnp.max(kmax, axis=0, keepdims=True)
        span = 1 << 25
        probe = jnp.maximum(jnp.where(kmax > I32_MIN + span, kmax - span, kmin), kmin)
        c_probe = count_where(lambda kk, cidx: kk >= probe)
        probe_ok = c_probe >= n_sel
        lo0 = jnp.where(few, I32_MIN, jnp.where(probe_ok, probe, kmin))
        hi0 = jnp.where(few, I32_MIN + 1, jnp.where(probe_ok, kmax + 1, probe))

        def open_lanes(lo, hi, c_lo):
            return jnp.logical_and(c_lo != n_sel, hi > lo + 1)

        def bisect_cond(st):
            it, lo, hi, c_lo, c_hi = st
            return jnp.logical_and(it < 34, jnp.max(jnp.where(open_lanes(lo, hi, c_lo), 1, 0)) > 0)

        def bisect(st):
            it, lo, hi, c_lo, c_hi = st
            mid = (lo >> 1) + (hi >> 1) + (lo & hi & 1)
            cnt = count_where(lambda kk, cidx: kk >= mid)
            up = jnp.logical_and(open_lanes(lo, hi, c_lo), cnt >= n_sel)
            dn = jnp.logical_and(open_lanes(lo, hi, c_lo), cnt < n_sel)
            return (it + 1, jnp.where(up, mid, lo), jnp.where(dn, mid, hi),
                    jnp.where(up, cnt, c_lo), jnp.where(dn, cnt, c_hi))

        c_lo0 = jnp.where(few, n_sel, jnp.where(probe_ok, c_probe, n_vis))
        c_hi0 = jnp.where(jnp.logical_or(few, probe_ok), 0, c_probe)
        _, thr, _, c_lo, c_hi = lax.while_loop(
            bisect_cond, bisect, (jnp.int32(0), lo0, hi0, c_lo0, c_hi0))

        need = n_sel - c_hi
        all_ties = c_lo == n_sel
        jmax0 = jnp.where(few, -1, jnp.where(all_ties, 2147483647, -1))
        unresolved = jnp.logical_and(jnp.logical_not(few), jnp.logical_not(all_ties))

        def tie_search(_):
            def step(it, lohi):
                lo, hi = lohi
                mid = (lo + hi) >> 1
                cnt = count_where(lambda kk, cidx: jnp.logical_and(kk == thr, krow + cidx * t <= mid))
                ok = cnt >= need
                return jnp.where(ok, lo, mid), jnp.where(ok, mid, hi)
            n_steps = 1 + max(1, (nk * t - 1).bit_length())
            _, hi = lax.fori_loop(0, n_steps, step, (jnp.full((1, t), -1, I32),
                                                     jnp.full((1, t), nk * t - 1, I32)))
            return jnp.where(unresolved, hi, jmax0)

        any_unresolved = jnp.max(jnp.where(unresolved, 1, 0)) > 0
        jmax = lax.cond(any_unresolved, tie_search, lambda _: jmax0, 0)

        def bias_chunk(cidx, carry):
            kk = kb_sc[cidx]
            cut = jnp.where(krow + cidx * t <= jmax, thr - 1, thr)
            bias = jnp.where(kk > cut, 0.0, NEG)
            kb_sc[cidx] = pltpu.bitcast(bias.astype(F32), I32)
            return carry

        lax.fori_loop(0, n_chunks, bias_chunk, 0)

    _attn_init(m_sc, acc_sc)
    qt = qt_ref[0]
    qh = [_half_rows(qt, sub) for sub in range(2)]
    hrows = HEAD_DIM + ONES_ROWS

    def attend(jt, n):
        off = pl.multiple_of(jt * t, t)
        k = k_ref[0, pl.ds(off, n * t), :]
        vt = vt_ref[0, :, pl.ds(off, n * t)]
        bias = jnp.concatenate([pltpu.bitcast(kb_sc[jt + c], F32) for c in range(n)], axis=0)
        s = [jnp.dot(k, qh[sub], preferred_element_type=F32) + bias for sub in range(2)]
        for sub in range(2):
            _attn_update(s[sub], vt[sub * hrows:(sub + 1) * hrows], m_sc, acc_sc, sub, fast)

    _for_tile_groups(n_chunks, attend)
    o_ref[0] = jnp.concatenate([_attn_out(acc_sc, sub, HEAD_DIM) for sub in range(2)], axis=0)


def dsa_attention(qt, k, vta, qit, ki, iwt, fast_ok):
    bsz, width, seq = qt.shape
    t = _pick_tile(seq, (ATT_TILE, 128))
    nq = seq // t
    n_sel = min(DSA_TOPK, seq // 4)
    prow = 2 * (HEAD_DIM + ONES_ROWS)

    def call(fast, *args):
        return pl.pallas_call(
            functools.partial(_dsa_kernel, t=t, nk=nq, n_sel=n_sel, fast=fast),
            grid=(bsz, nq, width // LANES),
            in_specs=[pl.BlockSpec((1, LANES, t), lambda b, i, p: (b, p, i)),
                      pl.BlockSpec((1, seq, LANES), lambda b, i, p: (b, 0, p)),
                      pl.BlockSpec((1, prow, seq), lambda b, i, p: (b, p, 0)),
                      pl.BlockSpec((1, IDX_HEADS * IDX_DIM, t), lambda b, i, p: (b, 0, i)),
                      pl.BlockSpec((1, seq, LANES), lambda b, i, p: (b, 0, 0)),
                      pl.BlockSpec((1, IDX_HEADS, t), lambda b, i, p: (b, 0, i))],
            out_specs=pl.BlockSpec((1, LANES, t), lambda b, i, p: (b, p, i)),
            out_shape=jax.ShapeDtypeStruct((bsz, width, seq), F32),
            scratch_shapes=[pltpu.VMEM((nq, t, t), I32), pltpu.VMEM((2, 1, t), F32),
                            pltpu.VMEM((2, HEAD_DIM + ONES_ROWS, t), F32)],
            compiler_params=_cparams(("parallel", "parallel", "arbitrary"), vmem_mb=56),
            name="dsa_attn_fast" if fast else "dsa_attn_safe",
        )(*args)

    args = (qt, k, vta, qit, ki, iwt)
    return lax.cond(fast_ok, functools.partial(call, True), functools.partial(call, False), *args)


def _kmean_kernel(k_ref, o_ref, *, nbb):
    x = k_ref[0]
    o_ref[0] = jnp.mean(x.reshape(nbb, MOBA_BLOCK, x.shape[-1]), axis=1)


def block_means(k_f32):
    bsz, seq, width = k_f32.shape
    nb = seq // MOBA_BLOCK
    nbb = 8 if nb % 8 == 0 else nb
    return pl.pallas_call(
        functools.partial(_kmean_kernel, nbb=nbb),
        grid=(bsz, nb // nbb),
        in_specs=[pl.BlockSpec((1, nbb * MOBA_BLOCK, width), lambda b, i: (b, i, 0))],
        out_specs=pl.BlockSpec((1, nbb, width), lambda b, i: (b, i, 0)),
        out_shape=jax.ShapeDtypeStruct((bsz, nb, width), F32),
        compiler_params=_cparams(("parallel", "parallel")),
        name="moba_kmean",
    )(k_f32)


def _moba_select_kernel(qt_ref, km_ref, o_ref, *, ts, n_sel, nbp):
    j = pl.program_id(2)
    blk = lax.broadcasted_iota(I32, (nbp, ts), 0)
    own = jnp.right_shift(j * ts + lax.broadcasted_iota(I32, (1, ts), 1), MOBA_BLOCK.bit_length() - 1)
    qt = qt_ref[0]
    for sub in range(2):
        gate = _dot3(km_ref[0], _half_rows(qt, sub))
        g = jnp.where(blk < own, gate, -jnp.inf)
        sel = jnp.zeros((nbp, ts), F32)
        for _ in range(n_sel):
            mx = jnp.max(g, axis=0, keepdims=True)
            is_mx = jnp.logical_and(g == mx, mx > -jnp.inf)
            first = jnp.min(jnp.where(is_mx, blk, 2 * nbp), axis=0, keepdims=True)
            pick = blk == first
            sel = jnp.where(pick, 1.0, sel)
            g = jnp.where(pick, -jnp.inf, g)
        o_ref[0, sub] = sel


def _moba_kernel(qt_ref, sel_ref, k_ref, vt_ref, o_ref, m_sc, acc_sc, *, t, fast):
    i = pl.program_id(2)
    _attn_init(m_sc, acc_sc)
    qt = qt_ref[0]
    qb = [_half_rows(qt, sub).astype(BF16) for sub in range(2)]
    hrows = HEAD_DIM + ONES_ROWS

    def attend(kb, n, mask_fn):
        off = pl.multiple_of(kb * t, t)
        k = k_ref[0, pl.ds(off, n * t), :]
        vt = vt_ref[0, :, pl.ds(off, n * t)]
        s = [mask_fn(jnp.dot(k, qb[sub], preferred_element_type=F32), sub) for sub in range(2)]
        for sub in range(2):
            _attn_update(s[sub], vt[sub * hrows:(sub + 1) * hrows], m_sc, acc_sc, sub, fast)

    causal = _causal_t(t)
    attend(i, 1, lambda s_t, sub: jnp.where(causal, s_t, NEG))

    def past_blocks(kb, n):
        def row_mask(s_t, sub):
            rows = [jnp.where(sel_ref[0, sub, pl.ds(kb + c, 1), :] > 0.0, 0.0, NEG) for c in range(n)]
            bias = jnp.concatenate([jnp.broadcast_to(r, (t, t)) for r in rows], axis=0)
            return s_t + bias
        attend(kb, n, row_mask)

    _for_tile_groups(i, past_blocks, MOBA_GROUP)
    o_ref[0] = jnp.concatenate([_attn_out(acc_sc, sub, HEAD_DIM) for sub in range(2)], axis=0)


def moba_attention(qt_f32, kmean, k, vta, fast_ok):
    bsz, width, seq = qt_f32.shape
    t = MOBA_BLOCK
    nb = seq // MOBA_BLOCK
    n_sel = max(1, min(MOBA_TOPK, nb - 1))
    nbp = -(-nb // 8) * 8
    km = jnp.zeros((bsz, nbp, width), F32).at[:, :nb].set(kmean)
    prow = 2 * (HEAD_DIM + ONES_ROWS)
    n_pairs = width // LANES
    ts = _pick_tile(seq, (1024, 512, 256))
    sel = pl.pallas_call(
        functools.partial(_moba_select_kernel, ts=ts, n_sel=n_sel, nbp=nbp),
        grid=(bsz, n_pairs, seq // ts),
        in_specs=[pl.BlockSpec((1, LANES, ts), lambda b, h, j: (b, h, j)),
                  pl.BlockSpec((1, nbp, LANES), lambda b, h, j: (b, 0, h))],
        out_specs=pl.BlockSpec((1, 2, nbp, ts), lambda b, h, j: (b, h, 0, j)),
        out_shape=jax.ShapeDtypeStruct((bsz, 2 * n_pairs, nbp, seq), F32),
        compiler_params=_cparams(("parallel", "parallel", "parallel")),
        name="moba_select",
    )(qt_f32, km)

    def call(fast, *args):
        return pl.pallas_call(
            functools.partial(_moba_kernel, t=t, fast=fast),
            grid=(bsz, n_pairs, nb),
            in_specs=[pl.BlockSpec((1, LANES, t), lambda b, h, i: (b, h, i)),
                      pl.BlockSpec((1, 2, nbp, t), lambda b, h, i: (b, h, 0, i)),
                      pl.BlockSpec((1, seq, LANES), lambda b, h, i: (b, 0, h)),
                      pl.BlockSpec((1, prow, seq), lambda b, h, i: (b, h, 0))],
            out_specs=pl.BlockSpec((1, LANES, t), lambda b, h, i: (b, h, i)),
            out_shape=jax.ShapeDtypeStruct((bsz, width, seq), F32),
            scratch_shapes=[pltpu.VMEM((2, 1, t), F32), pltpu.VMEM((2, HEAD_DIM + ONES_ROWS, t), F32)],
            compiler_params=_cparams(("parallel", "parallel", "arbitrary")),
            name="moba_attn_fast" if fast else "moba_attn_safe",
        )(*args)

    args = (qt_f32, sel, k, vta)
    return lax.cond(fast_ok, functools.partial(call, True), functools.partial(call, False), *args)


def s5_operators(lam_re, lam_im, b_re, b_im, c_re, c_im, d_skip, log_dt, chunk):
    n_g, n_p = lam_re.shape
    n_c = d_skip.shape[-1]
    dt = jnp.exp(log_dt.astype(F32))[:, None]
    den = lam_re * lam_re + lam_im * lam_im
    mag = jnp.exp(lam_re * dt)
    abar_re = mag * jnp.cos(lam_im * dt)
    abar_im = mag * jnp.sin(lam_im * dt)
    coef_re = ((abar_re - 1.0) * lam_re + abar_im * lam_im) / den
    coef_im = (abar_im * lam_re - (abar_re - 1.0) * lam_im) / den
    bbar_re = coef_re[..., None] * b_re - coef_im[..., None] * b_im
    bbar_im = coef_re[..., None] * b_im + coef_im[..., None] * b_re
    tau = jnp.arange(chunk + 1, dtype=F32)[:, None, None]
    pmag = jnp.exp(lam_re * dt * tau)
    pw_re = pmag * jnp.cos(lam_im * dt * tau)
    pw_im = pmag * jnp.sin(lam_im * dt * tau)
    ab_re = pw_re[..., None] * bbar_re - pw_im[..., None] * bbar_im
    ab_im = pw_re[..., None] * bbar_im + pw_im[..., None] * bbar_re
    kern = (jnp.einsum('gop,tgpi->tgio', c_re, ab_re, precision=HI)
            - jnp.einsum('gop,tgpi->tgio', c_im, ab_im, precision=HI))
    t_idx = jnp.arange(chunk)
    lag = t_idx[None, :] - t_idx[:, None]
    toep = jnp.where((lag >= 0)[None, :, None, :, None],
                     kern[jnp.clip(lag, 0, chunk)].transpose(2, 0, 3, 1, 4), 0.0)
    eye = (jnp.eye(chunk)[None, :, None, :, None] * jnp.eye(n_c)[None, None, :, None, :])
    toep = toep + eye * d_skip[:, None, :, None, None]
    toep = toep.reshape(n_g, chunk * n_c, chunk * n_c)
    rev = chunk - 1 - t_idx
    w_in = jnp.concatenate([ab_re[rev].transpose(1, 0, 3, 2), ab_im[rev].transpose(1, 0, 3, 2)], axis=-1)
    w_in = w_in.reshape(n_g, chunk * n_c, 2 * n_p)
    a_re, a_im = pw_re[1:chunk + 1], pw_im[1:chunk + 1]
    st_re = (c_re[None] * a_re[:, :, None, :] - c_im[None] * a_im[:, :, None, :])
    st_im = (-c_re[None] * a_im[:, :, None, :] - c_im[None] * a_re[:, :, None, :])
    w_st = jnp.concatenate([st_re.transpose(1, 3, 0, 2), st_im.transpose(1, 3, 0, 2)], axis=1)
    w_st = w_st.reshape(n_g, 2 * n_p, chunk * n_c)
    return toep, w_in, w_st, pw_re[chunk], pw_im[chunk]


def _s5_in_kernel(u_ref, w_ref, o_ref):
    o_ref[0, 0] = _dot3(u_ref[0, 0], w_ref[0])


def _s5_scan_kernel(x_ref, aa_ref, ab_ref, abs_ref, o_ref, h_sc, hs_sc, *, steps, n_p):
    @pl.when(pl.program_id(0) == 0)
    def _():
        h_sc[...] = jnp.zeros(h_sc.shape, F32)
        hs_sc[...] = jnp.zeros(hs_sc.shape, F32)

    aa, ab, abs_ = aa_ref[...], ab_ref[...], abs_ref[...]

    def body(t, carry):
        h, hs = carry
        o_ref[t] = h
        x = x_ref[t]
        xs = pltpu.roll(x, n_p, 1)
        return aa * h + ab * hs + x, aa * hs + abs_ * h + xs

    h, hs = lax.fori_loop(0, steps, body, (h_sc[...], hs_sc[...]))
    h_sc[...] = h
    hs_sc[...] = hs


def _s5_out_kernel(u_ref, h0_ref, t_ref, w_ref, o_ref):
    y = _dot3(u_ref[0, 0], t_ref[0]) + _dot3(h0_ref[0, 0], w_ref[0])
    o_ref[0, 0] = jax.nn.gelu(y)


def s5_layer(u, ops):
    toep, w_in, w_st, ac_re, ac_im = ops
    bsz, seq, width = u.shape
    n_g, n_p = ac_re.shape
    n_c = width // n_g
    chunk = S5_CHUNK
    nj = seq // chunk
    row = chunk * n_c
    ur = u.reshape(bsz, nj, chunk, n_g, n_c).transpose(0, 3, 1, 2, 4).reshape(bsz, n_g, nj, row)
    hloc = pl.pallas_call(
        _s5_in_kernel,
        grid=(bsz, n_g),
        in_specs=[pl.BlockSpec((1, 1, nj, row), lambda b, g: (b, g, 0, 0)),
                  pl.BlockSpec((1, row, 2 * n_p), lambda b, g: (g, 0, 0))],
        out_specs=pl.BlockSpec((1, 1, nj, 2 * n_p), lambda b, g: (b, g, 0, 0)),
        out_shape=jax.ShapeDtypeStruct((bsz, n_g, nj, 2 * n_p), F32),
        compiler_params=_cparams(("parallel", "parallel")),
        name="s5_in",
    )(ur, w_in)
    bg = bsz * n_g
    xs = hloc.transpose(2, 0, 1, 3).reshape(nj, bg, 2 * n_p)
    rep = lambda a: jnp.tile(a, (bsz, 1))
    aa = rep(jnp.concatenate([ac_re, ac_re], axis=-1))
    ab = rep(jnp.concatenate([-ac_im, ac_im], axis=-1))
    abs_ = rep(jnp.concatenate([ac_im, -ac_im], axis=-1))
    steps = _pick_tile(nj, (128, 64, 32, 16, 8))
    cst = pl.BlockSpec((bg, 2 * n_p), lambda t: (0, 0))
    h0 = pl.pallas_call(
        functools.partial(_s5_scan_kernel, steps=steps, n_p=n_p),
        grid=(nj // steps,),
        in_specs=[pl.BlockSpec((steps, bg, 2 * n_p), lambda t: (t, 0, 0)), cst, cst, cst],
        out_specs=pl.BlockSpec((steps, bg, 2 * n_p), lambda t: (t, 0, 0)),
        out_shape=jax.ShapeDtypeStruct((nj, bg, 2 * n_p), F32),
        scratch_shapes=[pltpu.VMEM((bg, 2 * n_p), F32), pltpu.VMEM((bg, 2 * n_p), F32)],
        compiler_params=_cparams(("arbitrary",)),
        name="s5_scan",
    )(xs, aa, ab, abs_)
    h0 = h0.reshape(nj, bsz, n_g, 2 * n_p).transpose(1, 2, 0, 3)
    y = pl.pallas_call(
        _s5_out_kernel,
        grid=(bsz, n_g),
        in_specs=[pl.BlockSpec((1, 1, nj, row), lambda b, g: (b, g, 0, 0)),
                  pl.BlockSpec((1, 1, nj, 2 * n_p), lambda b, g: (b, g, 0, 0)),
                  pl.BlockSpec((1, row, row), lambda b, g: (g, 0, 0)),
                  pl.BlockSpec((1, 2 * n_p, row), lambda b, g: (g, 0, 0))],
        out_specs=pl.BlockSpec((1, 1, nj, row), lambda b, g: (b, g, 0, 0)),
        out_shape=jax.ShapeDtypeStruct((bsz, n_g, nj, row), F32),
        compiler_params=_cparams(("parallel", "parallel")),
        name="s5_out",
    )(ur, h0, toep, w_st)
    return y.reshape(bsz, n_g, nj, chunk, n_c).transpose(0, 2, 3, 1, 4).reshape(bsz, seq, width)


def _glu_kernel(y_ref, w_ref, b_ref, o_ref):
    y = y_ref[...]
    gate = jnp.dot(y.astype(BF16), w_ref[...], preferred_element_type=F32) + b_ref[...]
    o_ref[...] = (y * jax.nn.sigmoid(gate)).astype(o_ref.dtype)


def glu(y, w_bf16, b, tm=1024):
    n_tok, width = y.shape
    tm = _pick_tile(n_tok, (tm, 512, 256, 128))
    return pl.pallas_call(
        _glu_kernel,
        grid=(n_tok // tm,),
        in_specs=[pl.BlockSpec((tm, width), lambda i: (i, 0)),
                  pl.BlockSpec((width, width), lambda i: (0, 0)),
                  pl.BlockSpec((1, width), lambda i: (0, 0))],
        out_specs=pl.BlockSpec((tm, width), lambda i: (i, 0)),
        out_shape=jax.ShapeDtypeStruct((n_tok, width), F32),
        compiler_params=_cparams(("parallel",)),
        name="s5_glu",
    )(y, w_bf16, b.astype(F32).reshape(1, width))


def _outproj_kernel(a_ref, b_ref, w_ref, x_ref, g_ref, o_ref, *, half, a_t, b_t):
    a = a_ref[0].T if a_t else a_ref[0]
    b = b_ref[0].T if b_t else b_ref[0]
    y = (jnp.dot(a.astype(BF16), w_ref[:half, :], preferred_element_type=F32)
         + jnp.dot(b.astype(BF16), w_ref[half:, :], preferred_element_type=F32))
    o_ref[0] = x_ref[0] + g_ref[0] * y


def out_proj(a, b, w_bf16, x, gate, a_t=False, b_t=False, tm=512):
    bsz, seq, d = x.shape
    half = w_bf16.shape[0] // 2
    tm = _pick_tile(seq, (tm, 256, 128))

    def spec(transposed):
        if transposed:
            return pl.BlockSpec((1, half, tm), lambda bb, i: (bb, 0, i))
        return pl.BlockSpec((1, tm, half), lambda bb, i: (bb, i, 0))

    return pl.pallas_call(
        functools.partial(_outproj_kernel, half=half, a_t=a_t, b_t=b_t),
        grid=(bsz, seq // tm),
        in_specs=[spec(a_t), spec(b_t),
                  pl.BlockSpec((2 * half, d), lambda bb, i: (0, 0)),
                  pl.BlockSpec((1, tm, d), lambda bb, i: (bb, i, 0)),
                  pl.BlockSpec((1, 1, d), lambda bb, i: (bb, 0, 0))],
        out_specs=pl.BlockSpec((1, tm, d), lambda bb, i: (bb, i, 0)),
        out_shape=jax.ShapeDtypeStruct((bsz, seq, d), F32),
        compiler_params=_cparams(("parallel", "parallel")),
        name="out_proj",
    )(a, b, w_bf16, x, gate)


def _router_kernel(x_ref, sc_ref, sh_ref, wr_ref, br_ref, h_ref, r_ref):
    h = _norm_mod(x_ref[0], sc_ref[0], sh_ref[0])
    h_ref[0] = h.astype(BF16)
    logits = _dot3(h, wr_ref[...]) + br_ref[...]
    tm = logits.shape[0]
    lane = lax.broadcasted_iota(I32, (tm, LANES), 1)
    big = 4 * LANES
    coarse = jnp.where(lane < N_GROUPS, logits, -jnp.inf)
    cmax = jnp.max(coarse, axis=1, keepdims=True)
    grp = jnp.min(jnp.where(coarse == cmax, lane, big), axis=1, keepdims=True)
    p_grp = 1.0 / jnp.sum(jnp.exp(coarse - cmax), axis=1, keepdims=True)
    lo = N_GROUPS + grp * EXPERTS_PER_GROUP
    fine = jnp.where(jnp.logical_and(lane >= lo, lane < lo + EXPERTS_PER_GROUP), logits, -jnp.inf)
    v1 = jnp.max(fine, axis=1, keepdims=True)
    i1 = jnp.min(jnp.where(fine == v1, lane, big), axis=1, keepdims=True)
    fine2 = jnp.where(lane == i1, -jnp.inf, fine)
    v2 = jnp.max(fine2, axis=1, keepdims=True)
    i2 = jnp.min(jnp.where(fine2 == v2, lane, big), axis=1, keepdims=True)
    e2 = jnp.exp(v2 - v1)
    wa = p_grp / (1.0 + e2)
    wb = p_grp * e2 / (1.0 + e2)
    ida = (i1 - N_GROUPS).astype(F32)
    idb = (i2 - N_GROUPS).astype(F32)
    r_ref[0] = jnp.where(lane == 0, ida, jnp.where(lane == 1, idb,
                         jnp.where(lane == 2, wa, jnp.where(lane == 3, wb, 0.0))))


def moe_router(x, sc, sh, wg, bg, we, be, tm=512):
    bsz, seq, d = x.shape
    tm = _pick_tile(seq, (tm, 256, 128))
    n_r = N_GROUPS + N_EXPERTS
    wr = jnp.zeros((d, LANES), F32).at[:, :N_GROUPS].set(wg).at[:, N_GROUPS:n_r].set(we)
    br = jnp.zeros((1, LANES), F32).at[0, :N_GROUPS].set(bg).at[0, N_GROUPS:n_r].set(be)
    return pl.pallas_call(
        _router_kernel,
        grid=(bsz, seq // tm),
        in_specs=[pl.BlockSpec((1, tm, d), lambda b, i: (b, i, 0)),
                  pl.BlockSpec((1, 1, d), lambda b, i: (b, 0, 0)),
                  pl.BlockSpec((1, 1, d), lambda b, i: (b, 0, 0)),
                  pl.BlockSpec((d, LANES), lambda b, i: (0, 0)),
                  pl.BlockSpec((1, LANES), lambda b, i: (0, 0))],
        out_specs=[pl.BlockSpec((1, tm, d), lambda b, i: (b, i, 0)),
                   pl.BlockSpec((1, tm, LANES), lambda b, i: (b, i, 0))],
        out_shape=[jax.ShapeDtypeStruct((bsz, seq, d), BF16),
                   jax.ShapeDtypeStruct((bsz, seq, LANES), F32)],
        compiler_params=_cparams(("parallel", "parallel")),
        name="moe_router",
    )(x, sc, sh, wr, br)


MOE_SEG = 256


MOE_PAD = 16
MOE_EXPERTS_PER_STEP = 1


def _pad_rows(count):
    return jnp.floor((count + (MOE_PAD - 1)) * (1.0 / MOE_PAD)) * MOE_PAD


def _moe_kernel(erow_ref, route_ref, h_ref, x_ref, g_ref, w1_ref, w3_ref, w2_ref, o_ref,
                xs_sc, ys_sc, meta_sc, drow_sc, dcol_sc, *, chunk, rows, na):
    step = pl.program_id(1)
    n_seg = chunk // MOE_SEG
    lane1 = lax.broadcasted_iota(I32, (1, LANES), 1)

    def lane_scalar(row, k):
        return jnp.sum(jnp.where(lane1 == k, row, 0.0)).astype(I32)

    @pl.when(step == 0)
    def _():
        si = lax.broadcasted_iota(I32, (MOE_SEG, MOE_SEG), 0)
        sj = lax.broadcasted_iota(I32, (MOE_SEG, MOE_SEG), 1)
        upper = jnp.where(si < sj, 1.0, 0.0).astype(BF16)
        lower = jnp.where(sj < si, 1.0, 0.0).astype(BF16)
        erow = erow_ref[0]
        eid = lax.broadcasted_iota(I32, (N_EXPERTS, chunk), 0)
        sel = [erow[k:k + 1] == eid for k in range(2)]
        member = jnp.where(jnp.logical_or(sel[0], sel[1]), 1.0, 0.0)
        count = jnp.zeros((N_EXPERTS, 1), F32)
        parts = []
        for sgi in range(n_seg):
            seg = member[:, sgi * MOE_SEG:(sgi + 1) * MOE_SEG]
            parts.append(jnp.dot(seg.astype(BF16), upper, preferred_element_type=F32) + count)
            count = count + jnp.sum(seg, axis=1, keepdims=True)
        rank = jnp.concatenate(parts, axis=1)
        padded = _pad_rows(count)
        ei = lax.broadcasted_iota(I32, (N_EXPERTS, N_EXPERTS), 0)
        ej = lax.broadcasted_iota(I32, (N_EXPERTS, N_EXPERTS), 1)
        below = jnp.where(ej < ei, 1.0, 0.0).astype(BF16)
        start = jnp.dot(below, jnp.broadcast_to(padded, (N_EXPERTS, LANES)).astype(BF16),
                        preferred_element_type=F32)
        total = jnp.sum(padded, axis=0, keepdims=True)
        lane_e = lax.broadcasted_iota(I32, (N_EXPERTS, LANES), 1)
        meta_sc[...] = jnp.where(lane_e == 0, start, jnp.where(lane_e == 1, count, jnp.where(lane_e == 2, total, 0.0)))
        dest = start[:, 0:1] + rank
        for k in range(2):
            drow_sc[k] = jnp.sum(jnp.where(sel[k], dest, 0.0), axis=0, keepdims=True)
        route = route_ref[...]
        lane_c = lax.broadcasted_iota(I32, (chunk, LANES), 1)
        lane_f = lane_c.astype(F32)
        sel_c = [route[:, k:k + 1] == lane_f for k in range(2)]
        member_c = jnp.where(jnp.logical_or(sel_c[0], sel_c[1]), 1.0, 0.0)
        count_c = jnp.zeros((1, LANES), F32)
        parts_c = []
        for sgi in range(n_seg):
            seg = member_c[sgi * MOE_SEG:(sgi + 1) * MOE_SEG]
            parts_c.append(jnp.dot(lower, seg.astype(BF16), preferred_element_type=F32) + count_c)
            count_c = count_c + jnp.sum(seg, axis=0, keepdims=True)
        rank_c = jnp.concatenate(parts_c, axis=0)
        li = lax.broadcasted_iota(I32, (LANES, LANES), 0)
        lj = lax.broadcasted_iota(I32, (LANES, LANES), 1)
        before = jnp.where(li < lj, 1.0, 0.0).astype(BF16)
        start_c = jnp.dot(jnp.broadcast_to(_pad_rows(count_c), (8, LANES)).astype(BF16), before,
                          preferred_element_type=F32)[0:1]
        dest_c = start_c + rank_c
        dcols = [jnp.sum(jnp.where(sel_c[k], dest_c, 0.0), axis=1, keepdims=True) for k in range(2)]
        dcol_sc[...] = jnp.where(lane_c == 0, dcols[0], jnp.where(lane_c == 1, dcols[1], 0.0))
        ys_sc[...] = jnp.zeros(ys_sc.shape, BF16)
        xs_sc[...] = jnp.zeros(xs_sc.shape, BF16)
        d0, d1 = drow_sc[0], drow_sc[1]
        n_slabs = (jnp.sum(total).astype(I32) + MOE_SEG - 1) // MOE_SEG

        def sort_slab(s, carry):
            off = pl.multiple_of(s * MOE_SEG, MOE_SEG)
            r = (lax.broadcasted_iota(I32, (MOE_SEG, 1), 0) + off).astype(F32)
            onehot = jnp.where(jnp.logical_or(d0 == r, d1 == r), 1.0, 0.0).astype(BF16)
            xs_sc[pl.ds(off, MOE_SEG), :] = jnp.dot(onehot, h_ref[...], preferred_element_type=F32).astype(BF16)
            return carry

        lax.fori_loop(0, n_slabs, sort_slab, 0)

    for slot in range(MOE_EXPERTS_PER_STEP):
        meta = meta_sc[pl.ds(step * MOE_EXPERTS_PER_STEP + slot, 1), :]
        first = lane_scalar(meta, 0)
        count_e = lane_scalar(meta, 1)
        padded_e = ((count_e + MOE_PAD - 1) // MOE_PAD) * MOE_PAD

        def block(bi, carry, slot=slot, first=first, padded_e=padded_e):
            r0 = pl.multiple_of(first + bi * rows, MOE_PAD)
            xb = xs_sc[pl.ds(r0, rows), :]
            a = jnp.dot(xb, w1_ref[slot], preferred_element_type=F32)
            hid = (a * jax.nn.sigmoid(a)) * jnp.dot(xb, w3_ref[slot], preferred_element_type=F32)
            yb = jnp.dot(hid.astype(BF16), w2_ref[slot], preferred_element_type=F32)
            mine = lax.broadcasted_iota(I32, (rows, 1), 0) + bi * rows < padded_e
            ys_sc[pl.ds(r0, rows), :] = jnp.where(mine, yb, ys_sc[pl.ds(r0, rows), :].astype(F32)).astype(BF16)
            return carry

        lax.fori_loop(0, (count_e + rows - 1) // rows, block, 0)

    @pl.when(step == N_EXPERTS // MOE_EXPERTS_PER_STEP - 1)
    def _():
        route = route_ref[...]
        dcol = dcol_sc[...]
        d0, d1 = dcol[:, 0:1], dcol[:, 1:2]
        w0, w1 = route[:, 2:3], route[:, 3:4]
        r = lax.broadcasted_iota(I32, (1, na), 1).astype(F32)
        comb = (jnp.where(d0 == r, w0, 0.0) + jnp.where(d1 == r, w1, 0.0)).astype(BF16)
        moe = jnp.dot(comb, ys_sc[:na, :], preferred_element_type=F32)
        o_ref[...] = x_ref[...] + g_ref[0] * moe


def moe_experts(h, route, x, gate, w1_bf16, w3_bf16, w2_bf16, chunk=1024, rows=128):
    bsz, seq, d = x.shape
    chunk = _pick_tile(seq, (chunk, 512, 256))
    n_tok = bsz * seq
    n_chunks = n_tok // chunk
    per_b = seq // chunk
    ff = w1_bf16.shape[-1]
    na = -(-(2 * chunk + N_EXPERTS * (MOE_PAD - 1)) // MOE_SEG) * MOE_SEG
    route2 = route.reshape(n_tok, LANES)
    erow = route2[:, :2].astype(I32).T.reshape(2, n_chunks, chunk).transpose(1, 0, 2)
    eps = MOE_EXPERTS_PER_STEP
    out = pl.pallas_call(
        functools.partial(_moe_kernel, chunk=chunk, rows=rows, na=na),
        grid=(n_chunks, N_EXPERTS // eps),
        in_specs=[pl.BlockSpec((1, 2, chunk), lambda c, e: (c, 0, 0)),
                  pl.BlockSpec((chunk, LANES), lambda c, e: (c, 0)),
                  pl.BlockSpec((chunk, d), lambda c, e: (c, 0)),
                  pl.BlockSpec((chunk, d), lambda c, e: (c, 0)),
                  pl.BlockSpec((1, 1, d), lambda c, e: (c // per_b, 0, 0)),
                  pl.BlockSpec((eps, d, ff), lambda c, e: (e, 0, 0)),
                  pl.BlockSpec((eps, d, ff), lambda c, e: (e, 0, 0)),
                  pl.BlockSpec((eps, ff, d), lambda c, e: (e, 0, 0))],
        out_specs=pl.BlockSpec((chunk, d), lambda c, e: (c, 0)),
        out_shape=jax.ShapeDtypeStruct((n_tok, d), F32),
        scratch_shapes=[pltpu.VMEM((na + rows, d), BF16), pltpu.VMEM((na + rows, d), BF16),
                        pltpu.VMEM((N_EXPERTS, LANES), F32), pltpu.VMEM((2, 1, chunk), F32),
                        pltpu.VMEM((chunk, LANES), F32)],
        compiler_params=_cparams(("parallel", "arbitrary")),
        name="moe_experts",
    )(erow, route2, h.reshape(n_tok, d), x.reshape(n_tok, d), gate, w1_bf16, w3_bf16, w2_bf16)
    return out.reshape(bsz, seq, d)


def kernel(x, c, ada_w, ada_b, moe_wg, moe_bg, moe_we, moe_be, moe_w1, moe_w3, moe_w2,
           ev_w_in, ev_w_out, a_qn, a_kn, a_lq1, a_lk1, a_lq2, a_lk2, a_subln,
           b_qn, b_kn, b_idx_kn, od_w_in, od_w_out, c_qn, c_kn,
           s5_lam_re, s5_lam_im, s5_b_re, s5_b_im, s5_c_re, s5_c_im, s5_d, s5_log_dt,
           s5_glu_w, s5_glu_b):
    bsz, seq, d = x.shape
    depth = ada_w.shape[0]
    mod = ada_mod(c, ada_w, ada_b)
    rope_h = rope_lane_tables(seq, HEAD_DIM, ROT_HEAD)
    rope_i = rope_lane_tables(seq, IDX_DIM, ROT_IDX)
    qk_scale = HEAD_DIM ** -0.5 * math.log2(math.e)
    half_h, half_i = ROT_HEAD // 2, ROT_IDX // 2
    hw = A_HEADS * 2 * HEAD_DIM

    def pad_cols(w):
        n = w.shape[1]
        npad = -(-n // LANES) * LANES
        return jnp.zeros((w.shape[0], npad), F32).at[:, :n].set(w).astype(BF16)

    for l in range(depth):
        sh1, sc1, g1, sh2, sc2, g2 = [m[:, None, :] for m in jnp.split(mod[l], 6, axis=-1)]
        i = l // 2
        if l % 2 == 0:
            lam_init = 0.8 - 0.6 * math.exp(-0.3 * l)
            z = in_proj(x, sc1, sh1, pad_cols(ev_w_in[i]))
            q_t, v_a, v_b = ((BF16, "t"),), ((BF16, ("aug", 2 * HEAD_DIM)),), ((BF16, ("aug", HEAD_DIM)),)
            (aq,) = prep(z, 0, hw, hd=HEAD_DIM, gain=a_qn[i], rope=rope_h, half=half_h, scale=qk_scale, outs=q_t)
            (ak,) = prep(z, hw, hw, hd=HEAD_DIM, gain=a_kn[i], rope=rope_h, half=half_h)
            (av,) = prep(z, 2 * hw, hw, hd=HEAD_DIM, outs=v_a)
            o_a = diff_attention(aq, ak, av, a_lq1[i], a_lk1[i], a_lq2[i], a_lk2[i], a_subln[i], lam_init,
                                 _logit_bound(a_qn[i], a_kn[i]) <= MAX_FAST_LOGIT)
            (bq,) = prep(z, 3 * hw, hw, hd=HEAD_DIM, gain=b_qn[i], rope=rope_h, half=half_h, scale=qk_scale, outs=q_t)
            (bk,) = prep(z, 4 * hw, hw, hd=HEAD_DIM, gain=b_kn[i], rope=rope_h, half=half_h)
            (bv,) = prep(z, 5 * hw, hw, hd=HEAD_DIM, outs=v_b)
            iq_off = 6 * hw
            ik_off = iq_off + IDX_HEADS * IDX_DIM
            (iq,) = prep(z, iq_off, IDX_HEADS * IDX_DIM, hd=IDX_DIM, rope=rope_i, half=half_i, outs=q_t)
            ik, iw = prep(z, ik_off, LANES, hd=IDX_DIM, gain=b_idx_kn[i], rope=rope_i, half=half_i,
                          outs=((BF16, "n"), (F32, ("rawt", IDX_DIM, IDX_DIM + IDX_HEADS))))
            o_b = dsa_attention(bq, bk, bv, iq, ik, iw, _logit_bound(b_qn[i], b_kn[i]) <= MAX_FAST_LOGIT)
            x = out_proj(o_a, o_b, ev_w_out[i].astype(BF16), x, g1, a_t=True, b_t=True)
        else:
            z = in_proj(x, sc1, sh1, pad_cols(od_w_in[i]))
            (cq,) = prep(z, 0, hw, hd=HEAD_DIM, gain=c_qn[i], rope=rope_h, half=half_h, scale=qk_scale,
                         outs=((F32, "t"),))
            ck32, ck = prep(z, hw, hw, hd=HEAD_DIM, gain=c_kn[i], rope=rope_h, half=half_h,
                            outs=((F32, "n"), (BF16, "n")))
            (cv,) = prep(z, 2 * hw, hw, hd=HEAD_DIM, outs=((BF16, ("aug", HEAD_DIM)),))
            o_c = moba_attention(cq, block_means(ck32), ck, cv, _logit_bound(c_qn[i], c_kn[i]) <= MAX_FAST_LOGIT)
            ops = s5_operators(s5_lam_re[i], s5_lam_im[i], s5_b_re[i], s5_b_im[i], s5_c_re[i], s5_c_im[i],
                               s5_d[i], s5_log_dt[i], S5_CHUNK)
            y = s5_layer(z[:, :, 3 * hw:4 * hw], ops)
            y = glu(y.reshape(bsz * seq, hw), s5_glu_w[i].astype(BF16), s5_glu_b[i]).reshape(bsz, seq, hw)
            x = out_proj(o_c, y, od_w_out[i].astype(BF16), x, g1, a_t=True)
        h, route = moe_router(x, sc2, sh2, moe_wg[l], moe_bg[l], moe_we[l], moe_be[l])
        x = moe_experts(h, route, x, g2, moe_w1[l].astype(BF16), moe_w3[l].astype(BF16), moe_w2[l].astype(BF16))
    return x
```

```python
import functools
import math

import jax
import jax.numpy as jnp
from jax import lax
from jax.experimental import pallas as pl
from jax.experimental.pallas import tpu as pltpu

F32 = jnp.float32
BF16 = jnp.bfloat16
I32 = jnp.int32
HI = lax.Precision.HIGHEST

LANES = 128
NEG = -1e30
I32_MIN = -2147483648

HEAD_DIM = 64
A_HEADS = 4
B_HEADS = 8
IDX_HEADS = 8
IDX_DIM = 32
DSA_TOPK = 256
C_HEADS = 8
MOBA_BLOCK = 256
MOBA_TOPK = 3
S5_GROUP_CH = 16
S5_GROUPS = 32
S5_STATE = 64
ROPE_THETA = 500000.0
ROT_HEAD = HEAD_DIM // 4
ROT_IDX = IDX_DIM // 4
N_GROUPS = 4
EXPERTS_PER_GROUP = 8
N_EXPERTS = N_GROUPS * EXPERTS_PER_GROUP
EXPERT_FF = 512
NORM_EPS = 1e-6
S5_CHUNK = 16


def _cparams(sem, vmem_mb=48):
    return pltpu.CompilerParams(dimension_semantics=sem, vmem_limit_bytes=vmem_mb * 1024 * 1024)


def _dot3(a, b):
    a_hi = a.astype(BF16)
    a_lo = (a - a_hi.astype(F32)).astype(BF16)
    b_hi = b.astype(BF16)
    b_lo = (b - b_hi.astype(F32)).astype(BF16)
    dot = functools.partial(jnp.dot, preferred_element_type=F32)
    return dot(a_hi, b_hi) + (dot(a_hi, b_lo) + dot(a_lo, b_hi))


def _mod_kernel(c_ref, w_ref, b_ref, o_ref):
    c = c_ref[...]
    ca = c * jax.nn.sigmoid(c)
    o_ref[0] = jnp.dot(ca, w_ref[0], precision=HI, preferred_element_type=F32) + b_ref[0]


def ada_mod(c, ada_w, ada_b):
    depth, d, n6 = ada_w.shape
    bsz = c.shape[0]
    rows = 8
    c_pad = jnp.zeros((rows, d), F32).at[:bsz].set(c)
    out = pl.pallas_call(
        _mod_kernel,
        grid=(depth, n6 // d),
        in_specs=[pl.BlockSpec((rows, d), lambda l, n: (0, 0)),
                  pl.BlockSpec((1, d, d), lambda l, n: (l, 0, n)),
                  pl.BlockSpec((1, 1, d), lambda l, n: (l, 0, n))],
        out_specs=pl.BlockSpec((1, rows, d), lambda l, n: (l, 0, n)),
        out_shape=jax.ShapeDtypeStruct((depth, rows, n6), F32),
        compiler_params=_cparams(("parallel", "parallel")),
        name="ada_mod",
    )(c_pad, ada_w, ada_b.reshape(depth, 1, n6))
    return out[:, :bsz]


def _norm_mod(x, sc, sh):
    y = x * lax.rsqrt(jnp.mean(x * x, axis=-1, keepdims=True) + NORM_EPS)
    return y * (1.0 + sc) + sh


def _inproj_kernel(x_ref, sc_ref, sh_ref, w_ref, o_ref, h_sc):
    @pl.when(pl.program_id(2) == 0)
    def _():
        h_sc[...] = _norm_mod(x_ref[0], sc_ref[0], sh_ref[0]).astype(BF16)

    o_ref[0] = jnp.dot(h_sc[...], w_ref[...], preferred_element_type=F32)


def _pick_tile(n, prefs):
    for p in prefs:
        if n % p == 0:
            return p
    return n


def in_proj(x, sc, sh, w_bf16, tm=512):
    bsz, seq, d = x.shape
    n = w_bf16.shape[1]
    tm = _pick_tile(seq, (tm, 256, 128))
    tn = n if n <= 4096 else _pick_tile(n, (1152, 1024, 512, 384, 256, 128))
    return pl.pallas_call(
        _inproj_kernel,
        grid=(bsz, seq // tm, n // tn),
        in_specs=[pl.BlockSpec((1, tm, d), lambda b, i, j: (b, i, 0)),
                  pl.BlockSpec((1, 1, d), lambda b, i, j: (b, 0, 0)),
                  pl.BlockSpec((1, 1, d), lambda b, i, j: (b, 0, 0)),
                  pl.BlockSpec((d, tn), lambda b, i, j: (0, j))],
        out_specs=pl.BlockSpec((1, tm, tn), lambda b, i, j: (b, i, j)),
        out_shape=jax.ShapeDtypeStruct((bsz, seq, n), F32),
        scratch_shapes=[pltpu.VMEM((tm, d), BF16)],
        compiler_params=_cparams(("parallel", "parallel", "arbitrary")),
        name="in_proj",
    )(x, sc, sh, w_bf16)


def rope_lane_tables(seq, hd, rot):
    half = rot // 2
    pos = jnp.arange(seq, dtype=F32)
    inv = ROPE_THETA ** (-(jnp.arange(0, rot, 2, dtype=F32) / rot))
    ang = pos[:, None] * inv[None, :]
    cos, sin = jnp.cos(ang), jnp.sin(ang)
    dl = jnp.arange(LANES) % hd
    first = dl < half
    second = (dl >= half) & (dl < rot)
    idx = jnp.where(first, dl, jnp.where(second, dl - half, 0))
    cos_l = jnp.where(first | second, cos[:, idx], 1.0)
    sa_l = jnp.where(second, sin[:, idx], 0.0)
    sb_l = jnp.where(first, -sin[:, idx], 0.0)
    return cos_l.astype(F32), sa_l.astype(F32), sb_l.astype(F32)


def _prep_out_rows(mode, width):
    if mode == "t":
        return width
    if mode[0] == "aug":
        return (width // mode[1]) * (mode[1] + ONES_ROWS)
    return mode[2] - mode[1]


def _prep_kernel(*refs, hd, half, do_norm, do_rope, scale, outs):
    z_ref, g_ref, gm_ref, cos_ref, sa_ref, sb_ref = refs[:6]
    o_refs = refs[6:6 + len(outs)]
    x = z_ref[0]
    raw = x
    width = x.shape[-1]
    if do_norm:
        ms = jnp.dot((x * x).astype(BF16), gm_ref[...], preferred_element_type=F32) * (1.0 / hd)
        x = x * lax.rsqrt(ms + NORM_EPS) * g_ref[...]
    if do_rope:
        cos, sa, sb = cos_ref[...], sa_ref[...], sb_ref[...]
        cols = []
        for r in range(width // LANES):
            xc = x[:, r * LANES:(r + 1) * LANES]
            cols.append(xc * cos + pltpu.roll(xc, half, 1) * sa + pltpu.roll(xc, LANES - half, 1) * sb)
        x = cols[0] if len(cols) == 1 else jnp.concatenate(cols, axis=-1)
    if scale != 1.0:
        x = x * scale
    tm = x.shape[0]
    for o, (_, mode) in zip(o_refs, outs):
        if mode == "n":
            o[0] = x.astype(o.dtype)
        elif mode == "t":
            o[0] = x.T.astype(o.dtype)
        elif mode[0] == "aug":
            dv = mode[1]
            xt = x.T
            ones = jnp.ones((ONES_ROWS, tm), F32)
            parts = []
            for h in range(width // dv):
                parts += [xt[h * dv:(h + 1) * dv], ones]
            o[0] = jnp.concatenate(parts, axis=0).astype(o.dtype)
        else:
            o[0] = raw.T[mode[1]:mode[2]].astype(o.dtype)


def prep(z, col_off, width, *, hd, gain=None, rope=None, half=0, scale=1.0, outs=((BF16, "n"),), tm=2048):
    bsz, seq, _ = z.shape
    tm = _pick_tile(seq, (tm, 256, 128))
    cb = col_off // width
    assert col_off % width == 0 and width % LANES == 0
    do_norm = gain is not None
    do_rope = rope is not None
    g = jnp.tile(gain.astype(F32), width // hd).reshape(1, width) if do_norm else jnp.ones((1, width), F32)
    gi = jnp.arange(width) // hd
    gm = (gi[:, None] == gi[None, :]).astype(BF16)
    if do_rope:
        cos_l, sa_l, sb_l = rope
    else:
        cos_l = sa_l = sb_l = jnp.zeros((seq, LANES), F32)
    out_specs, out_shape = [], []
    for dt, mode in outs:
        if mode == "n":
            out_specs.append(pl.BlockSpec((1, tm, width), lambda b, i: (b, i, 0)))
            out_shape.append(jax.ShapeDtypeStruct((bsz, seq, width), dt))
        else:
            rows = _prep_out_rows(mode, width)
            out_specs.append(pl.BlockSpec((1, rows, tm), lambda b, i: (b, 0, i)))
            out_shape.append(jax.ShapeDtypeStruct((bsz, rows, seq), dt))
    return pl.pallas_call(
        functools.partial(_prep_kernel, hd=hd, half=half, do_norm=do_norm, do_rope=do_rope,
                          scale=scale, outs=tuple(outs)),
        grid=(bsz, seq // tm),
        in_specs=[pl.BlockSpec((1, tm, width), lambda b, i: (b, i, cb)),
                  pl.BlockSpec((1, width), lambda b, i: (0, 0)),
                  pl.BlockSpec((width, width), lambda b, i: (0, 0)),
                  pl.BlockSpec((tm, LANES), lambda b, i: (i, 0)),
                  pl.BlockSpec((tm, LANES), lambda b, i: (i, 0)),
                  pl.BlockSpec((tm, LANES), lambda b, i: (i, 0))],
        out_specs=out_specs,
        out_shape=out_shape,
        compiler_params=_cparams(("parallel", "parallel")),
        name="prep",
    )(z, g, gm, cos_l, sa_l, sb_l)


ATT_TILE = 256
ATT_GROUP = 8
MOBA_GROUP = 16


def _for_tile_groups(n_tiles, body, group_size=ATT_GROUP):
    def group(jg, carry):
        body(group_size * jg, group_size)
        return carry

    lax.fori_loop(0, n_tiles // group_size, group, 0)
    size = group_size // 2
    while size >= 1:
        def tail(size=size):
            body((n_tiles // (2 * size)) * (2 * size), size)
        pl.when((n_tiles // size) % 2 == 1)(tail)
        size //= 2


ONES_ROWS = 16
MAX_FAST_LOGIT = 60.0


def _attn_update(s_t, vt_aug, m_ref, acc_ref, idx, fast):
    if fast:
        acc_ref[idx] += jnp.dot(vt_aug, jnp.exp2(s_t).astype(BF16), preferred_element_type=F32)
    else:
        m_prev = m_ref[idx]
        m_new = jnp.maximum(m_prev, jnp.max(s_t, axis=0, keepdims=True))
        p_t = jnp.exp2(s_t - m_new).astype(BF16)
        acc_ref[idx] = (jnp.exp2(m_prev - m_new) * acc_ref[idx]
                        + jnp.dot(vt_aug, p_t, preferred_element_type=F32))
        m_ref[idx] = m_new


def _attn_init(m_sc, acc_sc):
    m_sc[...] = jnp.full(m_sc.shape, NEG, F32)
    acc_sc[...] = jnp.zeros(acc_sc.shape, F32)


def _attn_out(acc_ref, idx, dv):
    acc = acc_ref[idx]
    return acc[:dv] / acc[dv:dv + 1]


def _logit_bound(gain_q, gain_k):
    return (HEAD_DIM ** 0.5 * math.log2(math.e) * 1.02) * jnp.max(jnp.abs(gain_q)) * jnp.max(jnp.abs(gain_k))


def _half_rows(x, sub):
    row = lax.broadcasted_iota(I32, (LANES, 1), 0)
    return jnp.where((row >= sub * HEAD_DIM) & (row < (sub + 1) * HEAD_DIM), x, jnp.zeros_like(x))


def _causal_t(t):
    return lax.broadcasted_iota(I32, (t, t), 0) <= lax.broadcasted_iota(I32, (t, t), 1)


def _diff_attn_kernel(qt_ref, k_ref, vt_ref, lq1_ref, lk1_ref, lq2_ref, lk2_ref, g_ref, o_ref,
                      m_sc, acc_sc, *, t, lam_init, fast):
    i = pl.program_id(2)
    _attn_init(m_sc, acc_sc)
    qt = qt_ref[0]
    qc = [_half_rows(qt, c) for c in range(2)]

    def attend(jt, n, mask_fn):
        off = pl.multiple_of(jt * t, t)
        k = k_ref[0, pl.ds(off, n * t), :]
        vt = vt_ref[0, :, pl.ds(off, n * t)]
        s = [mask_fn(jnp.dot(k, qc[c], preferred_element_type=F32)) for c in range(2)]
        for c in range(2):
            _attn_update(s[c], vt, m_sc, acc_sc, c, fast)

    _for_tile_groups(i, lambda jt, n: attend(jt, n, lambda s_t: s_t), DIFF_GROUP)
    causal = _causal_t(t)
    attend(i, 1, lambda s_t: jnp.where(causal, s_t, NEG))

    lam = (jnp.exp(jnp.sum(lq1_ref[...] * lk1_ref[...], axis=-1, keepdims=True))
           - jnp.exp(jnp.sum(lq2_ref[...] * lk2_ref[...], axis=-1, keepdims=True)) + lam_init)
    o = _attn_out(acc_sc, 0, LANES) - lam * _attn_out(acc_sc, 1, LANES)
    o = o * lax.rsqrt(jnp.mean(o * o, axis=0, keepdims=True) + NORM_EPS)
    o_ref[0] = o * g_ref[...] * (1.0 - lam_init)


DIFF_TILE = 512
DIFF_GROUP = 4


def diff_attention(qt, k, vta, lq1, lk1, lq2, lk2, subln, lam_init, fast_ok):
    bsz, width, seq = qt.shape
    n_h = width // LANES
    t = _pick_tile(seq, (DIFF_TILE, 256, 128))
    rows = LANES + ONES_ROWS
    vec = lambda a: a.astype(F32).reshape(1, -1)
    small = pl.BlockSpec((1, HEAD_DIM), lambda b, h, i: (0, 0))

    def call(fast, *args):
        return pl.pallas_call(
            functools.partial(_diff_attn_kernel, t=t, lam_init=lam_init, fast=fast),
            grid=(bsz, n_h, seq // t),
            in_specs=[pl.BlockSpec((1, LANES, t), lambda b, h, i: (b, h, i)),
                      pl.BlockSpec((1, seq, LANES), lambda b, h, i: (b, 0, h)),
                      pl.BlockSpec((1, rows, seq), lambda b, h, i: (b, h, 0)),
                      small, small, small, small,
                      pl.BlockSpec((LANES, 1), lambda b, h, i: (0, 0))],
            out_specs=pl.BlockSpec((1, LANES, t), lambda b, h, i: (b, h, i)),
            out_shape=jax.ShapeDtypeStruct((bsz, width, seq), F32),
            scratch_shapes=[pltpu.VMEM((2, 1, t), F32), pltpu.VMEM((2, rows, t), F32)],
            compiler_params=_cparams(("parallel", "parallel", "arbitrary")),
            name="diff_attn_fast" if fast else "diff_attn_safe",
        )(*args)

    args = (qt, k, vta, vec(lq1), vec(lk1), vec(lq2), vec(lk2), subln.astype(F32).reshape(-1, 1))
    return lax.cond(fast_ok, functools.partial(call, True), functools.partial(call, False), *args)


def _sortable_key(score):
    bits = pltpu.bitcast(score + 0.0, I32)
    return bits ^ ((bits >> 31) & 0x7FFFFFFF)


def _dsa_kernel(qt_ref, k_ref, vt_ref, qit_ref, ki_ref, wt_ref, o_ref, kb_sc, m_sc, acc_sc,
                *, t, nk, n_sel, fast):
    i = pl.program_id(1)
    pair = pl.program_id(2)
    n_chunks = i + 1
    krow = lax.broadcasted_iota(I32, (t, t), 0)

    def count_where(pred_fn):
        def chunk_count(cidx):
            g = jnp.where(pred_fn(kb_sc[cidx], cidx), 1, 0)
            return jnp.sum(g.reshape(t // 8, 8, t), axis=0)

        def body(jp, acc):
            return acc + chunk_count(2 * jp) + chunk_count(2 * jp + 1)

        acc = lax.fori_loop(0, n_chunks // 2, body, jnp.zeros((8, t), I32))
        last = chunk_count(n_chunks - 1)
        acc = acc + jnp.where(n_chunks % 2 == 1, last, 0)
        return jnp.sum(acc, axis=0, keepdims=True)

    @pl.when(pair == 0)
    def _():
        qit = qit_ref[0]
        pad = jnp.zeros((LANES - IDX_DIM, t), BF16)
        q_heads = [jnp.concatenate([qit[h * IDX_DIM:(h + 1) * IDX_DIM], pad], axis=0)
                   for h in range(IDX_HEADS)]
        wt = wt_ref[0]
        w_rows = [wt[h:h + 1] for h in range(IDX_HEADS)]
        qcol = i * t + lax.broadcasted_iota(I32, (1, t), 1)

        def score_chunk(cidx, mm):
            off = pl.multiple_of(cidx * t, t)
            ki = ki_ref[0, pl.ds(off, t), :]
            score = jnp.zeros((t, t), F32)
            for h in range(IDX_HEADS):
                lg = jnp.dot(ki, q_heads[h], preferred_element_type=F32)
                score = score + w_rows[h] * jnp.maximum(lg, 0.0)
            key = _sortable_key(score)
            visible = krow + off <= qcol
            kb_sc[cidx] = jnp.where(visible, key, I32_MIN)
            lo_c = jnp.min(jnp.where(visible, key, 2147483647).reshape(t // 8, 8, t), axis=0)
            hi_c = jnp.max(jnp.where(visible, key, I32_MIN).reshape(t // 8, 8, t), axis=0)
            return jnp.minimum(mm[0], lo_c), jnp.maximum(mm[1], hi_c)

        kmin, kmax = lax.fori_loop(0, n_chunks, score_chunk, (jnp.full((8, t), 2147483647, I32),
                                                              jnp.full((8, t), I32_MIN, I32)))
        n_vis = qcol + 1
        few = n_vis <= n_sel
        kmin = jnp.min(kmin, axis=0, keepdims=True)
        kmax = jnp.max(kmax, axis=0, keepdims=True)
        span = 1 << 25
        probe = jnp.maximum(jnp.where(kmax > I32_MIN + span, kmax - span, kmin), kmin)
        c_probe = count_where(lambda kk, cidx: kk >= probe)
        probe_ok = c_probe >= n_sel
        lo0 = jnp.where(few, I32_MIN, jnp.where(probe_ok, probe, kmin))
        hi0 = jnp.where(few, I32_MIN + 1, jnp.where(probe_ok, kmax + 1, probe))

        def open_lanes(lo, hi, c_lo):
            return jnp.logical_and(c_lo != n_sel, hi > lo + 1)

        def bisect_cond(st):
            it, lo, hi, c_lo, c_hi = st
            return jnp.logical_and(it < 34, jnp.max(jnp.where(open_lanes(lo, hi, c_lo), 1, 0)) > 0)

        def bisect(st):
            it, lo, hi, c_lo, c_hi = st
            mid = (lo >> 1) + (hi >> 1) + (lo & hi & 1)
            cnt = count_where(lambda kk, cidx: kk >= mid)
            up = jnp.logical_and(open_lanes(lo, hi, c_lo), cnt >= n_sel)
            dn = jnp.logical_and(open_lanes(lo, hi, c_lo), cnt < n_sel)
            return (it + 1, jnp.where(up, mid, lo), jnp.where(dn, mid, hi),
                    jnp.where(up, cnt, c_lo), jnp.where(dn, cnt, c_hi))

        c_lo0 = jnp.where(few, n_sel, jnp.where(probe_ok, c_probe, n_vis))
        c_hi0 = jnp.where(jnp.logical_or(few, probe_ok), 0, c_probe)
        _, thr, _, c_lo, c_hi = lax.while_loop(
            bisect_cond, bisect, (jnp.int32(0), lo0, hi0, c_lo0, c_hi0))

        need = n_sel - c_hi
        all_ties = c_lo == n_sel
        jmax0 = jnp.where(few, -1, jnp.where(all_ties, 2147483647, -1))
        unresolved = jnp.logical_and(jnp.logical_not(few), jnp.logical_not(all_ties))

        def tie_search(_):
            def step(it, lohi):
                lo, hi = lohi
                mid = (lo + hi) >> 1
                cnt = count_where(lambda kk, cidx: jnp.logical_and(kk == thr, krow + cidx * t <= mid))
                ok = cnt >= need
                return jnp.where(ok, lo, mid), jnp.where(ok, mid, hi)
            n_steps = 1 + max(1, (nk * t - 1).bit_length())
            _, hi = lax.fori_loop(0, n_steps, step, (jnp.full((1, t), -1, I32),
                                                     jnp.full((1, t), nk * t - 1, I32)))
            return jnp.where(unresolved, hi, jmax0)

        any_unresolved = jnp.max(jnp.where(unresolved, 1, 0)) > 0
        jmax = lax.cond(any_unresolved, tie_search, lambda _: jmax0, 0)

        def bias_chunk(cidx, carry):
            kk = kb_sc[cidx]
            cut = jnp.where(krow + cidx * t <= jmax, thr - 1, thr)
            bias = jnp.where(kk > cut, 0.0, NEG)
            kb_sc[cidx] = pltpu.bitcast(bias.astype(F32), I32)
            return carry

        lax.fori_loop(0, n_chunks, bias_chunk, 0)

    _attn_init(m_sc, acc_sc)
    qt = qt_ref[0]
    qh = [_half_rows(qt, sub) for sub in range(2)]
    hrows = HEAD_DIM + ONES_ROWS

    def attend(jt, n):
        off = pl.multiple_of(jt * t, t)
        k = k_ref[0, pl.ds(off, n * t), :]
        vt = vt_ref[0, :, pl.ds(off, n * t)]
        bias = jnp.concatenate([pltpu.bitcast(kb_sc[jt + c], F32) for c in range(n)], axis=0)
        s = [jnp.dot(k, qh[sub], preferred_element_type=F32) + bias for sub in range(2)]
        for sub in range(2):
            _attn_update(s[sub], vt[sub * hrows:(sub + 1) * hrows], m_sc, acc_sc, sub, fast)

    _for_tile_groups(n_chunks, attend)
    o_ref[0] = jnp.concatenate([_attn_out(acc_sc, sub, HEAD_DIM) for sub in range(2)], axis=0)


def dsa_attention(qt, k, vta, qit, ki, iwt, fast_ok):
    bsz, width, seq = qt.shape
    t = _pick_tile(seq, (ATT_TILE, 128))
    nq = seq // t
    n_sel = min(DSA_TOPK, seq // 4)
    prow = 2 * (HEAD_DIM + ONES_ROWS)

    def call(fast, *args):
        return pl.pallas_call(
            functools.partial(_dsa_kernel, t=t, nk=nq, n_sel=n_sel, fast=fast),
            grid=(bsz, nq, width // LANES),
            in_specs=[pl.BlockSpec((1, LANES, t), lambda b, i, p: (b, p, i)),
                      pl.BlockSpec((1, seq, LANES), lambda b, i, p: (b, 0, p)),
                      pl.BlockSpec((1, prow, seq), lambda b, i, p: (b, p, 0)),
                      pl.BlockSpec((1, IDX_HEADS * IDX_DIM, t), lambda b, i, p: (b, 0, i)),
                      pl.BlockSpec((1, seq, LANES), lambda b, i, p: (b, 0, 0)),
                      pl.BlockSpec((1, IDX_HEADS, t), lambda b, i, p: (b, 0, i))],
            out_specs=pl.BlockSpec((1, LANES, t), lambda b, i, p: (b, p, i)),
            out_shape=jax.ShapeDtypeStruct((bsz, width, seq), F32),
            scratch_shapes=[pltpu.VMEM((nq, t, t), I32), pltpu.VMEM((2, 1, t), F32),
                            pltpu.VMEM((2, HEAD_DIM + ONES_ROWS, t), F32)],
            compiler_params=_cparams(("parallel", "parallel", "arbitrary"), vmem_mb=56),
            name="dsa_attn_fast" if fast else "dsa_attn_safe",
        )(*args)

    args = (qt, k, vta, qit, ki, iwt)
    return lax.cond(fast_ok, functools.partial(call, True), functools.partial(call, False), *args)


def _kmean_kernel(k_ref, o_ref, *, nbb):
    x = k_ref[0]
    o_ref[0] = jnp.mean(x.reshape(nbb, MOBA_BLOCK, x.shape[-1]), axis=1)


def block_means(k_f32):
    bsz, seq, width = k_f32.shape
    nb = seq // MOBA_BLOCK
    nbb = 8 if nb % 8 == 0 else nb
    return pl.pallas_call(
        functools.partial(_kmean_kernel, nbb=nbb),
        grid=(bsz, nb // nbb),
        in_specs=[pl.BlockSpec((1, nbb * MOBA_BLOCK, width), lambda b, i: (b, i, 0))],
        out_specs=pl.BlockSpec((1, nbb, width), lambda b, i: (b, i, 0)),
        out_shape=jax.ShapeDtypeStruct((bsz, nb, width), F32),
        compiler_params=_cparams(("parallel", "parallel")),
        name="moba_kmean",
    )(k_f32)


def _moba_select_kernel(qt_ref, km_ref, o_ref, *, ts, n_sel, nbp):
    j = pl.program_id(2)
    blk = lax.broadcasted_iota(I32, (nbp, ts), 0)
    own = jnp.right_shift(j * ts + lax.broadcasted_iota(I32, (1, ts), 1), MOBA_BLOCK.bit_length() - 1)
    qt = qt_ref[0]
    for sub in range(2):
        gate = _dot3(km_ref[0], _half_rows(qt, sub))
        g = jnp.where(blk < own, gate, -jnp.inf)
        sel = jnp.zeros((nbp, ts), F32)
        for _ in range(n_sel):
            mx = jnp.max(g, axis=0, keepdims=True)
            is_mx = jnp.logical_and(g == mx, mx > -jnp.inf)
            first = jnp.min(jnp.where(is_mx, blk, 2 * nbp), axis=0, keepdims=True)
            pick = blk == first
            sel = jnp.where(pick, 1.0, sel)
            g = jnp.where(pick, -jnp.inf, g)
        o_ref[0, sub] = sel


def _moba_kernel(qt_ref, sel_ref, k_ref, vt_ref, o_ref, m_sc, acc_sc, *, t, fast):
    i = pl.program_id(2)
    _attn_init(m_sc, acc_sc)
    qt = qt_ref[0]
    qb = [_half_rows(qt, sub).astype(BF16) for sub in range(2)]
    hrows = HEAD_DIM + ONES_ROWS

    def attend(kb, n, mask_fn):
        off = pl.multiple_of(kb * t, t)
        k = k_ref[0, pl.ds(off, n * t), :]
        vt = vt_ref[0, :, pl.ds(off, n * t)]
        s = [mask_fn(jnp.dot(k, qb[sub], preferred_element_type=F32), sub) for sub in range(2)]
        for sub in range(2):
            _attn_update(s[sub], vt[sub * hrows:(sub + 1) * hrows], m_sc, acc_sc, sub, fast)

    causal = _causal_t(t)
    attend(i, 1, lambda s_t, sub: jnp.where(causal, s_t, NEG))

    def past_blocks(kb, n):
        def row_mask(s_t, sub):
            rows = [jnp.where(sel_ref[0, sub, pl.ds(kb + c, 1), :] > 0.0, 0.0, NEG) for c in range(n)]
            bias = jnp.concatenate([jnp.broadcast_to(r, (t, t)) for r in rows], axis=0)
            return s_t + bias
        attend(kb, n, row_mask)

    _for_tile_groups(i, past_blocks, MOBA_GROUP)
    o_ref[0] = jnp.concatenate([_attn_out(acc_sc, sub, HEAD_DIM) for sub in range(2)], axis=0)


def moba_attention(qt_f32, kmean, k, vta, fast_ok):
    bsz, width, seq = qt_f32.shape
    t = MOBA_BLOCK
    nb = seq // MOBA_BLOCK
    n_sel = max(1, min(MOBA_TOPK, nb - 1))
    nbp = -(-nb // 8) * 8
    km = jnp.zeros((bsz, nbp, width), F32).at[:, :nb].set(kmean)
    prow = 2 * (HEAD_DIM + ONES_ROWS)
    n_pairs = width // LANES
    ts = _pick_tile(seq, (1024, 512, 256))
    sel = pl.pallas_call(
        functools.partial(_moba_select_kernel, ts=ts, n_sel=n_sel, nbp=nbp),
        grid=(bsz, n_pairs, seq // ts),
        in_specs=[pl.BlockSpec((1, LANES, ts), lambda b, h, j: (b, h, j)),
                  pl.BlockSpec((1, nbp, LANES), lambda b, h, j: (b, 0, h))],
        out_specs=pl.BlockSpec((1, 2, nbp, ts), lambda b, h, j: (b, h, 0, j)),
        out_shape=jax.ShapeDtypeStruct((bsz, 2 * n_pairs, nbp, seq), F32),
        compiler_params=_cparams(("parallel", "parallel", "parallel")),
        name="moba_select",
    )(qt_f32, km)

    def call(fast, *args):
        return pl.pallas_call(
            functools.partial(_moba_kernel, t=t, fast=fast),
            grid=(bsz, n_pairs, nb),
            in_specs=[pl.BlockSpec((1, LANES, t), lambda b, h, i: (b, h, i)),
                      pl.BlockSpec((1, 2, nbp, t), lambda b, h, i: (b, h, 0, i)),
                      pl.BlockSpec((1, seq, LANES), lambda b, h, i: (b, 0, h)),
                      pl.BlockSpec((1, prow, seq), lambda b, h, i: (b, h, 0))],
            out_specs=pl.BlockSpec((1, LANES, t), lambda b, h, i: (b, h, i)),
            out_shape=jax.ShapeDtypeStruct((bsz, width, seq), F32),
            scratch_shapes=[pltpu.VMEM((2, 1, t), F32), pltpu.VMEM((2, HEAD_DIM + ONES_ROWS, t), F32)],
            compiler_params=_cparams(("parallel", "parallel", "arbitrary")),
            name="moba_attn_fast" if fast else "moba_attn_safe",
        )(*args)

    args = (qt_f32, sel, k, vta)
    return lax.cond(fast_ok, functools.partial(call, True), functools.partial(call, False), *args)


def s5_operators(lam_re, lam_im, b_re, b_im, c_re, c_im, d_skip, log_dt, chunk):
    n_g, n_p = lam_re.shape
    n_c = d_skip.shape[-1]
    dt = jnp.exp(log_dt.astype(F32))[:, None]
    den = lam_re * lam_re + lam_im * lam_im
    mag = jnp.exp(lam_re * dt)
    abar_re = mag * jnp.cos(lam_im * dt)
    abar_im = mag * jnp.sin(lam_im * dt)
    coef_re = ((abar_re - 1.0) * lam_re + abar_im * lam_im) / den
    coef_im = (abar_im * lam_re - (abar_re - 1.0) * lam_im) / den
    bbar_re = coef_re[..., None] * b_re - coef_im[..., None] * b_im
    bbar_im = coef_re[..., None] * b_im + coef_im[..., None] * b_re
    tau = jnp.arange(chunk + 1, dtype=F32)[:, None, None]
    pmag = jnp.exp(lam_re * dt * tau)
    pw_re = pmag * jnp.cos(lam_im * dt * tau)
    pw_im = pmag * jnp.sin(lam_im * dt * tau)
    ab_re = pw_re[..., None] * bbar_re - pw_im[..., None] * bbar_im
    ab_im = pw_re[..., None] * bbar_im + pw_im[..., None] * bbar_re
    kern = (jnp.einsum('gop,tgpi->tgio', c_re, ab_re, precision=HI)
            - jnp.einsum('gop,tgpi->tgio', c_im, ab_im, precision=HI))
    t_idx = jnp.arange(chunk)
    lag = t_idx[None, :] - t_idx[:, None]
    toep = jnp.where((lag >= 0)[None, :, None, :, None],
                     kern[jnp.clip(lag, 0, chunk)].transpose(2, 0, 3, 1, 4), 0.0)
    eye = (jnp.eye(chunk)[None, :, None, :, None] * jnp.eye(n_c)[None, None, :, None, :])
    toep = toep + eye * d_skip[:, None, :, None, None]
    toep = toep.reshape(n_g, chunk * n_c, chunk * n_c)
    rev = chunk - 1 - t_idx
    w_in = jnp.concatenate([ab_re[rev].transpose(1, 0, 3, 2), ab_im[rev].transpose(1, 0, 3, 2)], axis=-1)
    w_in = w_in.reshape(n_g, chunk * n_c, 2 * n_p)
    a_re, a_im = pw_re[1:chunk + 1], pw_im[1:chunk + 1]
    st_re = (c_re[None] * a_re[:, :, None, :] - c_im[None] * a_im[:, :, None, :])
    st_im = (-c_re[None] * a_im[:, :, None, :] - c_im[None] * a_re[:, :, None, :])
    w_st = jnp.concatenate([st_re.transpose(1, 3, 0, 2), st_im.transpose(1, 3, 0, 2)], axis=1)
    w_st = w_st.reshape(n_g, 2 * n_p, chunk * n_c)
    return toep, w_in, w_st, pw_re[chunk], pw_im[chunk]


def _s5_in_kernel(u_ref, w_ref, o_ref):
    o_ref[0, 0] = _dot3(u_ref[0, 0], w_ref[0])


def _s5_scan_kernel(x_ref, aa_ref, ab_ref, abs_ref, o_ref, h_sc, hs_sc, *, steps, n_p):
    @pl.when(pl.program_id(0) == 0)
    def _():
        h_sc[...] = jnp.zeros(h_sc.shape, F32)
        hs_sc[...] = jnp.zeros(hs_sc.shape, F32)

    aa, ab, abs_ = aa_ref[...], ab_ref[...], abs_ref[...]

    def body(t, carry):
        h, hs = carry
        o_ref[t] = h
        x = x_ref[t]
        xs = pltpu.roll(x, n_p, 1)
        return aa * h + ab * hs + x, aa * hs + abs_ * h + xs

    h, hs = lax.fori_loop(0, steps, body, (h_sc[...], hs_sc[...]))
    h_sc[...] = h
    hs_sc[...] = hs


def _s5_out_kernel(u_ref, h0_ref, t_ref, w_ref, o_ref):
    y = _dot3(u_ref[0, 0], t_ref[0]) + _dot3(h0_ref[0, 0], w_ref[0])
    o_ref[0, 0] = jax.nn.gelu(y)


def s5_layer(u, ops):
    toep, w_in, w_st, ac_re, ac_im = ops
    bsz, seq, width = u.shape
    n_g, n_p = ac_re.shape
    n_c = width // n_g
    chunk = S5_CHUNK
    nj = seq // chunk
    row = chunk * n_c
    ur = u.reshape(bsz, nj, chunk, n_g, n_c).transpose(0, 3, 1, 2, 4).reshape(bsz, n_g, nj, row)
    hloc = pl.pallas_call(
        _s5_in_kernel,
        grid=(bsz, n_g),
        in_specs=[pl.BlockSpec((1, 1, nj, row), lambda b, g: (b, g, 0, 0)),
                  pl.BlockSpec((1, row, 2 * n_p), lambda b, g: (g, 0, 0))],
        out_specs=pl.BlockSpec((1, 1, nj, 2 * n_p), lambda b, g: (b, g, 0, 0)),
        out_shape=jax.ShapeDtypeStruct((bsz, n_g, nj, 2 * n_p), F32),
        compiler_params=_cparams(("parallel", "parallel")),
        name="s5_in",
    )(ur, w_in)
    bg = bsz * n_g
    xs = hloc.transpose(2, 0, 1, 3).reshape(nj, bg, 2 * n_p)
    rep = lambda a: jnp.tile(a, (bsz, 1))
    aa = rep(jnp.concatenate([ac_re, ac_re], axis=-1))
    ab = rep(jnp.concatenate([-ac_im, ac_im], axis=-1))
    abs_ = rep(jnp.concatenate([ac_im, -ac_im], axis=-1))
    steps = _pick_tile(nj, (128, 64, 32, 16, 8))
    cst = pl.BlockSpec((bg, 2 * n_p), lambda t: (0, 0))
    h0 = pl.pallas_call(
        functools.partial(_s5_scan_kernel, steps=steps, n_p=n_p),
        grid=(nj // steps,),
        in_specs=[pl.BlockSpec((steps, bg, 2 * n_p), lambda t: (t, 0, 0)), cst, cst, cst],
        out_specs=pl.BlockSpec((steps, bg, 2 * n_p), lambda t: (t, 0, 0)),
        out_shape=jax.ShapeDtypeStruct((nj, bg, 2 * n_p), F32),
        scratch_shapes=[pltpu.VMEM((bg, 2 * n_p), F32), pltpu.VMEM((bg, 2 * n_p), F32)],
        compiler_params=_cparams(("arbitrary",)),
        name="s5_scan",
    )(xs, aa, ab, abs_)
    h0 = h0.reshape(nj, bsz, n_g, 2 * n_p).transpose(1, 2, 0, 3)
    y = pl.pallas_call(
        _s5_out_kernel,
        grid=(bsz, n_g),
        in_specs=[pl.BlockSpec((1, 1, nj, row), lambda b, g: (b, g, 0, 0)),
                  pl.BlockSpec((1, 1, nj, 2 * n_p), lambda b, g: (b, g, 0, 0)),
                  pl.BlockSpec((1, row, row), lambda b, g: (g, 0, 0)),
                  pl.BlockSpec((1, 2 * n_p, row), lambda b, g: (g, 0, 0))],
        out_specs=pl.BlockSpec((1, 1, nj, row), lambda b, g: (b, g, 0, 0)),
        out_shape=jax.ShapeDtypeStruct((bsz, n_g, nj, row), F32),
        compiler_params=_cparams(("parallel", "parallel")),
        name="s5_out",
    )(ur, h0, toep, w_st)
    return y.reshape(bsz, n_g, nj, chunk, n_c).transpose(0, 2, 3, 1, 4).reshape(bsz, seq, width)


def _glu_kernel(y_ref, w_ref, b_ref, o_ref):
    y = y_ref[...]
    gate = jnp.dot(y.astype(BF16), w_ref[...], preferred_element_type=F32) + b_ref[...]
    o_ref[...] = (y * jax.nn.sigmoid(gate)).astype(o_ref.dtype)


def glu(y, w_bf16, b, tm=1024):
    n_tok, width = y.shape
    tm = _pick_tile(n_tok, (tm, 512, 256, 128))
    return pl.pallas_call(
        _glu_kernel,
        grid=(n_tok // tm,),
        in_specs=[pl.BlockSpec((tm, width), lambda i: (i, 0)),
                  pl.BlockSpec((width, width), lambda i: (0, 0)),
                  pl.BlockSpec((1, width), lambda i: (0, 0))],
        out_specs=pl.BlockSpec((tm, width), lambda i: (i, 0)),
        out_shape=jax.ShapeDtypeStruct((n_tok, width), F32),
        compiler_params=_cparams(("parallel",)),
        name="s5_glu",
    )(y, w_bf16, b.astype(F32).reshape(1, width))


def _outproj_kernel(a_ref, b_ref, w_ref, x_ref, g_ref, o_ref, *, half, a_t, b_t):
    a = a_ref[0].T if a_t else a_ref[0]
    b = b_ref[0].T if b_t else b_ref[0]
    y = (jnp.dot(a.astype(BF16), w_ref[:half, :], preferred_element_type=F32)
         + jnp.dot(b.astype(BF16), w_ref[half:, :], preferred_element_type=F32))
    o_ref[0] = x_ref[0] + g_ref[0] * y


def out_proj(a, b, w_bf16, x, gate, a_t=False, b_t=False, tm=1024):
    bsz, seq, d = x.shape
    half = w_bf16.shape[0] // 2
    tm = _pick_tile(seq, (tm, 256, 128))

    def spec(transposed):
        if transposed:
            return pl.BlockSpec((1, half, tm), lambda bb, i: (bb, 0, i))
        return pl.BlockSpec((1, tm, half), lambda bb, i: (bb, i, 0))

    return pl.pallas_call(
        functools.partial(_outproj_kernel, half=half, a_t=a_t, b_t=b_t),
        grid=(bsz, seq // tm),
        in_specs=[spec(a_t), spec(b_t),
                  pl.BlockSpec((2 * half, d), lambda bb, i: (0, 0)),
                  pl.BlockSpec((1, tm, d), lambda bb, i: (bb, i, 0)),
                  pl.BlockSpec((1, 1, d), lambda bb, i: (bb, 0, 0))],
        out_specs=pl.BlockSpec((1, tm, d), lambda bb, i: (bb, i, 0)),
        out_shape=jax.ShapeDtypeStruct((bsz, seq, d), F32),
        compiler_params=_cparams(("parallel", "parallel")),
        name="out_proj",
    )(a, b, w_bf16, x, gate)


def _router_kernel(x_ref, sc_ref, sh_ref, wr_ref, br_ref, h_ref, r_ref):
    h = _norm_mod(x_ref[0], sc_ref[0], sh_ref[0])
    h_ref[0] = h.astype(BF16)
    logits = _dot3(h, wr_ref[...]) + br_ref[...]
    tm = logits.shape[0]
    lane = lax.broadcasted_iota(I32, (tm, LANES), 1)
    big = 4 * LANES
    coarse = jnp.where(lane < N_GROUPS, logits, -jnp.inf)
    cmax = jnp.max(coarse, axis=1, keepdims=True)
    grp = jnp.min(jnp.where(coarse == cmax, lane, big), axis=1, keepdims=True)
    p_grp = 1.0 / jnp.sum(jnp.exp(coarse - cmax), axis=1, keepdims=True)
    lo = N_GROUPS + grp * EXPERTS_PER_GROUP
    fine = jnp.where(jnp.logical_and(lane >= lo, lane < lo + EXPERTS_PER_GROUP), logits, -jnp.inf)
    v1 = jnp.max(fine, axis=1, keepdims=True)
    i1 = jnp.min(jnp.where(fine == v1, lane, big), axis=1, keepdims=True)
    fine2 = jnp.where(lane == i1, -jnp.inf, fine)
    v2 = jnp.max(fine2, axis=1, keepdims=True)
    i2 = jnp.min(jnp.where(fine2 == v2, lane, big), axis=1, keepdims=True)
    e2 = jnp.exp(v2 - v1)
    wa = p_grp / (1.0 + e2)
    wb = p_grp * e2 / (1.0 + e2)
    ida = (i1 - N_GROUPS).astype(F32)
    idb = (i2 - N_GROUPS).astype(F32)
    r_ref[0] = jnp.where(lane == 0, ida, jnp.where(lane == 1, idb,
                         jnp.where(lane == 2, wa, jnp.where(lane == 3, wb, 0.0))))


def moe_router(x, sc, sh, wg, bg, we, be, tm=1024):
    bsz, seq, d = x.shape
    tm = _pick_tile(seq, (tm, 256, 128))
    n_r = N_GROUPS + N_EXPERTS
    wr = jnp.zeros((d, LANES), F32).at[:, :N_GROUPS].set(wg).at[:, N_GROUPS:n_r].set(we)
    br = jnp.zeros((1, LANES), F32).at[0, :N_GROUPS].set(bg).at[0, N_GROUPS:n_r].set(be)
    return pl.pallas_call(
        _router_kernel,
        grid=(bsz, seq // tm),
        in_specs=[pl.BlockSpec((1, tm, d), lambda b, i: (b, i, 0)),
                  pl.BlockSpec((1, 1, d), lambda b, i: (b, 0, 0)),
                  pl.BlockSpec((1, 1, d), lambda b, i: (b, 0, 0)),
                  pl.BlockSpec((d, LANES), lambda b, i: (0, 0)),
                  pl.BlockSpec((1, LANES), lambda b, i: (0, 0))],
        out_specs=[pl.BlockSpec((1, tm, d), lambda b, i: (b, i, 0)),
                   pl.BlockSpec((1, tm, LANES), lambda b, i: (b, i, 0))],
        out_shape=[jax.ShapeDtypeStruct((bsz, seq, d), BF16),
                   jax.ShapeDtypeStruct((bsz, seq, LANES), F32)],
        compiler_params=_cparams(("parallel", "parallel")),
        name="moe_router",
    )(x, sc, sh, wr, br)


MOE_SEG = 256


MOE_PAD = 16
MOE_EXPERTS_PER_STEP = 1


def _pad_rows(count):
    return jnp.floor((count + (MOE_PAD - 1)) * (1.0 / MOE_PAD)) * MOE_PAD


def _moe_kernel(erow_ref, route_ref, h_ref, x_ref, g_ref, w1_ref, w3_ref, w2_ref, o_ref,
                xs_sc, ys_sc, meta_sc, drow_sc, dcol_sc, *, chunk, rows, na):
    step = pl.program_id(1)
    n_seg = chunk // MOE_SEG
    lane1 = lax.broadcasted_iota(I32, (1, LANES), 1)

    def lane_scalar(row, k):
        return jnp.sum(jnp.where(lane1 == k, row, 0.0)).astype(I32)

    @pl.when(step == 0)
    def _():
        si = lax.broadcasted_iota(I32, (MOE_SEG, MOE_SEG), 0)
        sj = lax.broadcasted_iota(I32, (MOE_SEG, MOE_SEG), 1)
        upper = jnp.where(si < sj, 1.0, 0.0).astype(BF16)
        lower = jnp.where(sj < si, 1.0, 0.0).astype(BF16)
        erow = erow_ref[0]
        eid = lax.broadcasted_iota(I32, (N_EXPERTS, chunk), 0)
        sel = [erow[k:k + 1] == eid for k in range(2)]
        member = jnp.where(jnp.logical_or(sel[0], sel[1]), 1.0, 0.0)
        count = jnp.zeros((N_EXPERTS, 1), F32)
        parts = []
        for sgi in range(n_seg):
            seg = member[:, sgi * MOE_SEG:(sgi + 1) * MOE_SEG]
            parts.append(jnp.dot(seg.astype(BF16), upper, preferred_element_type=F32) + count)
            count = count + jnp.sum(seg, axis=1, keepdims=True)
        rank = jnp.concatenate(parts, axis=1)
        padded = _pad_rows(count)
        ei = lax.broadcasted_iota(I32, (N_EXPERTS, N_EXPERTS), 0)
        ej = lax.broadcasted_iota(I32, (N_EXPERTS, N_EXPERTS), 1)
        below = jnp.where(ej < ei, 1.0, 0.0).astype(BF16)
        start = jnp.dot(below, jnp.broadcast_to(padded, (N_EXPERTS, LANES)).astype(BF16),
                        preferred_element_type=F32)
        total = jnp.sum(padded, axis=0, keepdims=True)
        lane_e = lax.broadcasted_iota(I32, (N_EXPERTS, LANES), 1)
        meta_sc[...] = jnp.where(lane_e == 0, start, jnp.where(lane_e == 1, count, jnp.where(lane_e == 2, total, 0.0)))
        dest = start[:, 0:1] + rank
        for k in range(2):
            drow_sc[k] = jnp.sum(jnp.where(sel[k], dest, 0.0), axis=0, keepdims=True)
        route = route_ref[...]
        lane_c = lax.broadcasted_iota(I32, (chunk, LANES), 1)
        lane_f = lane_c.astype(F32)
        sel_c = [route[:, k:k + 1] == lane_f for k in range(2)]
        member_c = jnp.where(jnp.logical_or(sel_c[0], sel_c[1]), 1.0, 0.0)
        count_c = jnp.zeros((1, LANES), F32)
        parts_c = []
        for sgi in range(n_seg):
            seg = member_c[sgi * MOE_SEG:(sgi + 1) * MOE_SEG]
            parts_c.append(jnp.dot(lower, seg.astype(BF16), preferred_element_type=F32) + count_c)
            count_c = count_c + jnp.sum(seg, axis=0, keepdims=True)
        rank_c = jnp.concatenate(parts_c, axis=0)
        li = lax.broadcasted_iota(I32, (LANES, LANES), 0)
        lj = lax.broadcasted_iota(I32, (LANES, LANES), 1)
        before = jnp.where(li < lj, 1.0, 0.0).astype(BF16)
        start_c = jnp.dot(jnp.broadcast_to(_pad_rows(count_c), (8, LANES)).astype(BF16), before,
                          preferred_element_type=F32)[0:1]
        dest_c = start_c + rank_c
        dcols = [jnp.sum(jnp.where(sel_c[k], dest_c, 0.0), axis=1, keepdims=True) for k in range(2)]
        dcol_sc[...] = jnp.where(lane_c == 0, dcols[0], jnp.where(lane_c == 1, dcols[1], 0.0))
        ys_sc[...] = jnp.zeros(ys_sc.shape, BF16)
        xs_sc[...] = jnp.zeros(xs_sc.shape, BF16)
        d0, d1 = drow_sc[0], drow_sc[1]
        n_slabs = (jnp.sum(total).astype(I32) + MOE_SEG - 1) // MOE_SEG

        def sort_slab(s, carry):
            off = pl.multiple_of(s * MOE_SEG, MOE_SEG)
            r = (lax.broadcasted_iota(I32, (MOE_SEG, 1), 0) + off).astype(F32)
            onehot = jnp.where(jnp.logical_or(d0 == r, d1 == r), 1.0, 0.0).astype(BF16)
            xs_sc[pl.ds(off, MOE_SEG), :] = jnp.dot(onehot, h_ref[...], preferred_element_type=F32).astype(BF16)
            return carry

        lax.fori_loop(0, n_slabs, sort_slab, 0)

    for slot in range(MOE_EXPERTS_PER_STEP):
        meta = meta_sc[pl.ds(step * MOE_EXPERTS_PER_STEP + slot, 1), :]
        first = lane_scalar(meta, 0)
        count_e = lane_scalar(meta, 1)
        padded_e = ((count_e + MOE_PAD - 1) // MOE_PAD) * MOE_PAD

        def block(bi, carry, slot=slot, first=first, padded_e=padded_e):
            r0 = pl.multiple_of(first + bi * rows, MOE_PAD)
            xb = xs_sc[pl.ds(r0, rows), :]
            a = jnp.dot(xb, w1_ref[slot], preferred_element_type=F32)
            hid = (a * jax.nn.sigmoid(a)) * jnp.dot(xb, w3_ref[slot], preferred_element_type=F32)
            yb = jnp.dot(hid.astype(BF16), w2_ref[slot], preferred_element_type=F32)
            mine = lax.broadcasted_iota(I32, (rows, 1), 0) + bi * rows < padded_e
            ys_sc[pl.ds(r0, rows), :] = jnp.where(mine, yb, ys_sc[pl.ds(r0, rows), :].astype(F32)).astype(BF16)
            return carry

        lax.fori_loop(0, (count_e + rows - 1) // rows, block, 0)

    @pl.when(step == N_EXPERTS // MOE_EXPERTS_PER_STEP - 1)
    def _():
        route = route_ref[...]
        dcol = dcol_sc[...]
        d0, d1 = dcol[:, 0:1], dcol[:, 1:2]
        w0, w1 = route[:, 2:3], route[:, 3:4]
        r = lax.broadcasted_iota(I32, (1, na), 1).astype(F32)
        comb = (jnp.where(d0 == r, w0, 0.0) + jnp.where(d1 == r, w1, 0.0)).astype(BF16)
        moe = jnp.dot(comb, ys_sc[:na, :], preferred_element_type=F32)
        o_ref[...] = x_ref[...] + g_ref[0] * moe


def moe_experts(h, route, x, gate, w1_bf16, w3_bf16, w2_bf16, chunk=1024, rows=128):
    bsz, seq, d = x.shape
    chunk = _pick_tile(seq, (chunk, 512, 256))
    n_tok = bsz * seq
    n_chunks = n_tok // chunk
    per_b = seq // chunk
    ff = w1_bf16.shape[-1]
    na = -(-(2 * chunk + N_EXPERTS * (MOE_PAD - 1)) // MOE_SEG) * MOE_SEG
    route2 = route.reshape(n_tok, LANES)
    erow = route2[:, :2].astype(I32).T.reshape(2, n_chunks, chunk).transpose(1, 0, 2)
    eps = MOE_EXPERTS_PER_STEP
    out = pl.pallas_call(
        functools.partial(_moe_kernel, chunk=chunk, rows=rows, na=na),
        grid=(n_chunks, N_EXPERTS // eps),
        in_specs=[pl.BlockSpec((1, 2, chunk), lambda c, e: (c, 0, 0)),
                  pl.BlockSpec((chunk, LANES), lambda c, e: (c, 0)),
                  pl.BlockSpec((chunk, d), lambda c, e: (c, 0)),
                  pl.BlockSpec((chunk, d), lambda c, e: (c, 0)),
                  pl.BlockSpec((1, 1, d), lambda c, e: (c // per_b, 0, 0)),
                  pl.BlockSpec((eps, d, ff), lambda c, e: (e, 0, 0)),
                  pl.BlockSpec((eps, d, ff), lambda c, e: (e, 0, 0)),
                  pl.BlockSpec((eps, ff, d), lambda c, e: (e, 0, 0))],
        out_specs=pl.BlockSpec((chunk, d), lambda c, e: (c, 0)),
        out_shape=jax.ShapeDtypeStruct((n_tok, d), F32),
        scratch_shapes=[pltpu.VMEM((na + rows, d), BF16), pltpu.VMEM((na + rows, d), BF16),
                        pltpu.VMEM((N_EXPERTS, LANES), F32), pltpu.VMEM((2, 1, chunk), F32),
                        pltpu.VMEM((chunk, LANES), F32)],
        compiler_params=_cparams(("parallel", "arbitrary")),
        name="moe_experts",
    )(erow, route2, h.reshape(n_tok, d), x.reshape(n_tok, d), gate, w1_bf16, w3_bf16, w2_bf16)
    return out.reshape(bsz, seq, d)


def kernel(x, c, ada_w, ada_b, moe_wg, moe_bg, moe_we, moe_be, moe_w1, moe_w3, moe_w2,
           ev_w_in, ev_w_out, a_qn, a_kn, a_lq1, a_lk1, a_lq2, a_lk2, a_subln,
           b_qn, b_kn, b_idx_kn, od_w_in, od_w_out, c_qn, c_kn,
           s5_lam_re, s5_lam_im, s5_b_re, s5_b_im, s5_c_re, s5_c_im, s5_d, s5_log_dt,
           s5_glu_w, s5_glu_b):
    bsz, seq, d = x.shape
    depth = ada_w.shape[0]
    mod = ada_mod(c, ada_w, ada_b)
    rope_h = rope_lane_tables(seq, HEAD_DIM, ROT_HEAD)
    rope_i = rope_lane_tables(seq, IDX_DIM, ROT_IDX)
    qk_scale = HEAD_DIM ** -0.5 * math.log2(math.e)
    half_h, half_i = ROT_HEAD // 2, ROT_IDX // 2
    hw = A_HEADS * 2 * HEAD_DIM

    def pad_cols(w):
        n = w.shape[1]
        npad = -(-n // LANES) * LANES
        return jnp.zeros((w.shape[0], npad), F32).at[:, :n].set(w).astype(BF16)

    for l in range(depth):
        sh1, sc1, g1, sh2, sc2, g2 = [m[:, None, :] for m in jnp.split(mod[l], 6, axis=-1)]
        i = l // 2
        if l % 2 == 0:
            lam_init = 0.8 - 0.6 * math.exp(-0.3 * l)
            z = in_proj(x, sc1, sh1, pad_cols(ev_w_in[i]))
            q_t, v_a, v_b = ((BF16, "t"),), ((BF16, ("aug", 2 * HEAD_DIM)),), ((BF16, ("aug", HEAD_DIM)),)
            (aq,) = prep(z, 0, hw, hd=HEAD_DIM, gain=a_qn[i], rope=rope_h, half=half_h, scale=qk_scale, outs=q_t)
            (ak,) = prep(z, hw, hw, hd=HEAD_DIM, gain=a_kn[i], rope=rope_h, half=half_h)
            (av,) = prep(z, 2 * hw, hw, hd=HEAD_DIM, outs=v_a)
            o_a = diff_attention(aq, ak, av, a_lq1[i], a_lk1[i], a_lq2[i], a_lk2[i], a_subln[i], lam_init,
                                 _logit_bound(a_qn[i], a_kn[i]) <= MAX_FAST_LOGIT)
            (bq,) = prep(z, 3 * hw, hw, hd=HEAD_DIM, gain=b_qn[i], rope=rope_h, half=half_h, scale=qk_scale, outs=q_t)
            (bk,) = prep(z, 4 * hw, hw, hd=HEAD_DIM, gain=b_kn[i], rope=rope_h, half=half_h)
            (bv,) = prep(z, 5 * hw, hw, hd=HEAD_DIM, outs=v_b)
            iq_off = 6 * hw
            ik_off = iq_off + IDX_HEADS * IDX_DIM
            (iq,) = prep(z, iq_off, IDX_HEADS * IDX_DIM, hd=IDX_DIM, rope=rope_i, half=half_i, outs=q_t)
            ik, iw = prep(z, ik_off, LANES, hd=IDX_DIM, gain=b_idx_kn[i], rope=rope_i, half=half_i,
                          outs=((BF16, "n"), (F32, ("rawt", IDX_DIM, IDX_DIM + IDX_HEADS))))
            o_b = dsa_attention(bq, bk, bv, iq, ik, iw, _logit_bound(b_qn[i], b_kn[i]) <= MAX_FAST_LOGIT)
            x = out_proj(o_a, o_b, ev_w_out[i].astype(BF16), x, g1, a_t=True, b_t=True)
        else:
            z = in_proj(x, sc1, sh1, pad_cols(od_w_in[i]))
            (cq,) = prep(z, 0, hw, hd=HEAD_DIM, gain=c_qn[i], rope=rope_h, half=half_h, scale=qk_scale,
                         outs=((F32, "t"),))
            ck32, ck = prep(z, hw, hw, hd=HEAD_DIM, gain=c_kn[i], rope=rope_h, half=half_h,
                            outs=((F32, "n"), (BF16, "n")))
            (cv,) = prep(z, 2 * hw, hw, hd=HEAD_DIM, outs=((BF16, ("aug", HEAD_DIM)),))
            o_c = moba_attention(cq, block_means(ck32), ck, cv, _logit_bound(c_qn[i], c_kn[i]) <= MAX_FAST_LOGIT)
            ops = s5_operators(s5_lam_re[i], s5_lam_im[i], s5_b_re[i], s5_b_im[i], s5_c_re[i], s5_c_im[i],
                               s5_d[i], s5_log_dt[i], S5_CHUNK)
            y = s5_layer(z[:, :, 3 * hw:4 * hw], ops)
            y = glu(y.reshape(bsz * seq, hw), s5_glu_w[i].astype(BF16), s5_glu_b[i]).reshape(bsz, seq, hw)
            x = out_proj(o_c, y, od_w_out[i].astype(BF16), x, g1, a_t=True)
        h, route = moe_router(x, sc2, sh2, moe_wg[l], moe_bg[l], moe_we[l], moe_be[l])
        x = moe_experts(h, route, x, g2, moe_w1[l].astype(BF16), moe_w3[l].astype(BF16), moe_w2[l].astype(BF16))
    return x
```

```python
import functools
import math

import jax
import jax.numpy as jnp
from jax import lax
from jax.experimental import pallas as pl
from jax.experimental.pallas import tpu as pltpu

F32 = jnp.float32
BF16 = jnp.bfloat16
I32 = jnp.int32
HI = lax.Precision.HIGHEST

LANES = 128
NEG = -1e30
I32_MIN = -2147483648

HEAD_DIM = 64
A_HEADS = 4
B_HEADS = 8
IDX_HEADS = 8
IDX_DIM = 32
DSA_TOPK = 256
C_HEADS = 8
MOBA_BLOCK = 256
MOBA_TOPK = 3
S5_GROUP_CH = 16
S5_GROUPS = 32
S5_STATE = 64
ROPE_THETA = 500000.0
ROT_HEAD = HEAD_DIM // 4
ROT_IDX = IDX_DIM // 4
N_GROUPS = 4
EXPERTS_PER_GROUP = 8
N_EXPERTS = N_GROUPS * EXPERTS_PER_GROUP
EXPERT_FF = 512
NORM_EPS = 1e-6
S5_CHUNK = 16


def _cparams(sem, vmem_mb=48):
    return pltpu.CompilerParams(dimension_semantics=sem, vmem_limit_bytes=vmem_mb * 1024 * 1024)


def _dot3(a, b):
    a_hi = a.astype(BF16)
    a_lo = (a - a_hi.astype(F32)).astype(BF16)
    b_hi = b.astype(BF16)
    b_lo = (b - b_hi.astype(F32)).astype(BF16)
    dot = functools.partial(jnp.dot, preferred_element_type=F32)
    return dot(a_hi, b_hi) + (dot(a_hi, b_lo) + dot(a_lo, b_hi))


def _mod_kernel(c_ref, w_ref, b_ref, o_ref):
    c = c_ref[...]
    ca = c * jax.nn.sigmoid(c)
    o_ref[0] = jnp.dot(ca, w_ref[0], precision=HI, preferred_element_type=F32) + b_ref[0]


def ada_mod(c, ada_w, ada_b):
    depth, d, n6 = ada_w.shape
    bsz = c.shape[0]
    rows = 8
    c_pad = jnp.zeros((rows, d), F32).at[:bsz].set(c)
    out = pl.pallas_call(
        _mod_kernel,
        grid=(depth, n6 // d),
        in_specs=[pl.BlockSpec((rows, d), lambda l, n: (0, 0)),
                  pl.BlockSpec((1, d, d), lambda l, n: (l, 0, n)),
                  pl.BlockSpec((1, 1, d), lambda l, n: (l, 0, n))],
        out_specs=pl.BlockSpec((1, rows, d), lambda l, n: (l, 0, n)),
        out_shape=jax.ShapeDtypeStruct((depth, rows, n6), F32),
        compiler_params=_cparams(("parallel", "parallel")),
        name="ada_mod",
    )(c_pad, ada_w, ada_b.reshape(depth, 1, n6))
    return out[:, :bsz]


def _norm_mod(x, sc, sh):
    y = x * lax.rsqrt(jnp.mean(x * x, axis=-1, keepdims=True) + NORM_EPS)
    return y * (1.0 + sc) + sh


def _inproj_kernel(x_ref, sc_ref, sh_ref, w_ref, o_ref, h_sc):
    @pl.when(pl.program_id(2) == 0)
    def _():
        h_sc[...] = _norm_mod(x_ref[0], sc_ref[0], sh_ref[0]).astype(BF16)

    o_ref[0] = jnp.dot(h_sc[...], w_ref[...], preferred_element_type=F32)


def _pick_tile(n, prefs):
    for p in prefs:
        if n % p == 0:
            return p
    return n


def in_proj(x, sc, sh, w_bf16, tm=512):
    bsz, seq, d = x.shape
    n = w_bf16.shape[1]
    tm = _pick_tile(seq, (tm, 256, 128))
    tn = n if n <= 4096 else _pick_tile(n, (1152, 1024, 512, 384, 256, 128))
    return pl.pallas_call(
        _inproj_kernel,
        grid=(bsz, seq // tm, n // tn),
        in_specs=[pl.BlockSpec((1, tm, d), lambda b, i, j: (b, i, 0)),
                  pl.BlockSpec((1, 1, d), lambda b, i, j: (b, 0, 0)),
                  pl.BlockSpec((1, 1, d), lambda b, i, j: (b, 0, 0)),
                  pl.BlockSpec((d, tn), lambda b, i, j: (0, j))],
        out_specs=pl.BlockSpec((1, tm, tn), lambda b, i, j: (b, i, j)),
        out_shape=jax.ShapeDtypeStruct((bsz, seq, n), F32),
        scratch_shapes=[pltpu.VMEM((tm, d), BF16)],
        compiler_params=_cparams(("parallel", "parallel", "arbitrary")),
        name="in_proj",
    )(x, sc, sh, w_bf16)


def rope_lane_tables(seq, hd, rot):
    half = rot // 2
    pos = jnp.arange(seq, dtype=F32)
    inv = ROPE_THETA ** (-(jnp.arange(0, rot, 2, dtype=F32) / rot))
    ang = pos[:, None] * inv[None, :]
    cos, sin = jnp.cos(ang), jnp.sin(ang)
    dl = jnp.arange(LANES) % hd
    first = dl < half
    second = (dl >= half) & (dl < rot)
    idx = jnp.where(first, dl, jnp.where(second, dl - half, 0))
    cos_l = jnp.where(first | second, cos[:, idx], 1.0)
    sa_l = jnp.where(second, sin[:, idx], 0.0)
    sb_l = jnp.where(first, -sin[:, idx], 0.0)
    return cos_l.astype(F32), sa_l.astype(F32), sb_l.astype(F32)


def _prep_out_rows(mode, width):
    if mode == "t":
        return width
    if mode[0] == "aug":
        return (width // mode[1]) * (mode[1] + ONES_ROWS)
    return mode[2] - mode[1]


def _prep_kernel(*refs, hd, half, do_norm, do_rope, scale, outs):
    z_ref, g_ref, gm_ref, cos_ref, sa_ref, sb_ref = refs[:6]
    o_refs = refs[6:6 + len(outs)]
    x = z_ref[0]
    raw = x
    width = x.shape[-1]
    if do_norm:
        ms = jnp.dot((x * x).astype(BF16), gm_ref[...], preferred_element_type=F32) * (1.0 / hd)
        x = x * lax.rsqrt(ms + NORM_EPS) * g_ref[...]
    if do_rope:
        cos, sa, sb = cos_ref[...], sa_ref[...], sb_ref[...]
        cols = []
        for r in range(width // LANES):
            xc = x[:, r * LANES:(r + 1) * LANES]
            cols.append(xc * cos + pltpu.roll(xc, half, 1) * sa + pltpu.roll(xc, LANES - half, 1) * sb)
        x = cols[0] if len(cols) == 1 else jnp.concatenate(cols, axis=-1)
    if scale != 1.0:
        x = x * scale
    tm = x.shape[0]
    for o, (_, mode) in zip(o_refs, outs):
        if mode == "n":
            o[0] = x.astype(o.dtype)
        elif mode == "t":
            o[0] = x.T.astype(o.dtype)
        elif mode[0] == "aug":
            dv = mode[1]
            xt = x.T
            ones = jnp.ones((ONES_ROWS, tm), F32)
            parts = []
            for h in range(width // dv):
                parts += [xt[h * dv:(h + 1) * dv], ones]
            o[0] = jnp.concatenate(parts, axis=0).astype(o.dtype)
        else:
            o[0] = raw.T[mode[1]:mode[2]].astype(o.dtype)


def prep(z, col_off, width, *, hd, gain=None, rope=None, half=0, scale=1.0, outs=((BF16, "n"),), tm=2048):
    bsz, seq, _ = z.shape
    tm = _pick_tile(seq, (tm, 256, 128))
    cb = col_off // width
    assert col_off % width == 0 and width % LANES == 0
    do_norm = gain is not None
    do_rope = rope is not None
    g = jnp.tile(gain.astype(F32), width // hd).reshape(1, width) if do_norm else jnp.ones((1, width), F32)
    gi = jnp.arange(width) // hd
    gm = (gi[:, None] == gi[None, :]).astype(BF16)
    if do_rope:
        cos_l, sa_l, sb_l = rope
    else:
        cos_l = sa_l = sb_l = jnp.zeros((seq, LANES), F32)
    out_specs, out_shape = [], []
    for dt, mode in outs:
        if mode == "n":
            out_specs.append(pl.BlockSpec((1, tm, width), lambda b, i: (b, i, 0)))
            out_shape.append(jax.ShapeDtypeStruct((bsz, seq, width), dt))
        else:
            rows = _prep_out_rows(mode, width)
            out_specs.append(pl.BlockSpec((1, rows, tm), lambda b, i: (b, 0, i)))
            out_shape.append(jax.ShapeDtypeStruct((bsz, rows, seq), dt))
    return pl.pallas_call(
        functools.partial(_prep_kernel, hd=hd, half=half, do_norm=do_norm, do_rope=do_rope,
                          scale=scale, outs=tuple(outs)),
        grid=(bsz, seq // tm),
        in_specs=[pl.BlockSpec((1, tm, width), lambda b, i: (b, i, cb)),
                  pl.BlockSpec((1, width), lambda b, i: (0, 0)),
                  pl.BlockSpec((width, width), lambda b, i: (0, 0)),
                  pl.BlockSpec((tm, LANES), lambda b, i: (i, 0)),
                  pl.BlockSpec((tm, LANES), lambda b, i: (i, 0)),
                  pl.BlockSpec((tm, LANES), lambda b, i: (i, 0))],
        out_specs=out_specs,
        out_shape=out_shape,
        compiler_params=_cparams(("parallel", "parallel")),
        name="prep",
    )(z, g, gm, cos_l, sa_l, sb_l)


ATT_TILE = 256
ATT_GROUP = 16
MOBA_GROUP = 16


def _for_tile_groups(n_tiles, body, group_size=ATT_GROUP):
    def group(jg, carry):
        body(group_size * jg, group_size)
        return carry

    lax.fori_loop(0, n_tiles // group_size, group, 0)
    size = group_size // 2
    while size >= 1:
        def tail(size=size):
            body((n_tiles // (2 * size)) * (2 * size), size)
        pl.when((n_tiles // size) % 2 == 1)(tail)
        size //= 2


ONES_ROWS = 16
MAX_FAST_LOGIT = 60.0


def _attn_update(s_t, vt_aug, m_ref, acc_ref, idx, fast):
    if fast:
        acc_ref[idx] += jnp.dot(vt_aug, jnp.exp2(s_t).astype(BF16), preferred_element_type=F32)
    else:
        m_prev = m_ref[idx]
        m_new = jnp.maximum(m_prev, jnp.max(s_t, axis=0, keepdims=True))
        p_t = jnp.exp2(s_t - m_new).astype(BF16)
        acc_ref[idx] = (jnp.exp2(m_prev - m_new) * acc_ref[idx]
                        + jnp.dot(vt_aug, p_t, preferred_element_type=F32))
        m_ref[idx] = m_new


def _attn_init(m_sc, acc_sc):
    m_sc[...] = jnp.full(m_sc.shape, NEG, F32)
    acc_sc[...] = jnp.zeros(acc_sc.shape, F32)


def _attn_out(acc_ref, idx, dv):
    acc = acc_ref[idx]
    return acc[:dv] / acc[dv:dv + 1]


def _logit_bound(gain_q, gain_k):
    return (HEAD_DIM ** 0.5 * math.log2(math.e) * 1.02) * jnp.max(jnp.abs(gain_q)) * jnp.max(jnp.abs(gain_k))


def _half_rows(x, sub):
    row = lax.broadcasted_iota(I32, (LANES, 1), 0)
    return jnp.where((row >= sub * HEAD_DIM) & (row < (sub + 1) * HEAD_DIM), x, jnp.zeros_like(x))


def _causal_t(t):
    return lax.broadcasted_iota(I32, (t, t), 0) <= lax.broadcasted_iota(I32, (t, t), 1)


def _diff_attn_kernel(qt_ref, k_ref, vt_ref, lq1_ref, lk1_ref, lq2_ref, lk2_ref, g_ref, o_ref,
                      m_sc, acc_sc, *, t, lam_init, fast):
    i = pl.program_id(2)
    _attn_init(m_sc, acc_sc)
    qt = qt_ref[0]
    qc = [_half_rows(qt, c) for c in range(2)]

    def attend(jt, n, mask_fn):
        off = pl.multiple_of(jt * t, t)
        k = k_ref[0, pl.ds(off, n * t), :]
        vt = vt_ref[0, :, pl.ds(off, n * t)]
        s = [mask_fn(jnp.dot(k, qc[c], preferred_element_type=F32)) for c in range(2)]
        for c in range(2):
            _attn_update(s[c], vt, m_sc, acc_sc, c, fast)

    _for_tile_groups(i, lambda jt, n: attend(jt, n, lambda s_t: s_t), DIFF_GROUP)
    causal = _causal_t(t)
    attend(i, 1, lambda s_t: jnp.where(causal, s_t, NEG))

    lam = (jnp.exp(jnp.sum(lq1_ref[...] * lk1_ref[...], axis=-1, keepdims=True))
           - jnp.exp(jnp.sum(lq2_ref[...] * lk2_ref[...], axis=-1, keepdims=True)) + lam_init)
    o = _attn_out(acc_sc, 0, LANES) - lam * _attn_out(acc_sc, 1, LANES)
    o = o * lax.rsqrt(jnp.mean(o * o, axis=0, keepdims=True) + NORM_EPS)
    o_ref[0] = o * g_ref[...] * (1.0 - lam_init)


DIFF_TILE = 512
DIFF_GROUP = 4


def diff_attention(qt, k, vta, lq1, lk1, lq2, lk2, subln, lam_init, fast_ok):
    bsz, width, seq = qt.shape
    n_h = width // LANES
    t = _pick_tile(seq, (DIFF_TILE, 256, 128))
    rows = LANES + ONES_ROWS
    vec = lambda a: a.astype(F32).reshape(1, -1)
    small = pl.BlockSpec((1, HEAD_DIM), lambda b, h, i: (0, 0))

    def call(fast, *args):
        return pl.pallas_call(
            functools.partial(_diff_attn_kernel, t=t, lam_init=lam_init, fast=fast),
            grid=(bsz, n_h, seq // t),
            in_specs=[pl.BlockSpec((1, LANES, t), lambda b, h, i: (b, h, i)),
                      pl.BlockSpec((1, seq, LANES), lambda b, h, i: (b, 0, h)),
                      pl.BlockSpec((1, rows, seq), lambda b, h, i: (b, h, 0)),
                      small, small, small, small,
                      pl.BlockSpec((LANES, 1), lambda b, h, i: (0, 0))],
            out_specs=pl.BlockSpec((1, LANES, t), lambda b, h, i: (b, h, i)),
            out_shape=jax.ShapeDtypeStruct((bsz, width, seq), F32),
            scratch_shapes=[pltpu.VMEM((2, 1, t), F32), pltpu.VMEM((2, rows, t), F32)],
            compiler_params=_cparams(("parallel", "parallel", "arbitrary")),
            name="diff_attn_fast" if fast else "diff_attn_safe",
        )(*args)

    args = (qt, k, vta, vec(lq1), vec(lk1), vec(lq2), vec(lk2), subln.astype(F32).reshape(-1, 1))
    return lax.cond(fast_ok, functools.partial(call, True), functools.partial(call, False), *args)


def _sortable_key(score):
    bits = pltpu.bitcast(score + 0.0, I32)
    return bits ^ ((bits >> 31) & 0x7FFFFFFF)


def _dsa_kernel(qt_ref, k_ref, vt_ref, qit_ref, ki_ref, wt_ref, o_ref, kb_sc, m_sc, acc_sc,
                *, t, nk, n_sel, fast):
    i = pl.program_id(1)
    pair = pl.program_id(2)
    n_chunks = i + 1
    krow = lax.broadcasted_iota(I32, (t, t), 0)

    def count_where(pred_fn):
        def chunk_count(cidx):
            g = jnp.where(pred_fn(kb_sc[cidx], cidx), 1, 0)
            return jnp.sum(g.reshape(t // 8, 8, t), axis=0)

        def body(jp, acc):
            return acc + chunk_count(2 * jp) + chunk_count(2 * jp + 1)

        acc = lax.fori_loop(0, n_chunks // 2, body, jnp.zeros((8, t), I32))
        last = chunk_count(n_chunks - 1)
        acc = acc + jnp.where(n_chunks % 2 == 1, last, 0)
        return jnp.sum(acc, axis=0, keepdims=True)

    @pl.when(pair == 0)
    def _():
        qit = qit_ref[0]
        pad = jnp.zeros((LANES - IDX_DIM, t), BF16)
        q_heads = [jnp.concatenate([qit[h * IDX_DIM:(h + 1) * IDX_DIM], pad], axis=0)
                   for h in range(IDX_HEADS)]
        wt = wt_ref[0]
        w_rows = [wt[h:h + 1] for h in range(IDX_HEADS)]
        qcol = i * t + lax.broadcasted_iota(I32, (1, t), 1)

        def score_chunk(cidx, mm):
            off = pl.multiple_of(cidx * t, t)
            ki = ki_ref[0, pl.ds(off, t), :]
            score = jnp.zeros((t, t), F32)
            for h in range(IDX_HEADS):
                lg = jnp.dot(ki, q_heads[h], preferred_element_type=F32)
                score = score + w_rows[h] * jnp.maximum(lg, 0.0)
            key = _sortable_key(score)
            visible = krow + off <= qcol
            kb_sc[cidx] = jnp.where(visible, key, I32_MIN)
            lo_c = jnp.min(jnp.where(visible, key, 2147483647).reshape(t // 8, 8, t), axis=0)
            hi_c = jnp.max(jnp.where(visible, key, I32_MIN).reshape(t // 8, 8, t), axis=0)
            return jnp.minimum(mm[0], lo_c), jnp.maximum(mm[1], hi_c)

        kmin, kmax = lax.fori_loop(0, n_chunks, score_chunk, (jnp.full((8, t), 2147483647, I32),
                                                              jnp.full((8, t), I32_MIN, I32)))
        n_vis = qcol + 1
        few = n_vis <= n_sel
        kmin = jnp.min(kmin, axis=0, keepdims=True)
        kmax = jnp.max(kmax, axis=0, keepdims=True)
        span = 1 << 25
        probe = jnp.maximum(jnp.where(kmax > I32_MIN + span, kmax - span, kmin), kmin)
        c_probe = count_where(lambda kk, cidx: kk >= probe)
        probe_ok = c_probe >= n_sel
        lo0 = jnp.where(few, I32_MIN, jnp.where(probe_ok, probe, kmin))
        hi0 = jnp.where(few, I32_MIN + 1, jnp.where(probe_ok, kmax + 1, probe))

        def open_lanes(lo, hi, c_lo):
            return jnp.logical_and(c_lo != n_sel, hi > lo + 1)

        def bisect_cond(st):
            it, lo, hi, c_lo, c_hi = st
            return jnp.logical_and(it < 34, jnp.max(jnp.where(open_lanes(lo, hi, c_lo), 1, 0)) > 0)

        def bisect(st):
            it, lo, hi, c_lo, c_hi = st
            mid = (lo >> 1) + (hi >> 1) + (lo & hi & 1)
            cnt = count_where(lambda kk, cidx: kk >= mid)
            up = jnp.logical_and(open_lanes(lo, hi, c_lo), cnt >= n_sel)
            dn = jnp.logical_and(open_lanes(lo, hi, c_lo), cnt < n_sel)
            return (it + 1, jnp.where(up, mid, lo), jnp.where(dn, mid, hi),
                    jnp.where(up, cnt, c_lo), jnp.where(dn, cnt, c_hi))

        c_lo0 = jnp.where(few, n_sel, jnp.where(probe_ok, c_probe, n_vis))
        c_hi0 = jnp.where(jnp.logical_or(few, probe_ok), 0, c_probe)
        _, thr, _, c_lo, c_hi = lax.while_loop(
            bisect_cond, bisect, (jnp.int32(0), lo0, hi0, c_lo0, c_hi0))

        need = n_sel - c_hi
        all_ties = c_lo == n_sel
        jmax0 = jnp.where(few, -1, jnp.where(all_ties, 2147483647, -1))
        unresolved = jnp.logical_and(jnp.logical_not(few), jnp.logical_not(all_ties))

        def tie_search(_):
            def step(it, lohi):
                lo, hi = lohi
                mid = (lo + hi) >> 1
                cnt = count_where(lambda kk, cidx: jnp.logical_and(kk == thr, krow + cidx * t <= mid))
                ok = cnt >= need
                return jnp.where(ok, lo, mid), jnp.where(ok, mid, hi)
            n_steps = 1 + max(1, (nk * t - 1).bit_length())
            _, hi = lax.fori_loop(0, n_steps, step, (jnp.full((1, t), -1, I32),
                                                     jnp.full((1, t), nk * t - 1, I32)))
            return jnp.where(unresolved, hi, jmax0)

        any_unresolved = jnp.max(jnp.where(unresolved, 1, 0)) > 0
        jmax = lax.cond(any_unresolved, tie_search, lambda _: jmax0, 0)

        def bias_chunk(cidx, carry):
            kk = kb_sc[cidx]
            cut = jnp.where(krow + cidx * t <= jmax, thr - 1, thr)
            bias = jnp.where(kk > cut, 0.0, NEG)
            kb_sc[cidx] = pltpu.bitcast(bias.astype(F32), I32)
            return carry

        lax.fori_loop(0, n_chunks, bias_chunk, 0)

    _attn_init(m_sc, acc_sc)
    qt = qt_ref[0]
    qh = [_half_rows(qt, sub) for sub in range(2)]
    hrows = HEAD_DIM + ONES_ROWS

    def attend(jt, n):
        off = pl.multiple_of(jt * t, t)
        k = k_ref[0, pl.ds(off, n * t), :]
        vt = vt_ref[0, :, pl.ds(off, n * t)]
        bias = jnp.concatenate([pltpu.bitcast(kb_sc[jt + c], F32) for c in range(n)], axis=0)
        s = [jnp.dot(k, qh[sub], preferred_element_type=F32) + bias for sub in range(2)]
        for sub in range(2):
            _attn_update(s[sub], vt[sub * hrows:(sub + 1) * hrows], m_sc, acc_sc, sub, fast)

    _for_tile_groups(n_chunks, attend)
    o_ref[0] = jnp.concatenate([_attn_out(acc_sc, sub, HEAD_DIM) for sub in range(2)], axis=0)


def dsa_attention(qt, k, vta, qit, ki, iwt, fast_ok):
    bsz, width, seq = qt.shape
    t = _pick_tile(seq, (ATT_TILE, 128))
    nq = seq // t
    n_sel = min(DSA_TOPK, seq // 4)
    prow = 2 * (HEAD_DIM + ONES_ROWS)

    def call(fast, *args):
        return pl.pallas_call(
            functools.partial(_dsa_kernel, t=t, nk=nq, n_sel=n_sel, fast=fast),
            grid=(bsz, nq, width // LANES),
            in_specs=[pl.BlockSpec((1, LANES, t), lambda b, i, p: (b, p, i)),
                      pl.BlockSpec((1, seq, LANES), lambda b, i, p: (b, 0, p)),
                      pl.BlockSpec((1, prow, seq), lambda b, i, p: (b, p, 0)),
                      pl.BlockSpec((1, IDX_HEADS * IDX_DIM, t), lambda b, i, p: (b, 0, i)),
                      pl.BlockSpec((1, seq, LANES), lambda b, i, p: (b, 0, 0)),
                      pl.BlockSpec((1, IDX_HEADS, t), lambda b, i, p: (b, 0, i))],
            out_specs=pl.BlockSpec((1, LANES, t), lambda b, i, p: (b, p, i)),
            out_shape=jax.ShapeDtypeStruct((bsz, width, seq), F32),
            scratch_shapes=[pltpu.VMEM((nq, t, t), I32), pltpu.VMEM((2, 1, t), F32),
                            pltpu.VMEM((2, HEAD_DIM + ONES_ROWS, t), F32)],
            compiler_params=_cparams(("parallel", "parallel", "arbitrary"), vmem_mb=56),
            name="dsa_attn_fast" if fast else "dsa_attn_safe",
        )(*args)

    args = (qt, k, vta, qit, ki, iwt)
    return lax.cond(fast_ok, functools.partial(call, True), functools.partial(call, False), *args)


def _kmean_kernel(k_ref, o_ref, *, nbb):
    x = k_ref[0]
    o_ref[0] = jnp.mean(x.reshape(nbb, MOBA_BLOCK, x.shape[-1]), axis=1)


def block_means(k_f32):
    bsz, seq, width = k_f32.shape
    nb = seq // MOBA_BLOCK
    nbb = 8 if nb % 8 == 0 else nb
    return pl.pallas_call(
        functools.partial(_kmean_kernel, nbb=nbb),
        grid=(bsz, nb // nbb),
        in_specs=[pl.BlockSpec((1, nbb * MOBA_BLOCK, width), lambda b, i: (b, i, 0))],
        out_specs=pl.BlockSpec((1, nbb, width), lambda b, i: (b, i, 0)),
        out_shape=jax.ShapeDtypeStruct((bsz, nb, width), F32),
        compiler_params=_cparams(("parallel", "parallel")),
        name="moba_kmean",
    )(k_f32)


def _moba_select_kernel(qt_ref, km_ref, o_ref, *, ts, n_sel, nbp):
    j = pl.program_id(2)
    blk = lax.broadcasted_iota(I32, (nbp, ts), 0)
    own = jnp.right_shift(j * ts + lax.broadcasted_iota(I32, (1, ts), 1), MOBA_BLOCK.bit_length() - 1)
    qt = qt_ref[0]
    for sub in range(2):
        gate = _dot3(km_ref[0], _half_rows(qt, sub))
        g = jnp.where(blk < own, gate, -jnp.inf)
        sel = jnp.zeros((nbp, ts), F32)
        for _ in range(n_sel):
            mx = jnp.max(g, axis=0, keepdims=True)
            is_mx = jnp.logical_and(g == mx, mx > -jnp.inf)
            first = jnp.min(jnp.where(is_mx, blk, 2 * nbp), axis=0, keepdims=True)
            pick = blk == first
            sel = jnp.where(pick, 1.0, sel)
            g = jnp.where(pick, -jnp.inf, g)
        o_ref[0, sub] = sel


def _moba_kernel(qt_ref, sel_ref, k_ref, vt_ref, o_ref, m_sc, acc_sc, *, t, fast):
    i = pl.program_id(2)
    _attn_init(m_sc, acc_sc)
    qt = qt_ref[0]
    qb = [_half_rows(qt, sub).astype(BF16) for sub in range(2)]
    hrows = HEAD_DIM + ONES_ROWS

    def attend(kb, n, mask_fn):
        off = pl.multiple_of(kb * t, t)
        k = k_ref[0, pl.ds(off, n * t), :]
        vt = vt_ref[0, :, pl.ds(off, n * t)]
        s = [mask_fn(jnp.dot(k, qb[sub], preferred_element_type=F32), sub) for sub in range(2)]
        for sub in range(2):
            _attn_update(s[sub], vt[sub * hrows:(sub + 1) * hrows], m_sc, acc_sc, sub, fast)

    causal = _causal_t(t)
    attend(i, 1, lambda s_t, sub: jnp.where(causal, s_t, NEG))

    def past_blocks(kb, n):
        def row_mask(s_t, sub):
            rows = [jnp.where(sel_ref[0, sub, pl.ds(kb + c, 1), :] > 0.0, 0.0, NEG) for c in range(n)]
            bias = jnp.concatenate([jnp.broadcast_to(r, (t, t)) for r in rows], axis=0)
            return s_t + bias
        attend(kb, n, row_mask)

    _for_tile_groups(i, past_blocks, MOBA_GROUP)
    o_ref[0] = jnp.concatenate([_attn_out(acc_sc, sub, HEAD_DIM) for sub in range(2)], axis=0)


def moba_attention(qt_f32, kmean, k, vta, fast_ok):
    bsz, width, seq = qt_f32.shape
    t = MOBA_BLOCK
    nb = seq // MOBA_BLOCK
    n_sel = max(1, min(MOBA_TOPK, nb - 1))
    nbp = -(-nb // 8) * 8
    km = jnp.zeros((bsz, nbp, width), F32).at[:, :nb].set(kmean)
    prow = 2 * (HEAD_DIM + ONES_ROWS)
    n_pairs = width // LANES
    ts = _pick_tile(seq, (1024, 512, 256))
    sel = pl.pallas_call(
        functools.partial(_moba_select_kernel, ts=ts, n_sel=n_sel, nbp=nbp),
        grid=(bsz, n_pairs, seq // ts),
        in_specs=[pl.BlockSpec((1, LANES, ts), lambda b, h, j: (b, h, j)),
                  pl.BlockSpec((1, nbp, LANES), lambda b, h, j: (b, 0, h))],
        out_specs=pl.BlockSpec((1, 2, nbp, ts), lambda b, h, j: (b, h, 0, j)),
        out_shape=jax.ShapeDtypeStruct((bsz, 2 * n_pairs, nbp, seq), F32),
        compiler_params=_cparams(("parallel", "parallel", "parallel")),
        name="moba_select",
    )(qt_f32, km)

    def call(fast, *args):
        return pl.pallas_call(
            functools.partial(_moba_kernel, t=t, fast=fast),
            grid=(bsz, n_pairs, nb),
            in_specs=[pl.BlockSpec((1, LANES, t), lambda b, h, i: (b, h, i)),
                      pl.BlockSpec((1, 2, nbp, t), lambda b, h, i: (b, h, 0, i)),
                      pl.BlockSpec((1, seq, LANES), lambda b, h, i: (b, 0, h)),
                      pl.BlockSpec((1, prow, seq), lambda b, h, i: (b, h, 0))],
            out_specs=pl.BlockSpec((1, LANES, t), lambda b, h, i: (b, h, i)),
            out_shape=jax.ShapeDtypeStruct((bsz, width, seq), F32),
            scratch_shapes=[pltpu.VMEM((2, 1, t), F32), pltpu.VMEM((2, HEAD_DIM + ONES_ROWS, t), F32)],
            compiler_params=_cparams(("parallel", "parallel", "arbitrary")),
            name="moba_attn_fast" if fast else "moba_attn_safe",
        )(*args)

    args = (qt_f32, sel, k, vta)
    return lax.cond(fast_ok, functools.partial(call, True), functools.partial(call, False), *args)


def s5_operators(lam_re, lam_im, b_re, b_im, c_re, c_im, d_skip, log_dt, chunk):
    n_g, n_p = lam_re.shape
    n_c = d_skip.shape[-1]
    dt = jnp.exp(log_dt.astype(F32))[:, None]
    den = lam_re * lam_re + lam_im * lam_im
    mag = jnp.exp(lam_re * dt)
    abar_re = mag * jnp.cos(lam_im * dt)
    abar_im = mag * jnp.sin(lam_im * dt)
    coef_re = ((abar_re - 1.0) * lam_re + abar_im * lam_im) / den
    coef_im = (abar_im * lam_re - (abar_re - 1.0) * lam_im) / den
    bbar_re = coef_re[..., None] * b_re - coef_im[..., None] * b_im
    bbar_im = coef_re[..., None] * b_im + coef_im[..., None] * b_re
    tau = jnp.arange(chunk + 1, dtype=F32)[:, None, None]
    pmag = jnp.exp(lam_re * dt * tau)
    pw_re = pmag * jnp.cos(lam_im * dt * tau)
    pw_im = pmag * jnp.sin(lam_im * dt * tau)
    ab_re = pw_re[..., None] * bbar_re - pw_im[..., None] * bbar_im
    ab_im = pw_re[..., None] * bbar_im + pw_im[..., None] * bbar_re
    kern = (jnp.einsum('gop,tgpi->tgio', c_re, ab_re, precision=HI)
            - jnp.einsum('gop,tgpi->tgio', c_im, ab_im, precision=HI))
    t_idx = jnp.arange(chunk)
    lag = t_idx[None, :] - t_idx[:, None]
    toep = jnp.where((lag >= 0)[None, :, None, :, None],
                     kern[jnp.clip(lag, 0, chunk)].transpose(2, 0, 3, 1, 4), 0.0)
    eye = (jnp.eye(chunk)[None, :, None, :, None] * jnp.eye(n_c)[None, None, :, None, :])
    toep = toep + eye * d_skip[:, None, :, None, None]
    toep = toep.reshape(n_g, chunk * n_c, chunk * n_c)
    rev = chunk - 1 - t_idx
    w_in = jnp.concatenate([ab_re[rev].transpose(1, 0, 3, 2), ab_im[rev].transpose(1, 0, 3, 2)], axis=-1)
    w_in = w_in.reshape(n_g, chunk * n_c, 2 * n_p)
    a_re, a_im = pw_re[1:chunk + 1], pw_im[1:chunk + 1]
    st_re = (c_re[None] * a_re[:, :, None, :] - c_im[None] * a_im[:, :, None, :])
    st_im = (-c_re[None] * a_im[:, :, None, :] - c_im[None] * a_re[:, :, None, :])
    w_st = jnp.concatenate([st_re.transpose(1, 3, 0, 2), st_im.transpose(1, 3, 0, 2)], axis=1)
    w_st = w_st.reshape(n_g, 2 * n_p, chunk * n_c)
    return toep, w_in, w_st, pw_re[chunk], pw_im[chunk]


def _s5_in_kernel(u_ref, w_ref, o_ref):
    o_ref[0, 0] = _dot3(u_ref[0, 0], w_ref[0])


def _s5_scan_kernel(x_ref, aa_ref, ab_ref, abs_ref, o_ref, h_sc, hs_sc, *, steps, n_p):
    @pl.when(pl.program_id(0) == 0)
    def _():
        h_sc[...] = jnp.zeros(h_sc.shape, F32)
        hs_sc[...] = jnp.zeros(hs_sc.shape, F32)

    aa, ab, abs_ = aa_ref[...], ab_ref[...], abs_ref[...]

    def body(t, carry):
        h, hs = carry
        o_ref[t] = h
        x = x_ref[t]
        xs = pltpu.roll(x, n_p, 1)
        return aa * h + ab * hs + x, aa * hs + abs_ * h + xs

    h, hs = lax.fori_loop(0, steps, body, (h_sc[...], hs_sc[...]))
    h_sc[...] = h
    hs_sc[...] = hs


def _s5_out_kernel(u_ref, h0_ref, t_ref, w_ref, o_ref):
    y = _dot3(u_ref[0, 0], t_ref[0]) + _dot3(h0_ref[0, 0], w_ref[0])
    o_ref[0, 0] = jax.nn.gelu(y)


def s5_layer(u, ops):
    toep, w_in, w_st, ac_re, ac_im = ops
    bsz, seq, width = u.shape
    n_g, n_p = ac_re.shape
    n_c = width // n_g
    chunk = S5_CHUNK
    nj = seq // chunk
    row = chunk * n_c
    ur = u.reshape(bsz, nj, chunk, n_g, n_c).transpose(0, 3, 1, 2, 4).reshape(bsz, n_g, nj, row)
    hloc = pl.pallas_call(
        _s5_in_kernel,
        grid=(bsz, n_g),
        in_specs=[pl.BlockSpec((1, 1, nj, row), lambda b, g: (b, g, 0, 0)),
                  pl.BlockSpec((1, row, 2 * n_p), lambda b, g: (g, 0, 0))],
        out_specs=pl.BlockSpec((1, 1, nj, 2 * n_p), lambda b, g: (b, g, 0, 0)),
        out_shape=jax.ShapeDtypeStruct((bsz, n_g, nj, 2 * n_p), F32),
        compiler_params=_cparams(("parallel", "parallel")),
        name="s5_in",
    )(ur, w_in)
    bg = bsz * n_g
    xs = hloc.transpose(2, 0, 1, 3).reshape(nj, bg, 2 * n_p)
    rep = lambda a: jnp.tile(a, (bsz, 1))
    aa = rep(jnp.concatenate([ac_re, ac_re], axis=-1))
    ab = rep(jnp.concatenate([-ac_im, ac_im], axis=-1))
    abs_ = rep(jnp.concatenate([ac_im, -ac_im], axis=-1))
    steps = _pick_tile(nj, (128, 64, 32, 16, 8))
    cst = pl.BlockSpec((bg, 2 * n_p), lambda t: (0, 0))
    h0 = pl.pallas_call(
        functools.partial(_s5_scan_kernel, steps=steps, n_p=n_p),
        grid=(nj // steps,),
        in_specs=[pl.BlockSpec((steps, bg, 2 * n_p), lambda t: (t, 0, 0)), cst, cst, cst],
        out_specs=pl.BlockSpec((steps, bg, 2 * n_p), lambda t: (t, 0, 0)),
        out_shape=jax.ShapeDtypeStruct((nj, bg, 2 * n_p), F32),
        scratch_shapes=[pltpu.VMEM((bg, 2 * n_p), F32), pltpu.VMEM((bg, 2 * n_p), F32)],
        compiler_params=_cparams(("arbitrary",)),
        name="s5_scan",
    )(xs, aa, ab, abs_)
    h0 = h0.reshape(nj, bsz, n_g, 2 * n_p).transpose(1, 2, 0, 3)
    y = pl.pallas_call(
        _s5_out_kernel,
        grid=(bsz, n_g),
        in_specs=[pl.BlockSpec((1, 1, nj, row), lambda b, g: (b, g, 0, 0)),
                  pl.BlockSpec((1, 1, nj, 2 * n_p), lambda b, g: (b, g, 0, 0)),
                  pl.BlockSpec((1, row, row), lambda b, g: (g, 0, 0)),
                  pl.BlockSpec((1, 2 * n_p, row), lambda b, g: (g, 0, 0))],
        out_specs=pl.BlockSpec((1, 1, nj, row), lambda b, g: (b, g, 0, 0)),
        out_shape=jax.ShapeDtypeStruct((bsz, n_g, nj, row), F32),
        compiler_params=_cparams(("parallel", "parallel")),
        name="s5_out",
    )(ur, h0, toep, w_st)
    return y.reshape(bsz, n_g, nj, chunk, n_c).transpose(0, 2, 3, 1, 4).reshape(bsz, seq, width)


def _glu_kernel(y_ref, w_ref, b_ref, o_ref):
    y = y_ref[...]
    gate = jnp.dot(y.astype(BF16), w_ref[...], preferred_element_type=F32) + b_ref[...]
    o_ref[...] = (y * jax.nn.sigmoid(gate)).astype(o_ref.dtype)


def glu(y, w_bf16, b, tm=1024):
    n_tok, width = y.shape
    tm = _pick_tile(n_tok, (tm, 512, 256, 128))
    return pl.pallas_call(
        _glu_kernel,
        grid=(n_tok // tm,),
        in_specs=[pl.BlockSpec((tm, width), lambda i: (i, 0)),
                  pl.BlockSpec((width, width), lambda i: (0, 0)),
                  pl.BlockSpec((1, width), lambda i: (0, 0))],
        out_specs=pl.BlockSpec((tm, width), lambda i: (i, 0)),
        out_shape=jax.ShapeDtypeStruct((n_tok, width), F32),
        compiler_params=_cparams(("parallel",)),
        name="s5_glu",
    )(y, w_bf16, b.astype(F32).reshape(1, width))


def _outproj_kernel(a_ref, b_ref, w_ref, x_ref, g_ref, o_ref, *, half, a_t, b_t):
    a = a_ref[0].T if a_t else a_ref[0]
    b = b_ref[0].T if b_t else b_ref[0]
    y = (jnp.dot(a.astype(BF16), w_ref[:half, :], preferred_element_type=F32)
         + jnp.dot(b.astype(BF16), w_ref[half:, :], preferred_element_type=F32))
    o_ref[0] = x_ref[0] + g_ref[0] * y


def out_proj(a, b, w_bf16, x, gate, a_t=False, b_t=False, tm=1024):
    bsz, seq, d = x.shape
    half = w_bf16.shape[0] // 2
    tm = _pick_tile(seq, (tm, 256, 128))

    def spec(transposed):
        if transposed:
            return pl.BlockSpec((1, half, tm), lambda bb, i: (bb, 0, i))
        return pl.BlockSpec((1, tm, half), lambda bb, i: (bb, i, 0))

    return pl.pallas_call(
        functools.partial(_outproj_kernel, half=half, a_t=a_t, b_t=b_t),
        grid=(bsz, seq // tm),
        in_specs=[spec(a_t), spec(b_t),
                  pl.BlockSpec((2 * half, d), lambda bb, i: (0, 0)),
                  pl.BlockSpec((1, tm, d), lambda bb, i: (bb, i, 0)),
                  pl.BlockSpec((1, 1, d), lambda bb, i: (bb, 0, 0))],
        out_specs=pl.BlockSpec((1, tm, d), lambda bb, i: (bb, i, 0)),
        out_shape=jax.ShapeDtypeStruct((bsz, seq, d), F32),
        compiler_params=_cparams(("parallel", "parallel")),
        name="out_proj",
    )(a, b, w_bf16, x, gate)


def _router_kernel(x_ref, sc_ref, sh_ref, wr_ref, br_ref, h_ref, r_ref):
    h = _norm_mod(x_ref[0], sc_ref[0], sh_ref[0])
    h_ref[0] = h.astype(BF16)
    logits = _dot3(h, wr_ref[...]) + br_ref[...]
    tm = logits.shape[0]
    lane = lax.broadcasted_iota(I32, (tm, LANES), 1)
    big = 4 * LANES
    coarse = jnp.where(lane < N_GROUPS, logits, -jnp.inf)
    cmax = jnp.max(coarse, axis=1, keepdims=True)
    grp = jnp.min(jnp.where(coarse == cmax, lane, big), axis=1, keepdims=True)
    p_grp = 1.0 / jnp.sum(jnp.exp(coarse - cmax), axis=1, keepdims=True)
    lo = N_GROUPS + grp * EXPERTS_PER_GROUP
    fine = jnp.where(jnp.logical_and(lane >= lo, lane < lo + EXPERTS_PER_GROUP), logits, -jnp.inf)
    v1 = jnp.max(fine, axis=1, keepdims=True)
    i1 = jnp.min(jnp.where(fine == v1, lane, big), axis=1, keepdims=True)
    fine2 = jnp.where(lane == i1, -jnp.inf, fine)
    v2 = jnp.max(fine2, axis=1, keepdims=True)
    i2 = jnp.min(jnp.where(fine2 == v2, lane, big), axis=1, keepdims=True)
    e2 = jnp.exp(v2 - v1)
    wa = p_grp / (1.0 + e2)
    wb = p_grp * e2 / (1.0 + e2)
    ida = (i1 - N_GROUPS).astype(F32)
    idb = (i2 - N_GROUPS).astype(F32)
    r_ref[0] = jnp.where(lane == 0, ida, jnp.where(lane == 1, idb,
                         jnp.where(lane == 2, wa, jnp.where(lane == 3, wb, 0.0))))


def moe_router(x, sc, sh, wg, bg, we, be, tm=1024):
    bsz, seq, d = x.shape
    tm = _pick_tile(seq, (tm, 256, 128))
    n_r = N_GROUPS + N_EXPERTS
    wr = jnp.zeros((d, LANES), F32).at[:, :N_GROUPS].set(wg).at[:, N_GROUPS:n_r].set(we)
    br = jnp.zeros((1, LANES), F32).at[0, :N_GROUPS].set(bg).at[0, N_GROUPS:n_r].set(be)
    return pl.pallas_call(
        _router_kernel,
        grid=(bsz, seq // tm),
        in_specs=[pl.BlockSpec((1, tm, d), lambda b, i: (b, i, 0)),
                  pl.BlockSpec((1, 1, d), lambda b, i: (b, 0, 0)),
                  pl.BlockSpec((1, 1, d), lambda b, i: (b, 0, 0)),
                  pl.BlockSpec((d, LANES), lambda b, i: (0, 0)),
                  pl.BlockSpec((1, LANES), lambda b, i: (0, 0))],
        out_specs=[pl.BlockSpec((1, tm, d), lambda b, i: (b, i, 0)),
                   pl.BlockSpec((1, tm, LANES), lambda b, i: (b, i, 0))],
        out_shape=[jax.ShapeDtypeStruct((bsz, seq, d), BF16),
                   jax.ShapeDtypeStruct((bsz, seq, LANES), F32)],
        compiler_params=_cparams(("parallel", "parallel")),
        name="moe_router",
    )(x, sc, sh, wr, br)


MOE_SEG = 256


MOE_PAD = 16
MOE_EXPERTS_PER_STEP = 1


def _pad_rows(count):
    return jnp.floor((count + (MOE_PAD - 1)) * (1.0 / MOE_PAD)) * MOE_PAD


def _moe_kernel(erow_ref, route_ref, h_ref, x_ref, g_ref, w1_ref, w3_ref, w2_ref, o_ref,
                xs_sc, ys_sc, meta_sc, drow_sc, dcol_sc, *, chunk, rows, na):
    step = pl.program_id(1)
    n_seg = chunk // MOE_SEG
    lane1 = lax.broadcasted_iota(I32, (1, LANES), 1)

    def lane_scalar(row, k):
        return jnp.sum(jnp.where(lane1 == k, row, 0.0)).astype(I32)

    @pl.when(step == 0)
    def _():
        si = lax.broadcasted_iota(I32, (MOE_SEG, MOE_SEG), 0)
        sj = lax.broadcasted_iota(I32, (MOE_SEG, MOE_SEG), 1)
        upper = jnp.where(si < sj, 1.0, 0.0).astype(BF16)
        lower = jnp.where(sj < si, 1.0, 0.0).astype(BF16)
        erow = erow_ref[0]
        eid = lax.broadcasted_iota(I32, (N_EXPERTS, chunk), 0)
        sel = [erow[k:k + 1] == eid for k in range(2)]
        member = jnp.where(jnp.logical_or(sel[0], sel[1]), 1.0, 0.0)
        count = jnp.zeros((N_EXPERTS, 1), F32)
        parts = []
        for sgi in range(n_seg):
            seg = member[:, sgi * MOE_SEG:(sgi + 1) * MOE_SEG]
            parts.append(jnp.dot(seg.astype(BF16), upper, preferred_element_type=F32) + count)
            count = count + jnp.sum(seg, axis=1, keepdims=True)
        rank = jnp.concatenate(parts, axis=1)
        padded = _pad_rows(count)
        ei = lax.broadcasted_iota(I32, (N_EXPERTS, N_EXPERTS), 0)
        ej = lax.broadcasted_iota(I32, (N_EXPERTS, N_EXPERTS), 1)
        below = jnp.where(ej < ei, 1.0, 0.0).astype(BF16)
        start = jnp.dot(below, jnp.broadcast_to(padded, (N_EXPERTS, LANES)).astype(BF16),
                        preferred_element_type=F32)
        total = jnp.sum(padded, axis=0, keepdims=True)
        lane_e = lax.broadcasted_iota(I32, (N_EXPERTS, LANES), 1)
        meta_sc[...] = jnp.where(lane_e == 0, start, jnp.where(lane_e == 1, count, jnp.where(lane_e == 2, total, 0.0)))
        dest = start[:, 0:1] + rank
        for k in range(2):
            drow_sc[k] = jnp.sum(jnp.where(sel[k], dest, 0.0), axis=0, keepdims=True)
        route = route_ref[...]
        lane_c = lax.broadcasted_iota(I32, (chunk, LANES), 1)
        lane_f = lane_c.astype(F32)
        sel_c = [route[:, k:k + 1] == lane_f for k in range(2)]
        member_c = jnp.where(jnp.logical_or(sel_c[0], sel_c[1]), 1.0, 0.0)
        count_c = jnp.zeros((1, LANES), F32)
        parts_c = []
        for sgi in range(n_seg):
            seg = member_c[sgi * MOE_SEG:(sgi + 1) * MOE_SEG]
            parts_c.append(jnp.dot(lower, seg.astype(BF16), preferred_element_type=F32) + count_c)
            count_c = count_c + jnp.sum(seg, axis=0, keepdims=True)
        rank_c = jnp.concatenate(parts_c, axis=0)
        li = lax.broadcasted_iota(I32, (LANES, LANES), 0)
        lj = lax.broadcasted_iota(I32, (LANES, LANES), 1)
        before = jnp.where(li < lj, 1.0, 0.0).astype(BF16)
        start_c = jnp.dot(jnp.broadcast_to(_pad_rows(count_c), (8, LANES)).astype(BF16), before,
                          preferred_element_type=F32)[0:1]
        dest_c = start_c + rank_c
        dcols = [jnp.sum(jnp.where(sel_c[k], dest_c, 0.0), axis=1, keepdims=True) for k in range(2)]
        dcol_sc[...] = jnp.where(lane_c == 0, dcols[0], jnp.where(lane_c == 1, dcols[1], 0.0))
        ys_sc[...] = jnp.zeros(ys_sc.shape, BF16)
        xs_sc[...] = jnp.zeros(xs_sc.shape, BF16)
        d0, d1 = drow_sc[0], drow_sc[1]
        n_slabs = (jnp.sum(total).astype(I32) + MOE_SEG - 1) // MOE_SEG

        def sort_slab(s, carry):
            off = pl.multiple_of(s * MOE_SEG, MOE_SEG)
            r = (lax.broadcasted_iota(I32, (MOE_SEG, 1), 0) + off).astype(F32)
            onehot = jnp.where(jnp.logical_or(d0 == r, d1 == r), 1.0, 0.0).astype(BF16)
            xs_sc[pl.ds(off, MOE_SEG), :] = jnp.dot(onehot, h_ref[...], preferred_element_type=F32).astype(BF16)
            return carry

        lax.fori_loop(0, n_slabs, sort_slab, 0)

    for slot in range(MOE_EXPERTS_PER_STEP):
        meta = meta_sc[pl.ds(step * MOE_EXPERTS_PER_STEP + slot, 1), :]
        first = lane_scalar(meta, 0)
        count_e = lane_scalar(meta, 1)
        padded_e = ((count_e + MOE_PAD - 1) // MOE_PAD) * MOE_PAD

        def block(bi, carry, slot=slot, first=first, padded_e=padded_e):
            r0 = pl.multiple_of(first + bi * rows, MOE_PAD)
            xb = xs_sc[pl.ds(r0, rows), :]
            a = jnp.dot(xb, w1_ref[slot], preferred_element_type=F32)
            hid = (a * jax.nn.sigmoid(a)) * jnp.dot(xb, w3_ref[slot], preferred_element_type=F32)
            yb = jnp.dot(hid.astype(BF16), w2_ref[slot], preferred_element_type=F32)
            mine = lax.broadcasted_iota(I32, (rows, 1), 0) + bi * rows < padded_e
            ys_sc[pl.ds(r0, rows), :] = jnp.where(mine, yb, ys_sc[pl.ds(r0, rows), :].astype(F32)).astype(BF16)
            return carry

        lax.fori_loop(0, (count_e + rows - 1) // rows, block, 0)

    @pl.when(step == N_EXPERTS // MOE_EXPERTS_PER_STEP - 1)
    def _():
        route = route_ref[...]
        dcol = dcol_sc[...]
        d0, d1 = dcol[:, 0:1], dcol[:, 1:2]
        w0, w1 = route[:, 2:3], route[:, 3:4]
        r = lax.broadcasted_iota(I32, (1, na), 1).astype(F32)
        comb = (jnp.where(d0 == r, w0, 0.0) + jnp.where(d1 == r, w1, 0.0)).astype(BF16)
        moe = jnp.dot(comb, ys_sc[:na, :], preferred_element_type=F32)
        o_ref[...] = x_ref[...] + g_ref[0] * moe


def moe_experts(h, route, x, gate, w1_bf16, w3_bf16, w2_bf16, chunk=1024, rows=128):
    bsz, seq, d = x.shape
    chunk = _pick_tile(seq, (chunk, 512, 256))
    n_tok = bsz * seq
    n_chunks = n_tok // chunk
    per_b = seq // chunk
    ff = w1_bf16.shape[-1]
    na = -(-(2 * chunk + N_EXPERTS * (MOE_PAD - 1)) // MOE_SEG) * MOE_SEG
    route2 = route.reshape(n_tok, LANES)
    erow = route2[:, :2].astype(I32).T.reshape(2, n_chunks, chunk).transpose(1, 0, 2)
    eps = MOE_EXPERTS_PER_STEP
    out = pl.pallas_call(
        functools.partial(_moe_kernel, chunk=chunk, rows=rows, na=na),
        grid=(n_chunks, N_EXPERTS // eps),
        in_specs=[pl.BlockSpec((1, 2, chunk), lambda c, e: (c, 0, 0)),
                  pl.BlockSpec((chunk, LANES), lambda c, e: (c, 0)),
                  pl.BlockSpec((chunk, d), lambda c, e: (c, 0)),
                  pl.BlockSpec((chunk, d), lambda c, e: (c, 0)),
                  pl.BlockSpec((1, 1, d), lambda c, e: (c // per_b, 0, 0)),
                  pl.BlockSpec((eps, d, ff), lambda c, e: (e, 0, 0)),
                  pl.BlockSpec((eps, d, ff), lambda c, e: (e, 0, 0)),
                  pl.BlockSpec((eps, ff, d), lambda c, e: (e, 0, 0))],
        out_specs=pl.BlockSpec((chunk, d), lambda c, e: (c, 0)),
        out_shape=jax.ShapeDtypeStruct((n_tok, d), F32),
        scratch_shapes=[pltpu.VMEM((na + rows, d), BF16), pltpu.VMEM((na + rows, d), BF16),
                        pltpu.VMEM((N_EXPERTS, LANES), F32), pltpu.VMEM((2, 1, chunk), F32),
                        pltpu.VMEM((chunk, LANES), F32)],
        compiler_params=_cparams(("parallel", "arbitrary")),
        name="moe_experts",
    )(erow, route2, h.reshape(n_tok, d), x.reshape(n_tok, d), gate, w1_bf16, w3_bf16, w2_bf16)
    return out.reshape(bsz, seq, d)


def kernel(x, c, ada_w, ada_b, moe_wg, moe_bg, moe_we, moe_be, moe_w1, moe_w3, moe_w2,
           ev_w_in, ev_w_out, a_qn, a_kn, a_lq1, a_lk1, a_lq2, a_lk2, a_subln,
           b_qn, b_kn, b_idx_kn, od_w_in, od_w_out, c_qn, c_kn,
           s5_lam_re, s5_lam_im, s5_b_re, s5_b_im, s5_c_re, s5_c_im, s5_d, s5_log_dt,
           s5_glu_w, s5_glu_b):
    bsz, seq, d = x.shape
    depth = ada_w.shape[0]
    mod = ada_mod(c, ada_w, ada_b)
    rope_h = rope_lane_tables(seq, HEAD_DIM, ROT_HEAD)
    rope_i = rope_lane_tables(seq, IDX_DIM, ROT_IDX)
    qk_scale = HEAD_DIM ** -0.5 * math.log2(math.e)
    half_h, half_i = ROT_HEAD // 2, ROT_IDX // 2
    hw = A_HEADS * 2 * HEAD_DIM

    def pad_cols(w):
        n = w.shape[1]
        npad = -(-n // LANES) * LANES
        return jnp.zeros((w.shape[0], npad), F32).at[:, :n].set(w).astype(BF16)

    for l in range(depth):
        sh1, sc1, g1, sh2, sc2, g2 = [m[:, None, :] for m in jnp.split(mod[l], 6, axis=-1)]
        i = l // 2
        if l % 2 == 0:
            lam_init = 0.8 - 0.6 * math.exp(-0.3 * l)
            z = in_proj(x, sc1, sh1, pad_cols(ev_w_in[i]))
            q_t, v_a, v_b = ((BF16, "t"),), ((BF16, ("aug", 2 * HEAD_DIM)),), ((BF16, ("aug", HEAD_DIM)),)
            (aq,) = prep(z, 0, hw, hd=HEAD_DIM, gain=a_qn[i], rope=rope_h, half=half_h, scale=qk_scale, outs=q_t)
            (ak,) = prep(z, hw, hw, hd=HEAD_DIM, gain=a_kn[i], rope=rope_h, half=half_h)
            (av,) = prep(z, 2 * hw, hw, hd=HEAD_DIM, outs=v_a)
            o_a = diff_attention(aq, ak, av, a_lq1[i], a_lk1[i], a_lq2[i], a_lk2[i], a_subln[i], lam_init,
                                 _logit_bound(a_qn[i], a_kn[i]) <= MAX_FAST_LOGIT)
            (bq,) = prep(z, 3 * hw, hw, hd=HEAD_DIM, gain=b_qn[i], rope=rope_h, half=half_h, scale=qk_scale, outs=q_t)
            (bk,) = prep(z, 4 * hw, hw, hd=HEAD_DIM, gain=b_kn[i], rope=rope_h, half=half_h)
            (bv,) = prep(z, 5 * hw, hw, hd=HEAD_DIM, outs=v_b)
            iq_off = 6 * hw
            ik_off = iq_off + IDX_HEADS * IDX_DIM
            (iq,) = prep(z, iq_off, IDX_HEADS * IDX_DIM, hd=IDX_DIM, rope=rope_i, half=half_i, outs=q_t)
            ik, iw = prep(z, ik_off, LANES, hd=IDX_DIM, gain=b_idx_kn[i], rope=rope_i, half=half_i,
                          outs=((BF16, "n"), (F32, ("rawt", IDX_DIM, IDX_DIM + IDX_HEADS))))
            o_b = dsa_attention(bq, bk, bv, iq, ik, iw, _logit_bound(b_qn[i], b_kn[i]) <= MAX_FAST_LOGIT)
            x = out_proj(o_a, o_b, ev_w_out[i].astype(BF16), x, g1, a_t=True, b_t=True)
        else:
            z = in_proj(x, sc1, sh1, pad_cols(od_w_in[i]))
            (cq,) = prep(z, 0, hw, hd=HEAD_DIM, gain=c_qn[i], rope=rope_h, half=half_h, scale=qk_scale,
                         outs=((F32, "t"),))
            ck32, ck = prep(z, hw, hw, hd=HEAD_DIM, gain=c_kn[i], rope=rope_h, half=half_h,
                            outs=((F32, "n"), (BF16, "n")))
            (cv,) = prep(z, 2 * hw, hw, hd=HEAD_DIM, outs=((BF16, ("aug", HEAD_DIM)),))
            o_c = moba_attention(cq, block_means(ck32), ck, cv, _logit_bound(c_qn[i], c_kn[i]) <= MAX_FAST_LOGIT)
            ops = s5_operators(s5_lam_re[i], s5_lam_im[i], s5_b_re[i], s5_b_im[i], s5_c_re[i], s5_c_im[i],
                               s5_d[i], s5_log_dt[i], S5_CHUNK)
            y = s5_layer(z[:, :, 3 * hw:4 * hw], ops)
            y = glu(y.reshape(bsz * seq, hw), s5_glu_w[i].astype(BF16), s5_glu_b[i]).reshape(bsz, seq, hw)
            x = out_proj(o_c, y, od_w_out[i].astype(BF16), x, g1, a_t=True)
        h, route = moe_router(x, sc2, sh2, moe_wg[l], moe_bg[l], moe_we[l], moe_be[l])
        x = moe_experts(h, route, x, g2, moe_w1[l].astype(BF16), moe_w3[l].astype(BF16), moe_w2[l].astype(BF16))
    return x
```

```python
import functools
import math

import jax
import jax.numpy as jnp
from jax import lax
from jax.experimental import pallas as pl
from jax.experimental.pallas import tpu as pltpu

F32 = jnp.float32
BF16 = jnp.bfloat16
I32 = jnp.int32
HI = lax.Precision.HIGHEST

LANES = 128
NEG = -1e30
I32_MIN = -2147483648

HEAD_DIM = 64
A_HEADS = 4
B_HEADS = 8
IDX_HEADS = 8
IDX_DIM = 32
DSA_TOPK = 256
C_HEADS = 8
MOBA_BLOCK = 256
MOBA_TOPK = 3
S5_GROUP_CH = 16
S5_GROUPS = 32
S5_STATE = 64
ROPE_THETA = 500000.0
ROT_HEAD = HEAD_DIM // 4
ROT_IDX = IDX_DIM // 4
N_GROUPS = 4
EXPERTS_PER_GROUP = 8
N_EXPERTS = N_GROUPS * EXPERTS_PER_GROUP
EXPERT_FF = 512
NORM_EPS = 1e-6
S5_CHUNK = 16


def _cparams(sem, vmem_mb=48):
    return pltpu.CompilerParams(dimension_semantics=sem, vmem_limit_bytes=vmem_mb * 1024 * 1024)


def _dot3(a, b):
    a_hi = a.astype(BF16)
    a_lo = (a - a_hi.astype(F32)).astype(BF16)
    b_hi = b.astype(BF16)
    b_lo = (b - b_hi.astype(F32)).astype(BF16)
    dot = functools.partial(jnp.dot, preferred_element_type=F32)
    return dot(a_hi, b_hi) + (dot(a_hi, b_lo) + dot(a_lo, b_hi))


def _mod_kernel(c_ref, w_ref, b_ref, o_ref):
    c = c_ref[...]
    ca = c * jax.nn.sigmoid(c)
    o_ref[0] = jnp.dot(ca, w_ref[0], precision=HI, preferred_element_type=F32) + b_ref[0]


def ada_mod(c, ada_w, ada_b):
    depth, d, n6 = ada_w.shape
    bsz = c.shape[0]
    rows = 8
    c_pad = jnp.zeros((rows, d), F32).at[:bsz].set(c)
    out = pl.pallas_call(
        _mod_kernel,
        grid=(depth, n6 // d),
        in_specs=[pl.BlockSpec((rows, d), lambda l, n: (0, 0)),
                  pl.BlockSpec((1, d, d), lambda l, n: (l, 0, n)),
                  pl.BlockSpec((1, 1, d), lambda l, n: (l, 0, n))],
        out_specs=pl.BlockSpec((1, rows, d), lambda l, n: (l, 0, n)),
        out_shape=jax.ShapeDtypeStruct((depth, rows, n6), F32),
        compiler_params=_cparams(("parallel", "parallel")),
        name="ada_mod",
    )(c_pad, ada_w, ada_b.reshape(depth, 1, n6))
    return out[:, :bsz]


def _norm_mod(x, sc, sh):
    y = x * lax.rsqrt(jnp.mean(x * x, axis=-1, keepdims=True) + NORM_EPS)
    return y * (1.0 + sc) + sh


def _inproj_kernel(x_ref, sc_ref, sh_ref, w_ref, o_ref, h_sc):
    @pl.when(pl.program_id(2) == 0)
    def _():
        h_sc[...] = _norm_mod(x_ref[0], sc_ref[0], sh_ref[0]).astype(BF16)

    o_ref[0] = jnp.dot(h_sc[...], w_ref[...], preferred_element_type=F32)


def _pick_tile(n, prefs):
    for p in prefs:
        if n % p == 0:
            return p
    return n


def in_proj(x, sc, sh, w_bf16, tm=512):
    bsz, seq, d = x.shape
    n = w_bf16.shape[1]
    tm = _pick_tile(seq, (tm, 256, 128))
    tn = n if n <= 4096 else _pick_tile(n, (1152, 1024, 512, 384, 256, 128))
    return pl.pallas_call(
        _inproj_kernel,
        grid=(bsz, seq // tm, n // tn),
        in_specs=[pl.BlockSpec((1, tm, d), lambda b, i, j: (b, i, 0)),
                  pl.BlockSpec((1, 1, d), lambda b, i, j: (b, 0, 0)),
                  pl.BlockSpec((1, 1, d), lambda b, i, j: (b, 0, 0)),
                  pl.BlockSpec((d, tn), lambda b, i, j: (0, j))],
        out_specs=pl.BlockSpec((1, tm, tn), lambda b, i, j: (b, i, j)),
        out_shape=jax.ShapeDtypeStruct((bsz, seq, n), F32),
        scratch_shapes=[pltpu.VMEM((tm, d), BF16)],
        compiler_params=_cparams(("parallel", "parallel", "arbitrary")),
        name="in_proj",
    )(x, sc, sh, w_bf16)


def rope_lane_tables(seq, hd, rot):
    half = rot // 2
    pos = jnp.arange(seq, dtype=F32)
    inv = ROPE_THETA ** (-(jnp.arange(0, rot, 2, dtype=F32) / rot))
    ang = pos[:, None] * inv[None, :]
    cos, sin = jnp.cos(ang), jnp.sin(ang)
    dl = jnp.arange(LANES) % hd
    first = dl < half
    second = (dl >= half) & (dl < rot)
    idx = jnp.where(first, dl, jnp.where(second, dl - half, 0))
    cos_l = jnp.where(first | second, cos[:, idx], 1.0)
    sa_l = jnp.where(second, sin[:, idx], 0.0)
    sb_l = jnp.where(first, -sin[:, idx], 0.0)
    return cos_l.astype(F32), sa_l.astype(F32), sb_l.astype(F32)


def _prep_out_rows(mode, width):
    if mode == "t":
        return width
    if mode[0] == "aug":
        return (width // mode[1]) * (mode[1] + ONES_ROWS)
    return mode[2] - mode[1]


def _prep_kernel(*refs, hd, half, do_norm, do_rope, scale, outs):
    z_ref, g_ref, gm_ref, cos_ref, sa_ref, sb_ref = refs[:6]
    o_refs = refs[6:6 + len(outs)]
    x = z_ref[0]
    raw = x
    width = x.shape[-1]
    if do_norm:
        ms = jnp.dot((x * x).astype(BF16), gm_ref[...], preferred_element_type=F32) * (1.0 / hd)
        x = x * lax.rsqrt(ms + NORM_EPS) * g_ref[...]
    if do_rope:
        cos, sa, sb = cos_ref[...], sa_ref[...], sb_ref[...]
        cols = []
        for r in range(width // LANES):
            xc = x[:, r * LANES:(r + 1) * LANES]
            cols.append(xc * cos + pltpu.roll(xc, half, 1) * sa + pltpu.roll(xc, LANES - half, 1) * sb)
        x = cols[0] if len(cols) == 1 else jnp.concatenate(cols, axis=-1)
    if scale != 1.0:
        x = x * scale
    tm = x.shape[0]
    for o, (_, mode) in zip(o_refs, outs):
        if mode == "n":
            o[0] = x.astype(o.dtype)
        elif mode == "t":
            o[0] = x.T.astype(o.dtype)
        elif mode[0] == "aug":
            dv = mode[1]
            xt = x.T
            ones = jnp.ones((ONES_ROWS, tm), F32)
            parts = []
            for h in range(width // dv):
                parts += [xt[h * dv:(h + 1) * dv], ones]
            o[0] = jnp.concatenate(parts, axis=0).astype(o.dtype)
        else:
            o[0] = raw.T[mode[1]:mode[2]].astype(o.dtype)


def prep(z, col_off, width, *, hd, gain=None, rope=None, half=0, scale=1.0, outs=((BF16, "n"),), tm=2048):
    bsz, seq, _ = z.shape
    tm = _pick_tile(seq, (tm, 256, 128))
    cb = col_off // width
    assert col_off % width == 0 and width % LANES == 0
    do_norm = gain is not None
    do_rope = rope is not None
    g = jnp.tile(gain.astype(F32), width // hd).reshape(1, width) if do_norm else jnp.ones((1, width), F32)
    gi = jnp.arange(width) // hd
    gm = (gi[:, None] == gi[None, :]).astype(BF16)
    if do_rope:
        cos_l, sa_l, sb_l = rope
    else:
        cos_l = sa_l = sb_l = jnp.zeros((seq, LANES), F32)
    out_specs, out_shape = [], []
    for dt, mode in outs:
        if mode == "n":
            out_specs.append(pl.BlockSpec((1, tm, width), lambda b, i: (b, i, 0)))
            out_shape.append(jax.ShapeDtypeStruct((bsz, seq, width), dt))
        else:
            rows = _prep_out_rows(mode, width)
            out_specs.append(pl.BlockSpec((1, rows, tm), lambda b, i: (b, 0, i)))
            out_shape.append(jax.ShapeDtypeStruct((bsz, rows, seq), dt))
    return pl.pallas_call(
        functools.partial(_prep_kernel, hd=hd, half=half, do_norm=do_norm, do_rope=do_rope,
                          scale=scale, outs=tuple(outs)),
        grid=(bsz, seq // tm),
        in_specs=[pl.BlockSpec((1, tm, width), lambda b, i: (b, i, cb)),
                  pl.BlockSpec((1, width), lambda b, i: (0, 0)),
                  pl.BlockSpec((width, width), lambda b, i: (0, 0)),
                  pl.BlockSpec((tm, LANES), lambda b, i: (i, 0)),
                  pl.BlockSpec((tm, LANES), lambda b, i: (i, 0)),
                  pl.BlockSpec((tm, LANES), lambda b, i: (i, 0))],
        out_specs=out_specs,
        out_shape=out_shape,
        compiler_params=_cparams(("parallel", "parallel")),
        name="prep",
    )(z, g, gm, cos_l, sa_l, sb_l)


ATT_TILE = 256
ATT_GROUP = 16
MOBA_GROUP = 16


def _for_tile_groups(n_tiles, body, group_size=ATT_GROUP):
    def group(jg, carry):
        body(group_size * jg, group_size)
        return carry

    lax.fori_loop(0, n_tiles // group_size, group, 0)
    size = group_size // 2
    while size >= 1:
        def tail(size=size):
            body((n_tiles // (2 * size)) * (2 * size), size)
        pl.when((n_tiles // size) % 2 == 1)(tail)
        size //= 2


ONES_ROWS = 16
MAX_FAST_LOGIT = 60.0


def _attn_update(s_t, vt_aug, m_ref, acc_ref, idx, fast):
    if fast:
        acc_ref[idx] += jnp.dot(vt_aug, jnp.exp2(s_t).astype(BF16), preferred_element_type=F32)
    else:
        m_prev = m_ref[idx]
        m_new = jnp.maximum(m_prev, jnp.max(s_t, axis=0, keepdims=True))
        p_t = jnp.exp2(s_t - m_new).astype(BF16)
        acc_ref[idx] = (jnp.exp2(m_prev - m_new) * acc_ref[idx]
                        + jnp.dot(vt_aug, p_t, preferred_element_type=F32))
        m_ref[idx] = m_new


def _attn_init(m_sc, acc_sc):
    m_sc[...] = jnp.full(m_sc.shape, NEG, F32)
    acc_sc[...] = jnp.zeros(acc_sc.shape, F32)


def _attn_out(acc_ref, idx, dv):
    acc = acc_ref[idx]
    return acc[:dv] / acc[dv:dv + 1]


def _logit_bound(gain_q, gain_k):
    return (HEAD_DIM ** 0.5 * math.log2(math.e) * 1.02) * jnp.max(jnp.abs(gain_q)) * jnp.max(jnp.abs(gain_k))


def _half_rows(x, sub):
    row = lax.broadcasted_iota(I32, (LANES, 1), 0)
    return jnp.where((row >= sub * HEAD_DIM) & (row < (sub + 1) * HEAD_DIM), x, jnp.zeros_like(x))


def _causal_t(t):
    return lax.broadcasted_iota(I32, (t, t), 0) <= lax.broadcasted_iota(I32, (t, t), 1)


def _diff_attn_kernel(qt_ref, k_ref, vt_ref, lq1_ref, lk1_ref, lq2_ref, lk2_ref, g_ref, o_ref,
                      m_sc, acc_sc, *, t, lam_init, fast):
    i = pl.program_id(2)
    _attn_init(m_sc, acc_sc)
    qt = qt_ref[0]
    qc = [_half_rows(qt, c) for c in range(2)]

    def attend(jt, n, mask_fn):
        off = pl.multiple_of(jt * t, t)
        k = k_ref[0, pl.ds(off, n * t), :]
        vt = vt_ref[0, :, pl.ds(off, n * t)]
        s = [mask_fn(jnp.dot(k, qc[c], preferred_element_type=F32)) for c in range(2)]
        for c in range(2):
            _attn_update(s[c], vt, m_sc, acc_sc, c, fast)

    _for_tile_groups(i, lambda jt, n: attend(jt, n, lambda s_t: s_t), DIFF_GROUP)
    causal = _causal_t(t)
    attend(i, 1, lambda s_t: jnp.where(causal, s_t, NEG))

    lam = (jnp.exp(jnp.sum(lq1_ref[...] * lk1_ref[...], axis=-1, keepdims=True))
           - jnp.exp(jnp.sum(lq2_ref[...] * lk2_ref[...], axis=-1, keepdims=True)) + lam_init)
    o = _attn_out(acc_sc, 0, LANES) - lam * _attn_out(acc_sc, 1, LANES)
    o = o * lax.rsqrt(jnp.mean(o * o, axis=0, keepdims=True) + NORM_EPS)
    o_ref[0] = o * g_ref[...] * (1.0 - lam_init)


DIFF_TILE = 512
DIFF_GROUP = 8


def diff_attention(qt, k, vta, lq1, lk1, lq2, lk2, subln, lam_init, fast_ok):
    bsz, width, seq = qt.shape
    n_h = width // LANES
    t = _pick_tile(seq, (DIFF_TILE, 256, 128))
    rows = LANES + ONES_ROWS
    vec = lambda a: a.astype(F32).reshape(1, -1)
    small = pl.BlockSpec((1, HEAD_DIM), lambda b, h, i: (0, 0))

    def call(fast, *args):
        return pl.pallas_call(
            functools.partial(_diff_attn_kernel, t=t, lam_init=lam_init, fast=fast),
            grid=(bsz, n_h, seq // t),
            in_specs=[pl.BlockSpec((1, LANES, t), lambda b, h, i: (b, h, i)),
                      pl.BlockSpec((1, seq, LANES), lambda b, h, i: (b, 0, h)),
                      pl.BlockSpec((1, rows, seq), lambda b, h, i: (b, h, 0)),
                      small, small, small, small,
                      pl.BlockSpec((LANES, 1), lambda b, h, i: (0, 0))],
            out_specs=pl.BlockSpec((1, LANES, t), lambda b, h, i: (b, h, i)),
            out_shape=jax.ShapeDtypeStruct((bsz, width, seq), F32),
            scratch_shapes=[pltpu.VMEM((2, 1, t), F32), pltpu.VMEM((2, rows, t), F32)],
            compiler_params=_cparams(("parallel", "parallel", "arbitrary")),
            name="diff_attn_fast" if fast else "diff_attn_safe",
        )(*args)

    args = (qt, k, vta, vec(lq1), vec(lk1), vec(lq2), vec(lk2), subln.astype(F32).reshape(-1, 1))
    return lax.cond(fast_ok, functools.partial(call, True), functools.partial(call, False), *args)


def _sortable_key(score):
    bits = pltpu.bitcast(score + 0.0, I32)
    return bits ^ ((bits >> 31) & 0x7FFFFFFF)


def _dsa_kernel(qt_ref, k_ref, vt_ref, qit_ref, ki_ref, wt_ref, o_ref, kb_sc, m_sc, acc_sc,
                *, t, nk, n_sel, fast):
    i = pl.program_id(1)
    pair = pl.program_id(2)
    n_chunks = i + 1
    krow = lax.broadcasted_iota(I32, (t, t), 0)

    def count_where(pred_fn):
        def chunk_count(cidx):
            g = jnp.where(pred_fn(kb_sc[cidx], cidx), 1, 0)
            return jnp.sum(g.reshape(t // 8, 8, t), axis=0)

        def body(jp, acc):
            return acc + chunk_count(2 * jp) + chunk_count(2 * jp + 1)

        acc = lax.fori_loop(0, n_chunks // 2, body, jnp.zeros((8, t), I32))
        last = chunk_count(n_chunks - 1)
        acc = acc + jnp.where(n_chunks % 2 == 1, last, 0)
        return jnp.sum(acc, axis=0, keepdims=True)

    @pl.when(pair == 0)
    def _():
        qit = qit_ref[0]
        pad = jnp.zeros((LANES - IDX_DIM, t), BF16)
        q_heads = [jnp.concatenate([qit[h * IDX_DIM:(h + 1) * IDX_DIM], pad], axis=0)
                   for h in range(IDX_HEADS)]
        wt = wt_ref[0]
        w_rows = [wt[h:h + 1] for h in range(IDX_HEADS)]
        qcol = i * t + lax.broadcasted_iota(I32, (1, t), 1)

        def score_chunk(cidx, mm):
            off = pl.multiple_of(cidx * t, t)
            ki = ki_ref[0, pl.ds(off, t), :]
            score = jnp.zeros((t, t), F32)
            for h in range(IDX_HEADS):
                lg = jnp.dot(ki, q_heads[h], preferred_element_type=F32)
                score = score + w_rows[h] * jnp.maximum(lg, 0.0)
            key = _sortable_key(score)
            visible = krow + off <= qcol
            kb_sc[cidx] = jnp.where(visible, key, I32_MIN)
            lo_c = jnp.min(jnp.where(visible, key, 2147483647).reshape(t // 8, 8, t), axis=0)
            hi_c = jnp.max(jnp.where(visible, key, I32_MIN).reshape(t // 8, 8, t), axis=0)
            return jnp.minimum(mm[0], lo_c), jnp.maximum(mm[1], hi_c)

        kmin, kmax = lax.fori_loop(0, n_chunks, score_chunk, (jnp.full((8, t), 2147483647, I32),
                                                              jnp.full((8, t), I32_MIN, I32)))
        n_vis = qcol + 1
        few = n_vis <= n_sel
        kmin = jnp.min(kmin, axis=0, keepdims=True)
        kmax = jnp.max(kmax, axis=0, keepdims=True)
        span = 1 << 25
        probe = jnp.maximum(jnp.where(kmax > I32_MIN + span, kmax - span, kmin), kmin)
        c_probe = count_where(lambda kk, cidx: kk >= probe)
        probe_ok = c_probe >= n_sel
        lo0 = jnp.where(few, I32_MIN, jnp.where(probe_ok, probe, kmin))
        hi0 = jnp.where(few, I32_MIN + 1, jnp.where(probe_ok, kmax + 1, probe))

        def open_lanes(lo, hi, c_lo):
            return jnp.logical_and(c_lo != n_sel, hi > lo + 1)

        def bisect_cond(st):
            it, lo, hi, c_lo, c_hi = st
            return jnp.logical_and(it < 34, jnp.max(jnp.where(open_lanes(lo, hi, c_lo), 1, 0)) > 0)

        def bisect(st):
            it, lo, hi, c_lo, c_hi = st
            mid = (lo >> 1) + (hi >> 1) + (lo & hi & 1)
            cnt = count_where(lambda kk, cidx: kk >= mid)
            up = jnp.logical_and(open_lanes(lo, hi, c_lo), cnt >= n_sel)
            dn = jnp.logical_and(open_lanes(lo, hi, c_lo), cnt < n_sel)
            return (it + 1, jnp.where(up, mid, lo), jnp.where(dn, mid, hi),
                    jnp.where(up, cnt, c_lo), jnp.where(dn, cnt, c_hi))

        c_lo0 = jnp.where(few, n_sel, jnp.where(probe_ok, c_probe, n_vis))
        c_hi0 = jnp.where(jnp.logical_or(few, probe_ok), 0, c_probe)
        _, thr, _, c_lo, c_hi = lax.while_loop(
            bisect_cond, bisect, (jnp.int32(0), lo0, hi0, c_lo0, c_hi0))

        need = n_sel - c_hi
        all_ties = c_lo == n_sel
        jmax0 = jnp.where(few, -1, jnp.where(all_ties, 2147483647, -1))
        unresolved = jnp.logical_and(jnp.logical_not(few), jnp.logical_not(all_ties))

        def tie_search(_):
            def step(it, lohi):
                lo, hi = lohi
                mid = (lo + hi) >> 1
                cnt = count_where(lambda kk, cidx: jnp.logical_and(kk == thr, krow + cidx * t <= mid))
                ok = cnt >= need
                return jnp.where(ok, lo, mid), jnp.where(ok, mid, hi)
            n_steps = 1 + max(1, (nk * t - 1).bit_length())
            _, hi = lax.fori_loop(0, n_steps, step, (jnp.full((1, t), -1, I32),
                                                     jnp.full((1, t), nk * t - 1, I32)))
            return jnp.where(unresolved, hi, jmax0)

        any_unresolved = jnp.max(jnp.where(unresolved, 1, 0)) > 0
        jmax = lax.cond(any_unresolved, tie_search, lambda _: jmax0, 0)

        def bias_chunk(cidx, carry):
            kk = kb_sc[cidx]
            cut = jnp.where(krow + cidx * t <= jmax, thr - 1, thr)
            bias = jnp.where(kk > cut, 0.0, NEG)
            kb_sc[cidx] = pltpu.bitcast(bias.astype(F32), I32)
            return carry

        lax.fori_loop(0, n_chunks, bias_chunk, 0)

    _attn_init(m_sc, acc_sc)
    qt = qt_ref[0]
    qh = [_half_rows(qt, sub) for sub in range(2)]
    hrows = HEAD_DIM + ONES_ROWS

    def attend(jt, n):
        off = pl.multiple_of(jt * t, t)
        k = k_ref[0, pl.ds(off, n * t), :]
        vt = vt_ref[0, :, pl.ds(off, n * t)]
        bias = jnp.concatenate([pltpu.bitcast(kb_sc[jt + c], F32) for c in range(n)], axis=0)
        s = [jnp.dot(k, qh[sub], preferred_element_type=F32) + bias for sub in range(2)]
        for sub in range(2):
            _attn_update(s[sub], vt[sub * hrows:(sub + 1) * hrows], m_sc, acc_sc, sub, fast)

    _for_tile_groups(n_chunks, attend)
    o_ref[0] = jnp.concatenate([_attn_out(acc_sc, sub, HEAD_DIM) for sub in range(2)], axis=0)


def dsa_attention(qt, k, vta, qit, ki, iwt, fast_ok):
    bsz, width, seq = qt.shape
    t = _pick_tile(seq, (ATT_TILE, 128))
    nq = seq // t
    n_sel = min(DSA_TOPK, seq // 4)
    prow = 2 * (HEAD_DIM + ONES_ROWS)

    def call(fast, *args):
        return pl.pallas_call(
            functools.partial(_dsa_kernel, t=t, nk=nq, n_sel=n_sel, fast=fast),
            grid=(bsz, nq, width // LANES),
            in_specs=[pl.BlockSpec((1, LANES, t), lambda b, i, p: (b, p, i)),
                      pl.BlockSpec((1, seq, LANES), lambda b, i, p: (b, 0, p)),
                      pl.BlockSpec((1, prow, seq), lambda b, i, p: (b, p, 0)),
                      pl.BlockSpec((1, IDX_HEADS * IDX_DIM, t), lambda b, i, p: (b, 0, i)),
                      pl.BlockSpec((1, seq, LANES), lambda b, i, p: (b, 0, 0)),
                      pl.BlockSpec((1, IDX_HEADS, t), lambda b, i, p: (b, 0, i))],
            out_specs=pl.BlockSpec((1, LANES, t), lambda b, i, p: (b, p, i)),
            out_shape=jax.ShapeDtypeStruct((bsz, width, seq), F32),
            scratch_shapes=[pltpu.VMEM((nq, t, t), I32), pltpu.VMEM((2, 1, t), F32),
                            pltpu.VMEM((2, HEAD_DIM + ONES_ROWS, t), F32)],
            compiler_params=_cparams(("parallel", "parallel", "arbitrary"), vmem_mb=56),
            name="dsa_attn_fast" if fast else "dsa_attn_safe",
        )(*args)

    args = (qt, k, vta, qit, ki, iwt)
    return lax.cond(fast_ok, functools.partial(call, True), functools.partial(call, False), *args)


def _kmean_kernel(k_ref, o_ref, *, nbb):
    x = k_ref[0]
    o_ref[0] = jnp.mean(x.reshape(nbb, MOBA_BLOCK, x.shape[-1]), axis=1)


def block_means(k_f32):
    bsz, seq, width = k_f32.shape
    nb = seq // MOBA_BLOCK
    nbb = 8 if nb % 8 == 0 else nb
    return pl.pallas_call(
        functools.partial(_kmean_kernel, nbb=nbb),
        grid=(bsz, nb // nbb),
        in_specs=[pl.BlockSpec((1, nbb * MOBA_BLOCK, width), lambda b, i: (b, i, 0))],
        out_specs=pl.BlockSpec((1, nbb, width), lambda b, i: (b, i, 0)),
        out_shape=jax.ShapeDtypeStruct((bsz, nb, width), F32),
        compiler_params=_cparams(("parallel", "parallel")),
        name="moba_kmean",
    )(k_f32)


def _moba_select_kernel(qt_ref, km_ref, o_ref, *, ts, n_sel, nbp):
    j = pl.program_id(2)
    blk = lax.broadcasted_iota(I32, (nbp, ts), 0)
    own = jnp.right_shift(j * ts + lax.broadcasted_iota(I32, (1, ts), 1), MOBA_BLOCK.bit_length() - 1)
    qt = qt_ref[0]
    for sub in range(2):
        gate = _dot3(km_ref[0], _half_rows(qt, sub))
        g = jnp.where(blk < own, gate, -jnp.inf)
        sel = jnp.zeros((nbp, ts), F32)
        for _ in range(n_sel):
            mx = jnp.max(g, axis=0, keepdims=True)
            is_mx = jnp.logical_and(g == mx, mx > -jnp.inf)
            first = jnp.min(jnp.where(is_mx, blk, 2 * nbp), axis=0, keepdims=True)
            pick = blk == first
            sel = jnp.where(pick, 1.0, sel)
            g = jnp.where(pick, -jnp.inf, g)
        o_ref[0, sub] = sel


def _moba_kernel(qt_ref, sel_ref, k_ref, vt_ref, o_ref, m_sc, acc_sc, *, t, fast):
    i = pl.program_id(2)
    _attn_init(m_sc, acc_sc)
    qt = qt_ref[0]
    qb = [_half_rows(qt, sub).astype(BF16) for sub in range(2)]
    hrows = HEAD_DIM + ONES_ROWS

    def attend(kb, n, mask_fn):
        off = pl.multiple_of(kb * t, t)
        k = k_ref[0, pl.ds(off, n * t), :]
        vt = vt_ref[0, :, pl.ds(off, n * t)]
        s = [mask_fn(jnp.dot(k, qb[sub], preferred_element_type=F32), sub) for sub in range(2)]
        for sub in range(2):
            _attn_update(s[sub], vt[sub * hrows:(sub + 1) * hrows], m_sc, acc_sc, sub, fast)

    causal = _causal_t(t)
    attend(i, 1, lambda s_t, sub: jnp.where(causal, s_t, NEG))

    def past_blocks(kb, n):
        def row_mask(s_t, sub):
            rows = [jnp.where(sel_ref[0, sub, pl.ds(kb + c, 1), :] > 0.0, 0.0, NEG) for c in range(n)]
            bias = jnp.concatenate([jnp.broadcast_to(r, (t, t)) for r in rows], axis=0)
            return s_t + bias
        attend(kb, n, row_mask)

    _for_tile_groups(i, past_blocks, MOBA_GROUP)
    o_ref[0] = jnp.concatenate([_attn_out(acc_sc, sub, HEAD_DIM) for sub in range(2)], axis=0)


def moba_attention(qt_f32, kmean, k, vta, fast_ok):
    bsz, width, seq = qt_f32.shape
    t = MOBA_BLOCK
    nb = seq // MOBA_BLOCK
    n_sel = max(1, min(MOBA_TOPK, nb - 1))
    nbp = -(-nb // 8) * 8
    km = jnp.zeros((bsz, nbp, width), F32).at[:, :nb].set(kmean)
    prow = 2 * (HEAD_DIM + ONES_ROWS)
    n_pairs = width // LANES
    ts = _pick_tile(seq, (1024, 512, 256))
    sel = pl.pallas_call(
        functools.partial(_moba_select_kernel, ts=ts, n_sel=n_sel, nbp=nbp),
        grid=(bsz, n_pairs, seq // ts),
        in_specs=[pl.BlockSpec((1, LANES, ts), lambda b, h, j: (b, h, j)),
                  pl.BlockSpec((1, nbp, LANES), lambda b, h, j: (b, 0, h))],
        out_specs=pl.BlockSpec((1, 2, nbp, ts), lambda b, h, j: (b, h, 0, j)),
        out_shape=jax.ShapeDtypeStruct((bsz, 2 * n_pairs, nbp, seq), F32),
        compiler_params=_cparams(("parallel", "parallel", "parallel")),
        name="moba_select",
    )(qt_f32, km)

    def call(fast, *args):
        return pl.pallas_call(
            functools.partial(_moba_kernel, t=t, fast=fast),
            grid=(bsz, n_pairs, nb),
            in_specs=[pl.BlockSpec((1, LANES, t), lambda b, h, i: (b, h, i)),
                      pl.BlockSpec((1, 2, nbp, t), lambda b, h, i: (b, h, 0, i)),
                      pl.BlockSpec((1, seq, LANES), lambda b, h, i: (b, 0, h)),
                      pl.BlockSpec((1, prow, seq), lambda b, h, i: (b, h, 0))],
            out_specs=pl.BlockSpec((1, LANES, t), lambda b, h, i: (b, h, i)),
            out_shape=jax.ShapeDtypeStruct((bsz, width, seq), F32),
            scratch_shapes=[pltpu.VMEM((2, 1, t), F32), pltpu.VMEM((2, HEAD_DIM + ONES_ROWS, t), F32)],
            compiler_params=_cparams(("parallel", "parallel", "arbitrary")),
            name="moba_attn_fast" if fast else "moba_attn_safe",
        )(*args)

    args = (qt_f32, sel, k, vta)
    return lax.cond(fast_ok, functools.partial(call, True), functools.partial(call, False), *args)


def s5_operators(lam_re, lam_im, b_re, b_im, c_re, c_im, d_skip, log_dt, chunk):
    n_g, n_p = lam_re.shape
    n_c = d_skip.shape[-1]
    dt = jnp.exp(log_dt.astype(F32))[:, None]
    den = lam_re * lam_re + lam_im * lam_im
    mag = jnp.exp(lam_re * dt)
    abar_re = mag * jnp.cos(lam_im * dt)
    abar_im = mag * jnp.sin(lam_im * dt)
    coef_re = ((abar_re - 1.0) * lam_re + abar_im * lam_im) / den
    coef_im = (abar_im * lam_re - (abar_re - 1.0) * lam_im) / den
    bbar_re = coef_re[..., None] * b_re - coef_im[..., None] * b_im
    bbar_im = coef_re[..., None] * b_im + coef_im[..., None] * b_re
    tau = jnp.arange(chunk + 1, dtype=F32)[:, None, None]
    pmag = jnp.exp(lam_re * dt * tau)
    pw_re = pmag * jnp.cos(lam_im * dt * tau)
    pw_im = pmag * jnp.sin(lam_im * dt * tau)
    ab_re = pw_re[..., None] * bbar_re - pw_im[..., None] * bbar_im
    ab_im = pw_re[..., None] * bbar_im + pw_im[..., None] * bbar_re
    kern = (jnp.einsum('gop,tgpi->tgio', c_re, ab_re, precision=HI)
            - jnp.einsum('gop,tgpi->tgio', c_im, ab_im, precision=HI))
    t_idx = jnp.arange(chunk)
    lag = t_idx[None, :] - t_idx[:, None]
    toep = jnp.where((lag >= 0)[None, :, None, :, None],
                     kern[jnp.clip(lag, 0, chunk)].transpose(2, 0, 3, 1, 4), 0.0)
    eye = (jnp.eye(chunk)[None, :, None, :, None] * jnp.eye(n_c)[None, None, :, None, :])
    toep = toep + eye * d_skip[:, None, :, None, None]
    toep = toep.reshape(n_g, chunk * n_c, chunk * n_c)
    rev = chunk - 1 - t_idx
    w_in = jnp.concatenate([ab_re[rev].transpose(1, 0, 3, 2), ab_im[rev].transpose(1, 0, 3, 2)], axis=-1)
    w_in = w_in.reshape(n_g, chunk * n_c, 2 * n_p)
    a_re, a_im = pw_re[1:chunk + 1], pw_im[1:chunk + 1]
    st_re = (c_re[None] * a_re[:, :, None, :] - c_im[None] * a_im[:, :, None, :])
    st_im = (-c_re[None] * a_im[:, :, None, :] - c_im[None] * a_re[:, :, None, :])
    w_st = jnp.concatenate([st_re.transpose(1, 3, 0, 2), st_im.transpose(1, 3, 0, 2)], axis=1)
    w_st = w_st.reshape(n_g, 2 * n_p, chunk * n_c)
    return toep, w_in, w_st, pw_re[chunk], pw_im[chunk]


def _s5_in_kernel(u_ref, w_ref, o_ref):
    o_ref[0, 0] = _dot3(u_ref[0, 0], w_ref[0])


def _s5_scan_kernel(x_ref, aa_ref, ab_ref, abs_ref, o_ref, h_sc, hs_sc, *, steps, n_p):
    @pl.when(pl.program_id(0) == 0)
    def _():
        h_sc[...] = jnp.zeros(h_sc.shape, F32)
        hs_sc[...] = jnp.zeros(hs_sc.shape, F32)

    aa, ab, abs_ = aa_ref[...], ab_ref[...], abs_ref[...]

    def body(t, carry):
        h, hs = carry
        o_ref[t] = h
        x = x_ref[t]
        xs = pltpu.roll(x, n_p, 1)
        return aa * h + ab * hs + x, aa * hs + abs_ * h + xs

    h, hs = lax.fori_loop(0, steps, body, (h_sc[...], hs_sc[...]))
    h_sc[...] = h
    hs_sc[...] = hs


def _s5_out_kernel(u_ref, h0_ref, t_ref, w_ref, o_ref):
    y = _dot3(u_ref[0, 0], t_ref[0]) + _dot3(h0_ref[0, 0], w_ref[0])
    o_ref[0, 0] = jax.nn.gelu(y)


def s5_layer(u, ops):
    toep, w_in, w_st, ac_re, ac_im = ops
    bsz, seq, width = u.shape
    n_g, n_p = ac_re.shape
    n_c = width // n_g
    chunk = S5_CHUNK
    nj = seq // chunk
    row = chunk * n_c
    ur = u.reshape(bsz, nj, chunk, n_g, n_c).transpose(0, 3, 1, 2, 4).reshape(bsz, n_g, nj, row)
    hloc = pl.pallas_call(
        _s5_in_kernel,
        grid=(bsz, n_g),
        in_specs=[pl.BlockSpec((1, 1, nj, row), lambda b, g: (b, g, 0, 0)),
                  pl.BlockSpec((1, row, 2 * n_p), lambda b, g: (g, 0, 0))],
        out_specs=pl.BlockSpec((1, 1, nj, 2 * n_p), lambda b, g: (b, g, 0, 0)),
        out_shape=jax.ShapeDtypeStruct((bsz, n_g, nj, 2 * n_p), F32),
        compiler_params=_cparams(("parallel", "parallel")),
        name="s5_in",
    )(ur, w_in)
    bg = bsz * n_g
    xs = hloc.transpose(2, 0, 1, 3).reshape(nj, bg, 2 * n_p)
    rep = lambda a: jnp.tile(a, (bsz, 1))
    aa = rep(jnp.concatenate([ac_re, ac_re], axis=-1))
    ab = rep(jnp.concatenate([-ac_im, ac_im], axis=-1))
    abs_ = rep(jnp.concatenate([ac_im, -ac_im], axis=-1))
    steps = _pick_tile(nj, (128, 64, 32, 16, 8))
    cst = pl.BlockSpec((bg, 2 * n_p), lambda t: (0, 0))
    h0 = pl.pallas_call(
        functools.partial(_s5_scan_kernel, steps=steps, n_p=n_p),
        grid=(nj // steps,),
        in_specs=[pl.BlockSpec((steps, bg, 2 * n_p), lambda t: (t, 0, 0)), cst, cst, cst],
        out_specs=pl.BlockSpec((steps, bg, 2 * n_p), lambda t: (t, 0, 0)),
        out_shape=jax.ShapeDtypeStruct((nj, bg, 2 * n_p), F32),
        scratch_shapes=[pltpu.VMEM((bg, 2 * n_p), F32), pltpu.VMEM((bg, 2 * n_p), F32)],
        compiler_params=_cparams(("arbitrary",)),
        name="s5_scan",
    )(xs, aa, ab, abs_)
    h0 = h0.reshape(nj, bsz, n_g, 2 * n_p).transpose(1, 2, 0, 3)
    y = pl.pallas_call(
        _s5_out_kernel,
        grid=(bsz, n_g),
        in_specs=[pl.BlockSpec((1, 1, nj, row), lambda b, g: (b, g, 0, 0)),
                  pl.BlockSpec((1, 1, nj, 2 * n_p), lambda b, g: (b, g, 0, 0)),
                  pl.BlockSpec((1, row, row), lambda b, g: (g, 0, 0)),
                  pl.BlockSpec((1, 2 * n_p, row), lambda b, g: (g, 0, 0))],
        out_specs=pl.BlockSpec((1, 1, nj, row), lambda b, g: (b, g, 0, 0)),
        out_shape=jax.ShapeDtypeStruct((bsz, n_g, nj, row), F32),
        compiler_params=_cparams(("parallel", "parallel")),
        name="s5_out",
    )(ur, h0, toep, w_st)
    return y.reshape(bsz, n_g, nj, chunk, n_c).transpose(0, 2, 3, 1, 4).reshape(bsz, seq, width)


def _glu_kernel(y_ref, w_ref, b_ref, o_ref):
    y = y_ref[...]
    gate = jnp.dot(y.astype(BF16), w_ref[...], preferred_element_type=F32) + b_ref[...]
    o_ref[...] = (y * jax.nn.sigmoid(gate)).astype(o_ref.dtype)


def glu(y, w_bf16, b, tm=1024):
    n_tok, width = y.shape
    tm = _pick_tile(n_tok, (tm, 512, 256, 128))
    return pl.pallas_call(
        _glu_kernel,
        grid=(n_tok // tm,),
        in_specs=[pl.BlockSpec((tm, width), lambda i: (i, 0)),
                  pl.BlockSpec((width, width), lambda i: (0, 0)),
                  pl.BlockSpec((1, width), lambda i: (0, 0))],
        out_specs=pl.BlockSpec((tm, width), lambda i: (i, 0)),
        out_shape=jax.ShapeDtypeStruct((n_tok, width), F32),
        compiler_params=_cparams(("parallel",)),
        name="s5_glu",
    )(y, w_bf16, b.astype(F32).reshape(1, width))


def _outproj_kernel(a_ref, b_ref, w_ref, x_ref, g_ref, o_ref, *, half, a_t, b_t):
    a = a_ref[0].T if a_t else a_ref[0]
    b = b_ref[0].T if b_t else b_ref[0]
    y = (jnp.dot(a.astype(BF16), w_ref[:half, :], preferred_element_type=F32)
         + jnp.dot(b.astype(BF16), w_ref[half:, :], preferred_element_type=F32))
    o_ref[0] = x_ref[0] + g_ref[0] * y


def out_proj(a, b, w_bf16, x, gate, a_t=False, b_t=False, tm=1024):
    bsz, seq, d = x.shape
    half = w_bf16.shape[0] // 2
    tm = _pick_tile(seq, (tm, 256, 128))

    def spec(transposed):
        if transposed:
            return pl.BlockSpec((1, half, tm), lambda bb, i: (bb, 0, i))
        return pl.BlockSpec((1, tm, half), lambda bb, i: (bb, i, 0))

    return pl.pallas_call(
        functools.partial(_outproj_kernel, half=half, a_t=a_t, b_t=b_t),
        grid=(bsz, seq // tm),
        in_specs=[spec(a_t), spec(b_t),
                  pl.BlockSpec((2 * half, d), lambda bb, i: (0, 0)),
                  pl.BlockSpec((1, tm, d), lambda bb, i: (bb, i, 0)),
                  pl.BlockSpec((1, 1, d), lambda bb, i: (bb, 0, 0))],
        out_specs=pl.BlockSpec((1, tm, d), lambda bb, i: (bb, i, 0)),
        out_shape=jax.ShapeDtypeStruct((bsz, seq, d), F32),
        compiler_params=_cparams(("parallel", "parallel")),
        name="out_proj",
    )(a, b, w_bf16, x, gate)


def _router_kernel(x_ref, sc_ref, sh_ref, wr_ref, br_ref, h_ref, r_ref):
    h = _norm_mod(x_ref[0], sc_ref[0], sh_ref[0])
    h_ref[0] = h.astype(BF16)
    logits = _dot3(h, wr_ref[...]) + br_ref[...]
    tm = logits.shape[0]
    lane = lax.broadcasted_iota(I32, (tm, LANES), 1)
    big = 4 * LANES
    coarse = jnp.where(lane < N_GROUPS, logits, -jnp.inf)
    cmax = jnp.max(coarse, axis=1, keepdims=True)
    grp = jnp.min(jnp.where(coarse == cmax, lane, big), axis=1, keepdims=True)
    p_grp = 1.0 / jnp.sum(jnp.exp(coarse - cmax), axis=1, keepdims=True)
    lo = N_GROUPS + grp * EXPERTS_PER_GROUP
    fine = jnp.where(jnp.logical_and(lane >= lo, lane < lo + EXPERTS_PER_GROUP), logits, -jnp.inf)
    v1 = jnp.max(fine, axis=1, keepdims=True)
    i1 = jnp.min(jnp.where(fine == v1, lane, big), axis=1, keepdims=True)
    fine2 = jnp.where(lane == i1, -jnp.inf, fine)
    v2 = jnp.max(fine2, axis=1, keepdims=True)
    i2 = jnp.min(jnp.where(fine2 == v2, lane, big), axis=1, keepdims=True)
    e2 = jnp.exp(v2 - v1)
    wa = p_grp / (1.0 + e2)
    wb = p_grp * e2 / (1.0 + e2)
    ida = (i1 - N_GROUPS).astype(F32)
    idb = (i2 - N_GROUPS).astype(F32)
    r_ref[0] = jnp.where(lane == 0, ida, jnp.where(lane == 1, idb,
                         jnp.where(lane == 2, wa, jnp.where(lane == 3, wb, 0.0))))


def moe_router(x, sc, sh, wg, bg, we, be, tm=1024):
    bsz, seq, d = x.shape
    tm = _pick_tile(seq, (tm, 256, 128))
    n_r = N_GROUPS + N_EXPERTS
    wr = jnp.zeros((d, LANES), F32).at[:, :N_GROUPS].set(wg).at[:, N_GROUPS:n_r].set(we)
    br = jnp.zeros((1, LANES), F32).at[0, :N_GROUPS].set(bg).at[0, N_GROUPS:n_r].set(be)
    return pl.pallas_call(
        _router_kernel,
        grid=(bsz, seq // tm),
        in_specs=[pl.BlockSpec((1, tm, d), lambda b, i: (b, i, 0)),
                  pl.BlockSpec((1, 1, d), lambda b, i: (b, 0, 0)),
                  pl.BlockSpec((1, 1, d), lambda b, i: (b, 0, 0)),
                  pl.BlockSpec((d, LANES), lambda b, i: (0, 0)),
                  pl.BlockSpec((1, LANES), lambda b, i: (0, 0))],
        out_specs=[pl.BlockSpec((1, tm, d), lambda b, i: (b, i, 0)),
                   pl.BlockSpec((1, tm, LANES), lambda b, i: (b, i, 0))],
        out_shape=[jax.ShapeDtypeStruct((bsz, seq, d), BF16),
                   jax.ShapeDtypeStruct((bsz, seq, LANES), F32)],
        compiler_params=_cparams(("parallel", "parallel")),
        name="moe_router",
    )(x, sc, sh, wr, br)


MOE_SEG = 256


MOE_PAD = 16
MOE_EXPERTS_PER_STEP = 1


def _pad_rows(count):
    return jnp.floor((count + (MOE_PAD - 1)) * (1.0 / MOE_PAD)) * MOE_PAD


def _moe_kernel(erow_ref, route_ref, h_ref, x_ref, g_ref, w1_ref, w3_ref, w2_ref, o_ref,
                xs_sc, ys_sc, meta_sc, drow_sc, dcol_sc, *, chunk, rows, na):
    step = pl.program_id(1)
    n_seg = chunk // MOE_SEG
    lane1 = lax.broadcasted_iota(I32, (1, LANES), 1)

    def lane_scalar(row, k):
        return jnp.sum(jnp.where(lane1 == k, row, 0.0)).astype(I32)

    @pl.when(step == 0)
    def _():
        si = lax.broadcasted_iota(I32, (MOE_SEG, MOE_SEG), 0)
        sj = lax.broadcasted_iota(I32, (MOE_SEG, MOE_SEG), 1)
        upper = jnp.where(si < sj, 1.0, 0.0).astype(BF16)
        lower = jnp.where(sj < si, 1.0, 0.0).astype(BF16)
        erow = erow_ref[0]
        eid = lax.broadcasted_iota(I32, (N_EXPERTS, chunk), 0)
        sel = [erow[k:k + 1] == eid for k in range(2)]
        member = jnp.where(jnp.logical_or(sel[0], sel[1]), 1.0, 0.0)
        count = jnp.zeros((N_EXPERTS, 1), F32)
        parts = []
        for sgi in range(n_seg):
            seg = member[:, sgi * MOE_SEG:(sgi + 1) * MOE_SEG]
            parts.append(jnp.dot(seg.astype(BF16), upper, preferred_element_type=F32) + count)
            count = count + jnp.sum(seg, axis=1, keepdims=True)
        rank = jnp.concatenate(parts, axis=1)
        padded = _pad_rows(count)
        ei = lax.broadcasted_iota(I32, (N_EXPERTS, N_EXPERTS), 0)
        ej = lax.broadcasted_iota(I32, (N_EXPERTS, N_EXPERTS), 1)
        below = jnp.where(ej < ei, 1.0, 0.0).astype(BF16)
        start = jnp.dot(below, jnp.broadcast_to(padded, (N_EXPERTS, LANES)).astype(BF16),
                        preferred_element_type=F32)
        total = jnp.sum(padded, axis=0, keepdims=True)
        lane_e = lax.broadcasted_iota(I32, (N_EXPERTS, LANES), 1)
        meta_sc[...] = jnp.where(lane_e == 0, start, jnp.where(lane_e == 1, count, jnp.where(lane_e == 2, total, 0.0)))
        dest = start[:, 0:1] + rank
        for k in range(2):
            drow_sc[k] = jnp.sum(jnp.where(sel[k], dest, 0.0), axis=0, keepdims=True)
        route = route_ref[...]
        lane_c = lax.broadcasted_iota(I32, (chunk, LANES), 1)
        lane_f = lane_c.astype(F32)
        sel_c = [route[:, k:k + 1] == lane_f for k in range(2)]
        member_c = jnp.where(jnp.logical_or(sel_c[0], sel_c[1]), 1.0, 0.0)
        count_c = jnp.zeros((1, LANES), F32)
        parts_c = []
        for sgi in range(n_seg):
            seg = member_c[sgi * MOE_SEG:(sgi + 1) * MOE_SEG]
            parts_c.append(jnp.dot(lower, seg.astype(BF16), preferred_element_type=F32) + count_c)
            count_c = count_c + jnp.sum(seg, axis=0, keepdims=True)
        rank_c = jnp.concatenate(parts_c, axis=0)
        li = lax.broadcasted_iota(I32, (LANES, LANES), 0)
        lj = lax.broadcasted_iota(I32, (LANES, LANES), 1)
        before = jnp.where(li < lj, 1.0, 0.0).astype(BF16)
        start_c = jnp.dot(jnp.broadcast_to(_pad_rows(count_c), (8, LANES)).astype(BF16), before,
                          preferred_element_type=F32)[0:1]
        dest_c = start_c + rank_c
        dcols = [jnp.sum(jnp.where(sel_c[k], dest_c, 0.0), axis=1, keepdims=True) for k in range(2)]
        dcol_sc[...] = jnp.where(lane_c == 0, dcols[0], jnp.where(lane_c == 1, dcols[1], 0.0))
        ys_sc[...] = jnp.zeros(ys_sc.shape, BF16)
        xs_sc[...] = jnp.zeros(xs_sc.shape, BF16)
        d0, d1 = drow_sc[0], drow_sc[1]
        n_slabs = (jnp.sum(total).astype(I32) + MOE_SEG - 1) // MOE_SEG

        def sort_slab(s, carry):
            off = pl.multiple_of(s * MOE_SEG, MOE_SEG)
            r = (lax.broadcasted_iota(I32, (MOE_SEG, 1), 0) + off).astype(F32)
            onehot = jnp.where(jnp.logical_or(d0 == r, d1 == r), 1.0, 0.0).astype(BF16)
            xs_sc[pl.ds(off, MOE_SEG), :] = jnp.dot(onehot, h_ref[...], preferred_element_type=F32).astype(BF16)
            return carry

        lax.fori_loop(0, n_slabs, sort_slab, 0)

    for slot in range(MOE_EXPERTS_PER_STEP):
        meta = meta_sc[pl.ds(step * MOE_EXPERTS_PER_STEP + slot, 1), :]
        first = lane_scalar(meta, 0)
        count_e = lane_scalar(meta, 1)
        padded_e = ((count_e + MOE_PAD - 1) // MOE_PAD) * MOE_PAD

        def block(bi, carry, slot=slot, first=first, padded_e=padded_e):
            r0 = pl.multiple_of(first + bi * rows, MOE_PAD)
            xb = xs_sc[pl.ds(r0, rows), :]
            a = jnp.dot(xb, w1_ref[slot], preferred_element_type=F32)
            hid = (a * jax.nn.sigmoid(a)) * jnp.dot(xb, w3_ref[slot], preferred_element_type=F32)
            yb = jnp.dot(hid.astype(BF16), w2_ref[slot], preferred_element_type=F32)
            mine = lax.broadcasted_iota(I32, (rows, 1), 0) + bi * rows < padded_e
            ys_sc[pl.ds(r0, rows), :] = jnp.where(mine, yb, ys_sc[pl.ds(r0, rows), :].astype(F32)).astype(BF16)
            return carry

        lax.fori_loop(0, (count_e + rows - 1) // rows, block, 0)

    @pl.when(step == N_EXPERTS // MOE_EXPERTS_PER_STEP - 1)
    def _():
        route = route_ref[...]
        dcol = dcol_sc[...]
        d0, d1 = dcol[:, 0:1], dcol[:, 1:2]
        w0, w1 = route[:, 2:3], route[:, 3:4]
        r = lax.broadcasted_iota(I32, (1, na), 1).astype(F32)
        comb = (jnp.where(d0 == r, w0, 0.0) + jnp.where(d1 == r, w1, 0.0)).astype(BF16)
        moe = jnp.dot(comb, ys_sc[:na, :], preferred_element_type=F32)
        o_ref[...] = x_ref[...] + g_ref[0] * moe


def moe_experts(h, route, x, gate, w1_bf16, w3_bf16, w2_bf16, chunk=1024, rows=128):
    bsz, seq, d = x.shape
    chunk = _pick_tile(seq, (chunk, 512, 256))
    n_tok = bsz * seq
    n_chunks = n_tok // chunk
    per_b = seq // chunk
    ff = w1_bf16.shape[-1]
    na = -(-(2 * chunk + N_EXPERTS * (MOE_PAD - 1)) // MOE_SEG) * MOE_SEG
    route2 = route.reshape(n_tok, LANES)
    erow = route2[:, :2].astype(I32).T.reshape(2, n_chunks, chunk).transpose(1, 0, 2)
    eps = MOE_EXPERTS_PER_STEP
    out = pl.pallas_call(
        functools.partial(_moe_kernel, chunk=chunk, rows=rows, na=na),
        grid=(n_chunks, N_EXPERTS // eps),
        in_specs=[pl.BlockSpec((1, 2, chunk), lambda c, e: (c, 0, 0)),
                  pl.BlockSpec((chunk, LANES), lambda c, e: (c, 0)),
                  pl.BlockSpec((chunk, d), lambda c, e: (c, 0)),
                  pl.BlockSpec((chunk, d), lambda c, e: (c, 0)),
                  pl.BlockSpec((1, 1, d), lambda c, e: (c // per_b, 0, 0)),
                  pl.BlockSpec((eps, d, ff), lambda c, e: (e, 0, 0)),
                  pl.BlockSpec((eps, d, ff), lambda c, e: (e, 0, 0)),
                  pl.BlockSpec((eps, ff, d), lambda c, e: (e, 0, 0))],
        out_specs=pl.BlockSpec((chunk, d), lambda c, e: (c, 0)),
        out_shape=jax.ShapeDtypeStruct((n_tok, d), F32),
        scratch_shapes=[pltpu.VMEM((na + rows, d), BF16), pltpu.VMEM((na + rows, d), BF16),
                        pltpu.VMEM((N_EXPERTS, LANES), F32), pltpu.VMEM((2, 1, chunk), F32),
                        pltpu.VMEM((chunk, LANES), F32)],
        compiler_params=_cparams(("parallel", "arbitrary")),
        name="moe_experts",
    )(erow, route2, h.reshape(n_tok, d), x.reshape(n_tok, d), gate, w1_bf16, w3_bf16, w2_bf16)
    return out.reshape(bsz, seq, d)


def kernel(x, c, ada_w, ada_b, moe_wg, moe_bg, moe_we, moe_be, moe_w1, moe_w3, moe_w2,
           ev_w_in, ev_w_out, a_qn, a_kn, a_lq1, a_lk1, a_lq2, a_lk2, a_subln,
           b_qn, b_kn, b_idx_kn, od_w_in, od_w_out, c_qn, c_kn,
           s5_lam_re, s5_lam_im, s5_b_re, s5_b_im, s5_c_re, s5_c_im, s5_d, s5_log_dt,
           s5_glu_w, s5_glu_b):
    bsz, seq, d = x.shape
    depth = ada_w.shape[0]
    mod = ada_mod(c, ada_w, ada_b)
    rope_h = rope_lane_tables(seq, HEAD_DIM, ROT_HEAD)
    rope_i = rope_lane_tables(seq, IDX_DIM, ROT_IDX)
    qk_scale = HEAD_DIM ** -0.5 * math.log2(math.e)
    half_h, half_i = ROT_HEAD // 2, ROT_IDX // 2
    hw = A_HEADS * 2 * HEAD_DIM

    def pad_cols(w):
        n = w.shape[1]
        npad = -(-n // LANES) * LANES
        return jnp.zeros((w.shape[0], npad), F32).at[:, :n].set(w).astype(BF16)

    for l in range(depth):
        sh1, sc1, g1, sh2, sc2, g2 = [m[:, None, :] for m in jnp.split(mod[l], 6, axis=-1)]
        i = l // 2
        if l % 2 == 0:
            lam_init = 0.8 - 0.6 * math.exp(-0.3 * l)
            z = in_proj(x, sc1, sh1, pad_cols(ev_w_in[i]))
            q_t, v_a, v_b = ((BF16, "t"),), ((BF16, ("aug", 2 * HEAD_DIM)),), ((BF16, ("aug", HEAD_DIM)),)
            (aq,) = prep(z, 0, hw, hd=HEAD_DIM, gain=a_qn[i], rope=rope_h, half=half_h, scale=qk_scale, outs=q_t)
            (ak,) = prep(z, hw, hw, hd=HEAD_DIM, gain=a_kn[i], rope=rope_h, half=half_h)
            (av,) = prep(z, 2 * hw, hw, hd=HEAD_DIM, outs=v_a)
            o_a = diff_attention(aq, ak, av, a_lq1[i], a_lk1[i], a_lq2[i], a_lk2[i], a_subln[i], lam_init,
                                 _logit_bound(a_qn[i], a_kn[i]) <= MAX_FAST_LOGIT)
            (bq,) = prep(z, 3 * hw, hw, hd=HEAD_DIM, gain=b_qn[i], rope=rope_h, half=half_h, scale=qk_scale, outs=q_t)
            (bk,) = prep(z, 4 * hw, hw, hd=HEAD_DIM, gain=b_kn[i], rope=rope_h, half=half_h)
            (bv,) = prep(z, 5 * hw, hw, hd=HEAD_DIM, outs=v_b)
            iq_off = 6 * hw
            ik_off = iq_off + IDX_HEADS * IDX_DIM
            (iq,) = prep(z, iq_off, IDX_HEADS * IDX_DIM, hd=IDX_DIM, rope=rope_i, half=half_i, outs=q_t)
            ik, iw = prep(z, ik_off, LANES, hd=IDX_DIM, gain=b_idx_kn[i], rope=rope_i, half=half_i,
                          outs=((BF16, "n"), (F32, ("rawt", IDX_DIM, IDX_DIM + IDX_HEADS))))
            o_b = dsa_attention(bq, bk, bv, iq, ik, iw, _logit_bound(b_qn[i], b_kn[i]) <= MAX_FAST_LOGIT)
            x = out_proj(o_a, o_b, ev_w_out[i].astype(BF16), x, g1, a_t=True, b_t=True)
        else:
            z = in_proj(x, sc1, sh1, pad_cols(od_w_in[i]))
            (cq,) = prep(z, 0, hw, hd=HEAD_DIM, gain=c_qn[i], rope=rope_h, half=half_h, scale=qk_scale,
                         outs=((F32, "t"),))
            ck32, ck = prep(z, hw, hw, hd=HEAD_DIM, gain=c_kn[i], rope=rope_h, half=half_h,
                            outs=((F32, "n"), (BF16, "n")))
            (cv,) = prep(z, 2 * hw, hw, hd=HEAD_DIM, outs=((BF16, ("aug", HEAD_DIM)),))
            o_c = moba_attention(cq, block_means(ck32), ck, cv, _logit_bound(c_qn[i], c_kn[i]) <= MAX_FAST_LOGIT)
            ops = s5_operators(s5_lam_re[i], s5_lam_im[i], s5_b_re[i], s5_b_im[i], s5_c_re[i], s5_c_im[i],
                               s5_d[i], s5_log_dt[i], S5_CHUNK)
            y = s5_layer(z[:, :, 3 * hw:4 * hw], ops)
            y = glu(y.reshape(bsz * seq, hw), s5_glu_w[i].astype(BF16), s5_glu_b[i]).reshape(bsz, seq, hw)
            x = out_proj(o_c, y, od_w_out[i].astype(BF16), x, g1, a_t=True)
        h, route = moe_router(x, sc2, sh2, moe_wg[l], moe_bg[l], moe_we[l], moe_be[l])
        x = moe_experts(h, route, x, g2, moe_w1[l].astype(BF16), moe_w3[l].astype(BF16), moe_w2[l].astype(BF16))
    return x
```
